```python
import math, functools
import jax, jax.numpy as jnp
from jax import lax
import numpy as np


D_MODEL = 2048
BATCH = 4
SEQ = 2048
DEPTH = 1
DEC_BATCH = 128
DEC_SEQ = 4
PAST_LEN = 8192
PAGE_SIZE = 128

HEAD_DIM = 64
ATTN_WIDTH = D_MODEL // 2
N_HEADS = ATTN_WIDTH // HEAD_DIM
N_KV_HEADS = max(1, N_HEADS // 8)
GROUP = N_HEADS // N_KV_HEADS
WINDOW = 128
BLOCK = 128
GMLP_WIDTH = D_MODEL - ATTN_WIDTH
GMLP_HEADS = 8
GMLP_HEAD_DIM = GMLP_WIDTH // GMLP_HEADS
CHUNK = 128
D_FF = 4 * D_MODEL
NUM_BUCKETS = 32
MAX_DISTANCE = 128
EPS = 1e-6

Q_COLS = N_HEADS * HEAD_DIM
KV_COLS = N_KV_HEADS * HEAD_DIM
IN_COLS = Q_COLS + 2 * KV_COLS + 2 * GMLP_WIDTH
SPLITS = (Q_COLS, Q_COLS + KV_COLS, Q_COLS + 2 * KV_COLS, Q_COLS + 2 * KV_COLS + GMLP_WIDTH)

kernel_name = 'hymba_swa_sink_gmlp_decoder_step'


def rmsnorm(x, gain):
    x32 = x.astype(jnp.float32)
    y = x32 * lax.rsqrt(jnp.mean(x32 * x32, axis=-1, keepdims=True) + EPS)
    return (y * gain.astype(jnp.float32)).astype(x.dtype)


def adaln_params(c, w_ada, b_ada):
    mod = jax.nn.silu(c) @ w_ada + b_ada
    return jnp.split(mod[:, None, :], 6, axis=-1)


def t5_bucket(dist):
    n = jnp.maximum(dist, 0)
    max_exact = NUM_BUCKETS // 2
    nf = jnp.maximum(n, 1).astype(jnp.float32)
    large = max_exact + (jnp.log(nf / max_exact) / math.log(MAX_DISTANCE / max_exact)
                         * (NUM_BUCKETS - max_exact)).astype(jnp.int32)
    large = jnp.minimum(large, NUM_BUCKETS - 1)
    return jnp.where(n < max_exact, n, large)


def rel_bias(dist, rel_table):
    b = rel_table[t5_bucket(dist)].astype(jnp.float32)
    b = jnp.transpose(b, (2, 0, 1))
    return b.reshape(N_KV_HEADS, GROUP, dist.shape[0], dist.shape[1])


def sink_attend(q, k, v, bias, valid, sinks):
    scale = HEAD_DIM ** -0.5
    logits = jnp.einsum('...qkgd,...jkd->...kgqj', q, k).astype(jnp.float32) * scale + bias
    logits = jnp.where(valid, logits, -1e30)
    sink = sinks.astype(jnp.float32).reshape(N_KV_HEADS, GROUP)[:, :, None, None]
    m = jnp.maximum(jnp.max(logits, axis=-1, keepdims=True), sink)
    p = jnp.exp(logits - m)
    probs = p / (jnp.sum(p, axis=-1, keepdims=True) + jnp.exp(sink - m))
    return jnp.einsum('...kgqj,...jkd->...qkgd', probs.astype(v.dtype), v)


def attention_prompt(q, k, v, rel_table, sinks):
    B, S = q.shape[:2]
    nb = S // BLOCK
    qb = q.reshape(B, nb, BLOCK, N_KV_HEADS, GROUP, HEAD_DIM)
    pad = ((0, 0), (BLOCK, 0), (0, 0), (0, 0))
    k_prev = jnp.pad(k, pad)[:, :S].reshape(B, nb, BLOCK, N_KV_HEADS, HEAD_DIM)
    v_prev = jnp.pad(v, pad)[:, :S].reshape(B, nb, BLOCK, N_KV_HEADS, HEAD_DIM)
    kb = jnp.concatenate([k_prev, k.reshape(B, nb, BLOCK, N_KV_HEADS, HEAD_DIM)], axis=2)
    vb = jnp.concatenate([v_prev, v.reshape(B, nb, BLOCK, N_KV_HEADS, HEAD_DIM)], axis=2)
    qi = jnp.arange(BLOCK)[:, None]
    kj = jnp.arange(2 * BLOCK)[None, :]
    dist = BLOCK + qi - kj
    key_pos = jnp.arange(nb)[:, None, None] * BLOCK - BLOCK + kj[None]
    valid = (dist >= 0) & (dist < WINDOW) & (key_pos >= 0)
    out = sink_attend(qb, kb, vb, rel_bias(dist, rel_table), valid[:, None, None], sinks)
    return out.reshape(B, S, ATTN_WIDTH)


def attention_sample(q, k, v, cache_k, cache_v, rel_table, sinks):
    DB, T = q.shape[:2]
    W = cache_k.shape[1]
    kk = jnp.concatenate([cache_k, k], axis=1)
    vv = jnp.concatenate([cache_v, v], axis=1)
    dist = W + jnp.arange(T)[:, None] - jnp.arange(W + T)[None, :]
    valid = (dist >= 0) & (dist < WINDOW)
    out = sink_attend(q, kk, vv, rel_bias(dist, rel_table), valid, sinks)
    return out.reshape(DB, T, ATTN_WIDTH)


def gmlp_spatial_gate(u, v, w_s, b_s, v_gain):
    B, S, _ = u.shape
    L = min(S, CHUNK)
    n = S // L
    u = u.reshape(B, n, L, GMLP_HEADS, GMLP_HEAD_DIM)
    v = rmsnorm(v.reshape(B, n, L, GMLP_HEADS, GMLP_HEAD_DIM), v_gain)
    w = jnp.tril(w_s[:, :L, :L])
    mixed = jnp.einsum('hij,bnjhc->bnihc', w, v) + jnp.transpose(b_s[:, :L])[None, None, :, :, None]
    out = (u * mixed).reshape(B, S, GMLP_WIDTH)
    return out, v.reshape(B, S, GMLP_HEADS, GMLP_HEAD_DIM)


def trunk_layer(x, c, attend, w_ada, b_ada, g_pre_mix, w_in, gmlp_v_gain, gmlp_w_s, gmlp_b_s,
                g_attn_out, g_gmlp_out, w_out, g_post_mix, g_pre_ff, w_ff1, w_ff2, g_post_ff):
    B, S, _ = x.shape
    sh_m, sc_m, gt_m, sh_f, sc_f, gt_f = adaln_params(c, w_ada, b_ada)
    h = rmsnorm(x, g_pre_mix) * (1 + sc_m) + sh_m
    q, k, v, gu, gv = jnp.split(h @ w_in, SPLITS, axis=-1)
    q = q.reshape(B, S, N_KV_HEADS, GROUP, HEAD_DIM)
    k = k.reshape(B, S, N_KV_HEADS, HEAD_DIM)
    v = v.reshape(B, S, N_KV_HEADS, HEAD_DIM)
    attn_out = attend(q, k, v)
    gmlp_out, v_rows = gmlp_spatial_gate(jax.nn.gelu(gu), jax.nn.gelu(gv), gmlp_w_s, gmlp_b_s, gmlp_v_gain)
    merged = jnp.concatenate([rmsnorm(attn_out, g_attn_out), rmsnorm(gmlp_out, g_gmlp_out)], axis=-1)
    x = x + gt_m * rmsnorm(merged @ w_out, g_post_mix)
    h = rmsnorm(x, g_pre_ff) * (1 + sc_f) + sh_f
    f = jnp.square(jax.nn.relu(h @ w_ff1)) @ w_ff2
    x = x + gt_f * rmsnorm(f, g_post_ff)
    return x, k, v, v_rows


def setup_inputs(seed: int = 0) -> dict:
    key = jax.random.key(seed)
    ks = jax.random.split(key, 24)
    f32 = jnp.float32

    def nrm(k, shape, scale):
        return jax.random.normal(k, shape, f32) * scale

    cw = min(WINDOW, PAST_LEN)
    d_s = D_MODEL ** -0.5
    return {
        'x_prompt': nrm(ks[0], (BATCH, SEQ, D_MODEL), 1.0),
        'x_sample': nrm(ks[1], (DEC_BATCH, DEC_SEQ, D_MODEL), 1.0),
        'cache_k': nrm(ks[2], (DEPTH, DEC_BATCH, cw, N_KV_HEADS, HEAD_DIM), 1.0),
        'cache_v': nrm(ks[3], (DEPTH, DEC_BATCH, cw, N_KV_HEADS, HEAD_DIM), 1.0),
        'c_prompt': nrm(ks[4], (BATCH, D_MODEL), 1.0),
        'c_sample': nrm(ks[5], (DEC_BATCH, D_MODEL), 1.0),
        'rel_bias_table': nrm(ks[6], (NUM_BUCKETS, N_HEADS), 0.5),
        'w_ada': nrm(ks[7], (DEPTH, D_MODEL, 6 * D_MODEL), 0.5 * d_s),
        'b_ada': nrm(ks[8], (DEPTH, 6 * D_MODEL), 0.02),
        'g_pre_mix': 1.0 + nrm(ks[9], (DEPTH, D_MODEL), 0.05),
        'w_in': nrm(ks[10], (DEPTH, D_MODEL, IN_COLS), d_s),
        'attn_sinks': nrm(ks[11], (DEPTH, N_HEADS), 1.0),
        'gmlp_v_gain': 1.0 + nrm(ks[12], (DEPTH, GMLP_HEADS, GMLP_HEAD_DIM), 0.05),
        'gmlp_w_s': nrm(ks[13], (DEPTH, GMLP_HEADS, CHUNK, CHUNK), CHUNK ** -0.5),
        'gmlp_b_s': 1.0 + nrm(ks[14], (DEPTH, GMLP_HEADS, CHUNK), 0.05),
        'g_attn_out': 1.0 + nrm(ks[15], (DEPTH, ATTN_WIDTH), 0.05),
        'g_gmlp_out': 1.0 + nrm(ks[16], (DEPTH, GMLP_WIDTH), 0.05),
        'w_out': nrm(ks[17], (DEPTH, D_MODEL, D_MODEL), d_s),
        'g_post_mix': 1.0 + nrm(ks[18], (DEPTH, D_MODEL), 0.05),
        'g_pre_ff': 1.0 + nrm(ks[19], (DEPTH, D_MODEL), 0.05),
        'w_ff1': nrm(ks[20], (DEPTH, D_MODEL, D_FF), d_s),
        'w_ff2': nrm(ks[21], (DEPTH, D_FF, D_MODEL), D_FF ** -0.5),
        'g_post_ff': 1.0 + nrm(ks[22], (DEPTH, D_MODEL), 0.05),
    }


def reference(x_prompt, x_sample, cache_k, cache_v, c_prompt, c_sample, rel_bias_table, w_ada, b_ada,
              g_pre_mix, w_in, attn_sinks, gmlp_v_gain, gmlp_w_s, gmlp_b_s, g_attn_out, g_gmlp_out,
              w_out, g_post_mix, g_pre_ff, w_ff1, w_ff2, g_post_ff):
    y_p, y_s = x_prompt, x_sample
    kp_rows, vp_rows, ks_rows, vs_rows, gv_rows = [], [], [], [], []
    cw_prompt = min(WINDOW, x_prompt.shape[1])
    for l in range(DEPTH):
        weights = (w_ada[l], b_ada[l], g_pre_mix[l], w_in[l], gmlp_v_gain[l], gmlp_w_s[l], gmlp_b_s[l],
                   g_attn_out[l], g_gmlp_out[l], w_out[l], g_post_mix[l], g_pre_ff[l], w_ff1[l], w_ff2[l],
                   g_post_ff[l])
        attend_p = functools.partial(attention_prompt, rel_table=rel_bias_table, sinks=attn_sinks[l])
        attend_s = functools.partial(attention_sample, cache_k=cache_k[l], cache_v=cache_v[l],
                                     rel_table=rel_bias_table, sinks=attn_sinks[l])
        y_p, k_p, v_p, _ = trunk_layer(y_p, c_prompt, attend_p, *weights)
        y_s, k_s, v_s, gv_s = trunk_layer(y_s, c_sample, attend_s, *weights)
        kp_rows.append(k_p[:, k_p.shape[1] - cw_prompt:])
        vp_rows.append(v_p[:, v_p.shape[1] - cw_prompt:])
        ks_rows.append(k_s)
        vs_rows.append(v_s)
        gv_rows.append(gv_s)
    new_k_prompt = jnp.stack(kp_rows, axis=0)
    new_v_prompt = jnp.stack(vp_rows, axis=0)
    new_k_sample = jnp.stack(ks_rows, axis=0)
    new_v_sample = jnp.stack(vs_rows, axis=0)
    gmlp_v_sample = jnp.stack(gv_rows, axis=0)
    return (y_p, y_s, new_k_prompt, new_v_prompt, new_k_sample, new_v_sample, gmlp_v_sample)
```

```python
import functools
import math

import numpy as np
import jax
import jax.numpy as jnp
from jax import lax
from jax.experimental import pallas as pl
from jax.experimental.pallas import tpu as pltpu

D_MODEL = 2048
HEAD_DIM = 64
ATTN_WIDTH = 1024
N_HEADS = 16
N_KV_HEADS = 2
GROUP = 8
WINDOW = 128
BLOCK = 128
GMLP_WIDTH = 1024
GMLP_HEADS = 8
GMLP_HEAD_DIM = 128
D_FF = 4 * D_MODEL
NUM_BUCKETS = 32
MAX_DISTANCE = 128
EPS = 1e-6
KV_COLS = N_KV_HEADS * HEAD_DIM
Q_COLS = N_HEADS * HEAD_DIM
IN_COLS = Q_COLS + 2 * KV_COLS + 2 * GMLP_WIDTH
ATTN_SCALE = HEAD_DIM ** -0.5
NEG_INF = -1e30

V7X_VMEM_BYTES = 64 * 1024 * 1024
VMEM_LIMIT_BYTES = 56 * 1024 * 1024

BF16 = jnp.bfloat16
F32 = jnp.float32


def _params(n_axes, vmem=VMEM_LIMIT_BYTES):
    return pltpu.CompilerParams(
        dimension_semantics=("arbitrary",) * n_axes, vmem_limit_bytes=vmem)


def _resident(shape, index_map):
    return pl.BlockSpec(shape, index_map, pipeline_mode=pl.Buffered(1))


def _rms(x, gain):
    return x * lax.rsqrt(jnp.mean(x * x, axis=-1, keepdims=True) + EPS) * gain


def _rows(m, n_rows):
    r = m.shape[0]
    if r == 1 or r == n_rows:
        return m
    return jnp.concatenate([m] * (n_rows // r), axis=0)


def _dot(a, b):
    return jnp.dot(a, b, preferred_element_type=F32)


def _dot_nt(a, b):
    return lax.dot_general(a, b, (((1,), (1,)), ((), ())), preferred_element_type=F32)


def _t5_bucket(dist):
    n = jnp.maximum(dist, 0)
    max_exact = NUM_BUCKETS // 2
    nf = jnp.maximum(n, 1).astype(jnp.float32)
    large = max_exact + (jnp.log(nf / max_exact) / math.log(MAX_DISTANCE / max_exact)
                         * (NUM_BUCKETS - max_exact)).astype(jnp.int32)
    large = jnp.minimum(large, NUM_BUCKETS - 1)
    return jnp.where(n < max_exact, n, large)


SMP_KEYS = WINDOW + 8


def _prompt_dist():
    qi = np.arange(BLOCK)[:, None]
    kj = np.arange(2 * BLOCK)[None, :]
    return BLOCK + qi - kj


def _sample_dist(t_len):
    t = np.arange(t_len)[:, None]
    j = np.arange(SMP_KEYS)[None, :]
    return WINDOW + t - j


def _prep_body(tab_ref, tabt_ref, bkt_p_ref, bkt_s_ref, ws_ref,
               bias_p_ref, bias_s_ref, wtril_ref, *, t_len):
    bkt_p = bkt_p_ref[...]
    for kh in range(N_KV_HEADS):
        for g in range(GROUP):
            h = kh * GROUP + g
            acc = jnp.zeros((BLOCK, 2 * BLOCK), F32)
            for b in range(NUM_BUCKETS):
                acc = jnp.where(bkt_p == b, tab_ref[b, h], acc)
            bias_p_ref[kh, g * BLOCK:(g + 1) * BLOCK, :] = acc
    tabt = tabt_ref[...]
    for kh in range(N_KV_HEADS):
        for t in range(t_len):
            row = bkt_s_ref[t:t + 1, :]
            acc = jnp.zeros((GROUP, SMP_KEYS), F32)
            for b in range(NUM_BUCKETS):
                col = tabt[kh * GROUP:(kh + 1) * GROUP, b:b + 1]
                acc = jnp.where(row == b, col, acc)
            bias_s_ref[kh, t * GROUP:(t + 1) * GROUP, :] = acc
    ii = lax.broadcasted_iota(jnp.int32, (BLOCK, BLOCK), 0)
    jj = lax.broadcasted_iota(jnp.int32, (BLOCK, BLOCK), 1)
    for h in range(GMLP_HEADS):
        wtril_ref[h] = jnp.where(jj <= ii, ws_ref[h], 0.0).astype(BF16)


def _prep(rel_table, w_s, t_len):
    bkt_p = _t5_bucket(jnp.asarray(_prompt_dist(), jnp.int32))
    bkt_s = _t5_bucket(jnp.asarray(_sample_dist(8), jnp.int32))
    smem = pl.BlockSpec(memory_space=pltpu.SMEM)
    vmem = pl.BlockSpec(memory_space=pltpu.VMEM)
    return pl.pallas_call(
        functools.partial(_prep_body, t_len=t_len),
        out_shape=(jax.ShapeDtypeStruct((N_KV_HEADS, GROUP * BLOCK, 2 * BLOCK), F32),
                   jax.ShapeDtypeStruct((N_KV_HEADS, GROUP * t_len, SMP_KEYS), F32),
                   jax.ShapeDtypeStruct((GMLP_HEADS, BLOCK, BLOCK), BF16)),
        in_specs=[smem, vmem, vmem, vmem, vmem],
        out_specs=(vmem, vmem, vmem),
        compiler_params=pltpu.CompilerParams(vmem_limit_bytes=32 * 1024 * 1024),
        name="prep",
    )(rel_table, rel_table.T, bkt_p, bkt_s, w_s)


ADA_TN = 1024


def _adaln_body(c_ref, w_ref, b_ref, os_ref, op_ref, *, n_s, n_p):
    c = c_ref[...]
    s = (c * jax.nn.sigmoid(c)).astype(BF16)
    r = _dot(s, w_ref[...].astype(BF16)) + b_ref[...]
    os_ref[...] = r[:n_s]
    op_ref[...] = r[n_s:n_s + n_p]


def _adaln(c_prompt, c_sample, w_ada, b_ada):
    n_p, n_s = c_prompt.shape[0], c_sample.shape[0]
    pad = (-(n_p + n_s)) % 16
    c_all = jnp.concatenate([c_sample, c_prompt, jnp.zeros((pad, D_MODEL), F32)], axis=0)
    n_all = c_all.shape[0]
    n_out = w_ada.shape[1]
    return pl.pallas_call(
        functools.partial(_adaln_body, n_s=n_s, n_p=n_p),
        out_shape=(jax.ShapeDtypeStruct((n_s, n_out), F32),
                   jax.ShapeDtypeStruct((n_p, n_out), F32)),
        grid=(n_out // ADA_TN,),
        in_specs=[_resident((n_all, D_MODEL), lambda j: (0, 0)),
                  pl.BlockSpec((D_MODEL, ADA_TN), lambda j: (0, j)),
                  pl.BlockSpec((1, ADA_TN), lambda j: (0, j))],
        out_specs=(pl.BlockSpec((n_s, ADA_TN), lambda j: (0, j)),
                   pl.BlockSpec((n_p, ADA_TN), lambda j: (0, j))),
        compiler_params=_params(1, 40 * 1024 * 1024),
        name="adaln",
    )(c_all, w_ada, b_ada.reshape(1, n_out))


def _mix_in_body(x_ref, sh_ref, sc_ref, g_ref, w_ref, vg_ref,
                 q_ref, k_ref, v_ref, gu_ref, gv_ref):
    x = x_ref[...]
    tm = x.shape[0]
    h = _rms(x, g_ref[...]) * (1.0 + _rows(sc_ref[...], tm)) + _rows(sh_ref[...], tm)
    h = h.astype(BF16)
    c0, c1, c2, c3 = Q_COLS, Q_COLS + KV_COLS, Q_COLS + 2 * KV_COLS, Q_COLS + 2 * KV_COLS + GMLP_WIDTH
    q_ref[...] = _dot(h, w_ref[:, :c0]).astype(q_ref.dtype)
    kv = _dot(h, w_ref[:, c0:c2])
    k_ref[...] = kv[:, :KV_COLS]
    v_ref[...] = kv[:, KV_COLS:]
    gu_ref[...] = jax.nn.gelu(_dot(h, w_ref[:, c2:c3])).astype(gu_ref.dtype)
    gv = jax.nn.gelu(_dot(h, w_ref[:, c3:]))
    vg = vg_ref[...]
    for hd in range(GMLP_HEADS):
        sl = slice(hd * GMLP_HEAD_DIM, (hd + 1) * GMLP_HEAD_DIM)
        gv_ref[:, sl] = _rms(gv[:, sl], vg[:, sl]).astype(gv_ref.dtype)


def _mix_in(x, x_spec, sh_spec, sc_spec, mod, grid, g_pre, w_in, v_gain, outs, out_specs):
    return pl.pallas_call(
        _mix_in_body,
        out_shape=outs,
        grid=grid,
        in_specs=[x_spec, sh_spec, sc_spec,
                  _resident((1, D_MODEL), lambda i: (0, 0)),
                  _resident((D_MODEL, IN_COLS), lambda i: (0, 0)),
                  _resident((1, GMLP_WIDTH), lambda i: (0, 0))],
        out_specs=out_specs,
        compiler_params=_params(1),
        name="mix_in",
    )(x, mod, mod, g_pre, w_in, v_gain)


def _softmax_pv(s, sink, v):
    m = jnp.maximum(jnp.max(s, axis=-1, keepdims=True), sink)
    p = jnp.exp(s - m)
    den = jnp.sum(p, axis=-1, keepdims=True) + jnp.exp(sink - m)
    return _dot(p.astype(BF16), v) / den


def _mix_core_body(q_ref, kc_ref, kp_ref, vc_ref, vp_ref, gu_ref, gv_ref,
                   bias_ref, valid_ref, sink_ref, wtril_ref, bs_ref, ga_ref, gg_ref,
                   o_ref, *, blocks_per_seq):
    n = pl.program_id(0) % blocks_per_seq
    col = lax.broadcasted_iota(jnp.int32, (GROUP * BLOCK, 2 * BLOCK), 1)
    valid = jnp.logical_and(valid_ref[...] > 0.5, jnp.logical_or(n > 0, col >= BLOCK))
    q = q_ref[...]
    heads = []
    for kh in range(N_KV_HEADS):
        ks = slice(kh * HEAD_DIM, (kh + 1) * HEAD_DIM)
        k = jnp.concatenate([kp_ref[:, ks], kc_ref[:, ks]], axis=0).astype(BF16)
        v = jnp.concatenate([vp_ref[:, ks], vc_ref[:, ks]], axis=0).astype(BF16)
        base = kh * GROUP * HEAD_DIM
        qs = jnp.concatenate(
            [q[:, base + g * HEAD_DIM: base + (g + 1) * HEAD_DIM] for g in range(GROUP)], axis=0)
        s = _dot_nt(qs, k) * ATTN_SCALE + bias_ref[kh]
        s = jnp.where(valid, s, NEG_INF)
        o = _softmax_pv(s, sink_ref[kh], v)
        heads += [o[g * BLOCK:(g + 1) * BLOCK] for g in range(GROUP)]
    attn = jnp.concatenate(heads, axis=-1)
    bs = bs_ref[...]
    gated = []
    for hd in range(GMLP_HEADS):
        sl = slice(hd * GMLP_HEAD_DIM, (hd + 1) * GMLP_HEAD_DIM)
        mixed = _dot(wtril_ref[hd], gv_ref[:, sl]) + bs[:, hd:hd + 1]
        gated.append(gu_ref[:, sl] * mixed)
    gm = jnp.concatenate(gated, axis=-1)
    o_ref[:, :ATTN_WIDTH] = _rms(attn, ga_ref[...]).astype(o_ref.dtype)
    o_ref[:, ATTN_WIDTH:] = _rms(gm, gg_ref[...]).astype(o_ref.dtype)


def _mix_core(q, k, v, gu, gvn, bias_p, valid_p, sink_col, wtril, bs_t, g_attn, g_gmlp, seq):
    n_tok = q.shape[0]
    nblk = n_tok // BLOCK
    bps = seq // BLOCK
    cur = lambda i: (i, 0)
    prev = lambda i: (jnp.maximum(i - 1, 0), 0)
    full2 = lambda i: (0, 0)
    full3 = lambda i: (0, 0, 0)
    return pl.pallas_call(
        functools.partial(_mix_core_body, blocks_per_seq=bps),
        out_shape=jax.ShapeDtypeStruct((n_tok, D_MODEL), BF16),
        grid=(nblk,),
        in_specs=[pl.BlockSpec((BLOCK, Q_COLS), cur),
                  pl.BlockSpec((BLOCK, KV_COLS), cur), pl.BlockSpec((BLOCK, KV_COLS), prev),
                  pl.BlockSpec((BLOCK, KV_COLS), cur), pl.BlockSpec((BLOCK, KV_COLS), prev),
                  pl.BlockSpec((BLOCK, GMLP_WIDTH), cur), pl.BlockSpec((BLOCK, GMLP_WIDTH), cur),
                  _resident(bias_p.shape, full3), _resident(valid_p.shape, full2),
                  _resident(sink_col.shape, full3), _resident(wtril.shape, full3),
                  _resident(bs_t.shape, full2),
                  _resident((1, ATTN_WIDTH), full2), _resident((1, GMLP_WIDTH), full2)],
        out_specs=pl.BlockSpec((BLOCK, D_MODEL), cur),
        compiler_params=_params(1),
        name="mix_core",
    )(q, k, k, v, v, gu, gvn, bias_p, valid_p, sink_col, wtril, bs_t, g_attn, g_gmlp)


SMP_BB = 16


def _attn_smp_body(q_ref, ck_ref, cv_ref, nk_ref, nv_ref, bias_ref, valid_ref, sink_ref,
                   o_ref, *, t_len):
    valid = valid_ref[...] > 0.5
    pad = jnp.zeros((SMP_KEYS - WINDOW - t_len, HEAD_DIM), F32)
    for b in range(SMP_BB):
        for kh in range(N_KV_HEADS):
            ks = slice(kh * HEAD_DIM, (kh + 1) * HEAD_DIM)
            k = jnp.concatenate([ck_ref[b, :, ks], nk_ref[b, :, ks], pad], axis=0).astype(BF16)
            v = jnp.concatenate([cv_ref[b, :, ks], nv_ref[b, :, ks], pad], axis=0).astype(BF16)
            q = q_ref[b, kh].astype(BF16)
            s = _dot_nt(q, k) * ATTN_SCALE + bias_ref[kh]
            s = jnp.where(valid, s, NEG_INF)
            o_ref[b, kh] = _softmax_pv(s, sink_ref[kh], v)


def _attn_smp(q5, cache_k, cache_v, new_k, new_v, bias_s, valid_s, sink_s, t_len):
    n_b = q5.shape[0]
    rows = t_len * GROUP
    b4 = lambda i: (i, 0, 0, 0)
    b3 = lambda i: (i, 0, 0)
    return pl.pallas_call(
        functools.partial(_attn_smp_body, t_len=t_len),
        out_shape=jax.ShapeDtypeStruct((n_b, N_KV_HEADS, rows, HEAD_DIM), F32),
        grid=(n_b // SMP_BB,),
        in_specs=[pl.BlockSpec((SMP_BB, N_KV_HEADS, rows, HEAD_DIM), b4),
                  pl.BlockSpec((SMP_BB, WINDOW, KV_COLS), b3),
                  pl.BlockSpec((SMP_BB, WINDOW, KV_COLS), b3),
                  pl.BlockSpec((SMP_BB, t_len, KV_COLS), b3),
                  pl.BlockSpec((SMP_BB, t_len, KV_COLS), b3),
                  _resident(bias_s.shape, lambda i: (0, 0, 0)),
                  _resident(valid_s.shape, lambda i: (0, 0)),
                  _resident(sink_s.shape, lambda i: (0, 0, 0))],
        out_specs=pl.BlockSpec((SMP_BB, N_KV_HEADS, rows, HEAD_DIM), b4),
        compiler_params=_params(1, 32 * 1024 * 1024),
        name="attn_smp",
    )(q5, cache_k, cache_v, new_k, new_v, bias_s, valid_s, sink_s)


def _out_proj_tail(merged, x_ref, gt_ref, sh_ref, sc_ref, w_ref, gpm_ref, gpf_ref, x1_ref, hf_ref):
    tm = merged.shape[0]
    o = _dot(merged, w_ref[...])
    x1 = x_ref[...] + _rows(gt_ref[...], tm) * _rms(o, gpm_ref[...])
    x1_ref[...] = x1
    hf = _rms(x1, gpf_ref[...]) * (1.0 + _rows(sc_ref[...], tm)) + _rows(sh_ref[...], tm)
    hf_ref[...] = hf.astype(hf_ref.dtype)


def _out_proj_body(m_ref, *rest):
    _out_proj_tail(m_ref[...], *rest)


def _out_proj_smp_body(w4_ref, b4_ref, a_ref, gu_ref, gv_ref, ga_ref, gg_ref, *rest, t_len):
    i = pl.program_id(0)
    gated = []
    for hd in range(GMLP_HEADS):
        acc = jnp.zeros((a_ref.shape[0], GMLP_HEAD_DIM), F32)
        for j in range(t_len):
            w = jnp.where(j <= i, w4_ref[(hd * t_len + i) * t_len + j], 0.0)
            lo = j * GMLP_WIDTH + hd * GMLP_HEAD_DIM
            acc = acc + w * gv_ref[:, lo:lo + GMLP_HEAD_DIM]
        mixed = acc + b4_ref[hd * t_len + i]
        gated.append(gu_ref[:, hd * GMLP_HEAD_DIM:(hd + 1) * GMLP_HEAD_DIM] * mixed)
    gm = jnp.concatenate(gated, axis=-1)
    merged = jnp.concatenate([_rms(a_ref[...], ga_ref[...]), _rms(gm, gg_ref[...])], axis=-1)
    _out_proj_tail(merged.astype(BF16), *rest)


FFN_TF = 1024


def _ffn_body(h_ref, w1_ref, w2_ref, x1_ref, gt_ref, g_ref, o_ref):
    j = pl.program_id(1)

    @pl.when(j == 0)
    def _():
        o_ref[...] = jnp.zeros_like(o_ref)

    a = jnp.maximum(_dot(h_ref[...], w1_ref[...]), 0.0)
    o_ref[...] += _dot((a * a).astype(BF16), w2_ref[...])

    @pl.when(j == pl.num_programs(1) - 1)
    def _():
        tm = o_ref.shape[0]
        o_ref[...] = x1_ref[...] + _rows(gt_ref[...], tm) * _rms(o_ref[...], g_ref[...])


def _ffn(hf, w1, w2, x1, mod, gt_spec, g_post, tm):
    n_tok = hf.shape[0]
    return pl.pallas_call(
        _ffn_body,
        out_shape=jax.ShapeDtypeStruct((n_tok, D_MODEL), F32),
        grid=(n_tok // tm, D_FF // FFN_TF),
        in_specs=[pl.BlockSpec((tm, D_MODEL), lambda i, j: (i, 0)),
                  pl.BlockSpec((D_MODEL, FFN_TF), lambda i, j: (0, j)),
                  pl.BlockSpec((FFN_TF, D_MODEL), lambda i, j: (j, 0)),
                  pl.BlockSpec((tm, D_MODEL), lambda i, j: (i, 0)),
                  gt_spec,
                  _resident((1, D_MODEL), lambda i, j: (0, 0))],
        out_specs=pl.BlockSpec((tm, D_MODEL), lambda i, j: (i, 0)),
        compiler_params=_params(2),
        name="ffn",
    )(hf, w1, w2, x1, mod, g_post)


MIX_TM = 512
OUT_TM = 512
FFN_TM = 512


def _row(v):
    return v.reshape(1, -1)


def _layer(x_prompt, x_sample, cache_k, cache_v, c_prompt, c_sample, rel_table, w_ada, b_ada,
           g_pre_mix, w_in, sinks, v_gain, w_s, b_s, g_attn, g_gmlp, w_out, g_post_mix,
           g_pre_ff, w_ff1, w_ff2, g_post_ff):
    n_b, seq, _ = x_prompt.shape
    n_db, t_len, _ = x_sample.shape
    n_tok = n_b * seq
    D = D_MODEL

    w_in_b = w_in.astype(BF16)
    w_out_b = w_out.astype(BF16)
    w1_b = w_ff1.astype(BF16)
    w2_b = w_ff2.astype(BF16)
    g_pre_mix, g_attn, g_gmlp = _row(g_pre_mix), _row(g_attn), _row(g_gmlp)
    g_post_mix, g_pre_ff, g_post_ff = _row(g_post_mix), _row(g_pre_ff), _row(g_post_ff)
    v_gain = _row(v_gain)

    bias_p, bias_s, wtril = _prep(rel_table, w_s, t_len)
    mod_s, mod_p = _adaln(c_prompt, c_sample, w_ada, b_ada)
    mod_p = mod_p.reshape(n_b, 1, 6 * D)
    SH_M, SC_M, GT_M, SH_F, SC_F, GT_F = range(6)

    def pmod(chunk, tm, n_axes=1):
        per = seq // tm
        if n_axes == 1:
            return pl.BlockSpec((None, 1, D), lambda i: (i // per, 0, chunk))
        return pl.BlockSpec((None, 1, D), lambda i, j: (i // per, 0, chunk))

    def smod(chunk, n_axes=1):
        if n_axes == 1:
            return pl.BlockSpec((n_db, D), lambda i: (0, chunk))
        return pl.BlockSpec((n_db, D), lambda i, j: (0, chunk))

    valid_p = np.tile(((_prompt_dist() >= 0) & (_prompt_dist() < WINDOW)).astype(np.float32), (GROUP, 1))
    sd = _sample_dist(t_len)
    valid_s = ((sd >= 0) & (sd < WINDOW) & (np.arange(SMP_KEYS)[None, :] < WINDOW + t_len))
    valid_s = np.repeat(valid_s.astype(np.float32), GROUP, axis=0)
    sink2 = sinks.reshape(N_KV_HEADS, GROUP)
    sink_p = jnp.repeat(sink2, BLOCK, axis=1)[:, :, None]
    sink_s = jnp.tile(sink2, (1, t_len))[:, :, None]

    xp = x_prompt.reshape(n_tok, D)
    tile = lambda w: pl.BlockSpec((MIX_TM, w), lambda i: (i, 0))
    q_p, k_p, v_p, gu_p, gvn_p = _mix_in(
        xp, tile(D), pmod(SH_M, MIX_TM), pmod(SC_M, MIX_TM), mod_p, (n_tok // MIX_TM,),
        g_pre_mix, w_in_b, v_gain,
        (jax.ShapeDtypeStruct((n_tok, Q_COLS), BF16),
         jax.ShapeDtypeStruct((n_tok, KV_COLS), F32),
         jax.ShapeDtypeStruct((n_tok, KV_COLS), F32),
         jax.ShapeDtypeStruct((n_tok, GMLP_WIDTH), F32),
         jax.ShapeDtypeStruct((n_tok, GMLP_WIDTH), BF16)),
        (tile(Q_COLS), tile(KV_COLS), tile(KV_COLS), tile(GMLP_WIDTH), tile(GMLP_WIDTH)))

    merged_p = _mix_core(q_p, k_p, v_p, gu_p, gvn_p, bias_p, jnp.asarray(valid_p), sink_p,
                         wtril, b_s.T, g_attn, g_gmlp, seq)

    otile = lambda w: pl.BlockSpec((OUT_TM, w), lambda i: (i, 0))
    one = lambda i: (0, 0)
    x1_p, hf_p = pl.pallas_call(
        _out_proj_body,
        out_shape=(jax.ShapeDtypeStruct((n_tok, D), F32), jax.ShapeDtypeStruct((n_tok, D), BF16)),
        grid=(n_tok // OUT_TM,),
        in_specs=[otile(D), otile(D), pmod(GT_M, OUT_TM), pmod(SH_F, OUT_TM), pmod(SC_F, OUT_TM),
                  _resident((D, D), one), _resident((1, D), one), _resident((1, D), one)],
        out_specs=(otile(D), otile(D)),
        compiler_params=_params(1),
        name="out_proj",
    )(merged_p, xp, mod_p, mod_p, mod_p, w_out_b, g_post_mix, g_pre_ff)

    y_p = _ffn(hf_p, w1_b, w2_b, x1_p, mod_p, pmod(GT_F, FFN_TM, 2), g_post_ff, FFN_TM)

    xs = x_sample.reshape(n_db, t_len * D)
    lane = lambda w: pl.BlockSpec((n_db, w), lambda t: (0, t))
    q_s, k_s, v_s, gu_s, gvn_s = _mix_in(
        xs, lane(D), smod(SH_M), smod(SC_M), mod_s, (t_len,),
        g_pre_mix, w_in_b, v_gain,
        (jax.ShapeDtypeStruct((n_db, t_len * Q_COLS), F32),
         jax.ShapeDtypeStruct((n_db, t_len * KV_COLS), F32),
         jax.ShapeDtypeStruct((n_db, t_len * KV_COLS), F32),
         jax.ShapeDtypeStruct((n_db, t_len * GMLP_WIDTH), F32),
         jax.ShapeDtypeStruct((n_db, t_len * GMLP_WIDTH), F32)),
        (lane(Q_COLS), lane(KV_COLS), lane(KV_COLS), lane(GMLP_WIDTH), lane(GMLP_WIDTH)))

    q5 = q_s.reshape(n_db, t_len, N_KV_HEADS, GROUP, HEAD_DIM).transpose(0, 2, 1, 3, 4)
    q5 = q5.reshape(n_db, N_KV_HEADS, t_len * GROUP, HEAD_DIM)
    new_k = k_s.reshape(n_db, t_len, KV_COLS)
    new_v = v_s.reshape(n_db, t_len, KV_COLS)
    o5 = _attn_smp(q5, cache_k.reshape(n_db, WINDOW, KV_COLS), cache_v.reshape(n_db, WINDOW, KV_COLS),
                   new_k, new_v, bias_s, jnp.asarray(valid_s), sink_s, t_len)
    attn_s = o5.reshape(n_db, N_KV_HEADS, t_len, GROUP, HEAD_DIM).transpose(0, 2, 1, 3, 4)
    attn_s = attn_s.reshape(n_db, t_len * ATTN_WIDTH)

    smem = pl.BlockSpec(memory_space=pltpu.SMEM)
    w4 = w_s[:, :t_len, :t_len].reshape(-1)
    b4 = b_s[:, :t_len].reshape(-1)
    tmaj = lambda: pl.BlockSpec((None, n_db, D), lambda t: (t, 0, 0))
    x1_s, hf_s = pl.pallas_call(
        functools.partial(_out_proj_smp_body, t_len=t_len),
        out_shape=(jax.ShapeDtypeStruct((t_len, n_db, D), F32),
                   jax.ShapeDtypeStruct((t_len, n_db, D), BF16)),
        grid=(t_len,),
        in_specs=[smem, smem, lane(ATTN_WIDTH), lane(GMLP_WIDTH),
                  _resident((n_db, t_len * GMLP_WIDTH), one),
                  _resident((1, ATTN_WIDTH), one), _resident((1, GMLP_WIDTH), one),
                  lane(D), smod(GT_M), smod(SH_F), smod(SC_F),
                  _resident((D, D), one), _resident((1, D), one), _resident((1, D), one)],
        out_specs=(tmaj(), tmaj()),
        compiler_params=_params(1),
        name="out_proj_smp",
    )(w4, b4, attn_s, gu_s, gvn_s, g_attn, g_gmlp, xs, mod_s, mod_s, mod_s,
      w_out_b, g_post_mix, g_pre_ff)

    n_st = t_len * n_db
    y_s = _ffn(hf_s.reshape(n_st, D), w1_b, w2_b, x1_s.reshape(n_st, D), mod_s,
               smod(GT_F, 2), g_post_ff, n_st)
    y_s = y_s.reshape(t_len, n_db, D).transpose(1, 0, 2)

    cw = min(WINDOW, seq)
    k_p4 = k_p.reshape(n_b, seq, N_KV_HEADS, HEAD_DIM)[:, seq - cw:]
    v_p4 = v_p.reshape(n_b, seq, N_KV_HEADS, HEAD_DIM)[:, seq - cw:]
    return (y_p.reshape(n_b, seq, D), y_s, k_p4, v_p4,
            k_s.reshape(n_db, t_len, N_KV_HEADS, HEAD_DIM),
            v_s.reshape(n_db, t_len, N_KV_HEADS, HEAD_DIM),
            gvn_s.reshape(n_db, t_len, GMLP_HEADS, GMLP_HEAD_DIM))


def kernel(x_prompt, x_sample, cache_k, cache_v, c_prompt, c_sample, rel_bias_table, w_ada, b_ada,
           g_pre_mix, w_in, attn_sinks, gmlp_v_gain, gmlp_w_s, gmlp_b_s, g_attn_out, g_gmlp_out,
           w_out, g_post_mix, g_pre_ff, w_ff1, w_ff2, g_post_ff):
    depth = w_in.shape[0]
    assert depth == 1, "single-layer step"
    outs = _layer(x_prompt, x_sample, cache_k[0], cache_v[0], c_prompt, c_sample, rel_bias_table,
                  w_ada[0], b_ada[0], g_pre_mix[0], w_in[0], attn_sinks[0], gmlp_v_gain[0],
                  gmlp_w_s[0], gmlp_b_s[0], g_attn_out[0], g_gmlp_out[0], w_out[0], g_post_mix[0],
                  g_pre_ff[0], w_ff1[0], w_ff2[0], g_post_ff[0])
    y_p, y_s, k_p, v_p, k_s, v_s, gv_s = outs
    return (y_p, y_s, k_p[None], v_p[None], k_s[None], v_s[None], gv_s[None])
```

```python
import functools
import math

import numpy as np
import jax
import jax.numpy as jnp
from jax import lax
from jax.experimental import pallas as pl
from jax.experimental.pallas import tpu as pltpu

D_MODEL = 2048
HEAD_DIM = 64
ATTN_WIDTH = 1024
N_HEADS = 16
N_KV_HEADS = 2
GROUP = 8
WINDOW = 128
BLOCK = 128
GMLP_WIDTH = 1024
GMLP_HEADS = 8
GMLP_HEAD_DIM = 128
D_FF = 4 * D_MODEL
NUM_BUCKETS = 32
MAX_DISTANCE = 128
EPS = 1e-6
KV_COLS = N_KV_HEADS * HEAD_DIM
Q_COLS = N_HEADS * HEAD_DIM
IN_COLS = Q_COLS + 2 * KV_COLS + 2 * GMLP_WIDTH
ATTN_SCALE = HEAD_DIM ** -0.5
NEG_INF = -1e30

V7X_VMEM_BYTES = 64 * 1024 * 1024
VMEM_LIMIT_BYTES = 56 * 1024 * 1024

BF16 = jnp.bfloat16
F32 = jnp.float32


def _params(n_axes, vmem=VMEM_LIMIT_BYTES):
    return pltpu.CompilerParams(
        dimension_semantics=("arbitrary",) * n_axes, vmem_limit_bytes=vmem)


def _resident(shape, index_map):
    return pl.BlockSpec(shape, index_map, pipeline_mode=pl.Buffered(1))


def _rms(x, gain):
    return x * lax.rsqrt(jnp.mean(x * x, axis=-1, keepdims=True) + EPS) * gain


def _rows(m, n_rows):
    r = m.shape[0]
    if r == 1 or r == n_rows:
        return m
    return jnp.concatenate([m] * (n_rows // r), axis=0)


def _dot(a, b):
    return jnp.dot(a, b, preferred_element_type=F32)


def _dot_nt(a, b):
    return lax.dot_general(a, b, (((1,), (1,)), ((), ())), preferred_element_type=F32)


def _t5_bucket(dist):
    n = jnp.maximum(dist, 0)
    max_exact = NUM_BUCKETS // 2
    nf = jnp.maximum(n, 1).astype(jnp.float32)
    large = max_exact + (jnp.log(nf / max_exact) / math.log(MAX_DISTANCE / max_exact)
                         * (NUM_BUCKETS - max_exact)).astype(jnp.int32)
    large = jnp.minimum(large, NUM_BUCKETS - 1)
    return jnp.where(n < max_exact, n, large)


SMP_KEYS = WINDOW + 8


def _prompt_dist():
    qi = np.arange(BLOCK)[:, None]
    kj = np.arange(2 * BLOCK)[None, :]
    return BLOCK + qi - kj


def _sample_dist(t_len):
    t = np.arange(t_len)[:, None]
    j = np.arange(SMP_KEYS)[None, :]
    return WINDOW + t - j


def _prep_body(tab_ref, tabt_ref, bkt_p_ref, bkt_s_ref, ws_ref,
               bias_p_ref, bias_s_ref, wtril_ref, *, t_len):
    bkt_p = bkt_p_ref[...]
    for kh in range(N_KV_HEADS):
        for g in range(GROUP):
            h = kh * GROUP + g
            acc = jnp.zeros((2 * BLOCK, BLOCK), F32)
            for b in range(NUM_BUCKETS):
                acc = jnp.where(bkt_p == b, tab_ref[b, h], acc)
            bias_p_ref[kh, :, g * BLOCK:(g + 1) * BLOCK] = acc
    tabt = tabt_ref[...]
    for kh in range(N_KV_HEADS):
        for t in range(t_len):
            row = bkt_s_ref[t:t + 1, :]
            acc = jnp.zeros((GROUP, SMP_KEYS), F32)
            for b in range(NUM_BUCKETS):
                col = tabt[kh * GROUP:(kh + 1) * GROUP, b:b + 1]
                acc = jnp.where(row == b, col, acc)
            bias_s_ref[kh, t * GROUP:(t + 1) * GROUP, :] = acc
    ii = lax.broadcasted_iota(jnp.int32, (BLOCK, BLOCK), 0)
    jj = lax.broadcasted_iota(jnp.int32, (BLOCK, BLOCK), 1)
    for h in range(GMLP_HEADS):
        wtril_ref[h] = jnp.where(jj <= ii, ws_ref[h], 0.0).astype(BF16)


def _prep(rel_table, w_s, t_len):
    bkt_p = _t5_bucket(jnp.asarray(_prompt_dist().T, jnp.int32))
    bkt_s = _t5_bucket(jnp.asarray(_sample_dist(8), jnp.int32))
    smem = pl.BlockSpec(memory_space=pltpu.SMEM)
    vmem = pl.BlockSpec(memory_space=pltpu.VMEM)
    return pl.pallas_call(
        functools.partial(_prep_body, t_len=t_len),
        out_shape=(jax.ShapeDtypeStruct((N_KV_HEADS, 2 * BLOCK, GROUP * BLOCK), F32),
                   jax.ShapeDtypeStruct((N_KV_HEADS, GROUP * t_len, SMP_KEYS), F32),
                   jax.ShapeDtypeStruct((GMLP_HEADS, BLOCK, BLOCK), BF16)),
        in_specs=[smem, vmem, vmem, vmem, vmem],
        out_specs=(vmem, vmem, vmem),
        compiler_params=pltpu.CompilerParams(vmem_limit_bytes=32 * 1024 * 1024),
        name="prep",
    )(rel_table, rel_table.T, bkt_p, bkt_s, w_s)


ADA_TN = 1024


def _adaln_body(c_ref, w_ref, b_ref, os_ref, op_ref, *, n_s, n_p):
    c = c_ref[...]
    s = (c * jax.nn.sigmoid(c)).astype(BF16)
    r = _dot(s, w_ref[...].astype(BF16)) + b_ref[...]
    os_ref[...] = r[:n_s]
    op_ref[...] = r[n_s:n_s + n_p]


def _adaln(c_prompt, c_sample, w_ada, b_ada):
    n_p, n_s = c_prompt.shape[0], c_sample.shape[0]
    pad = (-(n_p + n_s)) % 16
    c_all = jnp.concatenate([c_sample, c_prompt, jnp.zeros((pad, D_MODEL), F32)], axis=0)
    n_all = c_all.shape[0]
    n_out = w_ada.shape[1]
    return pl.pallas_call(
        functools.partial(_adaln_body, n_s=n_s, n_p=n_p),
        out_shape=(jax.ShapeDtypeStruct((n_s, n_out), F32),
                   jax.ShapeDtypeStruct((n_p, n_out), F32)),
        grid=(n_out // ADA_TN,),
        in_specs=[_resident((n_all, D_MODEL), lambda j: (0, 0)),
                  pl.BlockSpec((D_MODEL, ADA_TN), lambda j: (0, j)),
                  pl.BlockSpec((1, ADA_TN), lambda j: (0, j))],
        out_specs=(pl.BlockSpec((n_s, ADA_TN), lambda j: (0, j)),
                   pl.BlockSpec((n_p, ADA_TN), lambda j: (0, j))),
        compiler_params=_params(1, 40 * 1024 * 1024),
        name="adaln",
    )(c_all, w_ada, b_ada.reshape(1, n_out))


def _mix_in_body(x_ref, sh_ref, sc_ref, g_ref, w_ref, vg_ref,
                 q_ref, k_ref, v_ref, gu_ref, gv_ref, *vt_ref, transposed):
    x = x_ref[...]
    tm = x.shape[0]
    h = _rms(x, g_ref[...]) * (1.0 + _rows(sc_ref[...], tm)) + _rows(sh_ref[...], tm)
    h = h.astype(BF16)
    c0, c1, c2, c3 = Q_COLS, Q_COLS + KV_COLS, Q_COLS + 2 * KV_COLS, Q_COLS + 2 * KV_COLS + GMLP_WIDTH
    q = _dot(h, w_ref[:, :c0]) * ATTN_SCALE
    kv = _dot(h, w_ref[:, c0:c2])
    k_ref[...] = kv[:, :KV_COLS]
    v_ref[...] = kv[:, KV_COLS:]
    if transposed:
        q_ref[...] = q.T.astype(q_ref.dtype)
        vt_ref[0][...] = kv[:, KV_COLS:].T.astype(vt_ref[0].dtype)
    else:
        q_ref[...] = q.astype(q_ref.dtype)
    gu_ref[...] = jax.nn.gelu(_dot(h, w_ref[:, c2:c3])).astype(gu_ref.dtype)
    gv = jax.nn.gelu(_dot(h, w_ref[:, c3:]))
    vg = vg_ref[...]
    for hd in range(GMLP_HEADS):
        sl = slice(hd * GMLP_HEAD_DIM, (hd + 1) * GMLP_HEAD_DIM)
        gv_ref[:, sl] = _rms(gv[:, sl], vg[:, sl]).astype(gv_ref.dtype)


def _mix_in(x, x_spec, sh_spec, sc_spec, mod, grid, g_pre, w_in, v_gain, outs, out_specs):
    return pl.pallas_call(
        functools.partial(_mix_in_body, transposed=len(outs) == 6),
        out_shape=outs,
        grid=grid,
        in_specs=[x_spec, sh_spec, sc_spec,
                  _resident((1, D_MODEL), lambda i: (0, 0)),
                  _resident((D_MODEL, IN_COLS), lambda i: (0, 0)),
                  _resident((1, GMLP_WIDTH), lambda i: (0, 0))],
        out_specs=out_specs,
        compiler_params=_params(1),
        name="mix_in",
    )(x, mod, mod, g_pre, w_in, v_gain)


def _softmax_pv(s, sink, v):
    m = jnp.maximum(jnp.max(s, axis=-1, keepdims=True), sink)
    p = jnp.exp(s - m)
    den = jnp.sum(p, axis=-1, keepdims=True) + jnp.exp(sink - m)
    return _dot(p.astype(BF16), v) / den


def _mix_core_body(qt_ref, kc_ref, kp_ref, vtc_ref, vtp_ref, gu_ref, gv_ref,
                   bias_ref, valid_ref, sink_ref, wtril_ref, bs_ref, gat_ref, gg_ref,
                   o_ref, *, blocks_per_seq):
    first = (pl.program_id(0) % blocks_per_seq == 0).astype(jnp.int32)
    valid = valid_ref[first] > 0.5
    k = jnp.concatenate([kp_ref[...], kc_ref[...]], axis=0).astype(BF16)
    vt = jnp.concatenate([vtp_ref[...], vtc_ref[...]], axis=1)
    zeros = jnp.zeros((HEAD_DIM, GROUP * BLOCK), BF16)
    outs = []
    for kh in range(N_KV_HEADS):
        base = kh * GROUP * HEAD_DIM
        qs = jnp.concatenate(
            [qt_ref[base + g * HEAD_DIM: base + (g + 1) * HEAD_DIM, :] for g in range(GROUP)], axis=1)
        qp = jnp.concatenate([qs, zeros] if kh == 0 else [zeros, qs], axis=0)
        s = jnp.where(valid, _dot(k, qp) + bias_ref[kh], NEG_INF)
        sink = sink_ref[kh]
        m = jnp.maximum(jnp.max(s, axis=0, keepdims=True), sink)
        p = jnp.exp(s - m)
        den = jnp.sum(p, axis=0, keepdims=True) + jnp.exp(sink - m)
        outs.append(_dot(vt[kh * HEAD_DIM:(kh + 1) * HEAD_DIM, :], p.astype(BF16)) / den)
    ot = jnp.concatenate(outs, axis=0)
    sq = jnp.sum(ot * ot, axis=0, keepdims=True)
    tot = sq[:, :BLOCK]
    for g in range(1, GROUP):
        tot = tot + sq[:, g * BLOCK:(g + 1) * BLOCK]
    r = lax.rsqrt(tot / ATTN_WIDTH + EPS)
    a = ot * jnp.concatenate([r] * GROUP, axis=1) * gat_ref[...]
    for g in range(GROUP):
        o_ref[:, g * BLOCK:(g + 1) * BLOCK] = a[:, g * BLOCK:(g + 1) * BLOCK].T.astype(o_ref.dtype)
    bs = bs_ref[...]
    gated = []
    for hd in range(GMLP_HEADS):
        sl = slice(hd * GMLP_HEAD_DIM, (hd + 1) * GMLP_HEAD_DIM)
        mixed = _dot(wtril_ref[hd], gv_ref[:, sl]) + bs[:, hd:hd + 1]
        gated.append(gu_ref[:, sl] * mixed)
    gm = jnp.concatenate(gated, axis=-1)
    o_ref[:, ATTN_WIDTH:] = _rms(gm, gg_ref[...]).astype(o_ref.dtype)


def _mix_core(qt, k, vt, gu, gvn, bias_t, valid_t, sink_row, wtril, bs_t, gain_t, g_gmlp, seq):
    n_tok = k.shape[0]
    nblk = n_tok // BLOCK
    bps = seq // BLOCK
    cur = lambda i: (i, 0)
    prev = lambda i: (jnp.maximum(i - 1, 0), 0)
    cur_t = lambda i: (0, i)
    prev_t = lambda i: (0, jnp.maximum(i - 1, 0))
    full2 = lambda i: (0, 0)
    full3 = lambda i: (0, 0, 0)
    return pl.pallas_call(
        functools.partial(_mix_core_body, blocks_per_seq=bps),
        out_shape=jax.ShapeDtypeStruct((n_tok, D_MODEL), BF16),
        grid=(nblk,),
        in_specs=[pl.BlockSpec((Q_COLS, BLOCK), cur_t),
                  pl.BlockSpec((BLOCK, KV_COLS), cur), pl.BlockSpec((BLOCK, KV_COLS), prev),
                  pl.BlockSpec((KV_COLS, BLOCK), cur_t), pl.BlockSpec((KV_COLS, BLOCK), prev_t),
                  pl.BlockSpec((BLOCK, GMLP_WIDTH), cur), pl.BlockSpec((BLOCK, GMLP_WIDTH), cur),
                  _resident(bias_t.shape, full3), _resident(valid_t.shape, full3),
                  _resident(sink_row.shape, full3), _resident(wtril.shape, full3),
                  _resident(bs_t.shape, full2),
                  _resident(gain_t.shape, full2), _resident((1, GMLP_WIDTH), full2)],
        out_specs=pl.BlockSpec((BLOCK, D_MODEL), cur),
        compiler_params=_params(1),
        name="mix_core",
    )(qt, k, k, vt, vt, gu, gvn, bias_t, valid_t, sink_row, wtril, bs_t, gain_t, g_gmlp)


SMP_BB = 16


def _attn_smp_body(q_ref, ck_ref, cv_ref, nk_ref, nv_ref, bias_ref, valid_ref, sink_ref,
                   o_ref, *, t_len):
    valid = valid_ref[...] > 0.5
    pad = jnp.zeros((SMP_KEYS - WINDOW - t_len, HEAD_DIM), F32)
    for b in range(SMP_BB):
        for kh in range(N_KV_HEADS):
            ks = slice(kh * HEAD_DIM, (kh + 1) * HEAD_DIM)
            k = jnp.concatenate([ck_ref[b, :, ks], nk_ref[b, :, ks], pad], axis=0).astype(BF16)
            v = jnp.concatenate([cv_ref[b, :, ks], nv_ref[b, :, ks], pad], axis=0).astype(BF16)
            q = q_ref[b, kh].astype(BF16)
            s = jnp.where(valid, _dot_nt(q, k) + bias_ref[kh], NEG_INF)
            o_ref[b, kh] = _softmax_pv(s, sink_ref[kh], v)


def _attn_smp(q5, cache_k, cache_v, new_k, new_v, bias_s, valid_s, sink_s, t_len):
    n_b = q5.shape[0]
    rows = t_len * GROUP
    b4 = lambda i: (i, 0, 0, 0)
    b3 = lambda i: (i, 0, 0)
    return pl.pallas_call(
        functools.partial(_attn_smp_body, t_len=t_len),
        out_shape=jax.ShapeDtypeStruct((n_b, N_KV_HEADS, rows, HEAD_DIM), F32),
        grid=(n_b // SMP_BB,),
        in_specs=[pl.BlockSpec((SMP_BB, N_KV_HEADS, rows, HEAD_DIM), b4),
                  pl.BlockSpec((SMP_BB, WINDOW, KV_COLS), b3),
                  pl.BlockSpec((SMP_BB, WINDOW, KV_COLS), b3),
                  pl.BlockSpec((SMP_BB, t_len, KV_COLS), b3),
                  pl.BlockSpec((SMP_BB, t_len, KV_COLS), b3),
                  _resident(bias_s.shape, lambda i: (0, 0, 0)),
                  _resident(valid_s.shape, lambda i: (0, 0)),
                  _resident(sink_s.shape, lambda i: (0, 0, 0))],
        out_specs=pl.BlockSpec((SMP_BB, N_KV_HEADS, rows, HEAD_DIM), b4),
        compiler_params=_params(1, 32 * 1024 * 1024),
        name="attn_smp",
    )(q5, cache_k, cache_v, new_k, new_v, bias_s, valid_s, sink_s)


def _out_proj_tail(merged, x_ref, gt_ref, sh_ref, sc_ref, w_ref, gpm_ref, gpf_ref, x1_ref, hf_ref):
    tm = merged.shape[0]
    o = _dot(merged, w_ref[...])
    x1 = x_ref[...] + _rows(gt_ref[...], tm) * _rms(o, gpm_ref[...])
    x1_ref[...] = x1
    hf = _rms(x1, gpf_ref[...]) * (1.0 + _rows(sc_ref[...], tm)) + _rows(sh_ref[...], tm)
    hf_ref[...] = hf.astype(hf_ref.dtype)


def _out_proj_body(m_ref, *rest):
    _out_proj_tail(m_ref[...], *rest)


def _out_proj_smp_body(w4_ref, b4_ref, a_ref, gu_ref, gv_ref, ga_ref, gg_ref, *rest, t_len):
    i = pl.program_id(0)
    gated = []
    for hd in range(GMLP_HEADS):
        acc = jnp.zeros((a_ref.shape[0], GMLP_HEAD_DIM), F32)
        for j in range(t_len):
            w = jnp.where(j <= i, w4_ref[(hd * t_len + i) * t_len + j], 0.0)
            lo = j * GMLP_WIDTH + hd * GMLP_HEAD_DIM
            acc = acc + w * gv_ref[:, lo:lo + GMLP_HEAD_DIM]
        mixed = acc + b4_ref[hd * t_len + i]
        gated.append(gu_ref[:, hd * GMLP_HEAD_DIM:(hd + 1) * GMLP_HEAD_DIM] * mixed)
    gm = jnp.concatenate(gated, axis=-1)
    merged = jnp.concatenate([_rms(a_ref[...], ga_ref[...]), _rms(gm, gg_ref[...])], axis=-1)
    _out_proj_tail(merged.astype(BF16), *rest)


FFN_TF = 1024


def _ffn_body(h_ref, w1_ref, w2_ref, x1_ref, gt_ref, g_ref, o_ref):
    j = pl.program_id(1)

    @pl.when(j == 0)
    def _():
        o_ref[...] = jnp.zeros_like(o_ref)

    a = jnp.maximum(_dot(h_ref[...], w1_ref[...]), 0.0)
    o_ref[...] += _dot((a * a).astype(BF16), w2_ref[...])

    @pl.when(j == pl.num_programs(1) - 1)
    def _():
        tm = o_ref.shape[0]
        o_ref[...] = x1_ref[...] + _rows(gt_ref[...], tm) * _rms(o_ref[...], g_ref[...])


def _ffn(hf, w1, w2, x1, mod, gt_spec, g_post, tm):
    n_tok = hf.shape[0]
    return pl.pallas_call(
        _ffn_body,
        out_shape=jax.ShapeDtypeStruct((n_tok, D_MODEL), F32),
        grid=(n_tok // tm, D_FF // FFN_TF),
        in_specs=[pl.BlockSpec((tm, D_MODEL), lambda i, j: (i, 0)),
                  pl.BlockSpec((D_MODEL, FFN_TF), lambda i, j: (0, j)),
                  pl.BlockSpec((FFN_TF, D_MODEL), lambda i, j: (j, 0)),
                  pl.BlockSpec((tm, D_MODEL), lambda i, j: (i, 0)),
                  gt_spec,
                  _resident((1, D_MODEL), lambda i, j: (0, 0))],
        out_specs=pl.BlockSpec((tm, D_MODEL), lambda i, j: (i, 0)),
        compiler_params=_params(2),
        name="ffn",
    )(hf, w1, w2, x1, mod, g_post)


MIX_TM = 512
OUT_TM = 512
FFN_TM = 512


def _row(v):
    return v.reshape(1, -1)


def _layer(x_prompt, x_sample, cache_k, cache_v, c_prompt, c_sample, rel_table, w_ada, b_ada,
           g_pre_mix, w_in, sinks, v_gain, w_s, b_s, g_attn, g_gmlp, w_out, g_post_mix,
           g_pre_ff, w_ff1, w_ff2, g_post_ff):
    n_b, seq, _ = x_prompt.shape
    n_db, t_len, _ = x_sample.shape
    n_tok = n_b * seq
    D = D_MODEL

    w_in_b = w_in.astype(BF16)
    w1_b = w_ff1.astype(BF16)
    w2_b = w_ff2.astype(BF16)
    g_pre_mix, g_attn, g_gmlp = _row(g_pre_mix), _row(g_attn), _row(g_gmlp)
    g_post_mix, g_pre_ff, g_post_ff = _row(g_post_mix), _row(g_pre_ff), _row(g_post_ff)
    v_gain = _row(v_gain)

    bias_p, bias_s, wtril = _prep(rel_table, w_s, t_len)
    mod_s, mod_p = _adaln(c_prompt, c_sample, w_ada, b_ada)
    mod_p = mod_p.reshape(n_b, 1, 6 * D)
    SH_M, SC_M, GT_M, SH_F, SC_F, GT_F = range(6)

    def pmod(chunk, tm, n_axes=1):
        per = seq // tm
        if n_axes == 1:
            return pl.BlockSpec((None, 1, D), lambda i: (i // per, 0, chunk))
        return pl.BlockSpec((None, 1, D), lambda i, j: (i // per, 0, chunk))

    def smod(chunk, n_axes=1):
        if n_axes == 1:
            return pl.BlockSpec((n_db, D), lambda i: (0, chunk))
        return pl.BlockSpec((n_db, D), lambda i, j: (0, chunk))

    pd = _prompt_dist().T
    valid_any = np.tile(((pd >= 0) & (pd < WINDOW)).astype(np.float32), (1, GROUP))
    valid_first = valid_any * (np.arange(2 * BLOCK)[:, None] >= BLOCK)
    valid_t = np.stack([valid_any, valid_first]).astype(np.float32)
    sd = _sample_dist(t_len)
    valid_s = ((sd >= 0) & (sd < WINDOW) & (np.arange(SMP_KEYS)[None, :] < WINDOW + t_len))
    valid_s = np.repeat(valid_s.astype(np.float32), GROUP, axis=0)
    sink2 = sinks.reshape(N_KV_HEADS, GROUP)
    sink_p = jnp.repeat(sink2, BLOCK, axis=1)[:, None, :]
    sink_s = jnp.tile(sink2, (1, t_len))[:, :, None]
    ga3 = g_attn.reshape(N_KV_HEADS, GROUP, HEAD_DIM)
    gain_t = jnp.repeat(ga3.transpose(0, 2, 1).reshape(KV_COLS, GROUP), BLOCK, axis=1)
    g_attn = ga3.transpose(1, 0, 2).reshape(1, ATTN_WIDTH)
    w_out_b = jnp.concatenate(
        [w_out[:ATTN_WIDTH].reshape(N_KV_HEADS, GROUP, HEAD_DIM, D).transpose(1, 0, 2, 3)
         .reshape(ATTN_WIDTH, D).astype(BF16), w_out[ATTN_WIDTH:].astype(BF16)], axis=0)

    xp = x_prompt.reshape(n_tok, D)
    tile = lambda w: pl.BlockSpec((MIX_TM, w), lambda i: (i, 0))
    ttile = lambda w: pl.BlockSpec((w, MIX_TM), lambda i: (0, i))
    qt_p, k_p, v_p, gu_p, gvn_p, vt_p = _mix_in(
        xp, tile(D), pmod(SH_M, MIX_TM), pmod(SC_M, MIX_TM), mod_p, (n_tok // MIX_TM,),
        g_pre_mix, w_in_b, v_gain,
        (jax.ShapeDtypeStruct((Q_COLS, n_tok), BF16),
         jax.ShapeDtypeStruct((n_tok, KV_COLS), F32),
         jax.ShapeDtypeStruct((n_tok, KV_COLS), F32),
         jax.ShapeDtypeStruct((n_tok, GMLP_WIDTH), F32),
         jax.ShapeDtypeStruct((n_tok, GMLP_WIDTH), BF16),
         jax.ShapeDtypeStruct((KV_COLS, n_tok), BF16)),
        (ttile(Q_COLS), tile(KV_COLS), tile(KV_COLS), tile(GMLP_WIDTH), tile(GMLP_WIDTH),
         ttile(KV_COLS)))

    merged_p = _mix_core(qt_p, k_p, vt_p, gu_p, gvn_p, bias_p, jnp.asarray(valid_t), sink_p,
                         wtril, b_s.T, gain_t, g_gmlp, seq)

    otile = lambda w: pl.BlockSpec((OUT_TM, w), lambda i: (i, 0))
    one = lambda i: (0, 0)
    x1_p, hf_p = pl.pallas_call(
        _out_proj_body,
        out_shape=(jax.ShapeDtypeStruct((n_tok, D), F32), jax.ShapeDtypeStruct((n_tok, D), BF16)),
        grid=(n_tok // OUT_TM,),
        in_specs=[otile(D), otile(D), pmod(GT_M, OUT_TM), pmod(SH_F, OUT_TM), pmod(SC_F, OUT_TM),
                  _resident((D, D), one), _resident((1, D), one), _resident((1, D), one)],
        out_specs=(otile(D), otile(D)),
        compiler_params=_params(1),
        name="out_proj",
    )(merged_p, xp, mod_p, mod_p, mod_p, w_out_b, g_post_mix, g_pre_ff)

    y_p = _ffn(hf_p, w1_b, w2_b, x1_p, mod_p, pmod(GT_F, FFN_TM, 2), g_post_ff, FFN_TM)

    xs = x_sample.reshape(n_db, t_len * D)
    lane = lambda w: pl.BlockSpec((n_db, w), lambda t: (0, t))
    q_s, k_s, v_s, gu_s, gvn_s = _mix_in(
        xs, lane(D), smod(SH_M), smod(SC_M), mod_s, (t_len,),
        g_pre_mix, w_in_b, v_gain,
        (jax.ShapeDtypeStruct((n_db, t_len * Q_COLS), F32),
         jax.ShapeDtypeStruct((n_db, t_len * KV_COLS), F32),
         jax.ShapeDtypeStruct((n_db, t_len * KV_COLS), F32),
         jax.ShapeDtypeStruct((n_db, t_len * GMLP_WIDTH), F32),
         jax.ShapeDtypeStruct((n_db, t_len * GMLP_WIDTH), F32)),
        (lane(Q_COLS), lane(KV_COLS), lane(KV_COLS), lane(GMLP_WIDTH), lane(GMLP_WIDTH)))

    q5 = q_s.reshape(n_db, t_len, N_KV_HEADS, GROUP, HEAD_DIM).transpose(0, 2, 1, 3, 4)
    q5 = q5.reshape(n_db, N_KV_HEADS, t_len * GROUP, HEAD_DIM)
    new_k = k_s.reshape(n_db, t_len, KV_COLS)
    new_v = v_s.reshape(n_db, t_len, KV_COLS)
    o5 = _attn_smp(q5, cache_k.reshape(n_db, WINDOW, KV_COLS), cache_v.reshape(n_db, WINDOW, KV_COLS),
                   new_k, new_v, bias_s, jnp.asarray(valid_s), sink_s, t_len)
    attn_s = o5.reshape(n_db, N_KV_HEADS, t_len, GROUP, HEAD_DIM).transpose(0, 2, 3, 1, 4)
    attn_s = attn_s.reshape(n_db, t_len * ATTN_WIDTH)

    smem = pl.BlockSpec(memory_space=pltpu.SMEM)
    w4 = w_s[:, :t_len, :t_len].reshape(-1)
    b4 = b_s[:, :t_len].reshape(-1)
    tmaj = lambda: pl.BlockSpec((None, n_db, D), lambda t: (t, 0, 0))
    x1_s, hf_s = pl.pallas_call(
        functools.partial(_out_proj_smp_body, t_len=t_len),
        out_shape=(jax.ShapeDtypeStruct((t_len, n_db, D), F32),
                   jax.ShapeDtypeStruct((t_len, n_db, D), BF16)),
        grid=(t_len,),
        in_specs=[smem, smem, lane(ATTN_WIDTH), lane(GMLP_WIDTH),
                  _resident((n_db, t_len * GMLP_WIDTH), one),
                  _resident((1, ATTN_WIDTH), one), _resident((1, GMLP_WIDTH), one),
                  lane(D), smod(GT_M), smod(SH_F), smod(SC_F),
                  _resident((D, D), one), _resident((1, D), one), _resident((1, D), one)],
        out_specs=(tmaj(), tmaj()),
        compiler_params=_params(1),
        name="out_proj_smp",
    )(w4, b4, attn_s, gu_s, gvn_s, g_attn, g_gmlp, xs, mod_s, mod_s, mod_s,
      w_out_b, g_post_mix, g_pre_ff)

    n_st = t_len * n_db
    y_s = _ffn(hf_s.reshape(n_st, D), w1_b, w2_b, x1_s.reshape(n_st, D), mod_s,
               smod(GT_F, 2), g_post_ff, n_st)
    y_s = y_s.reshape(t_len, n_db, D).transpose(1, 0, 2)

    cw = min(WINDOW, seq)
    k_p4 = k_p.reshape(n_b, seq, N_KV_HEADS, HEAD_DIM)[:, seq - cw:]
    v_p4 = v_p.reshape(n_b, seq, N_KV_HEADS, HEAD_DIM)[:, seq - cw:]
    return (y_p.reshape(n_b, seq, D), y_s, k_p4, v_p4,
            k_s.reshape(n_db, t_len, N_KV_HEADS, HEAD_DIM),
            v_s.reshape(n_db, t_len, N_KV_HEADS, HEAD_DIM),
            gvn_s.reshape(n_db, t_len, GMLP_HEADS, GMLP_HEAD_DIM))


def kernel(x_prompt, x_sample, cache_k, cache_v, c_prompt, c_sample, rel_bias_table, w_ada, b_ada,
           g_pre_mix, w_in, attn_sinks, gmlp_v_gain, gmlp_w_s, gmlp_b_s, g_attn_out, g_gmlp_out,
           w_out, g_post_mix, g_pre_ff, w_ff1, w_ff2, g_post_ff):
    depth = w_in.shape[0]
    assert depth == 1, "single-layer step"
    outs = _layer(x_prompt, x_sample, cache_k[0], cache_v[0], c_prompt, c_sample, rel_bias_table,
                  w_ada[0], b_ada[0], g_pre_mix[0], w_in[0], attn_sinks[0], gmlp_v_gain[0],
                  gmlp_w_s[0], gmlp_b_s[0], g_attn_out[0], g_gmlp_out[0], w_out[0], g_post_mix[0],
                  g_pre_ff[0], w_ff1[0], w_ff2[0], g_post_ff[0])
    y_p, y_s, k_p, v_p, k_s, v_s, gv_s = outs
    return (y_p, y_s, k_p[None], v_p[None], k_s[None], v_s[None], gv_s[None])
```

```python
import functools
import math

import numpy as np
import jax
import jax.numpy as jnp
from jax import lax
from jax.experimental import pallas as pl
from jax.experimental.pallas import tpu as pltpu

D_MODEL = 2048
HEAD_DIM = 64
ATTN_WIDTH = 1024
N_HEADS = 16
N_KV_HEADS = 2
GROUP = 8
WINDOW = 128
BLOCK = 128
GMLP_WIDTH = 1024
GMLP_HEADS = 8
GMLP_HEAD_DIM = 128
D_FF = 4 * D_MODEL
NUM_BUCKETS = 32
MAX_DISTANCE = 128
EPS = 1e-6
KV_COLS = N_KV_HEADS * HEAD_DIM
Q_COLS = N_HEADS * HEAD_DIM
IN_COLS = Q_COLS + 2 * KV_COLS + 2 * GMLP_WIDTH
ATTN_SCALE = HEAD_DIM ** -0.5
NEG_INF = -1e30

V7X_VMEM_BYTES = 64 * 1024 * 1024
VMEM_LIMIT_BYTES = 56 * 1024 * 1024

BF16 = jnp.bfloat16
F32 = jnp.float32


def _params(n_axes, vmem=VMEM_LIMIT_BYTES):
    return pltpu.CompilerParams(
        dimension_semantics=("arbitrary",) * n_axes, vmem_limit_bytes=vmem)


def _resident(shape, index_map):
    return pl.BlockSpec(shape, index_map, pipeline_mode=pl.Buffered(1))


def _rms(x, gain):
    return x * lax.rsqrt(jnp.mean(x * x, axis=-1, keepdims=True) + EPS) * gain


def _rows(m, n_rows):
    r = m.shape[0]
    if r == 1 or r == n_rows:
        return m
    return jnp.concatenate([m] * (n_rows // r), axis=0)


def _dot(a, b):
    return jnp.dot(a, b, preferred_element_type=F32)


def _dot_nt(a, b):
    return lax.dot_general(a, b, (((1,), (1,)), ((), ())), preferred_element_type=F32)


def _t5_bucket(dist):
    n = jnp.maximum(dist, 0)
    max_exact = NUM_BUCKETS // 2
    nf = jnp.maximum(n, 1).astype(jnp.float32)
    large = max_exact + (jnp.log(nf / max_exact) / math.log(MAX_DISTANCE / max_exact)
                         * (NUM_BUCKETS - max_exact)).astype(jnp.int32)
    large = jnp.minimum(large, NUM_BUCKETS - 1)
    return jnp.where(n < max_exact, n, large)


SMP_KEYS = 2 * WINDOW


def _prompt_dist():
    qi = np.arange(BLOCK)[:, None]
    kj = np.arange(2 * BLOCK)[None, :]
    return BLOCK + qi - kj


def _sample_rows(t_len):
    r = np.arange(N_KV_HEADS * t_len * GROUP)
    return r // (t_len * GROUP), (r // GROUP) % t_len, r % GROUP


def _sample_dist(t_len):
    _, t, _ = _sample_rows(t_len)
    return WINDOW + t[:, None] - np.arange(SMP_KEYS)[None, :]


def _prep_body(tab_ref, tabr_ref, bkt_p_ref, bkt_s_ref, ws_ref,
               bias_p_ref, bias_s_ref, wtril_ref):
    bkt_p = bkt_p_ref[...]
    for kh in range(N_KV_HEADS):
        for g in range(GROUP):
            h = kh * GROUP + g
            acc = jnp.zeros((2 * BLOCK, BLOCK), F32)
            for b in range(NUM_BUCKETS):
                acc = jnp.where(bkt_p == b, tab_ref[b, h], acc)
            bias_p_ref[kh, :, g * BLOCK:(g + 1) * BLOCK] = acc
    bkt_s = bkt_s_ref[...]
    tabr = tabr_ref[...]
    acc = jnp.zeros(bkt_s.shape, F32)
    for b in range(NUM_BUCKETS):
        acc = jnp.where(bkt_s == b, tabr[:, b:b + 1], acc)
    bias_s_ref[...] = acc
    ii = lax.broadcasted_iota(jnp.int32, (BLOCK, BLOCK), 0)
    jj = lax.broadcasted_iota(jnp.int32, (BLOCK, BLOCK), 1)
    for h in range(GMLP_HEADS):
        wtril_ref[h] = jnp.where(jj <= ii, ws_ref[h], 0.0).astype(BF16)


def _prep(rel_table, w_s, t_len):
    bkt_p = _t5_bucket(jnp.asarray(_prompt_dist().T, jnp.int32))
    bkt_s = _t5_bucket(jnp.asarray(_sample_dist(t_len), jnp.int32))
    n_rows = bkt_s.shape[0]
    tab_rows = jnp.broadcast_to(rel_table.T.reshape(N_KV_HEADS, 1, GROUP, NUM_BUCKETS),
                                (N_KV_HEADS, t_len, GROUP, NUM_BUCKETS)).reshape(n_rows, NUM_BUCKETS)
    smem = pl.BlockSpec(memory_space=pltpu.SMEM)
    vmem = pl.BlockSpec(memory_space=pltpu.VMEM)
    return pl.pallas_call(
        _prep_body,
        out_shape=(jax.ShapeDtypeStruct((N_KV_HEADS, 2 * BLOCK, GROUP * BLOCK), F32),
                   jax.ShapeDtypeStruct((n_rows, SMP_KEYS), F32),
                   jax.ShapeDtypeStruct((GMLP_HEADS, BLOCK, BLOCK), BF16)),
        in_specs=[smem, vmem, vmem, vmem, vmem],
        out_specs=(vmem, vmem, vmem),
        compiler_params=pltpu.CompilerParams(vmem_limit_bytes=32 * 1024 * 1024),
        name="prep",
    )(rel_table, tab_rows, bkt_p, bkt_s, w_s)


ADA_TN = 1024


def _adaln_body(c_ref, w_ref, b_ref, os_ref, op_ref, *, n_s, n_p):
    c = c_ref[...]
    s = (c * jax.nn.sigmoid(c)).astype(BF16)
    r = _dot(s, w_ref[...].astype(BF16)) + b_ref[...]
    os_ref[...] = r[:n_s]
    op_ref[...] = r[n_s:n_s + n_p]


def _adaln(c_prompt, c_sample, w_ada, b_ada):
    n_p, n_s = c_prompt.shape[0], c_sample.shape[0]
    pad = (-(n_p + n_s)) % 16
    c_all = jnp.concatenate([c_sample, c_prompt, jnp.zeros((pad, D_MODEL), F32)], axis=0)
    n_all = c_all.shape[0]
    n_out = w_ada.shape[1]
    return pl.pallas_call(
        functools.partial(_adaln_body, n_s=n_s, n_p=n_p),
        out_shape=(jax.ShapeDtypeStruct((n_s, n_out), F32),
                   jax.ShapeDtypeStruct((n_p, n_out), F32)),
        grid=(n_out // ADA_TN,),
        in_specs=[_resident((n_all, D_MODEL), lambda j: (0, 0)),
                  pl.BlockSpec((D_MODEL, ADA_TN), lambda j: (0, j)),
                  pl.BlockSpec((1, ADA_TN), lambda j: (0, j))],
        out_specs=(pl.BlockSpec((n_s, ADA_TN), lambda j: (0, j)),
                   pl.BlockSpec((n_p, ADA_TN), lambda j: (0, j))),
        compiler_params=_params(1, 40 * 1024 * 1024),
        name="adaln",
    )(c_all, w_ada, b_ada.reshape(1, n_out))


def _mix_in_body(x_ref, sh_ref, sc_ref, g_ref, w_ref, vg_ref,
                 q_ref, k_ref, v_ref, gu_ref, gv_ref, *kt_ref, transposed):
    x = x_ref[...]
    tm = x.shape[0]
    h = _rms(x, g_ref[...]) * (1.0 + _rows(sc_ref[...], tm)) + _rows(sh_ref[...], tm)
    h = h.astype(BF16)
    c0, c1, c2, c3 = Q_COLS, Q_COLS + KV_COLS, Q_COLS + 2 * KV_COLS, Q_COLS + 2 * KV_COLS + GMLP_WIDTH
    q = _dot(h, w_ref[:, :c0]) * ATTN_SCALE
    kv = _dot(h, w_ref[:, c0:c2])
    k_ref[...] = kv[:, :KV_COLS].astype(k_ref.dtype)
    if transposed:
        q_ref[...] = q.T.astype(q_ref.dtype)
        kt_ref[0][...] = kv[:, :KV_COLS].T
        v_ref[...] = kv[:, KV_COLS:].T
    else:
        q_ref[...] = q.astype(q_ref.dtype)
        v_ref[...] = kv[:, KV_COLS:]
    gu_ref[...] = jax.nn.gelu(_dot(h, w_ref[:, c2:c3])).astype(gu_ref.dtype)
    gv = jax.nn.gelu(_dot(h, w_ref[:, c3:]))
    vg = vg_ref[...]
    for hd in range(GMLP_HEADS):
        sl = slice(hd * GMLP_HEAD_DIM, (hd + 1) * GMLP_HEAD_DIM)
        gv_ref[:, sl] = _rms(gv[:, sl], vg[:, sl]).astype(gv_ref.dtype)


def _mix_in(x, x_spec, sh_spec, sc_spec, mod, grid, g_pre, w_in, v_gain, outs, out_specs):
    return pl.pallas_call(
        functools.partial(_mix_in_body, transposed=len(outs) == 6),
        out_shape=outs,
        grid=grid,
        in_specs=[x_spec, sh_spec, sc_spec,
                  _resident((1, D_MODEL), lambda i: (0, 0)),
                  _resident((D_MODEL, IN_COLS), lambda i: (0, 0)),
                  _resident((1, GMLP_WIDTH), lambda i: (0, 0))],
        out_specs=out_specs,
        compiler_params=_params(1),
        name="mix_in",
    )(x, mod, mod, g_pre, w_in, v_gain)


def _mix_core_body(qt_ref, kc_ref, kp_ref, vtc_ref, vtp_ref, gu_ref, gv_ref,
                   bias_ref, valid_ref, sink_ref, wtril_ref, bs_ref, gat_ref, gg_ref,
                   o_ref, *, blocks_per_seq):
    first = (pl.program_id(0) % blocks_per_seq == 0).astype(jnp.int32)
    valid = valid_ref[first] > 0.5
    k = jnp.concatenate([kp_ref[...], kc_ref[...]], axis=0)
    vt = jnp.concatenate([vtp_ref[...], vtc_ref[...]], axis=1).astype(BF16)
    zeros = jnp.zeros((HEAD_DIM, GROUP * BLOCK), BF16)
    outs = []
    for kh in range(N_KV_HEADS):
        base = kh * GROUP * HEAD_DIM
        qs = jnp.concatenate(
            [qt_ref[base + g * HEAD_DIM: base + (g + 1) * HEAD_DIM, :] for g in range(GROUP)], axis=1)
        qp = jnp.concatenate([qs, zeros] if kh == 0 else [zeros, qs], axis=0)
        s = jnp.where(valid, _dot(k, qp) + bias_ref[kh], NEG_INF)
        sink = sink_ref[kh]
        m = jnp.maximum(jnp.max(s, axis=0, keepdims=True), sink)
        p = jnp.exp(s - m)
        den = jnp.sum(p, axis=0, keepdims=True) + jnp.exp(sink - m)
        outs.append(_dot(vt[kh * HEAD_DIM:(kh + 1) * HEAD_DIM, :], p.astype(BF16)) / den)
    ot = jnp.concatenate(outs, axis=0)
    sq = jnp.sum(ot * ot, axis=0, keepdims=True)
    tot = sq[:, :BLOCK]
    for g in range(1, GROUP):
        tot = tot + sq[:, g * BLOCK:(g + 1) * BLOCK]
    r = lax.rsqrt(tot / ATTN_WIDTH + EPS)
    a = ot * jnp.concatenate([r] * GROUP, axis=1) * gat_ref[...]
    for g in range(GROUP):
        o_ref[:, g * BLOCK:(g + 1) * BLOCK] = a[:, g * BLOCK:(g + 1) * BLOCK].T.astype(o_ref.dtype)
    bs = bs_ref[...]
    gated = []
    for hd in range(GMLP_HEADS):
        sl = slice(hd * GMLP_HEAD_DIM, (hd + 1) * GMLP_HEAD_DIM)
        mixed = _dot(wtril_ref[hd], gv_ref[:, sl]) + bs[:, hd:hd + 1]
        gated.append(gu_ref[:, sl] * mixed)
    gm = jnp.concatenate(gated, axis=-1)
    o_ref[:, ATTN_WIDTH:] = _rms(gm, gg_ref[...]).astype(o_ref.dtype)


def _mix_core(qt, k, vt, gu, gvn, bias_t, valid_t, sink_row, wtril, bs_t, gain_t, g_gmlp, seq):
    n_tok = k.shape[0]
    nblk = n_tok // BLOCK
    bps = seq // BLOCK
    cur = lambda i: (i, 0)
    prev = lambda i: (jnp.maximum(i - 1, 0), 0)
    cur_t = lambda i: (0, i)
    prev_t = lambda i: (0, jnp.maximum(i - 1, 0))
    full2 = lambda i: (0, 0)
    full3 = lambda i: (0, 0, 0)
    return pl.pallas_call(
        functools.partial(_mix_core_body, blocks_per_seq=bps),
        out_shape=jax.ShapeDtypeStruct((n_tok, D_MODEL), BF16),
        grid=(nblk,),
        in_specs=[pl.BlockSpec((Q_COLS, BLOCK), cur_t),
                  pl.BlockSpec((BLOCK, KV_COLS), cur), pl.BlockSpec((BLOCK, KV_COLS), prev),
                  pl.BlockSpec((KV_COLS, BLOCK), cur_t), pl.BlockSpec((KV_COLS, BLOCK), prev_t),
                  pl.BlockSpec((BLOCK, GMLP_WIDTH), cur), pl.BlockSpec((BLOCK, GMLP_WIDTH), cur),
                  _resident(bias_t.shape, full3), _resident(valid_t.shape, full3),
                  _resident(sink_row.shape, full3), _resident(wtril.shape, full3),
                  _resident(bs_t.shape, full2),
                  _resident(gain_t.shape, full2), _resident((1, GMLP_WIDTH), full2)],
        out_specs=pl.BlockSpec((BLOCK, D_MODEL), cur),
        compiler_params=_params(1),
        name="mix_core",
    )(qt, k, k, vt, vt, gu, gvn, bias_t, valid_t, sink_row, wtril, bs_t, gain_t, g_gmlp)


SMP_BB = 16


def _attn_smp_body(q_ref, ckt_ref, cvt_ref, nk_ref, nv_ref, bias_ref, valid_ref, sink_ref,
                   o_ref, *, t_len):
    zpad = jnp.zeros((WINDOW - t_len, KV_COLS), F32)
    tiles = []
    for b in range(SMP_BB):
        q = q_ref[b]
        s_cache = _dot(q, ckt_ref[b].astype(BF16))
        kn = jnp.concatenate([nk_ref[b], zpad], axis=0).astype(BF16)
        tiles.append(jnp.concatenate([s_cache, _dot_nt(q, kn)], axis=1)[None])
    s = jnp.concatenate(tiles, axis=0)
    s = jnp.where(valid_ref[...] > 0.5, s + bias_ref[...], NEG_INF)
    sink = sink_ref[...]
    m = jnp.maximum(jnp.max(s, axis=-1, keepdims=True), sink)
    p = jnp.exp(s - m)
    den = jnp.sum(p, axis=-1, keepdims=True) + jnp.exp(sink - m)
    p = p.astype(BF16)
    for b in range(SMP_BB):
        vn = jnp.concatenate([nv_ref[b], zpad], axis=0).astype(BF16)
        o = _dot_nt(p[b, :, :WINDOW], cvt_ref[b].astype(BF16)) + _dot(p[b, :, WINDOW:], vn)
        o_ref[b] = o / den[b]


def _attn_smp(q6, cache_kt, cache_vt, new_k, new_v, bias_s, valid_s, sink_s, t_len):
    n_b, rows, _ = q6.shape
    b3 = lambda i: (i, 0, 0)
    one = lambda i: (0, 0)
    return pl.pallas_call(
        functools.partial(_attn_smp_body, t_len=t_len),
        out_shape=jax.ShapeDtypeStruct((n_b, rows, KV_COLS), F32),
        grid=(n_b // SMP_BB,),
        in_specs=[pl.BlockSpec((SMP_BB, rows, KV_COLS), b3),
                  pl.BlockSpec((SMP_BB, KV_COLS, WINDOW), b3),
                  pl.BlockSpec((SMP_BB, KV_COLS, WINDOW), b3),
                  pl.BlockSpec((SMP_BB, t_len, KV_COLS), b3),
                  pl.BlockSpec((SMP_BB, t_len, KV_COLS), b3),
                  _resident(bias_s.shape, one), _resident(valid_s.shape, one),
                  _resident(sink_s.shape, one)],
        out_specs=pl.BlockSpec((SMP_BB, rows, KV_COLS), b3),
        compiler_params=_params(1, 32 * 1024 * 1024),
        name="attn_smp",
    )(q6, cache_kt, cache_vt, new_k, new_v, bias_s, valid_s, sink_s)


def _out_proj_tail(merged, x_ref, gt_ref, sh_ref, sc_ref, w_ref, gpm_ref, gpf_ref, x1_ref, hf_ref):
    tm = merged.shape[0]
    o = _dot(merged, w_ref[...])
    x1 = x_ref[...] + _rows(gt_ref[...], tm) * _rms(o, gpm_ref[...])
    x1_ref[...] = x1
    hf = _rms(x1, gpf_ref[...]) * (1.0 + _rows(sc_ref[...], tm)) + _rows(sh_ref[...], tm)
    hf_ref[...] = hf.astype(hf_ref.dtype)


def _out_proj_body(m_ref, *rest):
    _out_proj_tail(m_ref[...], *rest)


def _out_proj_smp_body(w4_ref, b4_ref, a_ref, gu_ref, gv_ref, ga_ref, gg_ref, *rest, t_len):
    i = pl.program_id(0)
    gated = []
    for hd in range(GMLP_HEADS):
        acc = jnp.zeros((a_ref.shape[0], GMLP_HEAD_DIM), F32)
        for j in range(t_len):
            w = jnp.where(j <= i, w4_ref[(hd * t_len + i) * t_len + j], 0.0)
            lo = j * GMLP_WIDTH + hd * GMLP_HEAD_DIM
            acc = acc + w * gv_ref[:, lo:lo + GMLP_HEAD_DIM]
        mixed = acc + b4_ref[hd * t_len + i]
        gated.append(gu_ref[:, hd * GMLP_HEAD_DIM:(hd + 1) * GMLP_HEAD_DIM] * mixed)
    gm = jnp.concatenate(gated, axis=-1)
    merged = jnp.concatenate([_rms(a_ref[...], ga_ref[...]), _rms(gm, gg_ref[...])], axis=-1)
    _out_proj_tail(merged.astype(BF16), *rest)


FFN_TF = 1024


def _ffn_body(h_ref, w1_ref, w2_ref, x1_ref, gt_ref, g_ref, o_ref):
    j = pl.program_id(1)

    @pl.when(j == 0)
    def _():
        o_ref[...] = jnp.zeros_like(o_ref)

    a = jnp.maximum(_dot(h_ref[...], w1_ref[...]), 0.0)
    o_ref[...] += _dot((a * a).astype(BF16), w2_ref[...])

    @pl.when(j == pl.num_programs(1) - 1)
    def _():
        tm = o_ref.shape[0]
        o_ref[...] = x1_ref[...] + _rows(gt_ref[...], tm) * _rms(o_ref[...], g_ref[...])


def _ffn(hf, w1, w2, x1, mod, gt_spec, g_post, tm):
    n_tok = hf.shape[0]
    return pl.pallas_call(
        _ffn_body,
        out_shape=jax.ShapeDtypeStruct((n_tok, D_MODEL), F32),
        grid=(n_tok // tm, D_FF // FFN_TF),
        in_specs=[pl.BlockSpec((tm, D_MODEL), lambda i, j: (i, 0)),
                  pl.BlockSpec((D_MODEL, FFN_TF), lambda i, j: (0, j)),
                  pl.BlockSpec((FFN_TF, D_MODEL), lambda i, j: (j, 0)),
                  pl.BlockSpec((tm, D_MODEL), lambda i, j: (i, 0)),
                  gt_spec,
                  _resident((1, D_MODEL), lambda i, j: (0, 0))],
        out_specs=pl.BlockSpec((tm, D_MODEL), lambda i, j: (i, 0)),
        compiler_params=_params(2),
        name="ffn",
    )(hf, w1, w2, x1, mod, g_post)


MIX_TM = 512
OUT_TM = 512
FFN_TM = 512


def _row(v):
    return v.reshape(1, -1)


def _layer(x_prompt, x_sample, cache_k, cache_v, c_prompt, c_sample, rel_table, w_ada, b_ada,
           g_pre_mix, w_in, sinks, v_gain, w_s, b_s, g_attn, g_gmlp, w_out, g_post_mix,
           g_pre_ff, w_ff1, w_ff2, g_post_ff):
    n_b, seq, _ = x_prompt.shape
    n_db, t_len, _ = x_sample.shape
    n_tok = n_b * seq
    D = D_MODEL

    w_in_b = w_in.astype(BF16)
    w1_b = w_ff1.astype(BF16)
    w2_b = w_ff2.astype(BF16)
    g_pre_mix, g_attn, g_gmlp = _row(g_pre_mix), _row(g_attn), _row(g_gmlp)
    g_post_mix, g_pre_ff, g_post_ff = _row(g_post_mix), _row(g_pre_ff), _row(g_post_ff)
    v_gain = _row(v_gain)

    bias_p, bias_s, wtril = _prep(rel_table, w_s, t_len)
    mod_s, mod_p = _adaln(c_prompt, c_sample, w_ada, b_ada)
    mod_p = mod_p.reshape(n_b, 1, 6 * D)
    SH_M, SC_M, GT_M, SH_F, SC_F, GT_F = range(6)

    def pmod(chunk, tm, n_axes=1):
        per = seq // tm
        if n_axes == 1:
            return pl.BlockSpec((None, 1, D), lambda i: (i // per, 0, chunk))
        return pl.BlockSpec((None, 1, D), lambda i, j: (i // per, 0, chunk))

    def smod(chunk, n_axes=1):
        if n_axes == 1:
            return pl.BlockSpec((n_db, D), lambda i: (0, chunk))
        return pl.BlockSpec((n_db, D), lambda i, j: (0, chunk))

    pd = _prompt_dist().T
    valid_any = np.tile(((pd >= 0) & (pd < WINDOW)).astype(np.float32), (1, GROUP))
    valid_first = valid_any * (np.arange(2 * BLOCK)[:, None] >= BLOCK)
    valid_t = np.stack([valid_any, valid_first]).astype(np.float32)
    sd = _sample_dist(t_len)
    valid_s = ((sd >= 0) & (sd < WINDOW) & (np.arange(SMP_KEYS)[None, :] < WINDOW + t_len))
    valid_s = valid_s.astype(np.float32)
    sink2 = sinks.reshape(N_KV_HEADS, GROUP)
    sink_p = jnp.repeat(sink2, BLOCK, axis=1)[:, None, :]
    sink_s = jnp.tile(sink2, (1, t_len)).reshape(-1, 1)
    ga3 = g_attn.reshape(N_KV_HEADS, GROUP, HEAD_DIM)
    gain_t = jnp.repeat(ga3.transpose(0, 2, 1).reshape(KV_COLS, GROUP), BLOCK, axis=1)
    g_attn = ga3.transpose(1, 0, 2).reshape(1, ATTN_WIDTH)
    w_out_b = jnp.concatenate(
        [w_out[:ATTN_WIDTH].reshape(N_KV_HEADS, GROUP, HEAD_DIM, D).transpose(1, 0, 2, 3)
         .reshape(ATTN_WIDTH, D).astype(BF16), w_out[ATTN_WIDTH:].astype(BF16)], axis=0)

    xp = x_prompt.reshape(n_tok, D)
    tile = lambda w: pl.BlockSpec((MIX_TM, w), lambda i: (i, 0))
    ttile = lambda w: pl.BlockSpec((w, MIX_TM), lambda i: (0, i))
    qt_p, k_p, vt_p, gu_p, gvn_p, kt_p = _mix_in(
        xp, tile(D), pmod(SH_M, MIX_TM), pmod(SC_M, MIX_TM), mod_p, (n_tok // MIX_TM,),
        g_pre_mix, w_in_b, v_gain,
        (jax.ShapeDtypeStruct((Q_COLS, n_tok), BF16),
         jax.ShapeDtypeStruct((n_tok, KV_COLS), BF16),
         jax.ShapeDtypeStruct((KV_COLS, n_tok), F32),
         jax.ShapeDtypeStruct((n_tok, GMLP_WIDTH), F32),
         jax.ShapeDtypeStruct((n_tok, GMLP_WIDTH), BF16),
         jax.ShapeDtypeStruct((KV_COLS, n_tok), F32)),
        (ttile(Q_COLS), tile(KV_COLS), ttile(KV_COLS), tile(GMLP_WIDTH), tile(GMLP_WIDTH),
         ttile(KV_COLS)))

    merged_p = _mix_core(qt_p, k_p, vt_p, gu_p, gvn_p, bias_p, jnp.asarray(valid_t), sink_p,
                         wtril, b_s.T, gain_t, g_gmlp, seq)

    otile = lambda w: pl.BlockSpec((OUT_TM, w), lambda i: (i, 0))
    one = lambda i: (0, 0)
    x1_p, hf_p = pl.pallas_call(
        _out_proj_body,
        out_shape=(jax.ShapeDtypeStruct((n_tok, D), F32), jax.ShapeDtypeStruct((n_tok, D), BF16)),
        grid=(n_tok // OUT_TM,),
        in_specs=[otile(D), otile(D), pmod(GT_M, OUT_TM), pmod(SH_F, OUT_TM), pmod(SC_F, OUT_TM),
                  _resident((D, D), one), _resident((1, D), one), _resident((1, D), one)],
        out_specs=(otile(D), otile(D)),
        compiler_params=_params(1),
        name="out_proj",
    )(merged_p, xp, mod_p, mod_p, mod_p, w_out_b, g_post_mix, g_pre_ff)

    y_p = _ffn(hf_p, w1_b, w2_b, x1_p, mod_p, pmod(GT_F, FFN_TM, 2), g_post_ff, FFN_TM)

    xs = x_sample.reshape(n_db, t_len * D)
    lane = lambda w: pl.BlockSpec((n_db, w), lambda t: (0, t))
    q_s, k_s, v_s, gu_s, gvn_s = _mix_in(
        xs, lane(D), smod(SH_M), smod(SC_M), mod_s, (t_len,),
        g_pre_mix, w_in_b, v_gain,
        (jax.ShapeDtypeStruct((n_db, t_len * Q_COLS), F32),
         jax.ShapeDtypeStruct((n_db, t_len * KV_COLS), F32),
         jax.ShapeDtypeStruct((n_db, t_len * KV_COLS), F32),
         jax.ShapeDtypeStruct((n_db, t_len * GMLP_WIDTH), F32),
         jax.ShapeDtypeStruct((n_db, t_len * GMLP_WIDTH), F32)),
        (lane(Q_COLS), lane(KV_COLS), lane(KV_COLS), lane(GMLP_WIDTH), lane(GMLP_WIDTH)))

    q5 = q_s.reshape(n_db, t_len, N_KV_HEADS, GROUP, HEAD_DIM).transpose(0, 2, 1, 3, 4)
    q5 = q5.reshape(n_db, N_KV_HEADS, t_len * GROUP, HEAD_DIM).astype(BF16)
    z5 = jnp.zeros_like(q5[:, 0])
    q6 = jnp.concatenate([jnp.concatenate([q5[:, 0], z5], axis=-1),
                          jnp.concatenate([z5, q5[:, 1]], axis=-1)], axis=1)
    new_k = k_s.reshape(n_db, t_len, KV_COLS)
    new_v = v_s.reshape(n_db, t_len, KV_COLS)
    cache_kt = cache_k.transpose(0, 2, 3, 1).reshape(n_db, KV_COLS, WINDOW)
    cache_vt = cache_v.transpose(0, 2, 3, 1).reshape(n_db, KV_COLS, WINDOW)
    o6 = _attn_smp(q6, cache_kt, cache_vt, new_k, new_v, bias_s, jnp.asarray(valid_s), sink_s, t_len)
    o6 = o6.reshape(n_db, N_KV_HEADS, t_len, GROUP, N_KV_HEADS, HEAD_DIM)
    attn_s = jnp.stack([o6[:, kh, :, :, kh, :] for kh in range(N_KV_HEADS)], axis=3)
    attn_s = attn_s.reshape(n_db, t_len * ATTN_WIDTH)

    smem = pl.BlockSpec(memory_space=pltpu.SMEM)
    w4 = w_s[:, :t_len, :t_len].reshape(-1)
    b4 = b_s[:, :t_len].reshape(-1)
    tmaj = lambda: pl.BlockSpec((None, n_db, D), lambda t: (t, 0, 0))
    x1_s, hf_s = pl.pallas_call(
        functools.partial(_out_proj_smp_body, t_len=t_len),
        out_shape=(jax.ShapeDtypeStruct((t_len, n_db, D), F32),
                   jax.ShapeDtypeStruct((t_len, n_db, D), BF16)),
        grid=(t_len,),
        in_specs=[smem, smem, lane(ATTN_WIDTH), lane(GMLP_WIDTH),
                  _resident((n_db, t_len * GMLP_WIDTH), one),
                  _resident((1, ATTN_WIDTH), one), _resident((1, GMLP_WIDTH), one),
                  lane(D), smod(GT_M), smod(SH_F), smod(SC_F),
                  _resident((D, D), one), _resident((1, D), one), _resident((1, D), one)],
        out_specs=(tmaj(), tmaj()),
        compiler_params=_params(1),
        name="out_proj_smp",
    )(w4, b4, attn_s, gu_s, gvn_s, g_attn, g_gmlp, xs, mod_s, mod_s, mod_s,
      w_out_b, g_post_mix, g_pre_ff)

    n_st = t_len * n_db
    y_s = _ffn(hf_s.reshape(n_st, D), w1_b, w2_b, x1_s.reshape(n_st, D), mod_s,
               smod(GT_F, 2), g_post_ff, n_st)
    y_s = y_s.reshape(t_len, n_db, D).transpose(1, 0, 2)

    cw = min(WINDOW, seq)
    last = lambda t: (t.reshape(N_KV_HEADS, HEAD_DIM, n_b, seq)[:, :, :, seq - cw:]
                      .transpose(2, 3, 0, 1))
    k_p4, v_p4 = last(kt_p), last(vt_p)
    return (y_p.reshape(n_b, seq, D), y_s, k_p4, v_p4,
            k_s.reshape(n_db, t_len, N_KV_HEADS, HEAD_DIM),
            v_s.reshape(n_db, t_len, N_KV_HEADS, HEAD_DIM),
            gvn_s.reshape(n_db, t_len, GMLP_HEADS, GMLP_HEAD_DIM))


def kernel(x_prompt, x_sample, cache_k, cache_v, c_prompt, c_sample, rel_bias_table, w_ada, b_ada,
           g_pre_mix, w_in, attn_sinks, gmlp_v_gain, gmlp_w_s, gmlp_b_s, g_attn_out, g_gmlp_out,
           w_out, g_post_mix, g_pre_ff, w_ff1, w_ff2, g_post_ff):
    depth = w_in.shape[0]
    assert depth == 1, "single-layer step"
    outs = _layer(x_prompt, x_sample, cache_k[0], cache_v[0], c_prompt, c_sample, rel_bias_table,
                  w_ada[0], b_ada[0], g_pre_mix[0], w_in[0], attn_sinks[0], gmlp_v_gain[0],
                  gmlp_w_s[0], gmlp_b_s[0], g_attn_out[0], g_gmlp_out[0], w_out[0], g_post_mix[0],
                  g_pre_ff[0], w_ff1[0], w_ff2[0], g_post_ff[0])
    y_p, y_s, k_p, v_p, k_s, v_s, gv_s = outs
    return (y_p, y_s, k_p[None], v_p[None], k_s[None], v_s[None], gv_s[None])
```

```python
import functools
import math

import numpy as np
import jax
import jax.numpy as jnp
from jax import lax
from jax.experimental import pallas as pl
from jax.experimental.pallas import tpu as pltpu

D_MODEL = 2048
HEAD_DIM = 64
ATTN_WIDTH = 1024
N_HEADS = 16
N_KV_HEADS = 2
GROUP = 8
WINDOW = 128
BLOCK = 128
GMLP_WIDTH = 1024
GMLP_HEADS = 8
GMLP_HEAD_DIM = 128
D_FF = 4 * D_MODEL
NUM_BUCKETS = 32
MAX_DISTANCE = 128
EPS = 1e-6
KV_COLS = N_KV_HEADS * HEAD_DIM
Q_COLS = N_HEADS * HEAD_DIM
IN_COLS = Q_COLS + 2 * KV_COLS + 2 * GMLP_WIDTH
ATTN_SCALE = HEAD_DIM ** -0.5
NEG_INF = -1e30

V7X_VMEM_BYTES = 64 * 1024 * 1024
VMEM_LIMIT_BYTES = 56 * 1024 * 1024

BF16 = jnp.bfloat16
F32 = jnp.float32


def _params(n_axes, vmem=VMEM_LIMIT_BYTES):
    return pltpu.CompilerParams(
        dimension_semantics=("arbitrary",) * n_axes, vmem_limit_bytes=vmem)


def _resident(shape, index_map):
    return pl.BlockSpec(shape, index_map, pipeline_mode=pl.Buffered(1))


def _rms(x, gain):
    return x * lax.rsqrt(jnp.mean(x * x, axis=-1, keepdims=True) + EPS) * gain


def _rows(m, n_rows):
    r = m.shape[0]
    if r == 1 or r == n_rows:
        return m
    return jnp.concatenate([m] * (n_rows // r), axis=0)


def _dot(a, b):
    return jnp.dot(a, b, preferred_element_type=F32)


def _dot_nt(a, b):
    return lax.dot_general(a, b, (((1,), (1,)), ((), ())), preferred_element_type=F32)


def _t5_bucket(dist):
    n = jnp.maximum(dist, 0)
    max_exact = NUM_BUCKETS // 2
    nf = jnp.maximum(n, 1).astype(jnp.float32)
    large = max_exact + (jnp.log(nf / max_exact) / math.log(MAX_DISTANCE / max_exact)
                         * (NUM_BUCKETS - max_exact)).astype(jnp.int32)
    large = jnp.minimum(large, NUM_BUCKETS - 1)
    return jnp.where(n < max_exact, n, large)


SMP_KEYS = 2 * WINDOW


def _prompt_dist():
    qi = np.arange(BLOCK)[:, None]
    kj = np.arange(2 * BLOCK)[None, :]
    return BLOCK + qi - kj


def _sample_rows(t_len):
    r = np.arange(N_KV_HEADS * t_len * GROUP)
    return r // (t_len * GROUP), (r // GROUP) % t_len, r % GROUP


def _sample_dist(t_len):
    _, t, _ = _sample_rows(t_len)
    return WINDOW + t[:, None] - np.arange(SMP_KEYS)[None, :]


def _prep_body(tab_ref, tabr_ref, bkt_p_ref, bkt_s_ref, ws_ref,
               bias_p_ref, bias_s_ref, wtril_ref):
    bkt_p = bkt_p_ref[...]
    for kh in range(N_KV_HEADS):
        for g in range(GROUP):
            h = kh * GROUP + g
            acc = jnp.zeros((2 * BLOCK, BLOCK), F32)
            for b in range(NUM_BUCKETS):
                acc = jnp.where(bkt_p == b, tab_ref[b, h], acc)
            bias_p_ref[kh, :, g * BLOCK:(g + 1) * BLOCK] = acc
    bkt_s = bkt_s_ref[...]
    tabr = tabr_ref[...]
    acc = jnp.zeros(bkt_s.shape, F32)
    for b in range(NUM_BUCKETS):
        acc = jnp.where(bkt_s == b, tabr[:, b:b + 1], acc)
    bias_s_ref[...] = acc
    ii = lax.broadcasted_iota(jnp.int32, (BLOCK, BLOCK), 0)
    jj = lax.broadcasted_iota(jnp.int32, (BLOCK, BLOCK), 1)
    for h in range(GMLP_HEADS):
        wtril_ref[h] = jnp.where(jj <= ii, ws_ref[h], 0.0).astype(BF16)


def _prep(rel_table, w_s, t_len):
    bkt_p = _t5_bucket(jnp.asarray(_prompt_dist().T, jnp.int32))
    bkt_s = _t5_bucket(jnp.asarray(_sample_dist(t_len), jnp.int32))
    n_rows = bkt_s.shape[0]
    tab_rows = jnp.broadcast_to(rel_table.T.reshape(N_KV_HEADS, 1, GROUP, NUM_BUCKETS),
                                (N_KV_HEADS, t_len, GROUP, NUM_BUCKETS)).reshape(n_rows, NUM_BUCKETS)
    smem = pl.BlockSpec(memory_space=pltpu.SMEM)
    vmem = pl.BlockSpec(memory_space=pltpu.VMEM)
    return pl.pallas_call(
        _prep_body,
        out_shape=(jax.ShapeDtypeStruct((N_KV_HEADS, 2 * BLOCK, GROUP * BLOCK), F32),
                   jax.ShapeDtypeStruct((n_rows, SMP_KEYS), F32),
                   jax.ShapeDtypeStruct((GMLP_HEADS, BLOCK, BLOCK), BF16)),
        in_specs=[smem, vmem, vmem, vmem, vmem],
        out_specs=(vmem, vmem, vmem),
        compiler_params=pltpu.CompilerParams(vmem_limit_bytes=32 * 1024 * 1024),
        name="prep",
    )(rel_table, tab_rows, bkt_p, bkt_s, w_s)


ADA_TN = 1024


def _adaln_body(c_ref, w_ref, b_ref, os_ref, op_ref, *, n_s, n_p):
    c = c_ref[...]
    s = (c * jax.nn.sigmoid(c)).astype(BF16)
    r = _dot(s, w_ref[...].astype(BF16)) + b_ref[...]
    os_ref[...] = r[:n_s]
    op_ref[...] = r[n_s:n_s + n_p]


def _adaln(c_prompt, c_sample, w_ada, b_ada):
    n_p, n_s = c_prompt.shape[0], c_sample.shape[0]
    pad = (-(n_p + n_s)) % 16
    c_all = jnp.concatenate([c_sample, c_prompt, jnp.zeros((pad, D_MODEL), F32)], axis=0)
    n_all = c_all.shape[0]
    n_out = w_ada.shape[1]
    return pl.pallas_call(
        functools.partial(_adaln_body, n_s=n_s, n_p=n_p),
        out_shape=(jax.ShapeDtypeStruct((n_s, n_out), F32),
                   jax.ShapeDtypeStruct((n_p, n_out), F32)),
        grid=(n_out // ADA_TN,),
        in_specs=[_resident((n_all, D_MODEL), lambda j: (0, 0)),
                  pl.BlockSpec((D_MODEL, ADA_TN), lambda j: (0, j)),
                  pl.BlockSpec((1, ADA_TN), lambda j: (0, j))],
        out_specs=(pl.BlockSpec((n_s, ADA_TN), lambda j: (0, j)),
                   pl.BlockSpec((n_p, ADA_TN), lambda j: (0, j))),
        compiler_params=_params(1, 40 * 1024 * 1024),
        name="adaln",
    )(c_all, w_ada, b_ada.reshape(1, n_out))


def _mix_in_body(x_ref, sh_ref, sc_ref, g_ref, w_ref, vg_ref, *refs, transposed):
    if transposed:
        wc_ref, q_ref, k_ref, v_ref, gu_ref, gv_ref, kt_ref, wcb_ref = refs
        wcb_ref[...] = wc_ref[...].astype(wcb_ref.dtype)
    else:
        q_ref, k_ref, v_ref, gu_ref, gv_ref = refs
    x = x_ref[...]
    tm = x.shape[0]
    h = _rms(x, g_ref[...]) * (1.0 + _rows(sc_ref[...], tm)) + _rows(sh_ref[...], tm)
    h = h.astype(BF16)
    c0, c1, c2, c3 = Q_COLS, Q_COLS + KV_COLS, Q_COLS + 2 * KV_COLS, Q_COLS + 2 * KV_COLS + GMLP_WIDTH
    q = _dot(h, w_ref[:, :c0]) * ATTN_SCALE
    kv = _dot(h, w_ref[:, c0:c2])
    k_ref[...] = kv[:, :KV_COLS].astype(k_ref.dtype)
    if transposed:
        q_ref[...] = q.T.astype(q_ref.dtype)
        kt_ref[...] = kv[:, :KV_COLS].T
        v_ref[...] = kv[:, KV_COLS:].T
    else:
        q_ref[...] = q.astype(q_ref.dtype)
        v_ref[...] = kv[:, KV_COLS:]
    gu_ref[...] = jax.nn.gelu(_dot(h, w_ref[:, c2:c3])).astype(gu_ref.dtype)
    gv = jax.nn.gelu(_dot(h, w_ref[:, c3:]))
    vg = vg_ref[...]
    for hd in range(GMLP_HEADS):
        sl = slice(hd * GMLP_HEAD_DIM, (hd + 1) * GMLP_HEAD_DIM)
        gv_ref[:, sl] = _rms(gv[:, sl], vg[:, sl]).astype(gv_ref.dtype)


def _mix_in(x, x_spec, sh_spec, sc_spec, mod, grid, g_pre, w_in, v_gain, outs, out_specs,
            side=None):
    extra, extra_specs = ((), ()) if side is None else ((side[0],), (side[1],))
    return pl.pallas_call(
        functools.partial(_mix_in_body, transposed=side is not None),
        out_shape=outs,
        grid=grid,
        in_specs=[x_spec, sh_spec, sc_spec,
                  _resident((1, D_MODEL), lambda i: (0, 0)),
                  _resident((D_MODEL, IN_COLS), lambda i: (0, 0)),
                  _resident((1, GMLP_WIDTH), lambda i: (0, 0)), *extra_specs],
        out_specs=out_specs,
        compiler_params=_params(1),
        name="mix_in",
    )(x, mod, mod, g_pre, w_in, v_gain, *extra)


def _mix_core_body(qt_ref, kc_ref, kp_ref, vtc_ref, vtp_ref, gu_ref, gv_ref,
                   bias_ref, valid_ref, sink_ref, wtril_ref, bs_ref, gat_ref, gg_ref, wc_ref,
                   o_ref, wcb_ref, *, blocks_per_seq):
    wcb_ref[...] = wc_ref[...].astype(wcb_ref.dtype)
    first =(pl.program_id(0) % blocks_per_seq == 0).astype(jnp.int32)
    valid = valid_ref[first] > 0.5
    k = jnp.concatenate([kp_ref[...], kc_ref[...]], axis=0)
    vt = jnp.concatenate([vtp_ref[...], vtc_ref[...]], axis=1).astype(BF16)
    zeros = jnp.zeros((HEAD_DIM, GROUP * BLOCK), BF16)
    outs = []
    for kh in range(N_KV_HEADS):
        base = kh * GROUP * HEAD_DIM
        qs = jnp.concatenate(
            [qt_ref[base + g * HEAD_DIM: base + (g + 1) * HEAD_DIM, :] for g in range(GROUP)], axis=1)
        qp = jnp.concatenate([qs, zeros] if kh == 0 else [zeros, qs], axis=0)
        s = jnp.where(valid, _dot(k, qp) + bias_ref[kh], NEG_INF)
        sink = sink_ref[kh]
        m = jnp.maximum(jnp.max(s, axis=0, keepdims=True), sink)
        p = jnp.exp(s - m)
        den = jnp.sum(p, axis=0, keepdims=True) + jnp.exp(sink - m)
        outs.append(_dot(vt[kh * HEAD_DIM:(kh + 1) * HEAD_DIM, :], p.astype(BF16)) / den)
    ot = jnp.concatenate(outs, axis=0)
    sq = jnp.sum(ot * ot, axis=0, keepdims=True)
    tot = sq[:, :BLOCK]
    for g in range(1, GROUP):
        tot = tot + sq[:, g * BLOCK:(g + 1) * BLOCK]
    r = lax.rsqrt(tot / ATTN_WIDTH + EPS)
    a = ot * jnp.concatenate([r] * GROUP, axis=1) * gat_ref[...]
    for g in range(GROUP):
        o_ref[:, g * BLOCK:(g + 1) * BLOCK] = a[:, g * BLOCK:(g + 1) * BLOCK].T.astype(o_ref.dtype)
    bs = bs_ref[...]
    gated = []
    for hd in range(GMLP_HEADS):
        sl = slice(hd * GMLP_HEAD_DIM, (hd + 1) * GMLP_HEAD_DIM)
        mixed = _dot(wtril_ref[hd], gv_ref[:, sl]) + bs[:, hd:hd + 1]
        gated.append(gu_ref[:, sl] * mixed)
    gm = jnp.concatenate(gated, axis=-1)
    o_ref[:, ATTN_WIDTH:] = _rms(gm, gg_ref[...]).astype(o_ref.dtype)


def _mix_core(qt, k, vt, gu, gvn, bias_t, valid_t, sink_row, wtril, bs_t, gain_t, g_gmlp, seq, w_side):
    n_tok = k.shape[0]
    nblk = n_tok // BLOCK
    side_rows = w_side.shape[0] // nblk
    bps = seq // BLOCK
    cur = lambda i: (i, 0)
    prev = lambda i: (jnp.maximum(i - 1, 0), 0)
    cur_t = lambda i: (0, i)
    prev_t = lambda i: (0, jnp.maximum(i - 1, 0))
    full2 = lambda i: (0, 0)
    full3 = lambda i: (0, 0, 0)
    return pl.pallas_call(
        functools.partial(_mix_core_body, blocks_per_seq=bps),
        out_shape=(jax.ShapeDtypeStruct((n_tok, D_MODEL), BF16),
                   jax.ShapeDtypeStruct(w_side.shape, BF16)),
        grid=(nblk,),
        in_specs=[pl.BlockSpec((Q_COLS, BLOCK), cur_t),
                  pl.BlockSpec((BLOCK, KV_COLS), cur), pl.BlockSpec((BLOCK, KV_COLS), prev),
                  pl.BlockSpec((KV_COLS, BLOCK), cur_t), pl.BlockSpec((KV_COLS, BLOCK), prev_t),
                  pl.BlockSpec((BLOCK, GMLP_WIDTH), cur), pl.BlockSpec((BLOCK, GMLP_WIDTH), cur),
                  _resident(bias_t.shape, full3), _resident(valid_t.shape, full3),
                  _resident(sink_row.shape, full3), _resident(wtril.shape, full3),
                  _resident(bs_t.shape, full2),
                  _resident(gain_t.shape, full2), _resident((1, GMLP_WIDTH), full2),
                  pl.BlockSpec((side_rows, w_side.shape[1]), cur)],
        out_specs=(pl.BlockSpec((BLOCK, D_MODEL), cur),
                   pl.BlockSpec((side_rows, w_side.shape[1]), cur)),
        compiler_params=_params(1),
        name="mix_core",
    )(qt, k, k, vt, vt, gu, gvn, bias_t, valid_t, sink_row, wtril, bs_t, gain_t, g_gmlp, w_side)


SMP_BB = 16


def _attn_smp_body(q_ref, ckt_ref, cvt_ref, nk_ref, nv_ref, bias_ref, valid_ref, sink_ref,
                   o_ref, *, t_len):
    zpad = jnp.zeros((WINDOW - t_len, KV_COLS), F32)
    tiles = []
    for b in range(SMP_BB):
        q = q_ref[b]
        s_cache = _dot(q, ckt_ref[b].astype(BF16))
        kn = jnp.concatenate([nk_ref[b], zpad], axis=0).astype(BF16)
        tiles.append(jnp.concatenate([s_cache, _dot_nt(q, kn)], axis=1)[None])
    s = jnp.concatenate(tiles, axis=0)
    s = jnp.where(valid_ref[...] > 0.5, s + bias_ref[...], NEG_INF)
    sink = sink_ref[...]
    m = jnp.maximum(jnp.max(s, axis=-1, keepdims=True), sink)
    p = jnp.exp(s - m)
    den = jnp.sum(p, axis=-1, keepdims=True) + jnp.exp(sink - m)
    p = p.astype(BF16)
    for b in range(SMP_BB):
        vn = jnp.concatenate([nv_ref[b], zpad], axis=0).astype(BF16)
        o = _dot_nt(p[b, :, :WINDOW], cvt_ref[b].astype(BF16)) + _dot(p[b, :, WINDOW:], vn)
        o_ref[b] = o / den[b]


def _attn_smp(q6, cache_kt, cache_vt, new_k, new_v, bias_s, valid_s, sink_s, t_len):
    n_b, rows, _ = q6.shape
    b3 = lambda i: (i, 0, 0)
    one = lambda i: (0, 0)
    return pl.pallas_call(
        functools.partial(_attn_smp_body, t_len=t_len),
        out_shape=jax.ShapeDtypeStruct((n_b, rows, KV_COLS), F32),
        grid=(n_b // SMP_BB,),
        in_specs=[pl.BlockSpec((SMP_BB, rows, KV_COLS), b3),
                  pl.BlockSpec((SMP_BB, KV_COLS, WINDOW), b3),
                  pl.BlockSpec((SMP_BB, KV_COLS, WINDOW), b3),
                  pl.BlockSpec((SMP_BB, t_len, KV_COLS), b3),
                  pl.BlockSpec((SMP_BB, t_len, KV_COLS), b3),
                  _resident(bias_s.shape, one), _resident(valid_s.shape, one),
                  _resident(sink_s.shape, one)],
        out_specs=pl.BlockSpec((SMP_BB, rows, KV_COLS), b3),
        compiler_params=_params(1, 32 * 1024 * 1024),
        name="attn_smp",
    )(q6, cache_kt, cache_vt, new_k, new_v, bias_s, valid_s, sink_s)


def _stage_w_out(w_ref, wb_ref):
    for kh in range(N_KV_HEADS):
        for g in range(GROUP):
            src = (kh * GROUP + g) * HEAD_DIM
            dst = (g * N_KV_HEADS + kh) * HEAD_DIM
            wb_ref[dst:dst + HEAD_DIM, :] = w_ref[src:src + HEAD_DIM, :].astype(BF16)
    for r in range(ATTN_WIDTH, D_MODEL, BLOCK):
        wb_ref[r:r + BLOCK, :] = w_ref[r:r + BLOCK, :].astype(BF16)


def _out_proj_tail(merged, x_ref, gt_ref, sh_ref, sc_ref, w_ref, gpm_ref, gpf_ref, x1_ref, hf_ref,
                   wb_ref):
    @pl.when(pl.program_id(0) == 0)
    def _():
        _stage_w_out(w_ref, wb_ref)

    tm = merged.shape[0]
    o = _dot(merged, wb_ref[...])
    x1 = x_ref[...] + _rows(gt_ref[...], tm) * _rms(o, gpm_ref[...])
    x1_ref[...] = x1
    hf = _rms(x1, gpf_ref[...]) * (1.0 + _rows(sc_ref[...], tm)) + _rows(sh_ref[...], tm)
    hf_ref[...] = hf.astype(hf_ref.dtype)


def _out_proj_body(m_ref, *rest):
    _out_proj_tail(m_ref[...], *rest)


def _out_proj_smp_body(w4_ref, b4_ref, a_ref, gu_ref, gv_ref, ga_ref, gg_ref, *rest, t_len):
    i = pl.program_id(0)
    gated = []
    for hd in range(GMLP_HEADS):
        acc = jnp.zeros((a_ref.shape[0], GMLP_HEAD_DIM), F32)
        for j in range(t_len):
            w = jnp.where(j <= i, w4_ref[(hd * t_len + i) * t_len + j], 0.0)
            lo = j * GMLP_WIDTH + hd * GMLP_HEAD_DIM
            acc = acc + w * gv_ref[:, lo:lo + GMLP_HEAD_DIM]
        mixed = acc + b4_ref[hd * t_len + i]
        gated.append(gu_ref[:, hd * GMLP_HEAD_DIM:(hd + 1) * GMLP_HEAD_DIM] * mixed)
    gm = jnp.concatenate(gated, axis=-1)
    merged = jnp.concatenate([_rms(a_ref[...], ga_ref[...]), _rms(gm, gg_ref[...])], axis=-1)
    _out_proj_tail(merged.astype(BF16), *rest)


FFN_TF = 1024


def _ffn_body(h_ref, w1_ref, w2_ref, x1_ref, gt_ref, g_ref, o_ref):
    j = pl.program_id(1)

    @pl.when(j == 0)
    def _():
        o_ref[...] = jnp.zeros_like(o_ref)

    a = jnp.maximum(_dot(h_ref[...], w1_ref[...]), 0.0)
    o_ref[...] += _dot((a * a).astype(BF16), w2_ref[...])

    @pl.when(j == pl.num_programs(1) - 1)
    def _():
        tm = o_ref.shape[0]
        o_ref[...] = x1_ref[...] + _rows(gt_ref[...], tm) * _rms(o_ref[...], g_ref[...])


def _ffn(hf, w1, w2, x1, mod, gt_spec, g_post, tm):
    n_tok = hf.shape[0]
    return pl.pallas_call(
        _ffn_body,
        out_shape=jax.ShapeDtypeStruct((n_tok, D_MODEL), F32),
        grid=(n_tok // tm, D_FF // FFN_TF),
        in_specs=[pl.BlockSpec((tm, D_MODEL), lambda i, j: (i, 0)),
                  pl.BlockSpec((D_MODEL, FFN_TF), lambda i, j: (0, j)),
                  pl.BlockSpec((FFN_TF, D_MODEL), lambda i, j: (j, 0)),
                  pl.BlockSpec((tm, D_MODEL), lambda i, j: (i, 0)),
                  gt_spec,
                  _resident((1, D_MODEL), lambda i, j: (0, 0))],
        out_specs=pl.BlockSpec((tm, D_MODEL), lambda i, j: (i, 0)),
        compiler_params=_params(2),
        name="ffn",
    )(hf, w1, w2, x1, mod, g_post)


MIX_TM = 512
OUT_TM = 512
FFN_TM = 512


def _row(v):
    return v.reshape(1, -1)


def _layer(x_prompt, x_sample, cache_k, cache_v, c_prompt, c_sample, rel_table, w_ada, b_ada,
           g_pre_mix, w_in, sinks, v_gain, w_s, b_s, g_attn, g_gmlp, w_out, g_post_mix,
           g_pre_ff, w_ff1, w_ff2, g_post_ff):
    n_b, seq, _ = x_prompt.shape
    n_db, t_len, _ = x_sample.shape
    n_tok = n_b * seq
    D = D_MODEL

    w_in_b = w_in.astype(BF16)
    g_pre_mix, g_attn, g_gmlp = _row(g_pre_mix), _row(g_attn), _row(g_gmlp)
    g_post_mix, g_pre_ff, g_post_ff = _row(g_post_mix), _row(g_pre_ff), _row(g_post_ff)
    v_gain = _row(v_gain)

    bias_p, bias_s, wtril = _prep(rel_table, w_s, t_len)
    mod_s, mod_p = _adaln(c_prompt, c_sample, w_ada, b_ada)
    mod_p = mod_p.reshape(n_b, 1, 6 * D)
    SH_M, SC_M, GT_M, SH_F, SC_F, GT_F = range(6)

    def pmod(chunk, tm, n_axes=1):
        per = seq // tm
        if n_axes == 1:
            return pl.BlockSpec((None, 1, D), lambda i: (i // per, 0, chunk))
        return pl.BlockSpec((None, 1, D), lambda i, j: (i // per, 0, chunk))

    def smod(chunk, n_axes=1):
        if n_axes == 1:
            return pl.BlockSpec((n_db, D), lambda i: (0, chunk))
        return pl.BlockSpec((n_db, D), lambda i, j: (0, chunk))

    pd = _prompt_dist().T
    valid_any = np.tile(((pd >= 0) & (pd < WINDOW)).astype(np.float32), (1, GROUP))
    valid_first = valid_any * (np.arange(2 * BLOCK)[:, None] >= BLOCK)
    valid_t = np.stack([valid_any, valid_first]).astype(np.float32)
    sd = _sample_dist(t_len)
    valid_s = ((sd >= 0) & (sd < WINDOW) & (np.arange(SMP_KEYS)[None, :] < WINDOW + t_len))
    valid_s = valid_s.astype(np.float32)
    sink2 = sinks.reshape(N_KV_HEADS, GROUP)
    sink_p = jnp.repeat(sink2, BLOCK, axis=1)[:, None, :]
    sink_s = jnp.tile(sink2, (1, t_len)).reshape(-1, 1)
    ga3 = g_attn.reshape(N_KV_HEADS, GROUP, HEAD_DIM)
    gain_t = jnp.repeat(ga3.transpose(0, 2, 1).reshape(KV_COLS, GROUP), BLOCK, axis=1)
    g_attn = ga3.transpose(1, 0, 2).reshape(1, ATTN_WIDTH)

    xp = x_prompt.reshape(n_tok, D)
    tile = lambda w: pl.BlockSpec((MIX_TM, w), lambda i: (i, 0))
    ttile = lambda w: pl.BlockSpec((w, MIX_TM), lambda i: (0, i))
    n_mix = n_tok // MIX_TM
    slab1 = pl.BlockSpec((D // n_mix, D_FF), lambda i: (i, 0))
    qt_p, k_p, vt_p, gu_p, gvn_p, kt_p, w1_b = _mix_in(
        xp, tile(D), pmod(SH_M, MIX_TM), pmod(SC_M, MIX_TM), mod_p, (n_tok // MIX_TM,),
        g_pre_mix, w_in_b, v_gain,
        (jax.ShapeDtypeStruct((Q_COLS, n_tok), BF16),
         jax.ShapeDtypeStruct((n_tok, KV_COLS), BF16),
         jax.ShapeDtypeStruct((KV_COLS, n_tok), F32),
         jax.ShapeDtypeStruct((n_tok, GMLP_WIDTH), F32),
         jax.ShapeDtypeStruct((n_tok, GMLP_WIDTH), BF16),
         jax.ShapeDtypeStruct((KV_COLS, n_tok), F32),
         jax.ShapeDtypeStruct((D, D_FF), BF16)),
        (ttile(Q_COLS), tile(KV_COLS), ttile(KV_COLS), tile(GMLP_WIDTH), tile(GMLP_WIDTH),
         ttile(KV_COLS), slab1),
        side=(w_ff1, slab1))

    merged_p, w2_b = _mix_core(qt_p, k_p, vt_p, gu_p, gvn_p, bias_p, jnp.asarray(valid_t), sink_p,
                               wtril, b_s.T, gain_t, g_gmlp, seq, w_ff2)

    otile = lambda w: pl.BlockSpec((OUT_TM, w), lambda i: (i, 0))
    one = lambda i: (0, 0)
    x1_p, hf_p = pl.pallas_call(
        _out_proj_body,
        out_shape=(jax.ShapeDtypeStruct((n_tok, D), F32), jax.ShapeDtypeStruct((n_tok, D), BF16)),
        grid=(n_tok // OUT_TM,),
        in_specs=[otile(D), otile(D), pmod(GT_M, OUT_TM), pmod(SH_F, OUT_TM), pmod(SC_F, OUT_TM),
                  _resident((D, D), one), _resident((1, D), one), _resident((1, D), one)],
        out_specs=(otile(D), otile(D)),
        scratch_shapes=[pltpu.VMEM((D, D), BF16)],
        compiler_params=_params(1),
        name="out_proj",
    )(merged_p, xp, mod_p, mod_p, mod_p, w_out, g_post_mix, g_pre_ff)

    y_p = _ffn(hf_p, w1_b, w2_b, x1_p, mod_p, pmod(GT_F, FFN_TM, 2), g_post_ff, FFN_TM)

    xs = x_sample.reshape(n_db, t_len * D)
    lane = lambda w: pl.BlockSpec((n_db, w), lambda t: (0, t))
    q_s, k_s, v_s, gu_s, gvn_s = _mix_in(
        xs, lane(D), smod(SH_M), smod(SC_M), mod_s, (t_len,),
        g_pre_mix, w_in_b, v_gain,
        (jax.ShapeDtypeStruct((n_db, t_len * Q_COLS), F32),
         jax.ShapeDtypeStruct((n_db, t_len * KV_COLS), F32),
         jax.ShapeDtypeStruct((n_db, t_len * KV_COLS), F32),
         jax.ShapeDtypeStruct((n_db, t_len * GMLP_WIDTH), F32),
         jax.ShapeDtypeStruct((n_db, t_len * GMLP_WIDTH), F32)),
        (lane(Q_COLS), lane(KV_COLS), lane(KV_COLS), lane(GMLP_WIDTH), lane(GMLP_WIDTH)))

    q5 = q_s.reshape(n_db, t_len, N_KV_HEADS, GROUP, HEAD_DIM).transpose(0, 2, 1, 3, 4)
    q5 = q5.reshape(n_db, N_KV_HEADS, t_len * GROUP, HEAD_DIM).astype(BF16)
    z5 = jnp.zeros_like(q5[:, 0])
    q6 = jnp.concatenate([jnp.concatenate([q5[:, 0], z5], axis=-1),
                          jnp.concatenate([z5, q5[:, 1]], axis=-1)], axis=1)
    new_k = k_s.reshape(n_db, t_len, KV_COLS)
    new_v = v_s.reshape(n_db, t_len, KV_COLS)
    cache_kt = cache_k.transpose(0, 2, 3, 1).reshape(n_db, KV_COLS, WINDOW)
    cache_vt = cache_v.transpose(0, 2, 3, 1).reshape(n_db, KV_COLS, WINDOW)
    o6 = _attn_smp(q6, cache_kt, cache_vt, new_k, new_v, bias_s, jnp.asarray(valid_s), sink_s, t_len)
    o6 = o6.reshape(n_db, N_KV_HEADS, t_len, GROUP, N_KV_HEADS, HEAD_DIM)
    attn_s = jnp.stack([o6[:, kh, :, :, kh, :] for kh in range(N_KV_HEADS)], axis=3)
    attn_s = attn_s.reshape(n_db, t_len * ATTN_WIDTH)

    smem = pl.BlockSpec(memory_space=pltpu.SMEM)
    w4 = w_s[:, :t_len, :t_len].reshape(-1)
    b4 = b_s[:, :t_len].reshape(-1)
    tmaj = lambda: pl.BlockSpec((None, n_db, D), lambda t: (t, 0, 0))
    x1_s, hf_s = pl.pallas_call(
        functools.partial(_out_proj_smp_body, t_len=t_len),
        out_shape=(jax.ShapeDtypeStruct((t_len, n_db, D), F32),
                   jax.ShapeDtypeStruct((t_len, n_db, D), BF16)),
        grid=(t_len,),
        in_specs=[smem, smem, lane(ATTN_WIDTH), lane(GMLP_WIDTH),
                  _resident((n_db, t_len * GMLP_WIDTH), one),
                  _resident((1, ATTN_WIDTH), one), _resident((1, GMLP_WIDTH), one),
                  lane(D), smod(GT_M), smod(SH_F), smod(SC_F),
                  _resident((D, D), one), _resident((1, D), one), _resident((1, D), one)],
        out_specs=(tmaj(), tmaj()),
        scratch_shapes=[pltpu.VMEM((D, D), BF16)],
        compiler_params=_params(1),
        name="out_proj_smp",
    )(w4, b4, attn_s, gu_s, gvn_s, g_attn, g_gmlp, xs, mod_s, mod_s, mod_s,
      w_out, g_post_mix, g_pre_ff)

    n_st = t_len * n_db
    y_s = _ffn(hf_s.reshape(n_st, D), w1_b, w2_b, x1_s.reshape(n_st, D), mod_s,
               smod(GT_F, 2), g_post_ff, n_st)
    y_s = y_s.reshape(t_len, n_db, D).transpose(1, 0, 2)

    cw = min(WINDOW, seq)
    last = lambda t: (t.reshape(N_KV_HEADS, HEAD_DIM, n_b, seq)[:, :, :, seq - cw:]
                      .transpose(2, 3, 0, 1))
    k_p4, v_p4 = last(kt_p), last(vt_p)
    return (y_p.reshape(n_b, seq, D), y_s, k_p4, v_p4,
            k_s.reshape(n_db, t_len, N_KV_HEADS, HEAD_DIM),
            v_s.reshape(n_db, t_len, N_KV_HEADS, HEAD_DIM),
            gvn_s.reshape(n_db, t_len, GMLP_HEADS, GMLP_HEAD_DIM))


def kernel(x_prompt, x_sample, cache_k, cache_v, c_prompt, c_sample, rel_bias_table, w_ada, b_ada,
           g_pre_mix, w_in, attn_sinks, gmlp_v_gain, gmlp_w_s, gmlp_b_s, g_attn_out, g_gmlp_out,
           w_out, g_post_mix, g_pre_ff, w_ff1, w_ff2, g_post_ff):
    depth = w_in.shape[0]
    assert depth == 1, "single-layer step"
    outs = _layer(x_prompt, x_sample, cache_k[0], cache_v[0], c_prompt, c_sample, rel_bias_table,
                  w_ada[0], b_ada[0], g_pre_mix[0], w_in[0], attn_sinks[0], gmlp_v_gain[0],
                  gmlp_w_s[0], gmlp_b_s[0], g_attn_out[0], g_gmlp_out[0], w_out[0], g_post_mix[0],
                  g_pre_ff[0], w_ff1[0], w_ff2[0], g_post_ff[0])
    y_p, y_s, k_p, v_p, k_s, v_s, gv_s = outs
    return (y_p, y_s, k_p[None], v_p[None], k_s[None], v_s[None], gv_s[None])
```

```python
import functools
import math

import numpy as np
import jax
import jax.numpy as jnp
from jax import lax
from jax.experimental import pallas as pl
from jax.experimental.pallas import tpu as pltpu

D_MODEL = 2048
HEAD_DIM = 64
ATTN_WIDTH = 1024
N_HEADS = 16
N_KV_HEADS = 2
GROUP = 8
WINDOW = 128
BLOCK = 128
GMLP_WIDTH = 1024
GMLP_HEADS = 8
GMLP_HEAD_DIM = 128
D_FF = 4 * D_MODEL
NUM_BUCKETS = 32
MAX_DISTANCE = 128
EPS = 1e-6
KV_COLS = N_KV_HEADS * HEAD_DIM
Q_COLS = N_HEADS * HEAD_DIM
IN_COLS = Q_COLS + 2 * KV_COLS + 2 * GMLP_WIDTH
ATTN_SCALE = HEAD_DIM ** -0.5
NEG_INF = -1e30

V7X_VMEM_BYTES = 64 * 1024 * 1024
VMEM_LIMIT_BYTES = 56 * 1024 * 1024

BF16 = jnp.bfloat16
F32 = jnp.float32


def _params(n_axes, vmem=VMEM_LIMIT_BYTES):
    return pltpu.CompilerParams(
        dimension_semantics=("arbitrary",) * n_axes, vmem_limit_bytes=vmem)


def _resident(shape, index_map):
    return pl.BlockSpec(shape, index_map, pipeline_mode=pl.Buffered(1))


def _rms(x, gain):
    return x * lax.rsqrt(jnp.mean(x * x, axis=-1, keepdims=True) + EPS) * gain


def _rows(m, n_rows):
    r = m.shape[0]
    if r == 1 or r == n_rows:
        return m
    return jnp.concatenate([m] * (n_rows // r), axis=0)


def _dot(a, b):
    return jnp.dot(a, b, preferred_element_type=F32)


def _dot_nt(a, b):
    return lax.dot_general(a, b, (((1,), (1,)), ((), ())), preferred_element_type=F32)


def _t5_bucket(dist):
    n = jnp.maximum(dist, 0)
    max_exact = NUM_BUCKETS // 2
    nf = jnp.maximum(n, 1).astype(jnp.float32)
    large = max_exact + (jnp.log(nf / max_exact) / math.log(MAX_DISTANCE / max_exact)
                         * (NUM_BUCKETS - max_exact)).astype(jnp.int32)
    large = jnp.minimum(large, NUM_BUCKETS - 1)
    return jnp.where(n < max_exact, n, large)


SMP_KEYS = 2 * WINDOW


def _prompt_dist():
    r = np.arange(BLOCK)[:, None]
    i = np.arange(BLOCK)[None, :]
    return (i - r) % BLOCK


def _sample_rows(t_len):
    r = np.arange(N_KV_HEADS * t_len * GROUP)
    return r // (t_len * GROUP), (r // GROUP) % t_len, r % GROUP


def _sample_dist(t_len):
    _, t, _ = _sample_rows(t_len)
    return WINDOW + t[:, None] - np.arange(SMP_KEYS)[None, :]


def _prep_body(tab_ref, tabr_ref, bkt_p_ref, bkt_s_ref, ws_ref,
               bias_p_ref, bias_s_ref, wtril_ref):
    bkt_p = bkt_p_ref[...]
    for kh in range(N_KV_HEADS):
        for g in range(GROUP):
            h = kh * GROUP + g
            acc = jnp.zeros((BLOCK, BLOCK), F32)
            for b in range(NUM_BUCKETS):
                acc = jnp.where(bkt_p == b, tab_ref[b, h], acc)
            bias_p_ref[kh, :, g * BLOCK:(g + 1) * BLOCK] = acc
    bkt_s = bkt_s_ref[...]
    tabr = tabr_ref[...]
    acc = jnp.zeros(bkt_s.shape, F32)
    for b in range(NUM_BUCKETS):
        acc = jnp.where(bkt_s == b, tabr[:, b:b + 1], acc)
    bias_s_ref[...] = acc
    ii = lax.broadcasted_iota(jnp.int32, (BLOCK, BLOCK), 0)
    jj = lax.broadcasted_iota(jnp.int32, (BLOCK, BLOCK), 1)
    for h in range(GMLP_HEADS):
        wtril_ref[h] = jnp.where(jj <= ii, ws_ref[h], 0.0).astype(BF16)


def _prep(rel_table, w_s, t_len):
    bkt_p = _t5_bucket(jnp.asarray(_prompt_dist(), jnp.int32))
    bkt_s = _t5_bucket(jnp.asarray(_sample_dist(t_len), jnp.int32))
    n_rows = bkt_s.shape[0]
    tab_rows = jnp.broadcast_to(rel_table.T.reshape(N_KV_HEADS, 1, GROUP, NUM_BUCKETS),
                                (N_KV_HEADS, t_len, GROUP, NUM_BUCKETS)).reshape(n_rows, NUM_BUCKETS)
    smem = pl.BlockSpec(memory_space=pltpu.SMEM)
    vmem = pl.BlockSpec(memory_space=pltpu.VMEM)
    return pl.pallas_call(
        _prep_body,
        out_shape=(jax.ShapeDtypeStruct((N_KV_HEADS, BLOCK, GROUP * BLOCK), F32),
                   jax.ShapeDtypeStruct((n_rows, SMP_KEYS), F32),
                   jax.ShapeDtypeStruct((GMLP_HEADS, BLOCK, BLOCK), BF16)),
        in_specs=[smem, vmem, vmem, vmem, vmem],
        out_specs=(vmem, vmem, vmem),
        compiler_params=pltpu.CompilerParams(vmem_limit_bytes=32 * 1024 * 1024),
        name="prep",
    )(rel_table, tab_rows, bkt_p, bkt_s, w_s)


ADA_TN = 1024


def _adaln_body(c_ref, w_ref, b_ref, os_ref, op_ref, *, n_s, n_p):
    c = c_ref[...]
    s = (c * jax.nn.sigmoid(c)).astype(BF16)
    r = _dot(s, w_ref[...].astype(BF16)) + b_ref[...]
    os_ref[...] = r[:n_s]
    op_ref[...] = r[n_s:n_s + n_p]


def _adaln(c_prompt, c_sample, w_ada, b_ada):
    n_p, n_s = c_prompt.shape[0], c_sample.shape[0]
    pad = (-(n_p + n_s)) % 16
    c_all = jnp.concatenate([c_sample, c_prompt, jnp.zeros((pad, D_MODEL), F32)], axis=0)
    n_all = c_all.shape[0]
    n_out = w_ada.shape[1]
    return pl.pallas_call(
        functools.partial(_adaln_body, n_s=n_s, n_p=n_p),
        out_shape=(jax.ShapeDtypeStruct((n_s, n_out), F32),
                   jax.ShapeDtypeStruct((n_p, n_out), F32)),
        grid=(n_out // ADA_TN,),
        in_specs=[_resident((n_all, D_MODEL), lambda j: (0, 0)),
                  pl.BlockSpec((D_MODEL, ADA_TN), lambda j: (0, j)),
                  pl.BlockSpec((1, ADA_TN), lambda j: (0, j))],
        out_specs=(pl.BlockSpec((n_s, ADA_TN), lambda j: (0, j)),
                   pl.BlockSpec((n_p, ADA_TN), lambda j: (0, j))),
        compiler_params=_params(1, 40 * 1024 * 1024),
        name="adaln",
    )(c_all, w_ada, b_ada.reshape(1, n_out))


def _mix_in_body(x_ref, sh_ref, sc_ref, g_ref, w_ref, vg_ref, *refs, transposed):
    if transposed:
        wc_ref, q_ref, k_ref, v_ref, gu_ref, gv_ref, kt_ref, wcb_ref = refs
        wcb_ref[...] = wc_ref[...].astype(wcb_ref.dtype)
    else:
        q_ref, k_ref, v_ref, gu_ref, gv_ref = refs
    tm = x_ref.shape[0]
    sub = min(tm, MIX_SUB)
    c0, c1, c2, c3 = Q_COLS, Q_COLS + KV_COLS, Q_COLS + 2 * KV_COLS, Q_COLS + 2 * KV_COLS + GMLP_WIDTH
    vg = vg_ref[...]
    for r in range(0, tm, sub):
        rows = slice(r, r + sub)
        mod = lambda ref: ref[...] if ref.shape[0] == 1 else ref[rows, :]
        h = (_rms(x_ref[rows, :], g_ref[...]) * (1.0 + mod(sc_ref)) + mod(sh_ref)).astype(BF16)
        q = _dot(h, w_ref[:, :c0]) * ATTN_SCALE
        kv = _dot(h, w_ref[:, c0:c2])
        k_ref[rows, :] = kv[:, :KV_COLS].astype(k_ref.dtype)
        if transposed:
            q_ref[:, rows] = q.T.astype(q_ref.dtype)
            kt_ref[:, rows] = kv[:, :KV_COLS].T
            v_ref[:, rows] = kv[:, KV_COLS:].T
        else:
            q_ref[rows, :] = q.astype(q_ref.dtype)
            v_ref[rows, :] = kv[:, KV_COLS:]
        gu_ref[rows, :] = jax.nn.gelu(_dot(h, w_ref[:, c2:c3])).astype(gu_ref.dtype)
        gv = jax.nn.gelu(_dot(h, w_ref[:, c3:]))
        for hd in range(GMLP_HEADS):
            sl = slice(hd * GMLP_HEAD_DIM, (hd + 1) * GMLP_HEAD_DIM)
            gv_ref[rows, sl] = _rms(gv[:, sl], vg[:, sl]).astype(gv_ref.dtype)


def _mix_in(x, x_spec, sh_spec, sc_spec, mod, grid, g_pre, w_in, v_gain, outs, out_specs,
            side=None):
    extra, extra_specs = ((), ()) if side is None else ((side[0],), (side[1],))
    return pl.pallas_call(
        functools.partial(_mix_in_body, transposed=side is not None),
        out_shape=outs,
        grid=grid,
        in_specs=[x_spec, sh_spec, sc_spec,
                  _resident((1, D_MODEL), lambda i: (0, 0)),
                  _resident((D_MODEL, IN_COLS), lambda i: (0, 0)),
                  _resident((1, GMLP_WIDTH), lambda i: (0, 0)), *extra_specs],
        out_specs=out_specs,
        compiler_params=_params(1),
        name="mix_in",
    )(x, mod, mod, g_pre, w_in, v_gain, *extra)


CORE_NB = 4


def _mix_core_body(qt_ref, kc_ref, kp_ref, vtc_ref, vtp_ref, gu_ref, gv_ref,
                   bias_ref, pm_ref, sink_ref, wtril_ref, bs_ref, gat_ref, gg_ref, wc_ref,
                   o_ref, wcb_ref, *, steps_per_seq):
    wcb_ref[...] = wc_ref[...].astype(wcb_ref.dtype)
    first = pl.program_id(0) % steps_per_seq == 0
    tri = (lax.broadcasted_iota(jnp.int32, (BLOCK, BLOCK), 0)
           > lax.broadcasted_iota(jnp.int32, (BLOCK, BLOCK), 1))
    from_prev = jnp.concatenate([tri] * GROUP, axis=1)
    zeros = jnp.zeros((HEAD_DIM, GROUP * BLOCK), BF16)
    bs = bs_ref[...]
    for u in range(CORE_NB):
        tok = slice(u * BLOCK, (u + 1) * BLOCK)
        k_prev = kp_ref[...] if u == 0 else kc_ref[(u - 1) * BLOCK:u * BLOCK, :]
        vt_prev = vtp_ref[...] if u == 0 else vtc_ref[:, (u - 1) * BLOCK:u * BLOCK]
        k = jnp.concatenate([k_prev, kc_ref[tok, :]], axis=0)
        vt = jnp.concatenate([vt_prev, vtc_ref[:, tok]], axis=1).astype(BF16)
        outs = []
        for kh in range(N_KV_HEADS):
            base = kh * GROUP * HEAD_DIM
            qs = jnp.concatenate(
                [qt_ref[base + g * HEAD_DIM: base + (g + 1) * HEAD_DIM, tok] for g in range(GROUP)],
                axis=1)
            qp = jnp.concatenate([qs, zeros] if kh == 0 else [zeros, qs], axis=0)
            qk = _dot(k, qp)
            s = jnp.where(from_prev, qk[:BLOCK], qk[BLOCK:]) + bias_ref[kh]
            if u == 0:
                s = jnp.where(jnp.logical_and(first, from_prev), NEG_INF, s)
            sink = sink_ref[kh]
            m = jnp.maximum(jnp.max(s, axis=0, keepdims=True), sink)
            p = jnp.exp(s - m)
            den = jnp.sum(p, axis=0, keepdims=True) + jnp.exp(sink - m)
            p = p.astype(BF16)
            p2 = jnp.concatenate([p * pm_ref[0], p * pm_ref[1]], axis=0)
            outs.append(_dot(vt[kh * HEAD_DIM:(kh + 1) * HEAD_DIM, :], p2) / den)
        ot = jnp.concatenate(outs, axis=0)
        sq = jnp.sum(ot * ot, axis=0, keepdims=True)
        tot = sq[:, :BLOCK]
        for g in range(1, GROUP):
            tot = tot + sq[:, g * BLOCK:(g + 1) * BLOCK]
        r = lax.rsqrt(tot / ATTN_WIDTH + EPS)
        a = ot * jnp.concatenate([r] * GROUP, axis=1) * gat_ref[...]
        for g in range(GROUP):
            o_ref[tok, g * BLOCK:(g + 1) * BLOCK] = (
                a[:, g * BLOCK:(g + 1) * BLOCK].T.astype(o_ref.dtype))
        gated = []
        for hd in range(GMLP_HEADS):
            sl = slice(hd * GMLP_HEAD_DIM, (hd + 1) * GMLP_HEAD_DIM)
            mixed = _dot(wtril_ref[hd], gv_ref[tok, sl]) + bs[:, hd:hd + 1]
            gated.append(gu_ref[tok, sl] * mixed)
        gm = jnp.concatenate(gated, axis=-1)
        o_ref[tok, ATTN_WIDTH:] = _rms(gm, gg_ref[...]).astype(o_ref.dtype)


def _mix_core(qt, k, vt, gu, gvn, bias_t, sink_row, wtril, bs_t, gain_t, g_gmlp, seq, w_side):
    r_gt_i = np.tile(np.arange(BLOCK)[:, None] > np.arange(BLOCK)[None, :], (1, GROUP))
    pmask = jnp.asarray(np.stack([r_gt_i, ~r_gt_i]), BF16)
    n_tok = k.shape[0]
    tm = CORE_NB * BLOCK
    n_steps = n_tok // tm
    side_rows = w_side.shape[0] // n_steps
    cur = lambda i: (i, 0)
    prev = lambda i: (jnp.maximum(i * CORE_NB - 1, 0), 0)
    cur_t = lambda i: (0, i)
    prev_t = lambda i: (0, jnp.maximum(i * CORE_NB - 1, 0))
    full2 = lambda i: (0, 0)
    full3 = lambda i: (0, 0, 0)
    return pl.pallas_call(
        functools.partial(_mix_core_body, steps_per_seq=seq // tm),
        out_shape=(jax.ShapeDtypeStruct((n_tok, D_MODEL), BF16),
                   jax.ShapeDtypeStruct(w_side.shape, BF16)),
        grid=(n_steps,),
        in_specs=[pl.BlockSpec((Q_COLS, tm), cur_t),
                  pl.BlockSpec((tm, KV_COLS), cur), pl.BlockSpec((BLOCK, KV_COLS), prev),
                  pl.BlockSpec((KV_COLS, tm), cur_t), pl.BlockSpec((KV_COLS, BLOCK), prev_t),
                  pl.BlockSpec((tm, GMLP_WIDTH), cur), pl.BlockSpec((tm, GMLP_WIDTH), cur),
                  _resident(bias_t.shape, full3), _resident(pmask.shape, full3),
                  _resident(sink_row.shape, full3), _resident(wtril.shape, full3),
                  _resident(bs_t.shape, full2),
                  _resident(gain_t.shape, full2), _resident((1, GMLP_WIDTH), full2),
                  pl.BlockSpec((side_rows, w_side.shape[1]), cur)],
        out_specs=(pl.BlockSpec((tm, D_MODEL), cur),
                   pl.BlockSpec((side_rows, w_side.shape[1]), cur)),
        compiler_params=_params(1),
        name="mix_core",
    )(qt, k, k, vt, vt, gu, gvn, bias_t, pmask, sink_row, wtril, bs_t, gain_t, g_gmlp, w_side)


SMP_BB = 16


def _attn_smp_body(q_ref, ckt_ref, cvt_ref, nk_ref, nv_ref, bias_ref, valid_ref, sink_ref,
                   o_ref, *, t_len):
    zpad = jnp.zeros((WINDOW - t_len, KV_COLS), F32)
    tiles = []
    for b in range(SMP_BB):
        q = q_ref[b]
        s_cache = _dot(q, ckt_ref[b].astype(BF16))
        kn = jnp.concatenate([nk_ref[b], zpad], axis=0).astype(BF16)
        tiles.append(jnp.concatenate([s_cache, _dot_nt(q, kn)], axis=1)[None])
    s = jnp.concatenate(tiles, axis=0)
    s = jnp.where(valid_ref[...] > 0.5, s + bias_ref[...], NEG_INF)
    sink = sink_ref[...]
    m = jnp.maximum(jnp.max(s, axis=-1, keepdims=True), sink)
    p = jnp.exp(s - m)
    den = jnp.sum(p, axis=-1, keepdims=True) + jnp.exp(sink - m)
    p = p.astype(BF16)
    for b in range(SMP_BB):
        vn = jnp.concatenate([nv_ref[b], zpad], axis=0).astype(BF16)
        o = _dot_nt(p[b, :, :WINDOW], cvt_ref[b].astype(BF16)) + _dot(p[b, :, WINDOW:], vn)
        o_ref[b] = o / den[b]


def _attn_smp(q6, cache_kt, cache_vt, new_k, new_v, bias_s, valid_s, sink_s, t_len):
    n_b, rows, _ = q6.shape
    b3 = lambda i: (i, 0, 0)
    one = lambda i: (0, 0)
    return pl.pallas_call(
        functools.partial(_attn_smp_body, t_len=t_len),
        out_shape=jax.ShapeDtypeStruct((n_b, rows, KV_COLS), F32),
        grid=(n_b // SMP_BB,),
        in_specs=[pl.BlockSpec((SMP_BB, rows, KV_COLS), b3),
                  pl.BlockSpec((SMP_BB, KV_COLS, WINDOW), b3),
                  pl.BlockSpec((SMP_BB, KV_COLS, WINDOW), b3),
                  pl.BlockSpec((SMP_BB, t_len, KV_COLS), b3),
                  pl.BlockSpec((SMP_BB, t_len, KV_COLS), b3),
                  _resident(bias_s.shape, one), _resident(valid_s.shape, one),
                  _resident(sink_s.shape, one)],
        out_specs=pl.BlockSpec((SMP_BB, rows, KV_COLS), b3),
        compiler_params=_params(1, 32 * 1024 * 1024),
        name="attn_smp",
    )(q6, cache_kt, cache_vt, new_k, new_v, bias_s, valid_s, sink_s)


def _stage_w_out(w_ref, wb_ref):
    for kh in range(N_KV_HEADS):
        for g in range(GROUP):
            src = (kh * GROUP + g) * HEAD_DIM
            dst = (g * N_KV_HEADS + kh) * HEAD_DIM
            wb_ref[dst:dst + HEAD_DIM, :] = w_ref[src:src + HEAD_DIM, :].astype(BF16)
    for r in range(ATTN_WIDTH, D_MODEL, BLOCK):
        wb_ref[r:r + BLOCK, :] = w_ref[r:r + BLOCK, :].astype(BF16)


def _out_proj_tail(merged, x_ref, gt_ref, sh_ref, sc_ref, w_ref, gpm_ref, gpf_ref, x1_ref, hf_ref,
                   wb_ref):
    @pl.when(pl.program_id(0) == 0)
    def _():
        _stage_w_out(w_ref, wb_ref)

    tm = x_ref.shape[0]
    sub = min(tm, OUT_SUB)
    for r in range(0, tm, sub):
        rows = slice(r, r + sub)
        mod = lambda ref: ref[...] if ref.shape[0] == 1 else ref[rows, :]
        o = _dot(merged(rows), wb_ref[...])
        x1 = x_ref[rows, :] + mod(gt_ref) * _rms(o, gpm_ref[...])
        x1_ref[rows, :] = x1
        hf = _rms(x1, gpf_ref[...]) * (1.0 + mod(sc_ref)) + mod(sh_ref)
        hf_ref[rows, :] = hf.astype(hf_ref.dtype)


def _out_proj_body(m_ref, *rest):
    _out_proj_tail(lambda rows: m_ref[rows, :], *rest)


def _out_proj_smp_body(w4_ref, b4_ref, a_ref, gu_ref, gv_ref, ga_ref, gg_ref, *rest, t_len):
    i = pl.program_id(0)
    gated = []
    for hd in range(GMLP_HEADS):
        acc = jnp.zeros((a_ref.shape[0], GMLP_HEAD_DIM), F32)
        for j in range(t_len):
            w = jnp.where(j <= i, w4_ref[(hd * t_len + i) * t_len + j], 0.0)
            lo = j * GMLP_WIDTH + hd * GMLP_HEAD_DIM
            acc = acc + w * gv_ref[:, lo:lo + GMLP_HEAD_DIM]
        mixed = acc + b4_ref[hd * t_len + i]
        gated.append(gu_ref[:, hd * GMLP_HEAD_DIM:(hd + 1) * GMLP_HEAD_DIM] * mixed)
    gm = jnp.concatenate(gated, axis=-1)
    merged = jnp.concatenate([_rms(a_ref[...], ga_ref[...]), _rms(gm, gg_ref[...])], axis=-1)
    merged = merged.astype(BF16)
    _out_proj_tail(lambda rows: merged[rows, :], *rest)


FFN_TF = 1024
FFN_SUB = 256


def _ffn_body(h_ref, w1_ref, w2_ref, x1_ref, gt_ref, g_ref, o_ref):
    j = pl.program_id(1)

    @pl.when(j == 0)
    def _():
        o_ref[...] = jnp.zeros_like(o_ref)

    for r in range(0, o_ref.shape[0], FFN_SUB):
        rows = slice(r, r + FFN_SUB)
        a = jnp.maximum(_dot(h_ref[rows, :], w1_ref[...]), 0.0)
        o_ref[rows, :] += _dot((a * a).astype(BF16), w2_ref[...])

    @pl.when(j == pl.num_programs(1) - 1)
    def _():
        tm = o_ref.shape[0]
        o_ref[...] = x1_ref[...] + _rows(gt_ref[...], tm) * _rms(o_ref[...], g_ref[...])


def _ffn(hf, w1, w2, x1, mod, gt_spec, g_post, tm):
    n_tok = hf.shape[0]
    return pl.pallas_call(
        _ffn_body,
        out_shape=jax.ShapeDtypeStruct((n_tok, D_MODEL), F32),
        grid=(n_tok // tm, D_FF // FFN_TF),
        in_specs=[pl.BlockSpec((tm, D_MODEL), lambda i, j: (i, 0)),
                  pl.BlockSpec((D_MODEL, FFN_TF), lambda i, j: (0, j)),
                  pl.BlockSpec((FFN_TF, D_MODEL), lambda i, j: (j, 0)),
                  pl.BlockSpec((tm, D_MODEL), lambda i, j: (i, 0)),
                  gt_spec,
                  _resident((1, D_MODEL), lambda i, j: (0, 0))],
        out_specs=pl.BlockSpec((tm, D_MODEL), lambda i, j: (i, 0)),
        compiler_params=_params(2),
        name="ffn",
    )(hf, w1, w2, x1, mod, g_post)


MIX_TM = 512
MIX_SUB = 128
OUT_TM = 512
OUT_SUB = 128
FFN_TM = 512


def _row(v):
    return v.reshape(1, -1)


def _layer(x_prompt, x_sample, cache_k, cache_v, c_prompt, c_sample, rel_table, w_ada, b_ada,
           g_pre_mix, w_in, sinks, v_gain, w_s, b_s, g_attn, g_gmlp, w_out, g_post_mix,
           g_pre_ff, w_ff1, w_ff2, g_post_ff):
    n_b, seq, _ = x_prompt.shape
    n_db, t_len, _ = x_sample.shape
    n_tok = n_b * seq
    D = D_MODEL

    w_in_b = w_in.astype(BF16)
    g_pre_mix, g_attn, g_gmlp = _row(g_pre_mix), _row(g_attn), _row(g_gmlp)
    g_post_mix, g_pre_ff, g_post_ff = _row(g_post_mix), _row(g_pre_ff), _row(g_post_ff)
    v_gain = _row(v_gain)

    bias_p, bias_s, wtril = _prep(rel_table, w_s, t_len)
    mod_s, mod_p = _adaln(c_prompt, c_sample, w_ada, b_ada)
    mod_p = mod_p.reshape(n_b, 1, 6 * D)
    SH_M, SC_M, GT_M, SH_F, SC_F, GT_F = range(6)

    def pmod(chunk, tm, n_axes=1):
        per = seq // tm
        if n_axes == 1:
            return pl.BlockSpec((None, 1, D), lambda i: (i // per, 0, chunk))
        return pl.BlockSpec((None, 1, D), lambda i, j: (i // per, 0, chunk))

    def smod(chunk, n_axes=1):
        if n_axes == 1:
            return pl.BlockSpec((n_db, D), lambda i: (0, chunk))
        return pl.BlockSpec((n_db, D), lambda i, j: (0, chunk))

    sd = _sample_dist(t_len)
    valid_s = ((sd >= 0) & (sd < WINDOW) & (np.arange(SMP_KEYS)[None, :] < WINDOW + t_len))
    valid_s = valid_s.astype(np.float32)
    sink2 = sinks.reshape(N_KV_HEADS, GROUP)
    sink_p = jnp.repeat(sink2, BLOCK, axis=1)[:, None, :]
    sink_s = jnp.tile(sink2, (1, t_len)).reshape(-1, 1)
    ga3 = g_attn.reshape(N_KV_HEADS, GROUP, HEAD_DIM)
    gain_t = jnp.repeat(ga3.transpose(0, 2, 1).reshape(KV_COLS, GROUP), BLOCK, axis=1)
    g_attn = ga3.transpose(1, 0, 2).reshape(1, ATTN_WIDTH)

    xp = x_prompt.reshape(n_tok, D)
    tile = lambda w: pl.BlockSpec((MIX_TM, w), lambda i: (i, 0))
    ttile = lambda w: pl.BlockSpec((w, MIX_TM), lambda i: (0, i))
    n_mix = n_tok // MIX_TM
    slab1 = pl.BlockSpec((D // n_mix, D_FF), lambda i: (i, 0))
    qt_p, k_p, vt_p, gu_p, gvn_p, kt_p, w1_b = _mix_in(
        xp, tile(D), pmod(SH_M, MIX_TM), pmod(SC_M, MIX_TM), mod_p, (n_tok // MIX_TM,),
        g_pre_mix, w_in_b, v_gain,
        (jax.ShapeDtypeStruct((Q_COLS, n_tok), BF16),
         jax.ShapeDtypeStruct((n_tok, KV_COLS), BF16),
         jax.ShapeDtypeStruct((KV_COLS, n_tok), F32),
         jax.ShapeDtypeStruct((n_tok, GMLP_WIDTH), F32),
         jax.ShapeDtypeStruct((n_tok, GMLP_WIDTH), BF16),
         jax.ShapeDtypeStruct((KV_COLS, n_tok), F32),
         jax.ShapeDtypeStruct((D, D_FF), BF16)),
        (ttile(Q_COLS), tile(KV_COLS), ttile(KV_COLS), tile(GMLP_WIDTH), tile(GMLP_WIDTH),
         ttile(KV_COLS), slab1),
        side=(w_ff1, slab1))

    merged_p, w2_b = _mix_core(qt_p, k_p, vt_p, gu_p, gvn_p, bias_p, sink_p,
                               wtril, b_s.T, gain_t, g_gmlp, seq, w_ff2)

    otile = lambda w: pl.BlockSpec((OUT_TM, w), lambda i: (i, 0))
    one = lambda i: (0, 0)
    x1_p, hf_p = pl.pallas_call(
        _out_proj_body,
        out_shape=(jax.ShapeDtypeStruct((n_tok, D), F32), jax.ShapeDtypeStruct((n_tok, D), BF16)),
        grid=(n_tok // OUT_TM,),
        in_specs=[otile(D), otile(D), pmod(GT_M, OUT_TM), pmod(SH_F, OUT_TM), pmod(SC_F, OUT_TM),
                  _resident((D, D), one), _resident((1, D), one), _resident((1, D), one)],
        out_specs=(otile(D), otile(D)),
        scratch_shapes=[pltpu.VMEM((D, D), BF16)],
        compiler_params=_params(1),
        name="out_proj",
    )(merged_p, xp, mod_p, mod_p, mod_p, w_out, g_post_mix, g_pre_ff)

    y_p = _ffn(hf_p, w1_b, w2_b, x1_p, mod_p, pmod(GT_F, FFN_TM, 2), g_post_ff, FFN_TM)

    xs = x_sample.reshape(n_db, t_len * D)
    lane = lambda w: pl.BlockSpec((n_db, w), lambda t: (0, t))
    q_s, k_s, v_s, gu_s, gvn_s = _mix_in(
        xs, lane(D), smod(SH_M), smod(SC_M), mod_s, (t_len,),
        g_pre_mix, w_in_b, v_gain,
        (jax.ShapeDtypeStruct((n_db, t_len * Q_COLS), F32),
         jax.ShapeDtypeStruct((n_db, t_len * KV_COLS), F32),
         jax.ShapeDtypeStruct((n_db, t_len * KV_COLS), F32),
         jax.ShapeDtypeStruct((n_db, t_len * GMLP_WIDTH), F32),
         jax.ShapeDtypeStruct((n_db, t_len * GMLP_WIDTH), F32)),
        (lane(Q_COLS), lane(KV_COLS), lane(KV_COLS), lane(GMLP_WIDTH), lane(GMLP_WIDTH)))

    q5 = q_s.reshape(n_db, t_len, N_KV_HEADS, GROUP, HEAD_DIM).transpose(0, 2, 1, 3, 4)
    q5 = q5.reshape(n_db, N_KV_HEADS, t_len * GROUP, HEAD_DIM).astype(BF16)
    z5 = jnp.zeros_like(q5[:, 0])
    q6 = jnp.concatenate([jnp.concatenate([q5[:, 0], z5], axis=-1),
                          jnp.concatenate([z5, q5[:, 1]], axis=-1)], axis=1)
    new_k = k_s.reshape(n_db, t_len, KV_COLS)
    new_v = v_s.reshape(n_db, t_len, KV_COLS)
    cache_kt = cache_k.transpose(0, 2, 3, 1).reshape(n_db, KV_COLS, WINDOW)
    cache_vt = cache_v.transpose(0, 2, 3, 1).reshape(n_db, KV_COLS, WINDOW)
    o6 = _attn_smp(q6, cache_kt, cache_vt, new_k, new_v, bias_s, jnp.asarray(valid_s), sink_s, t_len)
    o6 = o6.reshape(n_db, N_KV_HEADS, t_len, GROUP, N_KV_HEADS, HEAD_DIM)
    attn_s = jnp.stack([o6[:, kh, :, :, kh, :] for kh in range(N_KV_HEADS)], axis=3)
    attn_s = attn_s.reshape(n_db, t_len * ATTN_WIDTH)

    smem = pl.BlockSpec(memory_space=pltpu.SMEM)
    w4 = w_s[:, :t_len, :t_len].reshape(-1)
    b4 = b_s[:, :t_len].reshape(-1)
    tmaj = lambda: pl.BlockSpec((None, n_db, D), lambda t: (t, 0, 0))
    x1_s, hf_s = pl.pallas_call(
        functools.partial(_out_proj_smp_body, t_len=t_len),
        out_shape=(jax.ShapeDtypeStruct((t_len, n_db, D), F32),
                   jax.ShapeDtypeStruct((t_len, n_db, D), BF16)),
        grid=(t_len,),
        in_specs=[smem, smem, lane(ATTN_WIDTH), lane(GMLP_WIDTH),
                  _resident((n_db, t_len * GMLP_WIDTH), one),
                  _resident((1, ATTN_WIDTH), one), _resident((1, GMLP_WIDTH), one),
                  lane(D), smod(GT_M), smod(SH_F), smod(SC_F),
                  _resident((D, D), one), _resident((1, D), one), _resident((1, D), one)],
        out_specs=(tmaj(), tmaj()),
        scratch_shapes=[pltpu.VMEM((D, D), BF16)],
        compiler_params=_params(1),
        name="out_proj_smp",
    )(w4, b4, attn_s, gu_s, gvn_s, g_attn, g_gmlp, xs, mod_s, mod_s, mod_s,
      w_out, g_post_mix, g_pre_ff)

    n_st = t_len * n_db
    y_s = _ffn(hf_s.reshape(n_st, D), w1_b, w2_b, x1_s.reshape(n_st, D), mod_s,
               smod(GT_F, 2), g_post_ff, n_st)
    y_s = y_s.reshape(t_len, n_db, D).transpose(1, 0, 2)

    cw = min(WINDOW, seq)
    last = lambda t: (t.reshape(N_KV_HEADS, HEAD_DIM, n_b, seq)[:, :, :, seq - cw:]
                      .transpose(2, 3, 0, 1))
    k_p4, v_p4 = last(kt_p), last(vt_p)
    return (y_p.reshape(n_b, seq, D), y_s, k_p4, v_p4,
            k_s.reshape(n_db, t_len, N_KV_HEADS, HEAD_DIM),
            v_s.reshape(n_db, t_len, N_KV_HEADS, HEAD_DIM),
            gvn_s.reshape(n_db, t_len, GMLP_HEADS, GMLP_HEAD_DIM))


def kernel(x_prompt, x_sample, cache_k, cache_v, c_prompt, c_sample, rel_bias_table, w_ada, b_ada,
           g_pre_mix, w_in, attn_sinks, gmlp_v_gain, gmlp_w_s, gmlp_b_s, g_attn_out, g_gmlp_out,
           w_out, g_post_mix, g_pre_ff, w_ff1, w_ff2, g_post_ff):
    depth = w_in.shape[0]
    assert depth == 1, "single-layer step"
    outs = _layer(x_prompt, x_sample, cache_k[0], cache_v[0], c_prompt, c_sample, rel_bias_table,
                  w_ada[0], b_ada[0], g_pre_mix[0], w_in[0], attn_sinks[0], gmlp_v_gain[0],
                  gmlp_w_s[0], gmlp_b_s[0], g_attn_out[0], g_gmlp_out[0], w_out[0], g_post_mix[0],
                  g_pre_ff[0], w_ff1[0], w_ff2[0], g_post_ff[0])
    y_p, y_s, k_p, v_p, k_s, v_s, gv_s = outs
    return (y_p, y_s, k_p[None], v_p[None], k_s[None], v_s[None], gv_s[None])
```

```python
import functools
import math

import numpy as np
import jax
import jax.numpy as jnp
from jax import lax
from jax.experimental import pallas as pl
from jax.experimental.pallas import tpu as pltpu

D_MODEL = 2048
HEAD_DIM = 64
ATTN_WIDTH = 1024
N_HEADS = 16
N_KV_HEADS = 2
GROUP = 8
WINDOW = 128
BLOCK = 128
GMLP_WIDTH = 1024
GMLP_HEADS = 8
GMLP_HEAD_DIM = 128
D_FF = 4 * D_MODEL
NUM_BUCKETS = 32
MAX_DISTANCE = 128
EPS = 1e-6
KV_COLS = N_KV_HEADS * HEAD_DIM
Q_COLS = N_HEADS * HEAD_DIM
IN_COLS = Q_COLS + 2 * KV_COLS + 2 * GMLP_WIDTH
ATTN_SCALE = HEAD_DIM ** -0.5
NEG_INF = -1e30

V7X_VMEM_BYTES = 64 * 1024 * 1024
VMEM_LIMIT_BYTES = 56 * 1024 * 1024

BF16 = jnp.bfloat16
F32 = jnp.float32


def _params(n_axes, vmem=VMEM_LIMIT_BYTES):
    return pltpu.CompilerParams(
        dimension_semantics=("arbitrary",) * n_axes, vmem_limit_bytes=vmem)


def _resident(shape, index_map):
    return pl.BlockSpec(shape, index_map, pipeline_mode=pl.Buffered(1))


def _rms(x, gain):
    return x * lax.rsqrt(jnp.mean(x * x, axis=-1, keepdims=True) + EPS) * gain


def _rows(m, n_rows):
    r = m.shape[0]
    if r == 1 or r == n_rows:
        return m
    return jnp.concatenate([m] * (n_rows // r), axis=0)


def _dot(a, b):
    return jnp.dot(a, b, preferred_element_type=F32)


def _dot_nt(a, b):
    return lax.dot_general(a, b, (((1,), (1,)), ((), ())), preferred_element_type=F32)


def _t5_bucket(dist):
    n = jnp.maximum(dist, 0)
    max_exact = NUM_BUCKETS // 2
    nf = jnp.maximum(n, 1).astype(jnp.float32)
    large = max_exact + (jnp.log(nf / max_exact) / math.log(MAX_DISTANCE / max_exact)
                         * (NUM_BUCKETS - max_exact)).astype(jnp.int32)
    large = jnp.minimum(large, NUM_BUCKETS - 1)
    return jnp.where(n < max_exact, n, large)


SMP_KEYS = 2 * WINDOW


def _prompt_dist():
    r = np.arange(BLOCK)[:, None]
    i = np.arange(BLOCK)[None, :]
    return (i - r) % BLOCK


def _sample_rows(t_len):
    r = np.arange(N_KV_HEADS * t_len * GROUP)
    return r // (t_len * GROUP), (r // GROUP) % t_len, r % GROUP


def _sample_dist(t_len):
    _, t, _ = _sample_rows(t_len)
    return WINDOW + t[:, None] - np.arange(SMP_KEYS)[None, :]


def _prep_body(tab_ref, tabr_ref, bkt_p_ref, bkt_s_ref, ws_ref,
               bias_p_ref, bias_s_ref, wtril_ref):
    bkt_p = bkt_p_ref[...]
    for kh in range(N_KV_HEADS):
        for g in range(GROUP):
            h = kh * GROUP + g
            acc = jnp.zeros((BLOCK, BLOCK), F32)
            for b in range(NUM_BUCKETS):
                acc = jnp.where(bkt_p == b, tab_ref[b, h], acc)
            bias_p_ref[kh, :, g * BLOCK:(g + 1) * BLOCK] = acc
    bkt_s = bkt_s_ref[...]
    tabr = tabr_ref[...]
    acc = jnp.zeros(bkt_s.shape, F32)
    for b in range(NUM_BUCKETS):
        acc = jnp.where(bkt_s == b, tabr[:, b:b + 1], acc)
    bias_s_ref[...] = acc
    ii = lax.broadcasted_iota(jnp.int32, (BLOCK, BLOCK), 0)
    jj = lax.broadcasted_iota(jnp.int32, (BLOCK, BLOCK), 1)
    for h in range(GMLP_HEADS):
        wtril_ref[h] = jnp.where(jj <= ii, ws_ref[h], 0.0).astype(BF16)


def _prep(rel_table, w_s, t_len):
    bkt_p = _t5_bucket(jnp.asarray(_prompt_dist(), jnp.int32))
    bkt_s = _t5_bucket(jnp.asarray(_sample_dist(t_len), jnp.int32))
    n_rows = bkt_s.shape[0]
    tab_rows = jnp.broadcast_to(rel_table.T.reshape(N_KV_HEADS, 1, GROUP, NUM_BUCKETS),
                                (N_KV_HEADS, t_len, GROUP, NUM_BUCKETS)).reshape(n_rows, NUM_BUCKETS)
    smem = pl.BlockSpec(memory_space=pltpu.SMEM)
    vmem = pl.BlockSpec(memory_space=pltpu.VMEM)
    return pl.pallas_call(
        _prep_body,
        out_shape=(jax.ShapeDtypeStruct((N_KV_HEADS, BLOCK, GROUP * BLOCK), F32),
                   jax.ShapeDtypeStruct((n_rows, SMP_KEYS), F32),
                   jax.ShapeDtypeStruct((GMLP_HEADS, BLOCK, BLOCK), BF16)),
        in_specs=[smem, vmem, vmem, vmem, vmem],
        out_specs=(vmem, vmem, vmem),
        compiler_params=pltpu.CompilerParams(vmem_limit_bytes=32 * 1024 * 1024),
        name="prep",
    )(rel_table, tab_rows, bkt_p, bkt_s, w_s)


ADA_TN = 1024


def _adaln_body(c_ref, w_ref, b_ref, os_ref, op_ref, *, n_s, n_p):
    c = c_ref[...]
    s = (c * jax.nn.sigmoid(c)).astype(BF16)
    r = _dot(s, w_ref[...].astype(BF16)) + b_ref[...]
    os_ref[...] = r[:n_s]
    op_ref[...] = r[n_s:n_s + n_p]


def _adaln(c_prompt, c_sample, w_ada, b_ada):
    n_p, n_s = c_prompt.shape[0], c_sample.shape[0]
    pad = (-(n_p + n_s)) % 16
    c_all = jnp.concatenate([c_sample, c_prompt, jnp.zeros((pad, D_MODEL), F32)], axis=0)
    n_all = c_all.shape[0]
    n_out = w_ada.shape[1]
    return pl.pallas_call(
        functools.partial(_adaln_body, n_s=n_s, n_p=n_p),
        out_shape=(jax.ShapeDtypeStruct((n_s, n_out), F32),
                   jax.ShapeDtypeStruct((n_p, n_out), F32)),
        grid=(n_out // ADA_TN,),
        in_specs=[_resident((n_all, D_MODEL), lambda j: (0, 0)),
                  pl.BlockSpec((D_MODEL, ADA_TN), lambda j: (0, j)),
                  pl.BlockSpec((1, ADA_TN), lambda j: (0, j))],
        out_specs=(pl.BlockSpec((n_s, ADA_TN), lambda j: (0, j)),
                   pl.BlockSpec((n_p, ADA_TN), lambda j: (0, j))),
        compiler_params=_params(1, 40 * 1024 * 1024),
        name="adaln",
    )(c_all, w_ada, b_ada.reshape(1, n_out))


def _mix_in_body(x_ref, sh_ref, sc_ref, g_ref, w_ref, vg_ref, *refs, transposed):
    if transposed:
        wc_ref, q_ref, k_ref, v_ref, gu_ref, gv_ref, kt_ref, wcb_ref = refs
        wcb_ref[...] = wc_ref[...].astype(wcb_ref.dtype)
    else:
        q_ref, k_ref, v_ref, gu_ref, gv_ref = refs
    tm = x_ref.shape[0]
    sub = min(tm, MIX_SUB)
    c0, c1, c2, c3 = Q_COLS, Q_COLS + KV_COLS, Q_COLS + 2 * KV_COLS, Q_COLS + 2 * KV_COLS + GMLP_WIDTH
    vg = vg_ref[...]
    for r in range(0, tm, sub):
        rows = slice(r, r + sub)
        mod = lambda ref: ref[...] if ref.shape[0] == 1 else ref[rows, :]
        h = (_rms(x_ref[rows, :], g_ref[...]) * (1.0 + mod(sc_ref)) + mod(sh_ref)).astype(BF16)
        q = _dot(h, w_ref[:, :c0]) * ATTN_SCALE
        kv = _dot(h, w_ref[:, c0:c2])
        k_ref[rows, :] = kv[:, :KV_COLS].astype(k_ref.dtype)
        if transposed:
            q_ref[:, rows] = q.T.astype(q_ref.dtype)
            kt_ref[:, rows] = kv[:, :KV_COLS].T
            v_ref[:, rows] = kv[:, KV_COLS:].T
        else:
            q_ref[rows, :] = q.astype(q_ref.dtype)
            v_ref[rows, :] = kv[:, KV_COLS:]
        gu_ref[rows, :] = jax.nn.gelu(_dot(h, w_ref[:, c2:c3])).astype(gu_ref.dtype)
        gv = jax.nn.gelu(_dot(h, w_ref[:, c3:]))
        for hd in range(GMLP_HEADS):
            sl = slice(hd * GMLP_HEAD_DIM, (hd + 1) * GMLP_HEAD_DIM)
            gv_ref[rows, sl] = _rms(gv[:, sl], vg[:, sl]).astype(gv_ref.dtype)


def _mix_in(x, x_spec, sh_spec, sc_spec, mod, grid, g_pre, w_in, v_gain, outs, out_specs,
            side=None):
    extra, extra_specs = ((), ()) if side is None else ((side[0],), (side[1],))
    return pl.pallas_call(
        functools.partial(_mix_in_body, transposed=side is not None),
        out_shape=outs,
        grid=grid,
        in_specs=[x_spec, sh_spec, sc_spec,
                  _resident((1, D_MODEL), lambda i: (0, 0)),
                  _resident((D_MODEL, IN_COLS), lambda i: (0, 0)),
                  _resident((1, GMLP_WIDTH), lambda i: (0, 0)), *extra_specs],
        out_specs=out_specs,
        compiler_params=_params(1),
        name="mix_in",
    )(x, mod, mod, g_pre, w_in, v_gain, *extra)


CORE_NB = 4


def _mix_core_body(qt_ref, kc_ref, kp_ref, vtc_ref, vtp_ref, gu_ref, gv_ref,
                   bias_ref, pm_ref, sink_ref, wtril_ref, bs_ref, gat_ref, gg_ref, wc_ref,
                   o_ref, wcb_ref, *, steps_per_seq):
    wcb_ref[...] = wc_ref[...].astype(wcb_ref.dtype)
    first = pl.program_id(0) % steps_per_seq == 0
    tri = (lax.broadcasted_iota(jnp.int32, (BLOCK, BLOCK), 0)
           > lax.broadcasted_iota(jnp.int32, (BLOCK, BLOCK), 1))
    from_prev = jnp.concatenate([tri] * GROUP, axis=1)
    zeros = jnp.zeros((HEAD_DIM, GROUP * BLOCK), BF16)
    bs = bs_ref[...]
    for u in range(CORE_NB):
        tok = slice(u * BLOCK, (u + 1) * BLOCK)
        k_prev = kp_ref[...] if u == 0 else kc_ref[(u - 1) * BLOCK:u * BLOCK, :]
        vt_prev = vtp_ref[...] if u == 0 else vtc_ref[:, (u - 1) * BLOCK:u * BLOCK]
        k = jnp.concatenate([k_prev, kc_ref[tok, :]], axis=0)
        vt = jnp.concatenate([vt_prev, vtc_ref[:, tok]], axis=1).astype(BF16)
        outs = []
        for kh in range(N_KV_HEADS):
            base = kh * GROUP * HEAD_DIM
            qs = jnp.concatenate(
                [qt_ref[base + g * HEAD_DIM: base + (g + 1) * HEAD_DIM, tok] for g in range(GROUP)],
                axis=1)
            qp = jnp.concatenate([qs, zeros] if kh == 0 else [zeros, qs], axis=0)
            qk = _dot(k, qp)
            s = jnp.where(from_prev, qk[:BLOCK], qk[BLOCK:]) + bias_ref[kh]
            if u == 0:
                s = jnp.where(jnp.logical_and(first, from_prev), NEG_INF, s)
            sink = sink_ref[kh]
            m = jnp.maximum(jnp.max(s, axis=0, keepdims=True), sink)
            p = jnp.exp(s - m)
            den = jnp.sum(p, axis=0, keepdims=True) + jnp.exp(sink - m)
            p = p.astype(BF16)
            p2 = jnp.concatenate([p * pm_ref[0], p * pm_ref[1]], axis=0)
            outs.append(_dot(vt[kh * HEAD_DIM:(kh + 1) * HEAD_DIM, :], p2) / den)
        ot = jnp.concatenate(outs, axis=0)
        sq = jnp.sum(ot * ot, axis=0, keepdims=True)
        tot = sq[:, :BLOCK]
        for g in range(1, GROUP):
            tot = tot + sq[:, g * BLOCK:(g + 1) * BLOCK]
        r = lax.rsqrt(tot / ATTN_WIDTH + EPS)
        a = ot * jnp.concatenate([r] * GROUP, axis=1) * gat_ref[...]
        for g in range(GROUP):
            o_ref[tok, g * BLOCK:(g + 1) * BLOCK] = (
                a[:, g * BLOCK:(g + 1) * BLOCK].T.astype(o_ref.dtype))
        gated = []
        for hd in range(GMLP_HEADS):
            sl = slice(hd * GMLP_HEAD_DIM, (hd + 1) * GMLP_HEAD_DIM)
            mixed = _dot(wtril_ref[hd], gv_ref[tok, sl]) + bs[:, hd:hd + 1]
            gated.append(gu_ref[tok, sl] * mixed)
        gm = jnp.concatenate(gated, axis=-1)
        o_ref[tok, ATTN_WIDTH:] = _rms(gm, gg_ref[...]).astype(o_ref.dtype)


def _mix_core(qt, k, vt, gu, gvn, bias_t, sink_row, wtril, bs_t, gain_t, g_gmlp, seq, w_side):
    r_gt_i = np.tile(np.arange(BLOCK)[:, None] > np.arange(BLOCK)[None, :], (1, GROUP))
    pmask = jnp.asarray(np.stack([r_gt_i, ~r_gt_i]), BF16)
    n_tok = k.shape[0]
    tm = CORE_NB * BLOCK
    n_steps = n_tok // tm
    side_rows = w_side.shape[0] // n_steps
    cur = lambda i: (i, 0)
    prev = lambda i: (jnp.maximum(i * CORE_NB - 1, 0), 0)
    cur_t = lambda i: (0, i)
    prev_t = lambda i: (0, jnp.maximum(i * CORE_NB - 1, 0))
    full2 = lambda i: (0, 0)
    full3 = lambda i: (0, 0, 0)
    return pl.pallas_call(
        functools.partial(_mix_core_body, steps_per_seq=seq // tm),
        out_shape=(jax.ShapeDtypeStruct((n_tok, D_MODEL), BF16),
                   jax.ShapeDtypeStruct(w_side.shape, BF16)),
        grid=(n_steps,),
        in_specs=[pl.BlockSpec((Q_COLS, tm), cur_t),
                  pl.BlockSpec((tm, KV_COLS), cur), pl.BlockSpec((BLOCK, KV_COLS), prev),
                  pl.BlockSpec((KV_COLS, tm), cur_t), pl.BlockSpec((KV_COLS, BLOCK), prev_t),
                  pl.BlockSpec((tm, GMLP_WIDTH), cur), pl.BlockSpec((tm, GMLP_WIDTH), cur),
                  _resident(bias_t.shape, full3), _resident(pmask.shape, full3),
                  _resident(sink_row.shape, full3), _resident(wtril.shape, full3),
                  _resident(bs_t.shape, full2),
                  _resident(gain_t.shape, full2), _resident((1, GMLP_WIDTH), full2),
                  pl.BlockSpec((side_rows, w_side.shape[1]), cur)],
        out_specs=(pl.BlockSpec((tm, D_MODEL), cur),
                   pl.BlockSpec((side_rows, w_side.shape[1]), cur)),
        compiler_params=_params(1),
        name="mix_core",
    )(qt, k, k, vt, vt, gu, gvn, bias_t, pmask, sink_row, wtril, bs_t, gain_t, g_gmlp, w_side)


SMP_BB = 16


def _attn_smp_body(q_ref, ckt_ref, cvt_ref, nk_ref, nv_ref, bias_ref, valid_ref, sink_ref,
                   o_ref, *, t_len):
    zpad = jnp.zeros((WINDOW - t_len, KV_COLS), F32)
    tiles = []
    for b in range(SMP_BB):
        q = q_ref[b]
        s_cache = _dot(q, ckt_ref[b].astype(BF16))
        kn = jnp.concatenate([nk_ref[b], zpad], axis=0).astype(BF16)
        tiles.append(jnp.concatenate([s_cache, _dot_nt(q, kn)], axis=1)[None])
    s = jnp.concatenate(tiles, axis=0)
    s = jnp.where(valid_ref[...] > 0.5, s + bias_ref[...], NEG_INF)
    sink = sink_ref[...]
    m = jnp.maximum(jnp.max(s, axis=-1, keepdims=True), sink)
    p = jnp.exp(s - m)
    den = jnp.sum(p, axis=-1, keepdims=True) + jnp.exp(sink - m)
    p = p.astype(BF16)
    for b in range(SMP_BB):
        vn = jnp.concatenate([nv_ref[b], zpad], axis=0).astype(BF16)
        o = _dot_nt(p[b, :, :WINDOW], cvt_ref[b].astype(BF16)) + _dot(p[b, :, WINDOW:], vn)
        o_ref[b] = o / den[b]


def _attn_smp(q6, cache_kt, cache_vt, new_k, new_v, bias_s, valid_s, sink_s, t_len):
    n_b, rows, _ = q6.shape
    b3 = lambda i: (i, 0, 0)
    one = lambda i: (0, 0)
    return pl.pallas_call(
        functools.partial(_attn_smp_body, t_len=t_len),
        out_shape=jax.ShapeDtypeStruct((n_b, rows, KV_COLS), F32),
        grid=(n_b // SMP_BB,),
        in_specs=[pl.BlockSpec((SMP_BB, rows, KV_COLS), b3),
                  pl.BlockSpec((SMP_BB, KV_COLS, WINDOW), b3),
                  pl.BlockSpec((SMP_BB, KV_COLS, WINDOW), b3),
                  pl.BlockSpec((SMP_BB, t_len, KV_COLS), b3),
                  pl.BlockSpec((SMP_BB, t_len, KV_COLS), b3),
                  _resident(bias_s.shape, one), _resident(valid_s.shape, one),
                  _resident(sink_s.shape, one)],
        out_specs=pl.BlockSpec((SMP_BB, rows, KV_COLS), b3),
        compiler_params=_params(1, 32 * 1024 * 1024),
        name="attn_smp",
    )(q6, cache_kt, cache_vt, new_k, new_v, bias_s, valid_s, sink_s)


def _stage_w_out(w_ref, wb_ref):
    for kh in range(N_KV_HEADS):
        for g in range(GROUP):
            src = (kh * GROUP + g) * HEAD_DIM
            dst = (g * N_KV_HEADS + kh) * HEAD_DIM
            wb_ref[dst:dst + HEAD_DIM, :] = w_ref[src:src + HEAD_DIM, :].astype(BF16)
    for r in range(ATTN_WIDTH, D_MODEL, BLOCK):
        wb_ref[r:r + BLOCK, :] = w_ref[r:r + BLOCK, :].astype(BF16)


def _out_proj_tail(merged, x_ref, gt_ref, sh_ref, sc_ref, w_ref, gpm_ref, gpf_ref, x1_ref, hf_ref,
                   wb_ref):
    @pl.when(pl.program_id(0) == 0)
    def _():
        _stage_w_out(w_ref, wb_ref)

    tm = x_ref.shape[0]
    sub = min(tm, OUT_SUB)
    for r in range(0, tm, sub):
        rows = slice(r, r + sub)
        mod = lambda ref: ref[...] if ref.shape[0] == 1 else ref[rows, :]
        o = _dot(merged(rows), wb_ref[...])
        x1 = x_ref[rows, :] + mod(gt_ref) * _rms(o, gpm_ref[...])
        x1_ref[rows, :] = x1
        hf = _rms(x1, gpf_ref[...]) * (1.0 + mod(sc_ref)) + mod(sh_ref)
        hf_ref[rows, :] = hf.astype(hf_ref.dtype)


def _out_proj_body(m_ref, *rest):
    _out_proj_tail(lambda rows: m_ref[rows, :], *rest)


def _out_proj_smp_body(w4_ref, b4_ref, a_ref, gu_ref, gv_ref, ga_ref, gg_ref, *rest, t_len):
    i = pl.program_id(0)
    gated = []
    for hd in range(GMLP_HEADS):
        acc = jnp.zeros((a_ref.shape[0], GMLP_HEAD_DIM), F32)
        for j in range(t_len):
            w = jnp.where(j <= i, w4_ref[(hd * t_len + i) * t_len + j], 0.0)
            lo = j * GMLP_WIDTH + hd * GMLP_HEAD_DIM
            acc = acc + w * gv_ref[:, lo:lo + GMLP_HEAD_DIM]
        mixed = acc + b4_ref[hd * t_len + i]
        gated.append(gu_ref[:, hd * GMLP_HEAD_DIM:(hd + 1) * GMLP_HEAD_DIM] * mixed)
    gm = jnp.concatenate(gated, axis=-1)
    merged = jnp.concatenate([_rms(a_ref[...], ga_ref[...]), _rms(gm, gg_ref[...])], axis=-1)
    merged = merged.astype(BF16)
    _out_proj_tail(lambda rows: merged[rows, :], *rest)


FFN_TF = 1024
FFN_SUB = 256


def _ffn_body(h_ref, w1_ref, w2_ref, x1_ref, gt_ref, g_ref, o_ref, acc_ref, done_ref, *,
              n_tiles, n_chunks):
    s = pl.program_id(0)
    i = s // n_chunks
    j = s % n_chunks
    tm = o_ref.shape[0]
    whole = slice(0, tm)

    def contribution(rows):
        a = jnp.maximum(_dot(h_ref[rows, :], w1_ref[...]), 0.0)
        return _dot((a * a).astype(BF16), w2_ref[...])

    def finalize(rows):
        gt = gt_ref[...] if gt_ref.shape[0] == 1 else _rows(gt_ref[...], tm)[rows, :]
        o_ref[rows, :] = x1_ref[rows, :] + gt * _rms(done_ref[rows, :], g_ref[...])

    @pl.when(s == 0)
    def _():
        acc_ref[...] = contribution(whole)

    @pl.when(jnp.logical_and(j == 0, jnp.logical_and(i > 0, i < n_tiles)))
    def _():
        for r in range(0, tm, FFN_SUB):
            rows = slice(r, r + FFN_SUB)
            acc_ref[rows, :] = contribution(rows)
            finalize(rows)

    @pl.when(jnp.logical_and(j > 0, j < n_chunks - 1))
    def _():
        acc_ref[...] += contribution(whole)

    @pl.when(jnp.logical_and(j == n_chunks - 1, i < n_tiles))
    def _():
        done_ref[...] = acc_ref[...] + contribution(whole)

    @pl.when(i == n_tiles)
    def _():
        finalize(whole)


def _ffn(hf, w1, w2, x1, mod, gt_spec, g_post, tm):
    n_tok = hf.shape[0]
    n_tiles, n_chunks = n_tok // tm, D_FF // FFN_TF
    last = n_tiles * n_chunks
    mm_tile = lambda s: jnp.minimum(s // n_chunks, n_tiles - 1)
    chunk = lambda s: jnp.where(s == last, n_chunks - 1, s % n_chunks)
    out_tile = lambda s: jnp.maximum(s // n_chunks - (s % n_chunks == 0).astype(jnp.int32), 0)
    return pl.pallas_call(
        functools.partial(_ffn_body, n_tiles=n_tiles, n_chunks=n_chunks),
        out_shape=jax.ShapeDtypeStruct((n_tok, D_MODEL), F32),
        grid=(last + 1,),
        in_specs=[pl.BlockSpec((tm, D_MODEL), lambda s: (mm_tile(s), 0)),
                  pl.BlockSpec((D_MODEL, FFN_TF), lambda s: (0, chunk(s))),
                  pl.BlockSpec((FFN_TF, D_MODEL), lambda s: (chunk(s), 0)),
                  pl.BlockSpec((tm, D_MODEL), lambda s: (out_tile(s), 0)),
                  gt_spec(out_tile),
                  _resident((1, D_MODEL), lambda s: (0, 0))],
        out_specs=pl.BlockSpec((tm, D_MODEL), lambda s: (out_tile(s), 0)),
        scratch_shapes=[pltpu.VMEM((tm, D_MODEL), F32), pltpu.VMEM((tm, D_MODEL), F32)],
        compiler_params=_params(1),
        name="ffn",
    )(hf, w1, w2, x1, mod, g_post)


MIX_TM = 512
MIX_SUB = 128
OUT_TM = 512
OUT_SUB = 128
FFN_TM = 512


def _row(v):
    return v.reshape(1, -1)


def _layer(x_prompt, x_sample, cache_k, cache_v, c_prompt, c_sample, rel_table, w_ada, b_ada,
           g_pre_mix, w_in, sinks, v_gain, w_s, b_s, g_attn, g_gmlp, w_out, g_post_mix,
           g_pre_ff, w_ff1, w_ff2, g_post_ff):
    n_b, seq, _ = x_prompt.shape
    n_db, t_len, _ = x_sample.shape
    n_tok = n_b * seq
    D = D_MODEL

    w_in_b = w_in.astype(BF16)
    g_pre_mix, g_attn, g_gmlp = _row(g_pre_mix), _row(g_attn), _row(g_gmlp)
    g_post_mix, g_pre_ff, g_post_ff = _row(g_post_mix), _row(g_pre_ff), _row(g_post_ff)
    v_gain = _row(v_gain)

    bias_p, bias_s, wtril = _prep(rel_table, w_s, t_len)
    mod_s, mod_p = _adaln(c_prompt, c_sample, w_ada, b_ada)
    mod_p = mod_p.reshape(n_b, 1, 6 * D)
    SH_M, SC_M, GT_M, SH_F, SC_F, GT_F = range(6)

    def pmod(chunk, tm):
        per = seq // tm
        return pl.BlockSpec((None, 1, D), lambda i: (i // per, 0, chunk))

    def smod(chunk):
        return pl.BlockSpec((n_db, D), lambda i: (0, chunk))

    sd = _sample_dist(t_len)
    valid_s = ((sd >= 0) & (sd < WINDOW) & (np.arange(SMP_KEYS)[None, :] < WINDOW + t_len))
    valid_s = valid_s.astype(np.float32)
    sink2 = sinks.reshape(N_KV_HEADS, GROUP)
    sink_p = jnp.repeat(sink2, BLOCK, axis=1)[:, None, :]
    sink_s = jnp.tile(sink2, (1, t_len)).reshape(-1, 1)
    ga3 = g_attn.reshape(N_KV_HEADS, GROUP, HEAD_DIM)
    gain_t = jnp.repeat(ga3.transpose(0, 2, 1).reshape(KV_COLS, GROUP), BLOCK, axis=1)
    g_attn = ga3.transpose(1, 0, 2).reshape(1, ATTN_WIDTH)

    xp = x_prompt.reshape(n_tok, D)
    tile = lambda w: pl.BlockSpec((MIX_TM, w), lambda i: (i, 0))
    ttile = lambda w: pl.BlockSpec((w, MIX_TM), lambda i: (0, i))
    n_mix = n_tok // MIX_TM
    slab1 = pl.BlockSpec((D // n_mix, D_FF), lambda i: (i, 0))
    qt_p, k_p, vt_p, gu_p, gvn_p, kt_p, w1_b = _mix_in(
        xp, tile(D), pmod(SH_M, MIX_TM), pmod(SC_M, MIX_TM), mod_p, (n_tok // MIX_TM,),
        g_pre_mix, w_in_b, v_gain,
        (jax.ShapeDtypeStruct((Q_COLS, n_tok), BF16),
         jax.ShapeDtypeStruct((n_tok, KV_COLS), BF16),
         jax.ShapeDtypeStruct((KV_COLS, n_tok), F32),
         jax.ShapeDtypeStruct((n_tok, GMLP_WIDTH), F32),
         jax.ShapeDtypeStruct((n_tok, GMLP_WIDTH), BF16),
         jax.ShapeDtypeStruct((KV_COLS, n_tok), F32),
         jax.ShapeDtypeStruct((D, D_FF), BF16)),
        (ttile(Q_COLS), tile(KV_COLS), ttile(KV_COLS), tile(GMLP_WIDTH), tile(GMLP_WIDTH),
         ttile(KV_COLS), slab1),
        side=(w_ff1, slab1))

    merged_p, w2_b = _mix_core(qt_p, k_p, vt_p, gu_p, gvn_p, bias_p, sink_p,
                               wtril, b_s.T, gain_t, g_gmlp, seq, w_ff2)

    otile = lambda w: pl.BlockSpec((OUT_TM, w), lambda i: (i, 0))
    one = lambda i: (0, 0)
    x1_p, hf_p = pl.pallas_call(
        _out_proj_body,
        out_shape=(jax.ShapeDtypeStruct((n_tok, D), F32), jax.ShapeDtypeStruct((n_tok, D), BF16)),
        grid=(n_tok // OUT_TM,),
        in_specs=[otile(D), otile(D), pmod(GT_M, OUT_TM), pmod(SH_F, OUT_TM), pmod(SC_F, OUT_TM),
                  _resident((D, D), one), _resident((1, D), one), _resident((1, D), one)],
        out_specs=(otile(D), otile(D)),
        scratch_shapes=[pltpu.VMEM((D, D), BF16)],
        compiler_params=_params(1),
        name="out_proj",
    )(merged_p, xp, mod_p, mod_p, mod_p, w_out, g_post_mix, g_pre_ff)

    ffn_per = seq // FFN_TM
    gate_p = lambda tile: pl.BlockSpec((None, 1, D), lambda s: (tile(s) // ffn_per, 0, GT_F))
    y_p = _ffn(hf_p, w1_b, w2_b, x1_p, mod_p, gate_p, g_post_ff, FFN_TM)

    xs = x_sample.reshape(n_db, t_len * D)
    lane = lambda w: pl.BlockSpec((n_db, w), lambda t: (0, t))
    q_s, k_s, v_s, gu_s, gvn_s = _mix_in(
        xs, lane(D), smod(SH_M), smod(SC_M), mod_s, (t_len,),
        g_pre_mix, w_in_b, v_gain,
        (jax.ShapeDtypeStruct((n_db, t_len * Q_COLS), F32),
         jax.ShapeDtypeStruct((n_db, t_len * KV_COLS), F32),
         jax.ShapeDtypeStruct((n_db, t_len * KV_COLS), F32),
         jax.ShapeDtypeStruct((n_db, t_len * GMLP_WIDTH), F32),
         jax.ShapeDtypeStruct((n_db, t_len * GMLP_WIDTH), F32)),
        (lane(Q_COLS), lane(KV_COLS), lane(KV_COLS), lane(GMLP_WIDTH), lane(GMLP_WIDTH)))

    q5 = q_s.reshape(n_db, t_len, N_KV_HEADS, GROUP, HEAD_DIM).transpose(0, 2, 1, 3, 4)
    q5 = q5.reshape(n_db, N_KV_HEADS, t_len * GROUP, HEAD_DIM).astype(BF16)
    z5 = jnp.zeros_like(q5[:, 0])
    q6 = jnp.concatenate([jnp.concatenate([q5[:, 0], z5], axis=-1),
                          jnp.concatenate([z5, q5[:, 1]], axis=-1)], axis=1)
    new_k = k_s.reshape(n_db, t_len, KV_COLS)
    new_v = v_s.reshape(n_db, t_len, KV_COLS)
    cache_kt = cache_k.transpose(0, 2, 3, 1).reshape(n_db, KV_COLS, WINDOW)
    cache_vt = cache_v.transpose(0, 2, 3, 1).reshape(n_db, KV_COLS, WINDOW)
    o6 = _attn_smp(q6, cache_kt, cache_vt, new_k, new_v, bias_s, jnp.asarray(valid_s), sink_s, t_len)
    o6 = o6.reshape(n_db, N_KV_HEADS, t_len, GROUP, N_KV_HEADS, HEAD_DIM)
    attn_s = jnp.stack([o6[:, kh, :, :, kh, :] for kh in range(N_KV_HEADS)], axis=3)
    attn_s = attn_s.reshape(n_db, t_len * ATTN_WIDTH)

    smem = pl.BlockSpec(memory_space=pltpu.SMEM)
    w4 = w_s[:, :t_len, :t_len].reshape(-1)
    b4 = b_s[:, :t_len].reshape(-1)
    tmaj = lambda: pl.BlockSpec((None, n_db, D), lambda t: (t, 0, 0))
    x1_s, hf_s = pl.pallas_call(
        functools.partial(_out_proj_smp_body, t_len=t_len),
        out_shape=(jax.ShapeDtypeStruct((t_len, n_db, D), F32),
                   jax.ShapeDtypeStruct((t_len, n_db, D), BF16)),
        grid=(t_len,),
        in_specs=[smem, smem, lane(ATTN_WIDTH), lane(GMLP_WIDTH),
                  _resident((n_db, t_len * GMLP_WIDTH), one),
                  _resident((1, ATTN_WIDTH), one), _resident((1, GMLP_WIDTH), one),
                  lane(D), smod(GT_M), smod(SH_F), smod(SC_F),
                  _resident((D, D), one), _resident((1, D), one), _resident((1, D), one)],
        out_specs=(tmaj(), tmaj()),
        scratch_shapes=[pltpu.VMEM((D, D), BF16)],
        compiler_params=_params(1),
        name="out_proj_smp",
    )(w4, b4, attn_s, gu_s, gvn_s, g_attn, g_gmlp, xs, mod_s, mod_s, mod_s,
      w_out, g_post_mix, g_pre_ff)

    n_st = t_len * n_db
    y_s = _ffn(hf_s.reshape(n_st, D), w1_b, w2_b, x1_s.reshape(n_st, D), mod_s,
               lambda tile: smod(GT_F), g_post_ff, n_st)
    y_s = y_s.reshape(t_len, n_db, D).transpose(1, 0, 2)

    cw = min(WINDOW, seq)
    last = lambda t: (t.reshape(N_KV_HEADS, HEAD_DIM, n_b, seq)[:, :, :, seq - cw:]
                      .transpose(2, 3, 0, 1))
    k_p4, v_p4 = last(kt_p), last(vt_p)
    return (y_p.reshape(n_b, seq, D), y_s, k_p4, v_p4,
            k_s.reshape(n_db, t_len, N_KV_HEADS, HEAD_DIM),
            v_s.reshape(n_db, t_len, N_KV_HEADS, HEAD_DIM),
            gvn_s.reshape(n_db, t_len, GMLP_HEADS, GMLP_HEAD_DIM))


def kernel(x_prompt, x_sample, cache_k, cache_v, c_prompt, c_sample, rel_bias_table, w_ada, b_ada,
           g_pre_mix, w_in, attn_sinks, gmlp_v_gain, gmlp_w_s, gmlp_b_s, g_attn_out, g_gmlp_out,
           w_out, g_post_mix, g_pre_ff, w_ff1, w_ff2, g_post_ff):
    depth = w_in.shape[0]
    assert depth == 1, "single-layer step"
    outs = _layer(x_prompt, x_sample, cache_k[0], cache_v[0], c_prompt, c_sample, rel_bias_table,
                  w_ada[0], b_ada[0], g_pre_mix[0], w_in[0], attn_sinks[0], gmlp_v_gain[0],
                  gmlp_w_s[0], gmlp_b_s[0], g_attn_out[0], g_gmlp_out[0], w_out[0], g_post_mix[0],
                  g_pre_ff[0], w_ff1[0], w_ff2[0], g_post_ff[0])
    y_p, y_s, k_p, v_p, k_s, v_s, gv_s = outs
    return (y_p, y_s, k_p[None], v_p[None], k_s[None], v_s[None], gv_s[None])
```

```python
import functools
import math

import numpy as np
import jax
import jax.numpy as jnp
from jax import lax
from jax.experimental import pallas as pl
from jax.experimental.pallas import tpu as pltpu

D_MODEL = 2048
HEAD_DIM = 64
ATTN_WIDTH = 1024
N_HEADS = 16
N_KV_HEADS = 2
GROUP = 8
WINDOW = 128
BLOCK = 128
GMLP_WIDTH = 1024
GMLP_HEADS = 8
GMLP_HEAD_DIM = 128
D_FF = 4 * D_MODEL
NUM_BUCKETS = 32
MAX_DISTANCE = 128
EPS = 1e-6
KV_COLS = N_KV_HEADS * HEAD_DIM
Q_COLS = N_HEADS * HEAD_DIM
IN_COLS = Q_COLS + 2 * KV_COLS + 2 * GMLP_WIDTH
ATTN_SCALE = HEAD_DIM ** -0.5
NEG_INF = -1e30

V7X_VMEM_BYTES = 64 * 1024 * 1024
VMEM_LIMIT_BYTES = 56 * 1024 * 1024

BF16 = jnp.bfloat16
F32 = jnp.float32


def _params(n_axes, vmem=VMEM_LIMIT_BYTES):
    return pltpu.CompilerParams(
        dimension_semantics=("arbitrary",) * n_axes, vmem_limit_bytes=vmem)


def _resident(shape, index_map):
    return pl.BlockSpec(shape, index_map, pipeline_mode=pl.Buffered(1))


def _rms(x, gain):
    return x * lax.rsqrt(jnp.mean(x * x, axis=-1, keepdims=True) + EPS) * gain


def _rows(m, n_rows):
    r = m.shape[0]
    if r == 1 or r == n_rows:
        return m
    return jnp.concatenate([m] * (n_rows // r), axis=0)


def _dot(a, b):
    return jnp.dot(a, b, preferred_element_type=F32)


def _dot_nt(a, b):
    return lax.dot_general(a, b, (((1,), (1,)), ((), ())), preferred_element_type=F32)


def _t5_bucket(dist):
    n = jnp.maximum(dist, 0)
    max_exact = NUM_BUCKETS // 2
    nf = jnp.maximum(n, 1).astype(jnp.float32)
    large = max_exact + (jnp.log(nf / max_exact) / math.log(MAX_DISTANCE / max_exact)
                         * (NUM_BUCKETS - max_exact)).astype(jnp.int32)
    large = jnp.minimum(large, NUM_BUCKETS - 1)
    return jnp.where(n < max_exact, n, large)


SMP_KEYS = 2 * WINDOW


def _prompt_dist():
    r = np.arange(BLOCK)[:, None]
    i = np.arange(BLOCK)[None, :]
    return (i - r) % BLOCK


def _sample_rows(t_len):
    r = np.arange(N_KV_HEADS * t_len * GROUP)
    return r // (t_len * GROUP), (r // GROUP) % t_len, r % GROUP


def _sample_dist(t_len):
    _, t, _ = _sample_rows(t_len)
    return WINDOW + t[:, None] - np.arange(SMP_KEYS)[None, :]


def _prep_body(tab_ref, tabr_ref, bkt_p_ref, bkt_s_ref, ws_ref,
               bias_p_ref, bias_s_ref, wtril_ref):
    bkt_p = bkt_p_ref[...]
    for kh in range(N_KV_HEADS):
        for g in range(GROUP):
            h = kh * GROUP + g
            acc = jnp.zeros((BLOCK, BLOCK), F32)
            for b in range(NUM_BUCKETS):
                acc = jnp.where(bkt_p == b, tab_ref[b, h], acc)
            bias_p_ref[kh, :, g * BLOCK:(g + 1) * BLOCK] = acc
    bkt_s = bkt_s_ref[...]
    tabr = tabr_ref[...]
    acc = jnp.zeros(bkt_s.shape, F32)
    for b in range(NUM_BUCKETS):
        acc = jnp.where(bkt_s == b, tabr[:, b:b + 1], acc)
    bias_s_ref[...] = acc
    ii = lax.broadcasted_iota(jnp.int32, (BLOCK, BLOCK), 0)
    jj = lax.broadcasted_iota(jnp.int32, (BLOCK, BLOCK), 1)
    for h in range(GMLP_HEADS):
        wtril_ref[h] = jnp.where(jj <= ii, ws_ref[h], 0.0).astype(BF16)


def _prep(rel_table, w_s, t_len):
    bkt_p = _t5_bucket(jnp.asarray(_prompt_dist(), jnp.int32))
    bkt_s = _t5_bucket(jnp.asarray(_sample_dist(t_len), jnp.int32))
    n_rows = bkt_s.shape[0]
    tab_rows = jnp.broadcast_to(rel_table.T.reshape(N_KV_HEADS, 1, GROUP, NUM_BUCKETS),
                                (N_KV_HEADS, t_len, GROUP, NUM_BUCKETS)).reshape(n_rows, NUM_BUCKETS)
    smem = pl.BlockSpec(memory_space=pltpu.SMEM)
    vmem = pl.BlockSpec(memory_space=pltpu.VMEM)
    return pl.pallas_call(
        _prep_body,
        out_shape=(jax.ShapeDtypeStruct((N_KV_HEADS, BLOCK, GROUP * BLOCK), F32),
                   jax.ShapeDtypeStruct((n_rows, SMP_KEYS), F32),
                   jax.ShapeDtypeStruct((GMLP_HEADS, BLOCK, BLOCK), BF16)),
        in_specs=[smem, vmem, vmem, vmem, vmem],
        out_specs=(vmem, vmem, vmem),
        compiler_params=pltpu.CompilerParams(vmem_limit_bytes=32 * 1024 * 1024),
        name="prep",
    )(rel_table, tab_rows, bkt_p, bkt_s, w_s)


ADA_TN = 1024


def _adaln_body(c_ref, w_ref, b_ref, os_ref, op_ref, *, n_s, n_p):
    c = c_ref[...]
    s = (c * jax.nn.sigmoid(c)).astype(BF16)
    r = _dot(s, w_ref[...].astype(BF16)) + b_ref[...]
    os_ref[...] = r[:n_s]
    op_ref[...] = r[n_s:n_s + n_p]


def _adaln(c_prompt, c_sample, w_ada, b_ada):
    n_p, n_s = c_prompt.shape[0], c_sample.shape[0]
    pad = (-(n_p + n_s)) % 16
    c_all = jnp.concatenate([c_sample, c_prompt, jnp.zeros((pad, D_MODEL), F32)], axis=0)
    n_all = c_all.shape[0]
    n_out = w_ada.shape[1]
    return pl.pallas_call(
        functools.partial(_adaln_body, n_s=n_s, n_p=n_p),
        out_shape=(jax.ShapeDtypeStruct((n_s, n_out), F32),
                   jax.ShapeDtypeStruct((n_p, n_out), F32)),
        grid=(n_out // ADA_TN,),
        in_specs=[_resident((n_all, D_MODEL), lambda j: (0, 0)),
                  pl.BlockSpec((D_MODEL, ADA_TN), lambda j: (0, j)),
                  pl.BlockSpec((1, ADA_TN), lambda j: (0, j))],
        out_specs=(pl.BlockSpec((n_s, ADA_TN), lambda j: (0, j)),
                   pl.BlockSpec((n_p, ADA_TN), lambda j: (0, j))),
        compiler_params=_params(1, 40 * 1024 * 1024),
        name="adaln",
    )(c_all, w_ada, b_ada.reshape(1, n_out))


def _mix_in_body(x_ref, sh_ref, sc_ref, g_ref, w_ref, vg_ref, *refs, transposed):
    if transposed:
        (wc_ref, woa_ref, wob_ref, q_ref, k_ref, v_ref, gu_ref, gv_ref, kwin_ref, vwin_ref,
         wcb_ref, wo_ref) = refs
        wcb_ref[...] = wc_ref[...].astype(wcb_ref.dtype)
        wo_ref[:HEAD_DIM, :] = woa_ref[...].astype(wo_ref.dtype)
        wo_ref[HEAD_DIM:, :] = wob_ref[...].astype(wo_ref.dtype)
    else:
        q_ref, k_ref, v_ref, gu_ref, gv_ref = refs
    tm = x_ref.shape[0]
    sub = min(tm, MIX_SUB)
    c0, c1, c2, c3 = Q_COLS, Q_COLS + KV_COLS, Q_COLS + 2 * KV_COLS, Q_COLS + 2 * KV_COLS + GMLP_WIDTH
    vg = vg_ref[...]
    for r in range(0, tm, sub):
        rows = slice(r, r + sub)
        mod = lambda ref: ref[...] if ref.shape[0] == 1 else ref[rows, :]
        h = (_rms(x_ref[rows, :], g_ref[...]) * (1.0 + mod(sc_ref)) + mod(sh_ref)).astype(BF16)
        q = _dot(h, w_ref[:, :c0]) * ATTN_SCALE
        kv = _dot(h, w_ref[:, c0:c2])
        k_ref[rows, :] = kv[:, :KV_COLS].astype(k_ref.dtype)
        if transposed:
            q_ref[:, rows] = q.T.astype(q_ref.dtype)
            vt = kv[:, KV_COLS:].T
            v_ref[:, rows] = vt
            if r + sub == tm:
                kwin_ref[...] = kv[-WINDOW:, :KV_COLS].T
                vwin_ref[...] = vt[:, -WINDOW:]
        else:
            q_ref[rows, :] = q.astype(q_ref.dtype)
            v_ref[rows, :] = kv[:, KV_COLS:]
        gu_ref[rows, :] = jax.nn.gelu(_dot(h, w_ref[:, c2:c3])).astype(gu_ref.dtype)
        gv = jax.nn.gelu(_dot(h, w_ref[:, c3:]))
        for hd in range(GMLP_HEADS):
            sl = slice(hd * GMLP_HEAD_DIM, (hd + 1) * GMLP_HEAD_DIM)
            gv_ref[rows, sl] = _rms(gv[:, sl], vg[:, sl]).astype(gv_ref.dtype)


def _mix_in(x, x_spec, sh_spec, sc_spec, mod, grid, g_pre, w_in, v_gain, outs, out_specs,
            side=((), ())):
    extra, extra_specs = side
    return pl.pallas_call(
        functools.partial(_mix_in_body, transposed=len(extra) > 0),
        out_shape=outs,
        grid=grid,
        in_specs=[x_spec, sh_spec, sc_spec,
                  _resident((1, D_MODEL), lambda i: (0, 0)),
                  _resident((D_MODEL, IN_COLS), lambda i: (0, 0)),
                  _resident((1, GMLP_WIDTH), lambda i: (0, 0)), *extra_specs],
        out_specs=out_specs,
        compiler_params=_params(1),
        name="mix_in",
    )(x, mod, mod, g_pre, w_in, v_gain, *extra)


PROJ_TN = 256


def _out_proj_epilogue(o, rows, x_ref, gt_ref, sh_ref, sc_ref, gpm_ref, gpf_ref, x1_ref, hf_ref):
    mod = lambda ref: ref[...] if ref.shape[0] == 1 else ref[rows, :]
    x1 = x_ref[rows, :] + mod(gt_ref) * _rms(o, gpm_ref[...])
    x1_ref[rows, :] = x1
    hf = _rms(x1, gpf_ref[...]) * (1.0 + mod(sc_ref)) + mod(sh_ref)
    hf_ref[rows, :] = hf.astype(hf_ref.dtype)


CORE_NB = 4


def _mix_core_body(qt_ref, kc_ref, kp_ref, vtc_ref, vtp_ref, gu_ref, gv_ref,
                   bias_ref, pm_ref, sink_ref, wtril_ref, bs_ref, gat_ref, gg_ref, wc_ref,
                   x_ref, gt_ref, sh_ref, sc_ref, wb_ref, gpm_ref, gpf_ref,
                   x1_ref, hf_ref, wcb_ref, *, steps_per_seq):
    wcb_ref[...] = wc_ref[...].astype(wcb_ref.dtype)
    first = pl.program_id(0) % steps_per_seq == 0
    tri = (lax.broadcasted_iota(jnp.int32, (BLOCK, BLOCK), 0)
           > lax.broadcasted_iota(jnp.int32, (BLOCK, BLOCK), 1))
    from_prev = jnp.concatenate([tri] * GROUP, axis=1)
    zeros = jnp.zeros((HEAD_DIM, GROUP * BLOCK), BF16)
    bs = bs_ref[...]
    def mix_stages(u, st):
        tok = slice(u * BLOCK, (u + 1) * BLOCK)

        def logits(kh):
            if kh == 0:
                k_prev = kp_ref[...] if u == 0 else kc_ref[(u - 1) * BLOCK:u * BLOCK, :]
                vt_prev = vtp_ref[...] if u == 0 else vtc_ref[:, (u - 1) * BLOCK:u * BLOCK]
                st['k'] = jnp.concatenate([k_prev, kc_ref[tok, :]], axis=0)
                st['vt'] = jnp.concatenate([vt_prev, vtc_ref[:, tok]], axis=1).astype(BF16)
                st['outs'] = []
            base = kh * GROUP * HEAD_DIM
            qs = jnp.concatenate(
                [qt_ref[base + g * HEAD_DIM: base + (g + 1) * HEAD_DIM, tok] for g in range(GROUP)],
                axis=1)
            qp = jnp.concatenate([qs, zeros] if kh == 0 else [zeros, qs], axis=0)
            st['qk'] = _dot(st['k'], qp)

        def softmax(kh):
            qk = st['qk']
            s = jnp.where(from_prev, qk[:BLOCK], qk[BLOCK:]) + bias_ref[kh]
            if u == 0:
                s = jnp.where(jnp.logical_and(first, from_prev), NEG_INF, s)
            sink = sink_ref[kh]
            m =jnp.maximum(jnp.max(s, axis=0, keepdims=True), sink)
            p = jnp.exp(s - m)
            st['den'] = jnp.sum(p, axis=0, keepdims=True) + jnp.exp(sink - m)
            p = p.astype(BF16)
            st['p2'] = jnp.concatenate([p * pm_ref[0], p * pm_ref[1]], axis=0)

        def values(kh):
            vt = st['vt'][kh * HEAD_DIM:(kh + 1) * HEAD_DIM, :]
            st['outs'].append(_dot(vt, st['p2']) / st['den'])

        def attn_norm():
            ot = jnp.concatenate(st['outs'], axis=0)
            sq = jnp.sum(ot * ot, axis=0, keepdims=True)
            tot = sq[:, :BLOCK]
            for g in range(1, GROUP):
                tot = tot + sq[:, g * BLOCK:(g + 1) * BLOCK]
            r = lax.rsqrt(tot / ATTN_WIDTH + EPS)
            a = ot * jnp.concatenate([r] * GROUP, axis=1) * gat_ref[...]
            st['merged'] = [a[:, g * BLOCK:(g + 1) * BLOCK].T.astype(BF16) for g in range(GROUP)]

        def gating():
            gated = []
            for hd in range(GMLP_HEADS):
                sl = slice(hd * GMLP_HEAD_DIM, (hd + 1) * GMLP_HEAD_DIM)
                mixed = _dot(wtril_ref[hd], gv_ref[tok, sl]) + bs[:, hd:hd + 1]
                gated.append(gu_ref[tok, sl] * mixed)
            gm = jnp.concatenate(gated, axis=-1)
            st['merged'] = jnp.concatenate(
                st['merged'] + [_rms(gm, gg_ref[...]).astype(BF16)], axis=1)

        P = functools.partial
        return [P(logits, 0), P(softmax, 0), P(values, 0), P(logits, 1), P(softmax, 1),
                P(values, 1), attn_norm, gating]

    def proj_stages(u, st):
        tok = slice(u * BLOCK, (u + 1) * BLOCK)
        chunks = []

        def chunk(c):
            chunks.append(_dot(st['merged'], wb_ref[:, c * PROJ_TN:(c + 1) * PROJ_TN]))

        def finish():
            _out_proj_epilogue(jnp.concatenate(chunks, axis=1), tok, x_ref, gt_ref, sh_ref,
                               sc_ref, gpm_ref, gpf_ref, x1_ref, hf_ref)

        return [functools.partial(chunk, c) for c in range(D_MODEL // PROJ_TN)] + [finish]

    states = [dict() for _ in range(CORE_NB)]
    for u in range(CORE_NB + 1):
        a = mix_stages(u, states[u]) if u < CORE_NB else []
        b = proj_stages(u - 1, states[u - 1]) if u > 0 else []
        for i in range(max(len(a), len(b))):
            for stage in (b[i:i + 1] + a[i:i + 1]):
                stage()


def _mix_core(qt, k, vt, gu, gvn, bias_t, sink_row, wtril, bs_t, gain_t, g_gmlp, seq, w_side,
              x, mod, mod_spec, w_out_b, g_post_mix, g_pre_ff):
    r_gt_i = np.tile(np.arange(BLOCK)[:, None] > np.arange(BLOCK)[None, :], (1, GROUP))
    pmask = jnp.asarray(np.stack([r_gt_i, ~r_gt_i]), BF16)
    n_tok = k.shape[0]
    tm = CORE_NB * BLOCK
    n_steps = n_tok // tm
    side_rows = w_side.shape[0] // n_steps
    GT_M, SH_F, SC_F = 2, 3, 4
    cur = lambda i: (i, 0)
    prev = lambda i: (jnp.maximum(i * CORE_NB - 1, 0), 0)
    cur_t = lambda i: (0, i)
    prev_t = lambda i: (0, jnp.maximum(i * CORE_NB - 1, 0))
    full2 = lambda i: (0, 0)
    full3 = lambda i: (0, 0, 0)
    return pl.pallas_call(
        functools.partial(_mix_core_body, steps_per_seq=seq // tm),
        out_shape=(jax.ShapeDtypeStruct((n_tok, D_MODEL), F32),
                   jax.ShapeDtypeStruct((n_tok, D_MODEL), BF16),
                   jax.ShapeDtypeStruct(w_side.shape, BF16)),
        grid=(n_steps,),
        in_specs=[pl.BlockSpec((Q_COLS, tm), cur_t),
                  pl.BlockSpec((tm, KV_COLS), cur), pl.BlockSpec((BLOCK, KV_COLS), prev),
                  pl.BlockSpec((KV_COLS, tm), cur_t), pl.BlockSpec((KV_COLS, BLOCK), prev_t),
                  pl.BlockSpec((tm, GMLP_WIDTH), cur), pl.BlockSpec((tm, GMLP_WIDTH), cur),
                  _resident(bias_t.shape, full3), _resident(pmask.shape, full3),
                  _resident(sink_row.shape, full3), _resident(wtril.shape, full3),
                  _resident(bs_t.shape, full2),
                  _resident(gain_t.shape, full2), _resident((1, GMLP_WIDTH), full2),
                  pl.BlockSpec((side_rows, w_side.shape[1]), cur),
                  pl.BlockSpec((tm, D_MODEL), cur),
                  mod_spec(GT_M, tm), mod_spec(SH_F, tm), mod_spec(SC_F, tm),
                  _resident((D_MODEL, D_MODEL), full2),
                  _resident((1, D_MODEL), full2), _resident((1, D_MODEL), full2)],
        out_specs=(pl.BlockSpec((tm, D_MODEL), cur), pl.BlockSpec((tm, D_MODEL), cur),
                   pl.BlockSpec((side_rows, w_side.shape[1]), cur)),
        compiler_params=_params(1, 58 * 1024 * 1024),
        name="mix_core",
    )(qt, k, k, vt, vt, gu, gvn, bias_t, pmask, sink_row, wtril, bs_t, gain_t, g_gmlp, w_side,
      x, mod, mod, mod, w_out_b, g_post_mix, g_pre_ff)


SMP_BB = 16


def _attn_smp_body(q_ref, ckt_ref, cvt_ref, nk_ref, nv_ref, bias_ref, valid_ref, sink_ref,
                   o_ref, *, t_len):
    zpad = jnp.zeros((WINDOW - t_len, KV_COLS), F32)
    tiles = []
    for b in range(SMP_BB):
        q = q_ref[b]
        s_cache = _dot(q, ckt_ref[b].astype(BF16))
        kn = jnp.concatenate([nk_ref[b], zpad], axis=0).astype(BF16)
        tiles.append(jnp.concatenate([s_cache, _dot_nt(q, kn)], axis=1)[None])
    s = jnp.concatenate(tiles, axis=0)
    s = jnp.where(valid_ref[...] > 0.5, s + bias_ref[...], NEG_INF)
    sink = sink_ref[...]
    m = jnp.maximum(jnp.max(s, axis=-1, keepdims=True), sink)
    p = jnp.exp(s - m)
    den = jnp.sum(p, axis=-1, keepdims=True) + jnp.exp(sink - m)
    p = p.astype(BF16)
    for b in range(SMP_BB):
        vn = jnp.concatenate([nv_ref[b], zpad], axis=0).astype(BF16)
        o = _dot_nt(p[b, :, :WINDOW], cvt_ref[b].astype(BF16)) + _dot(p[b, :, WINDOW:], vn)
        o_ref[b] = o / den[b]


def _attn_smp(q6, cache_kt, cache_vt, new_k, new_v, bias_s, valid_s, sink_s, t_len):
    n_b, rows, _ = q6.shape
    b3 = lambda i: (i, 0, 0)
    one = lambda i: (0, 0)
    return pl.pallas_call(
        functools.partial(_attn_smp_body, t_len=t_len),
        out_shape=jax.ShapeDtypeStruct((n_b, rows, KV_COLS), F32),
        grid=(n_b // SMP_BB,),
        in_specs=[pl.BlockSpec((SMP_BB, rows, KV_COLS), b3),
                  pl.BlockSpec((SMP_BB, KV_COLS, WINDOW), b3),
                  pl.BlockSpec((SMP_BB, KV_COLS, WINDOW), b3),
                  pl.BlockSpec((SMP_BB, t_len, KV_COLS), b3),
                  pl.BlockSpec((SMP_BB, t_len, KV_COLS), b3),
                  _resident(bias_s.shape, one), _resident(valid_s.shape, one),
                  _resident(sink_s.shape, one)],
        out_specs=pl.BlockSpec((SMP_BB, rows, KV_COLS), b3),
        compiler_params=_params(1, 32 * 1024 * 1024),
        name="attn_smp",
    )(q6, cache_kt, cache_vt, new_k, new_v, bias_s, valid_s, sink_s)


def _out_proj_smp_body(w4_ref, b4_ref, a_ref, gu_ref, gv_ref, ga_ref, gg_ref,
                       x_ref, gt_ref, sh_ref, sc_ref, wb_ref, gpm_ref, gpf_ref, x1_ref, hf_ref, *,
                       t_len):
    i = pl.program_id(0)
    gated = []
    for hd in range(GMLP_HEADS):
        acc = jnp.zeros((a_ref.shape[0], GMLP_HEAD_DIM), F32)
        for j in range(t_len):
            w = jnp.where(j <= i, w4_ref[(hd * t_len + i) * t_len + j], 0.0)
            lo = j * GMLP_WIDTH + hd * GMLP_HEAD_DIM
            acc = acc + w * gv_ref[:, lo:lo + GMLP_HEAD_DIM]
        mixed = acc + b4_ref[hd * t_len + i]
        gated.append(gu_ref[:, hd * GMLP_HEAD_DIM:(hd + 1) * GMLP_HEAD_DIM] * mixed)
    gm = jnp.concatenate(gated, axis=-1)
    merged = jnp.concatenate([_rms(a_ref[...], ga_ref[...]), _rms(gm, gg_ref[...])], axis=-1)
    _out_proj_epilogue(_dot(merged.astype(BF16), wb_ref[...]), slice(0, a_ref.shape[0]),
                       x_ref, gt_ref, sh_ref, sc_ref, gpm_ref, gpf_ref, x1_ref, hf_ref)


FFN_TF = 1024
FFN_SUB = 256


def _ffn_body(h_ref, w1_ref, w2_ref, x1_ref, gt_ref, g_ref, o_ref, acc_ref, done_ref, *,
              n_tiles, n_chunks):
    s = pl.program_id(0)
    i = s // n_chunks
    j = s % n_chunks
    tm = o_ref.shape[0]
    whole = slice(0, tm)

    def contribution(rows):
        a = jnp.maximum(_dot(h_ref[rows, :], w1_ref[...]), 0.0)
        return _dot((a * a).astype(BF16), w2_ref[...])

    def finalize(rows):
        gt = gt_ref[...] if gt_ref.shape[0] == 1 else _rows(gt_ref[...], tm)[rows, :]
        o_ref[rows, :] = x1_ref[rows, :] + gt * _rms(done_ref[rows, :], g_ref[...])

    @pl.when(s == 0)
    def _():
        acc_ref[...] = contribution(whole)

    @pl.when(jnp.logical_and(j == 0, jnp.logical_and(i > 0, i < n_tiles)))
    def _():
        for r in range(0, tm, FFN_SUB):
            rows = slice(r, r + FFN_SUB)
            acc_ref[rows, :] = contribution(rows)
            finalize(rows)

    @pl.when(jnp.logical_and(j > 0, j < n_chunks - 1))
    def _():
        acc_ref[...] += contribution(whole)

    @pl.when(jnp.logical_and(j == n_chunks - 1, i < n_tiles))
    def _():
        done_ref[...] = acc_ref[...] + contribution(whole)

    @pl.when(i == n_tiles)
    def _():
        finalize(whole)


def _ffn(hf, w1, w2, x1, mod, gt_spec, g_post, tm):
    n_tok = hf.shape[0]
    n_tiles, n_chunks = n_tok // tm, D_FF // FFN_TF
    last = n_tiles * n_chunks
    mm_tile = lambda s: jnp.minimum(s // n_chunks, n_tiles - 1)
    chunk = lambda s: jnp.where(s == last, n_chunks - 1, s % n_chunks)
    out_tile = lambda s: jnp.maximum(s // n_chunks - (s % n_chunks == 0).astype(jnp.int32), 0)
    return pl.pallas_call(
        functools.partial(_ffn_body, n_tiles=n_tiles, n_chunks=n_chunks),
        out_shape=jax.ShapeDtypeStruct((n_tok, D_MODEL), F32),
        grid=(last + 1,),
        in_specs=[pl.BlockSpec((tm, D_MODEL), lambda s: (mm_tile(s), 0)),
                  pl.BlockSpec((D_MODEL, FFN_TF), lambda s: (0, chunk(s))),
                  pl.BlockSpec((FFN_TF, D_MODEL), lambda s: (chunk(s), 0)),
                  pl.BlockSpec((tm, D_MODEL), lambda s: (out_tile(s), 0)),
                  gt_spec(out_tile),
                  _resident((1, D_MODEL), lambda s: (0, 0))],
        out_specs=pl.BlockSpec((tm, D_MODEL), lambda s: (out_tile(s), 0)),
        scratch_shapes=[pltpu.VMEM((tm, D_MODEL), F32), pltpu.VMEM((tm, D_MODEL), F32)],
        compiler_params=_params(1),
        name="ffn",
    )(hf, w1, w2, x1, mod, g_post)


MIX_TM = 512
MIX_SUB = 128
FFN_TM = 512


def _row(v):
    return v.reshape(1, -1)


def _layer(x_prompt, x_sample, cache_k, cache_v, c_prompt, c_sample, rel_table, w_ada, b_ada,
           g_pre_mix, w_in, sinks, v_gain, w_s, b_s, g_attn, g_gmlp, w_out, g_post_mix,
           g_pre_ff, w_ff1, w_ff2, g_post_ff):
    n_b, seq, _ = x_prompt.shape
    n_db, t_len, _ = x_sample.shape
    n_tok = n_b * seq
    D = D_MODEL

    w_in_b = w_in.astype(BF16)
    g_pre_mix, g_attn, g_gmlp = _row(g_pre_mix), _row(g_attn), _row(g_gmlp)
    g_post_mix, g_pre_ff, g_post_ff = _row(g_post_mix), _row(g_pre_ff), _row(g_post_ff)
    v_gain = _row(v_gain)

    bias_p, bias_s, wtril = _prep(rel_table, w_s, t_len)
    mod_s, mod_p = _adaln(c_prompt, c_sample, w_ada, b_ada)
    mod_p = mod_p.reshape(n_b, 1, 6 * D)
    SH_M, SC_M, GT_M, SH_F, SC_F, GT_F = range(6)

    def pmod(chunk, tm):
        per = seq // tm
        return pl.BlockSpec((None, 1, D), lambda i: (i // per, 0, chunk))

    def smod(chunk):
        return pl.BlockSpec((n_db, D), lambda i: (0, chunk))

    sd = _sample_dist(t_len)
    valid_s = ((sd >= 0) & (sd < WINDOW) & (np.arange(SMP_KEYS)[None, :] < WINDOW + t_len))
    valid_s = valid_s.astype(np.float32)
    sink2 = sinks.reshape(N_KV_HEADS, GROUP)
    sink_p = jnp.repeat(sink2, BLOCK, axis=1)[:, None, :]
    sink_s = jnp.tile(sink2, (1, t_len)).reshape(-1, 1)
    ga3 = g_attn.reshape(N_KV_HEADS, GROUP, HEAD_DIM)
    gain_t = jnp.repeat(ga3.transpose(0, 2, 1).reshape(KV_COLS, GROUP), BLOCK, axis=1)
    g_attn = ga3.transpose(1, 0, 2).reshape(1, ATTN_WIDTH)

    xp = x_prompt.reshape(n_tok, D)
    tile = lambda w: pl.BlockSpec((MIX_TM, w), lambda i: (i, 0))
    ttile = lambda w: pl.BlockSpec((w, MIX_TM), lambda i: (0, i))
    n_mix = n_tok // MIX_TM
    slab1 = pl.BlockSpec((D // n_mix, D_FF), lambda i: (i, 0))
    assert seq >= WINDOW and MIX_TM >= WINDOW
    mix_per = seq // MIX_TM
    win = pl.BlockSpec((None, KV_COLS, WINDOW), lambda i: (i // mix_per, 0, 0))
    assert n_mix * BLOCK == D and GROUP * BLOCK == ATTN_WIDTH
    piece = lambda off: pl.BlockSpec(
        (HEAD_DIM, D), lambda i: (jnp.where(i < GROUP, i + off * GROUP, 2 * i + off), 0))
    slab_o = pl.BlockSpec((BLOCK, D), lambda i: (i, 0))
    qt_p, k_p, vt_p, gu_p, gvn_p, kwin, vwin, w1_b, w_out_b = _mix_in(
        xp, tile(D), pmod(SH_M, MIX_TM), pmod(SC_M, MIX_TM), mod_p, (n_tok // MIX_TM,),
        g_pre_mix, w_in_b, v_gain,
        (jax.ShapeDtypeStruct((Q_COLS, n_tok), BF16),
         jax.ShapeDtypeStruct((n_tok, KV_COLS), BF16),
         jax.ShapeDtypeStruct((KV_COLS, n_tok), F32),
         jax.ShapeDtypeStruct((n_tok, GMLP_WIDTH), F32),
         jax.ShapeDtypeStruct((n_tok, GMLP_WIDTH), BF16),
         jax.ShapeDtypeStruct((n_b, KV_COLS, WINDOW), F32),
         jax.ShapeDtypeStruct((n_b, KV_COLS, WINDOW), F32),
         jax.ShapeDtypeStruct((D, D_FF), BF16),
         jax.ShapeDtypeStruct((D, D), BF16)),
        (ttile(Q_COLS), tile(KV_COLS), ttile(KV_COLS), tile(GMLP_WIDTH), tile(GMLP_WIDTH),
         win, win, slab1, slab_o),
        side=((w_ff1, w_out, w_out), (slab1, piece(0), piece(1))))

    x1_p, hf_p, w2_b = _mix_core(qt_p, k_p, vt_p, gu_p, gvn_p, bias_p, sink_p,
                                 wtril, b_s.T, gain_t, g_gmlp, seq, w_ff2,
                                 xp, mod_p, pmod, w_out_b, g_post_mix, g_pre_ff)
    one = lambda i: (0, 0)

    ffn_per = seq // FFN_TM
    gate_p = lambda tile: pl.BlockSpec((None, 1, D), lambda s: (tile(s) // ffn_per, 0, GT_F))
    y_p = _ffn(hf_p, w1_b, w2_b, x1_p, mod_p, gate_p, g_post_ff, FFN_TM)

    xs = x_sample.reshape(n_db, t_len * D)
    lane = lambda w: pl.BlockSpec((n_db, w), lambda t: (0, t))
    q_s, k_s, v_s, gu_s, gvn_s = _mix_in(
        xs, lane(D), smod(SH_M), smod(SC_M), mod_s, (t_len,),
        g_pre_mix, w_in_b, v_gain,
        (jax.ShapeDtypeStruct((n_db, t_len * Q_COLS), F32),
         jax.ShapeDtypeStruct((n_db, t_len * KV_COLS), F32),
         jax.ShapeDtypeStruct((n_db, t_len * KV_COLS), F32),
         jax.ShapeDtypeStruct((n_db, t_len * GMLP_WIDTH), F32),
         jax.ShapeDtypeStruct((n_db, t_len * GMLP_WIDTH), F32)),
        (lane(Q_COLS), lane(KV_COLS), lane(KV_COLS), lane(GMLP_WIDTH), lane(GMLP_WIDTH)))

    q5 = q_s.reshape(n_db, t_len, N_KV_HEADS, GROUP, HEAD_DIM).transpose(0, 2, 1, 3, 4)
    q5 = q5.reshape(n_db, N_KV_HEADS, t_len * GROUP, HEAD_DIM).astype(BF16)
    z5 = jnp.zeros_like(q5[:, 0])
    q6 = jnp.concatenate([jnp.concatenate([q5[:, 0], z5], axis=-1),
                          jnp.concatenate([z5, q5[:, 1]], axis=-1)], axis=1)
    new_k = k_s.reshape(n_db, t_len, KV_COLS)
    new_v = v_s.reshape(n_db, t_len, KV_COLS)
    cache_kt = cache_k.transpose(0, 2, 3, 1).reshape(n_db, KV_COLS, WINDOW)
    cache_vt = cache_v.transpose(0, 2, 3, 1).reshape(n_db, KV_COLS, WINDOW)
    o6 = _attn_smp(q6, cache_kt, cache_vt, new_k, new_v, bias_s, jnp.asarray(valid_s), sink_s, t_len)
    o6 = o6.reshape(n_db, N_KV_HEADS, t_len, GROUP, N_KV_HEADS, HEAD_DIM)
    attn_s = jnp.stack([o6[:, kh, :, :, kh, :] for kh in range(N_KV_HEADS)], axis=3)
    attn_s = attn_s.reshape(n_db, t_len * ATTN_WIDTH)

    smem = pl.BlockSpec(memory_space=pltpu.SMEM)
    w4 = w_s[:, :t_len, :t_len].reshape(-1)
    b4 = b_s[:, :t_len].reshape(-1)
    tmaj = lambda: pl.BlockSpec((None, n_db, D), lambda t: (t, 0, 0))
    x1_s, hf_s = pl.pallas_call(
        functools.partial(_out_proj_smp_body, t_len=t_len),
        out_shape=(jax.ShapeDtypeStruct((t_len, n_db, D), F32),
                   jax.ShapeDtypeStruct((t_len, n_db, D), BF16)),
        grid=(t_len,),
        in_specs=[smem, smem, lane(ATTN_WIDTH), lane(GMLP_WIDTH),
                  _resident((n_db, t_len * GMLP_WIDTH), one),
                  _resident((1, ATTN_WIDTH), one), _resident((1, GMLP_WIDTH), one),
                  lane(D), smod(GT_M), smod(SH_F), smod(SC_F),
                  _resident((D, D), one), _resident((1, D), one), _resident((1, D), one)],
        out_specs=(tmaj(), tmaj()),
        compiler_params=_params(1),
        name="out_proj_smp",
    )(w4, b4, attn_s, gu_s, gvn_s, g_attn, g_gmlp, xs, mod_s, mod_s, mod_s,
      w_out_b, g_post_mix, g_pre_ff)

    n_st = t_len * n_db
    y_s = _ffn(hf_s.reshape(n_st, D), w1_b, w2_b, x1_s.reshape(n_st, D), mod_s,
               lambda tile: smod(GT_F), g_post_ff, n_st)
    y_s = y_s.reshape(t_len, n_db, D).transpose(1, 0, 2)

    last = lambda t: t.reshape(n_b, N_KV_HEADS, HEAD_DIM, WINDOW).transpose(0, 3, 1, 2)
    k_p4, v_p4 = last(kwin), last(vwin)
    return (y_p.reshape(n_b, seq, D), y_s, k_p4, v_p4,
            k_s.reshape(n_db, t_len, N_KV_HEADS, HEAD_DIM),
            v_s.reshape(n_db, t_len, N_KV_HEADS, HEAD_DIM),
            gvn_s.reshape(n_db, t_len, GMLP_HEADS, GMLP_HEAD_DIM))


def kernel(x_prompt, x_sample, cache_k, cache_v, c_prompt, c_sample, rel_bias_table, w_ada, b_ada,
           g_pre_mix, w_in, attn_sinks, gmlp_v_gain, gmlp_w_s, gmlp_b_s, g_attn_out, g_gmlp_out,
           w_out, g_post_mix, g_pre_ff, w_ff1, w_ff2, g_post_ff):
    depth = w_in.shape[0]
    assert depth == 1, "single-layer step"
    outs = _layer(x_prompt, x_sample, cache_k[0], cache_v[0], c_prompt, c_sample, rel_bias_table,
                  w_ada[0], b_ada[0], g_pre_mix[0], w_in[0], attn_sinks[0], gmlp_v_gain[0],
                  gmlp_w_s[0], gmlp_b_s[0], g_attn_out[0], g_gmlp_out[0], w_out[0], g_post_mix[0],
                  g_pre_ff[0], w_ff1[0], w_ff2[0], g_post_ff[0])
    y_p, y_s, k_p, v_p, k_s, v_s, gv_s = outs
    return (y_p, y_s, k_p[None], v_p[None], k_s[None], v_s[None], gv_s[None])
```

```python
import functools
import math

import numpy as np
import jax
import jax.numpy as jnp
from jax import lax
from jax.experimental import pallas as pl
from jax.experimental.pallas import tpu as pltpu

D_MODEL = 2048
HEAD_DIM = 64
ATTN_WIDTH = 1024
N_HEADS = 16
N_KV_HEADS = 2
GROUP = 8
WINDOW = 128
BLOCK = 128
GMLP_WIDTH = 1024
GMLP_HEADS = 8
GMLP_HEAD_DIM = 128
D_FF = 4 * D_MODEL
NUM_BUCKETS = 32
MAX_DISTANCE = 128
EPS = 1e-6
KV_COLS = N_KV_HEADS * HEAD_DIM
Q_COLS = N_HEADS * HEAD_DIM
IN_COLS = Q_COLS + 2 * KV_COLS + 2 * GMLP_WIDTH
ATTN_SCALE = HEAD_DIM ** -0.5
NEG_INF = -1e30

V7X_VMEM_BYTES = 64 * 1024 * 1024
VMEM_LIMIT_BYTES = 56 * 1024 * 1024

BF16 = jnp.bfloat16
F32 = jnp.float32


def _params(n_axes, vmem=VMEM_LIMIT_BYTES):
    return pltpu.CompilerParams(
        dimension_semantics=("arbitrary",) * n_axes, vmem_limit_bytes=vmem)


def _resident(shape, index_map):
    return pl.BlockSpec(shape, index_map, pipeline_mode=pl.Buffered(1))


def _rms(x, gain):
    return x * lax.rsqrt(jnp.mean(x * x, axis=-1, keepdims=True) + EPS) * gain


def _rows(m, n_rows):
    r = m.shape[0]
    if r == 1 or r == n_rows:
        return m
    return jnp.concatenate([m] * (n_rows // r), axis=0)


def _dot(a, b):
    return jnp.dot(a, b, preferred_element_type=F32)


def _dot_nt(a, b):
    return lax.dot_general(a, b, (((1,), (1,)), ((), ())), preferred_element_type=F32)


def _t5_bucket(dist):
    n = np.maximum(dist, 0)
    max_exact = NUM_BUCKETS // 2
    nf = np.maximum(n, 1).astype(np.float64)
    large = max_exact + (np.log(nf / max_exact) / math.log(MAX_DISTANCE / max_exact)
                         * (NUM_BUCKETS - max_exact)).astype(np.int32)
    large = np.minimum(large, NUM_BUCKETS - 1)
    return jnp.asarray(np.where(n < max_exact, n, large), jnp.int32)


SMP_KEYS = 2 * WINDOW


def _prompt_dist():
    r = np.arange(BLOCK)[:, None]
    i = np.arange(BLOCK)[None, :]
    return (i - r) % BLOCK


def _sample_rows(t_len):
    r = np.arange(N_KV_HEADS * t_len * GROUP)
    return r // (t_len * GROUP), (r // GROUP) % t_len, r % GROUP


def _sample_dist(t_len):
    _, t, _ = _sample_rows(t_len)
    return WINDOW + t[:, None] - np.arange(SMP_KEYS)[None, :]


def _prep_body(tab_ref, tabr_ref, bkt_p_ref, bkt_s_ref, ws_ref,
               bias_p_ref, bias_s_ref, wtril_ref):
    bkt_p = bkt_p_ref[...]
    for kh in range(N_KV_HEADS):
        for g in range(GROUP):
            h = kh * GROUP + g
            acc = jnp.zeros((BLOCK, BLOCK), F32)
            for b in range(NUM_BUCKETS):
                acc = jnp.where(bkt_p == b, tab_ref[b, h], acc)
            bias_p_ref[kh, :, g * BLOCK:(g + 1) * BLOCK] = acc
    bkt_s = bkt_s_ref[...]
    tabr = tabr_ref[...]
    acc = jnp.zeros(bkt_s.shape, F32)
    for b in range(NUM_BUCKETS):
        acc = jnp.where(bkt_s == b, tabr[:, b:b + 1], acc)
    bias_s_ref[...] = acc
    ii = lax.broadcasted_iota(jnp.int32, (BLOCK, BLOCK), 0)
    jj = lax.broadcasted_iota(jnp.int32, (BLOCK, BLOCK), 1)
    for h in range(GMLP_HEADS):
        wtril_ref[h] = jnp.where(jj <= ii, ws_ref[h], 0.0).astype(BF16)


def _prep(rel_table, w_s, t_len):
    bkt_p = _t5_bucket(_prompt_dist())
    bkt_s = _t5_bucket(_sample_dist(t_len))
    n_rows = bkt_s.shape[0]
    tab_rows = jnp.broadcast_to(rel_table.T.reshape(N_KV_HEADS, 1, GROUP, NUM_BUCKETS),
                                (N_KV_HEADS, t_len, GROUP, NUM_BUCKETS)).reshape(n_rows, NUM_BUCKETS)
    smem = pl.BlockSpec(memory_space=pltpu.SMEM)
    vmem = pl.BlockSpec(memory_space=pltpu.VMEM)
    return pl.pallas_call(
        _prep_body,
        out_shape=(jax.ShapeDtypeStruct((N_KV_HEADS, BLOCK, GROUP * BLOCK), F32),
                   jax.ShapeDtypeStruct((n_rows, SMP_KEYS), F32),
                   jax.ShapeDtypeStruct((GMLP_HEADS, BLOCK, BLOCK), BF16)),
        in_specs=[smem, vmem, vmem, vmem, vmem],
        out_specs=(vmem, vmem, vmem),
        compiler_params=pltpu.CompilerParams(vmem_limit_bytes=32 * 1024 * 1024),
        name="prep",
    )(rel_table, tab_rows, bkt_p, bkt_s, w_s)


ADA_TN = 1024


def _adaln_body(c_ref, w_ref, b_ref, os_ref, op_ref, *, n_s, n_p):
    c = c_ref[...]
    s = (c * jax.nn.sigmoid(c)).astype(BF16)
    r = _dot(s, w_ref[...].astype(BF16)) + b_ref[...]
    os_ref[...] = r[:n_s]
    op_ref[...] = r[n_s:n_s + n_p]


def _adaln(c_prompt, c_sample, w_ada, b_ada):
    n_p, n_s = c_prompt.shape[0], c_sample.shape[0]
    pad = (-(n_p + n_s)) % 16
    c_all = jnp.concatenate([c_sample, c_prompt, jnp.zeros((pad, D_MODEL), F32)], axis=0)
    n_all = c_all.shape[0]
    n_out = w_ada.shape[1]
    return pl.pallas_call(
        functools.partial(_adaln_body, n_s=n_s, n_p=n_p),
        out_shape=(jax.ShapeDtypeStruct((n_s, n_out), F32),
                   jax.ShapeDtypeStruct((n_p, n_out), F32)),
        grid=(n_out // ADA_TN,),
        in_specs=[_resident((n_all, D_MODEL), lambda j: (0, 0)),
                  pl.BlockSpec((D_MODEL, ADA_TN), lambda j: (0, j)),
                  pl.BlockSpec((1, ADA_TN), lambda j: (0, j))],
        out_specs=(pl.BlockSpec((n_s, ADA_TN), lambda j: (0, j)),
                   pl.BlockSpec((n_p, ADA_TN), lambda j: (0, j))),
        compiler_params=_params(1, 40 * 1024 * 1024),
        name="adaln",
    )(c_all, w_ada, b_ada.reshape(1, n_out))


W_STAGE_COLS = 256


def _stage_w_in(w_hbm, wb_ref, stage_ref, sem_ref):
    n = IN_COLS // W_STAGE_COLS

    def fetch(c):
        return pltpu.make_async_copy(w_hbm.at[:, pl.ds(c * W_STAGE_COLS, W_STAGE_COLS)],
                                     stage_ref.at[c % 2], sem_ref.at[c % 2])

    fetch(0).start()
    for c in range(n):
        if c + 1 < n:
            fetch(c + 1).start()
        fetch(c).wait()
        wb_ref[:, c * W_STAGE_COLS:(c + 1) * W_STAGE_COLS] = stage_ref[c % 2].astype(BF16)


def _mix_in_body(x_ref, sh_ref, sc_ref, g_ref, w_ref, vg_ref, *refs, transposed):
    if transposed:
        (wc_ref, woa_ref, wob_ref, q_ref, k_ref, v_ref, gu_ref, gv_ref, kwin_ref, vwin_ref,
         wcb_ref, wo_ref, wib_hbm, wb_ref, stage_ref, sem_ref, wsem_ref) = refs
        write_back = pltpu.make_async_copy(wb_ref, wib_hbm, wsem_ref)

        @pl.when(pl.program_id(0) == 0)
        def _():
            _stage_w_in(w_ref, wb_ref, stage_ref, sem_ref)
            write_back.start()

        @pl.when(pl.program_id(0) == pl.num_programs(0) - 1)
        def _():
            write_back.wait()

        w_ref = wb_ref
        wcb_ref[...] = wc_ref[...].astype(wcb_ref.dtype)
        wo_ref[:HEAD_DIM, :] = woa_ref[...].astype(wo_ref.dtype)
        wo_ref[HEAD_DIM:, :] = wob_ref[...].astype(wo_ref.dtype)
    else:
        q_ref, k_ref, v_ref, gu_ref, gv_ref = refs
    tm = x_ref.shape[0]
    sub = min(tm, MIX_SUB)
    c0, c1, c2, c3 = Q_COLS, Q_COLS + KV_COLS, Q_COLS + 2 * KV_COLS, Q_COLS + 2 * KV_COLS + GMLP_WIDTH
    vg = vg_ref[...]
    for r in range(0, tm, sub):
        rows = slice(r, r + sub)
        mod = lambda ref: ref[...] if ref.shape[0] == 1 else ref[rows, :]
        h = (_rms(x_ref[rows, :], g_ref[...] * (1.0 + mod(sc_ref))) + mod(sh_ref)).astype(BF16)
        q = _dot(h, w_ref[:, :c0]) * ATTN_SCALE
        kv = _dot(h, w_ref[:, c0:c2])
        k_ref[rows, :] = kv[:, :KV_COLS].astype(k_ref.dtype)
        if transposed:
            q_ref[:, rows] = q.T.astype(q_ref.dtype)
            vt = kv[:, KV_COLS:].T
            v_ref[:, rows] = vt
            if r + sub == tm:
                kwin_ref[...] = kv[-WINDOW:, :KV_COLS].T
                vwin_ref[...] = vt[:, -WINDOW:]
        else:
            q_ref[rows, :] = q.astype(q_ref.dtype)
            v_ref[rows, :] = kv[:, KV_COLS:]
        gu_ref[rows, :] = jax.nn.gelu(_dot(h, w_ref[:, c2:c3])).astype(gu_ref.dtype)
        gv = jax.nn.gelu(_dot(h, w_ref[:, c3:]))
        for hd in range(GMLP_HEADS):
            sl = slice(hd * GMLP_HEAD_DIM, (hd + 1) * GMLP_HEAD_DIM)
            gv_ref[rows, sl] = _rms(gv[:, sl], vg[:, sl]).astype(gv_ref.dtype)


def _mix_in(x, x_spec, sh_spec, sc_spec, mod, grid, g_pre, w_in, v_gain, outs, out_specs,
            side=((), ())):
    extra, extra_specs = side
    prompt = len(extra) > 0
    any_space = pl.BlockSpec(memory_space=pl.ANY)
    scratch = [pltpu.VMEM((D_MODEL, IN_COLS), BF16),
               pltpu.VMEM((2, D_MODEL, W_STAGE_COLS), F32),
               pltpu.SemaphoreType.DMA((2,)), pltpu.SemaphoreType.DMA(())] if prompt else []
    return pl.pallas_call(
        functools.partial(_mix_in_body, transposed=prompt),
        out_shape=outs,
        grid=grid,
        in_specs=[x_spec, sh_spec, sc_spec,
                  _resident((1, D_MODEL), lambda i: (0, 0)),
                  any_space if prompt else _resident((D_MODEL, IN_COLS), lambda i: (0, 0)),
                  _resident((1, GMLP_WIDTH), lambda i: (0, 0)), *extra_specs],
        out_specs=out_specs,
        scratch_shapes=scratch,
        compiler_params=_params(1),
        name="mix_in",
    )(x, mod, mod, g_pre, w_in, v_gain, *extra)


PROJ_TN = 256


def _out_proj_epilogue(o, rows, x_ref, gt_ref, sh_ref, sc_ref, gpm_ref, gpf_ref, x1_ref, hf_ref):
    mod = lambda ref: ref[...] if ref.shape[0] == 1 else ref[rows, :]
    x1 = x_ref[rows, :] + _rms(o, gpm_ref[...] * mod(gt_ref))
    x1_ref[rows, :] = x1
    hf = _rms(x1, gpf_ref[...] * (1.0 + mod(sc_ref))) + mod(sh_ref)
    hf_ref[rows, :] = hf.astype(hf_ref.dtype)


CORE_NB = 4


def _mix_core_body(qt_ref, kc_ref, kp_ref, vtc_ref, vtp_ref, gu_ref, gv_ref,
                   bias_ref, pm_ref, sink_ref, wtril_ref, bs_ref, gat_ref, gg_ref, wc_ref,
                   x_ref, gt_ref, sh_ref, sc_ref, wb_ref, gpm_ref, gpf_ref,
                   x1_ref, hf_ref, wcb_ref, *, steps_per_seq):
    wcb_ref[...] = wc_ref[...].astype(wcb_ref.dtype)
    first = pl.program_id(0) % steps_per_seq == 0
    tri = (lax.broadcasted_iota(jnp.int32, (BLOCK, BLOCK), 0)
           > lax.broadcasted_iota(jnp.int32, (BLOCK, BLOCK), 1))
    from_prev = jnp.concatenate([tri] * GROUP, axis=1)
    zeros = jnp.zeros((HEAD_DIM, GROUP * BLOCK), BF16)
    bs = bs_ref[...]
    def mix_stages(u, st):
        tok = slice(u * BLOCK, (u + 1) * BLOCK)

        def logits(kh):
            if kh == 0:
                k_prev = kp_ref[...] if u == 0 else kc_ref[(u - 1) * BLOCK:u * BLOCK, :]
                vt_prev = vtp_ref[...] if u == 0 else vtc_ref[:, (u - 1) * BLOCK:u * BLOCK]
                st['k'] = jnp.concatenate([k_prev, kc_ref[tok, :]], axis=0)
                st['vt'] = jnp.concatenate([vt_prev, vtc_ref[:, tok]], axis=1).astype(BF16)
                st['outs'] = []
            base = kh * GROUP * HEAD_DIM
            qs = jnp.concatenate(
                [qt_ref[base + g * HEAD_DIM: base + (g + 1) * HEAD_DIM, tok] for g in range(GROUP)],
                axis=1)
            qp = jnp.concatenate([qs, zeros] if kh == 0 else [zeros, qs], axis=0)
            st['qk'] = _dot(st['k'], qp)

        def softmax(kh):
            qk = st['qk']
            s = jnp.where(from_prev, qk[:BLOCK], qk[BLOCK:]) + bias_ref[kh]
            if u == 0:
                s = jnp.where(jnp.logical_and(first, from_prev), NEG_INF, s)
            sink = sink_ref[kh]
            m =jnp.maximum(jnp.max(s, axis=0, keepdims=True), sink)
            p = jnp.exp(s - m)
            st['den'] = jnp.sum(p, axis=0, keepdims=True) + jnp.exp(sink - m)
            p = p.astype(BF16)
            st['p2'] = jnp.concatenate([p * pm_ref[0], p * pm_ref[1]], axis=0)

        def values(kh):
            vt = st['vt'][kh * HEAD_DIM:(kh + 1) * HEAD_DIM, :]
            st['outs'].append(_dot(vt, st['p2']) / st['den'])

        def attn_norm():
            ot = jnp.concatenate(st['outs'], axis=0)
            sq = jnp.sum(ot * ot, axis=0, keepdims=True)
            tot = sq[:, :BLOCK]
            for g in range(1, GROUP):
                tot = tot + sq[:, g * BLOCK:(g + 1) * BLOCK]
            r = lax.rsqrt(tot / ATTN_WIDTH + EPS)
            a = ot * jnp.concatenate([r] * GROUP, axis=1) * gat_ref[...]
            st['merged'] = [a[:, g * BLOCK:(g + 1) * BLOCK].T.astype(BF16) for g in range(GROUP)]

        def gating():
            gated = []
            for hd in range(GMLP_HEADS):
                sl = slice(hd * GMLP_HEAD_DIM, (hd + 1) * GMLP_HEAD_DIM)
                mixed = _dot(wtril_ref[hd], gv_ref[tok, sl]) + bs[:, hd:hd + 1]
                gated.append(gu_ref[tok, sl] * mixed)
            gm = jnp.concatenate(gated, axis=-1)
            st['merged'] = jnp.concatenate(
                st['merged'] + [_rms(gm, gg_ref[...]).astype(BF16)], axis=1)

        P = functools.partial
        return [P(logits, 0), P(softmax, 0), P(values, 0), P(logits, 1), P(softmax, 1),
                P(values, 1), attn_norm, gating]

    def proj_stages(u, st):
        tok = slice(u * BLOCK, (u + 1) * BLOCK)
        chunks = []

        def chunk(c):
            chunks.append(_dot(st['merged'], wb_ref[:, c * PROJ_TN:(c + 1) * PROJ_TN]))

        def finish():
            _out_proj_epilogue(jnp.concatenate(chunks, axis=1), tok, x_ref, gt_ref, sh_ref,
                               sc_ref, gpm_ref, gpf_ref, x1_ref, hf_ref)

        return [functools.partial(chunk, c) for c in range(D_MODEL // PROJ_TN)] + [finish]

    states = [dict() for _ in range(CORE_NB)]
    for u in range(CORE_NB + 1):
        a = mix_stages(u, states[u]) if u < CORE_NB else []
        b = proj_stages(u - 1, states[u - 1]) if u > 0 else []
        for i in range(max(len(a), len(b))):
            for stage in (b[i:i + 1] + a[i:i + 1]):
                stage()


def _mix_core(qt, k, vt, gu, gvn, bias_t, sink_row, wtril, bs_t, gain_t, g_gmlp, seq, w_side,
              x, mod, mod_spec, w_out_b, g_post_mix, g_pre_ff):
    r_gt_i = np.tile(np.arange(BLOCK)[:, None] > np.arange(BLOCK)[None, :], (1, GROUP))
    pmask = jnp.asarray(np.stack([r_gt_i, ~r_gt_i]), BF16)
    n_tok = k.shape[0]
    tm = CORE_NB * BLOCK
    n_steps = n_tok // tm
    side_rows = w_side.shape[0] // n_steps
    GT_M, SH_F, SC_F = 2, 3, 4
    cur = lambda i: (i, 0)
    prev = lambda i: (jnp.maximum(i * CORE_NB - 1, 0), 0)
    cur_t = lambda i: (0, i)
    prev_t = lambda i: (0, jnp.maximum(i * CORE_NB - 1, 0))
    full2 = lambda i: (0, 0)
    full3 = lambda i: (0, 0, 0)
    return pl.pallas_call(
        functools.partial(_mix_core_body, steps_per_seq=seq // tm),
        out_shape=(jax.ShapeDtypeStruct((n_tok, D_MODEL), F32),
                   jax.ShapeDtypeStruct((n_tok, D_MODEL), BF16),
                   jax.ShapeDtypeStruct(w_side.shape, BF16)),
        grid=(n_steps,),
        in_specs=[pl.BlockSpec((Q_COLS, tm), cur_t),
                  pl.BlockSpec((tm, KV_COLS), cur), pl.BlockSpec((BLOCK, KV_COLS), prev),
                  pl.BlockSpec((KV_COLS, tm), cur_t), pl.BlockSpec((KV_COLS, BLOCK), prev_t),
                  pl.BlockSpec((tm, GMLP_WIDTH), cur), pl.BlockSpec((tm, GMLP_WIDTH), cur),
                  _resident(bias_t.shape, full3), _resident(pmask.shape, full3),
                  _resident(sink_row.shape, full3), _resident(wtril.shape, full3),
                  _resident(bs_t.shape, full2),
                  _resident(gain_t.shape, full2), _resident((1, GMLP_WIDTH), full2),
                  pl.BlockSpec((side_rows, w_side.shape[1]), cur),
                  pl.BlockSpec((tm, D_MODEL), cur),
                  mod_spec(GT_M, tm), mod_spec(SH_F, tm), mod_spec(SC_F, tm),
                  _resident((D_MODEL, D_MODEL), full2),
                  _resident((1, D_MODEL), full2), _resident((1, D_MODEL), full2)],
        out_specs=(pl.BlockSpec((tm, D_MODEL), cur), pl.BlockSpec((tm, D_MODEL), cur),
                   pl.BlockSpec((side_rows, w_side.shape[1]), cur)),
        compiler_params=_params(1, 58 * 1024 * 1024),
        name="mix_core",
    )(qt, k, k, vt, vt, gu, gvn, bias_t, pmask, sink_row, wtril, bs_t, gain_t, g_gmlp, w_side,
      x, mod, mod, mod, w_out_b, g_post_mix, g_pre_ff)


SMP_BB = 16


def _attn_smp_body(q_ref, ckt_ref, cvt_ref, nk_ref, nv_ref, bias_ref, valid_ref, sink_ref,
                   o_ref, *, t_len):
    zpad = jnp.zeros((WINDOW - t_len, KV_COLS), F32)
    tiles = []
    for b in range(SMP_BB):
        q = q_ref[b]
        s_cache = _dot(q, ckt_ref[b].astype(BF16))
        kn = jnp.concatenate([nk_ref[b], zpad], axis=0).astype(BF16)
        tiles.append(jnp.concatenate([s_cache, _dot_nt(q, kn)], axis=1)[None])
    s = jnp.concatenate(tiles, axis=0)
    s = jnp.where(valid_ref[...] > 0.5, s + bias_ref[...], NEG_INF)
    sink = sink_ref[...]
    m = jnp.maximum(jnp.max(s, axis=-1, keepdims=True), sink)
    p = jnp.exp(s - m)
    den = jnp.sum(p, axis=-1, keepdims=True) + jnp.exp(sink - m)
    p = p.astype(BF16)
    for b in range(SMP_BB):
        vn = jnp.concatenate([nv_ref[b], zpad], axis=0).astype(BF16)
        o = _dot_nt(p[b, :, :WINDOW], cvt_ref[b].astype(BF16)) + _dot(p[b, :, WINDOW:], vn)
        o_ref[b] = o / den[b]


def _attn_smp(q6, cache_kt, cache_vt, new_k, new_v, bias_s, valid_s, sink_s, t_len):
    n_b, rows, _ = q6.shape
    b3 = lambda i: (i, 0, 0)
    one = lambda i: (0, 0)
    return pl.pallas_call(
        functools.partial(_attn_smp_body, t_len=t_len),
        out_shape=jax.ShapeDtypeStruct((n_b, rows, KV_COLS), F32),
        grid=(n_b // SMP_BB,),
        in_specs=[pl.BlockSpec((SMP_BB, rows, KV_COLS), b3),
                  pl.BlockSpec((SMP_BB, KV_COLS, WINDOW), b3),
                  pl.BlockSpec((SMP_BB, KV_COLS, WINDOW), b3),
                  pl.BlockSpec((SMP_BB, t_len, KV_COLS), b3),
                  pl.BlockSpec((SMP_BB, t_len, KV_COLS), b3),
                  _resident(bias_s.shape, one), _resident(valid_s.shape, one),
                  _resident(sink_s.shape, one)],
        out_specs=pl.BlockSpec((SMP_BB, rows, KV_COLS), b3),
        compiler_params=_params(1, 32 * 1024 * 1024),
        name="attn_smp",
    )(q6, cache_kt, cache_vt, new_k, new_v, bias_s, valid_s, sink_s)


def _out_proj_smp_body(w4_ref, b4_ref, a_ref, gu_ref, gv_ref, ga_ref, gg_ref,
                       x_ref, gt_ref, sh_ref, sc_ref, wb_ref, gpm_ref, gpf_ref, x1_ref, hf_ref, *,
                       t_len):
    i = pl.program_id(0)
    gated = []
    for hd in range(GMLP_HEADS):
        acc = jnp.zeros((a_ref.shape[0], GMLP_HEAD_DIM), F32)
        for j in range(t_len):
            w = jnp.where(j <= i, w4_ref[(hd * t_len + i) * t_len + j], 0.0)
            lo = j * GMLP_WIDTH + hd * GMLP_HEAD_DIM
            acc = acc + w * gv_ref[:, lo:lo + GMLP_HEAD_DIM]
        mixed = acc + b4_ref[hd * t_len + i]
        gated.append(gu_ref[:, hd * GMLP_HEAD_DIM:(hd + 1) * GMLP_HEAD_DIM] * mixed)
    gm = jnp.concatenate(gated, axis=-1)
    merged = jnp.concatenate([_rms(a_ref[...], ga_ref[...]), _rms(gm, gg_ref[...])], axis=-1)
    _out_proj_epilogue(_dot(merged.astype(BF16), wb_ref[...]), slice(0, a_ref.shape[0]),
                       x_ref, gt_ref, sh_ref, sc_ref, gpm_ref, gpf_ref, x1_ref, hf_ref)


FFN_TF = 1024
FFN_SUB = 256


def _ffn_body(h_ref, w1_ref, w2_ref, x1_ref, gt_ref, g_ref, o_ref, acc_ref, done_ref, *,
              n_tiles, n_chunks):
    s = pl.program_id(0)
    i = s // n_chunks
    j = s % n_chunks
    tm = o_ref.shape[0]
    whole = slice(0, tm)

    def contribution(rows):
        a = jnp.maximum(_dot(h_ref[rows, :], w1_ref[...]), 0.0)
        return _dot((a * a).astype(BF16), w2_ref[...])

    def finalize(rows):
        gt = gt_ref[...] if gt_ref.shape[0] == 1 else _rows(gt_ref[...], tm)[rows, :]
        o_ref[rows, :] = x1_ref[rows, :] + _rms(done_ref[rows, :], g_ref[...] * gt)

    @pl.when(s == 0)
    def _():
        acc_ref[...] = contribution(whole)

    @pl.when(jnp.logical_and(j == 0, jnp.logical_and(i > 0, i < n_tiles)))
    def _():
        for r in range(0, tm, FFN_SUB):
            rows = slice(r, r + FFN_SUB)
            acc_ref[rows, :] = contribution(rows)
            finalize(rows)

    @pl.when(jnp.logical_and(j > 0, j < n_chunks - 1))
    def _():
        acc_ref[...] += contribution(whole)

    @pl.when(jnp.logical_and(j == n_chunks - 1, i < n_tiles))
    def _():
        done_ref[...] = acc_ref[...] + contribution(whole)

    @pl.when(i == n_tiles)
    def _():
        finalize(whole)


def _ffn(hf, w1, w2, x1, mod, gt_spec, g_post, tm):
    n_tok = hf.shape[0]
    n_tiles, n_chunks = n_tok // tm, D_FF // FFN_TF
    last = n_tiles * n_chunks
    mm_tile = lambda s: jnp.minimum(s // n_chunks, n_tiles - 1)
    chunk = lambda s: jnp.where(s == last, n_chunks - 1, s % n_chunks)
    out_tile = lambda s: jnp.maximum(s // n_chunks - (s % n_chunks == 0).astype(jnp.int32), 0)
    return pl.pallas_call(
        functools.partial(_ffn_body, n_tiles=n_tiles, n_chunks=n_chunks),
        out_shape=jax.ShapeDtypeStruct((n_tok, D_MODEL), F32),
        grid=(last + 1,),
        in_specs=[pl.BlockSpec((tm, D_MODEL), lambda s: (mm_tile(s), 0)),
                  pl.BlockSpec((D_MODEL, FFN_TF), lambda s: (0, chunk(s))),
                  pl.BlockSpec((FFN_TF, D_MODEL), lambda s: (chunk(s), 0)),
                  pl.BlockSpec((tm, D_MODEL), lambda s: (out_tile(s), 0)),
                  gt_spec(out_tile),
                  _resident((1, D_MODEL), lambda s: (0, 0))],
        out_specs=pl.BlockSpec((tm, D_MODEL), lambda s: (out_tile(s), 0)),
        scratch_shapes=[pltpu.VMEM((tm, D_MODEL), F32), pltpu.VMEM((tm, D_MODEL), F32)],
        compiler_params=_params(1),
        name="ffn",
    )(hf, w1, w2, x1, mod, g_post)


MIX_TM = 512
MIX_SUB = 128
FFN_TM = 512


def _row(v):
    return v.reshape(1, -1)


def _layer(x_prompt, x_sample, cache_k, cache_v, c_prompt, c_sample, rel_table, w_ada, b_ada,
           g_pre_mix, w_in, sinks, v_gain, w_s, b_s, g_attn, g_gmlp, w_out, g_post_mix,
           g_pre_ff, w_ff1, w_ff2, g_post_ff):
    n_b, seq, _ = x_prompt.shape
    n_db, t_len, _ = x_sample.shape
    n_tok = n_b * seq
    D = D_MODEL

    g_pre_mix, g_attn, g_gmlp = _row(g_pre_mix), _row(g_attn), _row(g_gmlp)
    g_post_mix, g_pre_ff, g_post_ff = _row(g_post_mix), _row(g_pre_ff), _row(g_post_ff)
    v_gain = _row(v_gain)

    bias_p, bias_s, wtril = _prep(rel_table, w_s, t_len)
    mod_s, mod_p = _adaln(c_prompt, c_sample, w_ada, b_ada)
    mod_p = mod_p.reshape(n_b, 1, 6 * D)
    SH_M, SC_M, GT_M, SH_F, SC_F, GT_F = range(6)

    def pmod(chunk, tm):
        per = seq // tm
        return pl.BlockSpec((None, 1, D), lambda i: (i // per, 0, chunk))

    def smod(chunk):
        return pl.BlockSpec((n_db, D), lambda i: (0, chunk))

    sd = _sample_dist(t_len)
    valid_s = ((sd >= 0) & (sd < WINDOW) & (np.arange(SMP_KEYS)[None, :] < WINDOW + t_len))
    valid_s = valid_s.astype(np.float32)
    sink2 = sinks.reshape(N_KV_HEADS, GROUP)
    sink_p = jnp.repeat(sink2, BLOCK, axis=1)[:, None, :]
    sink_s = jnp.tile(sink2, (1, t_len)).reshape(-1, 1)
    ga3 = g_attn.reshape(N_KV_HEADS, GROUP, HEAD_DIM)
    gain_t = jnp.repeat(ga3.transpose(0, 2, 1).reshape(KV_COLS, GROUP), BLOCK, axis=1)
    g_attn = ga3.transpose(1, 0, 2).reshape(1, ATTN_WIDTH)

    xp = x_prompt.reshape(n_tok, D)
    tile = lambda w: pl.BlockSpec((MIX_TM, w), lambda i: (i, 0))
    ttile = lambda w: pl.BlockSpec((w, MIX_TM), lambda i: (0, i))
    n_mix = n_tok // MIX_TM
    slab1 = pl.BlockSpec((D // n_mix, D_FF), lambda i: (i, 0))
    assert seq >= WINDOW and MIX_TM >= WINDOW
    mix_per = seq // MIX_TM
    win = pl.BlockSpec((None, KV_COLS, WINDOW), lambda i: (i // mix_per, 0, 0))
    assert n_mix * BLOCK == D and GROUP * BLOCK == ATTN_WIDTH
    piece = lambda off: pl.BlockSpec(
        (HEAD_DIM, D), lambda i: (jnp.where(i < GROUP, i + off * GROUP, 2 * i + off), 0))
    slab_o = pl.BlockSpec((BLOCK, D), lambda i: (i, 0))
    qt_p, k_p, vt_p, gu_p, gvn_p, kwin, vwin, w1_b, w_out_b, w_in_b = _mix_in(
        xp, tile(D), pmod(SH_M, MIX_TM), pmod(SC_M, MIX_TM), mod_p, (n_tok // MIX_TM,),
        g_pre_mix, w_in, v_gain,
        (jax.ShapeDtypeStruct((Q_COLS, n_tok), BF16),
         jax.ShapeDtypeStruct((n_tok, KV_COLS), BF16),
         jax.ShapeDtypeStruct((KV_COLS, n_tok), F32),
         jax.ShapeDtypeStruct((n_tok, GMLP_WIDTH), F32),
         jax.ShapeDtypeStruct((n_tok, GMLP_WIDTH), BF16),
         jax.ShapeDtypeStruct((n_b, KV_COLS, WINDOW), F32),
         jax.ShapeDtypeStruct((n_b, KV_COLS, WINDOW), F32),
         jax.ShapeDtypeStruct((D, D_FF), BF16),
         jax.ShapeDtypeStruct((D, D), BF16),
         jax.ShapeDtypeStruct((D, IN_COLS), BF16)),
        (ttile(Q_COLS), tile(KV_COLS), ttile(KV_COLS), tile(GMLP_WIDTH), tile(GMLP_WIDTH),
         win, win, slab1, slab_o, pl.BlockSpec(memory_space=pl.ANY)),
        side=((w_ff1, w_out, w_out), (slab1, piece(0), piece(1))))

    x1_p, hf_p, w2_b = _mix_core(qt_p, k_p, vt_p, gu_p, gvn_p, bias_p, sink_p,
                                 wtril, b_s.T, gain_t, g_gmlp, seq, w_ff2,
                                 xp, mod_p, pmod, w_out_b, g_post_mix, g_pre_ff)
    one = lambda i: (0, 0)

    ffn_per = seq // FFN_TM
    gate_p = lambda tile: pl.BlockSpec((None, 1, D), lambda s: (tile(s) // ffn_per, 0, GT_F))
    y_p = _ffn(hf_p, w1_b, w2_b, x1_p, mod_p, gate_p, g_post_ff, FFN_TM)

    xs = x_sample.reshape(n_db, t_len * D)
    lane = lambda w: pl.BlockSpec((n_db, w), lambda t: (0, t))
    q_s, k_s, v_s, gu_s, gvn_s = _mix_in(
        xs, lane(D), smod(SH_M), smod(SC_M), mod_s, (t_len,),
        g_pre_mix, w_in_b, v_gain,
        (jax.ShapeDtypeStruct((n_db, t_len * Q_COLS), F32),
         jax.ShapeDtypeStruct((n_db, t_len * KV_COLS), F32),
         jax.ShapeDtypeStruct((n_db, t_len * KV_COLS), F32),
         jax.ShapeDtypeStruct((n_db, t_len * GMLP_WIDTH), F32),
         jax.ShapeDtypeStruct((n_db, t_len * GMLP_WIDTH), F32)),
        (lane(Q_COLS), lane(KV_COLS), lane(KV_COLS), lane(GMLP_WIDTH), lane(GMLP_WIDTH)))

    q5 = q_s.reshape(n_db, t_len, N_KV_HEADS, GROUP, HEAD_DIM).transpose(0, 2, 1, 3, 4)
    q5 = q5.reshape(n_db, N_KV_HEADS, t_len * GROUP, HEAD_DIM).astype(BF16)
    z5 = jnp.zeros_like(q5[:, 0])
    q6 = jnp.concatenate([jnp.concatenate([q5[:, 0], z5], axis=-1),
                          jnp.concatenate([z5, q5[:, 1]], axis=-1)], axis=1)
    new_k = k_s.reshape(n_db, t_len, KV_COLS)
    new_v = v_s.reshape(n_db, t_len, KV_COLS)
    cache_kt = cache_k.transpose(0, 2, 3, 1).reshape(n_db, KV_COLS, WINDOW)
    cache_vt = cache_v.transpose(0, 2, 3, 1).reshape(n_db, KV_COLS, WINDOW)
    o6 = _attn_smp(q6, cache_kt, cache_vt, new_k, new_v, bias_s, jnp.asarray(valid_s), sink_s, t_len)
    o6 = o6.reshape(n_db, N_KV_HEADS, t_len, GROUP, N_KV_HEADS, HEAD_DIM)
    attn_s = jnp.stack([o6[:, kh, :, :, kh, :] for kh in range(N_KV_HEADS)], axis=3)
    attn_s = attn_s.reshape(n_db, t_len * ATTN_WIDTH)
    smem = pl.BlockSpec(memory_space=pltpu.SMEM)
    w4 = w_s[:, :t_len, :t_len].reshape(-1)
    b4 = b_s[:, :t_len].reshape(-1)
    tmaj = lambda: pl.BlockSpec((None, n_db, D), lambda t: (t, 0, 0))
    x1_s, hf_s = pl.pallas_call(
        functools.partial(_out_proj_smp_body, t_len=t_len),
        out_shape=(jax.ShapeDtypeStruct((t_len, n_db, D), F32),
                   jax.ShapeDtypeStruct((t_len, n_db, D), BF16)),
        grid=(t_len,),
        in_specs=[smem, smem, lane(ATTN_WIDTH), lane(GMLP_WIDTH),
                  _resident((n_db, t_len * GMLP_WIDTH), one),
                  _resident((1, ATTN_WIDTH), one), _resident((1, GMLP_WIDTH), one),
                  lane(D), smod(GT_M), smod(SH_F), smod(SC_F),
                  _resident((D, D), one), _resident((1, D), one), _resident((1, D), one)],
        out_specs=(tmaj(), tmaj()),
        compiler_params=_params(1),
        name="out_proj_smp",
    )(w4, b4, attn_s, gu_s, gvn_s, g_attn, g_gmlp, xs, mod_s, mod_s, mod_s,
      w_out_b, g_post_mix, g_pre_ff)

    n_st = t_len * n_db
    y_s = _ffn(hf_s.reshape(n_st, D), w1_b, w2_b, x1_s.reshape(n_st, D), mod_s,
               lambda tile: smod(GT_F), g_post_ff, n_st)
    y_s = y_s.reshape(t_len, n_db, D).transpose(1, 0, 2)

    last = lambda t: t.reshape(n_b, N_KV_HEADS, HEAD_DIM, WINDOW).transpose(0, 3, 1, 2)
    k_p4, v_p4 = last(kwin), last(vwin)
    return (y_p.reshape(n_b, seq, D), y_s, k_p4, v_p4,
            k_s.reshape(n_db, t_len, N_KV_HEADS, HEAD_DIM),
            v_s.reshape(n_db, t_len, N_KV_HEADS, HEAD_DIM),
            gvn_s.reshape(n_db, t_len, GMLP_HEADS, GMLP_HEAD_DIM))


def kernel(x_prompt, x_sample, cache_k, cache_v, c_prompt, c_sample, rel_bias_table, w_ada, b_ada,
           g_pre_mix, w_in, attn_sinks, gmlp_v_gain, gmlp_w_s, gmlp_b_s, g_attn_out, g_gmlp_out,
           w_out, g_post_mix, g_pre_ff, w_ff1, w_ff2, g_post_ff):
    depth = w_in.shape[0]
    assert depth == 1, "single-layer step"
    outs = _layer(x_prompt, x_sample, cache_k[0], cache_v[0], c_prompt, c_sample, rel_bias_table,
                  w_ada[0], b_ada[0], g_pre_mix[0], w_in[0], attn_sinks[0], gmlp_v_gain[0],
                  gmlp_w_s[0], gmlp_b_s[0], g_attn_out[0], g_gmlp_out[0], w_out[0], g_post_mix[0],
                  g_pre_ff[0], w_ff1[0], w_ff2[0], g_post_ff[0])
    y_p, y_s, k_p, v_p, k_s, v_s, gv_s = outs
    return (y_p, y_s, k_p[None], v_p[None], k_s[None], v_s[None], gv_s[None])
```

```python
import functools
import math

import numpy as np
import jax
import jax.numpy as jnp
from jax import lax
from jax.experimental import pallas as pl
from jax.experimental.pallas import tpu as pltpu

D_MODEL = 2048
HEAD_DIM = 64
ATTN_WIDTH = 1024
N_HEADS = 16
N_KV_HEADS = 2
GROUP = 8
WINDOW = 128
BLOCK = 128
GMLP_WIDTH = 1024
GMLP_HEADS = 8
GMLP_HEAD_DIM = 128
D_FF = 4 * D_MODEL
NUM_BUCKETS = 32
MAX_DISTANCE = 128
EPS = 1e-6
KV_COLS = N_KV_HEADS * HEAD_DIM
Q_COLS = N_HEADS * HEAD_DIM
IN_COLS = Q_COLS + 2 * KV_COLS + 2 * GMLP_WIDTH
ATTN_SCALE = HEAD_DIM ** -0.5
NEG_INF = -1e30

V7X_VMEM_BYTES = 64 * 1024 * 1024
VMEM_LIMIT_BYTES = 56 * 1024 * 1024

BF16 = jnp.bfloat16
F32 = jnp.float32


def _params(n_axes, vmem=VMEM_LIMIT_BYTES):
    return pltpu.CompilerParams(
        dimension_semantics=("arbitrary",) * n_axes, vmem_limit_bytes=vmem)


def _resident(shape, index_map):
    return pl.BlockSpec(shape, index_map, pipeline_mode=pl.Buffered(1))


def _rms(x, gain):
    return x * lax.rsqrt(jnp.mean(x * x, axis=-1, keepdims=True) + EPS) * gain


def _rows(m, n_rows):
    r = m.shape[0]
    if r == 1 or r == n_rows:
        return m
    return jnp.concatenate([m] * (n_rows // r), axis=0)


def _dot(a, b):
    return jnp.dot(a, b, preferred_element_type=F32)


def _dot_nt(a, b):
    return lax.dot_general(a, b, (((1,), (1,)), ((), ())), preferred_element_type=F32)


def _t5_bucket(dist):
    n = np.maximum(dist, 0)
    max_exact = NUM_BUCKETS // 2
    nf = np.maximum(n, 1).astype(np.float64)
    large = max_exact + (np.log(nf / max_exact) / math.log(MAX_DISTANCE / max_exact)
                         * (NUM_BUCKETS - max_exact)).astype(np.int32)
    large = np.minimum(large, NUM_BUCKETS - 1)
    return jnp.asarray(np.where(n < max_exact, n, large), jnp.int32)


SMP_KEYS = 2 * WINDOW


def _prompt_dist():
    r = np.arange(BLOCK)[:, None]
    i = np.arange(BLOCK)[None, :]
    return (i - r) % BLOCK


def _sample_rows(t_len):
    r = np.arange(N_KV_HEADS * t_len * GROUP)
    return r // (t_len * GROUP), (r // GROUP) % t_len, r % GROUP


def _sample_dist(t_len):
    _, t, _ = _sample_rows(t_len)
    return WINDOW + t[:, None] - np.arange(SMP_KEYS)[None, :]


def _prep_body(tab_ref, tabr_ref, bkt_p_ref, bkt_s_ref, ws_ref,
               bias_p_ref, bias_s_ref, wtril_ref):
    bkt_p = bkt_p_ref[...]
    for kh in range(N_KV_HEADS):
        for g in range(GROUP):
            h = kh * GROUP + g
            acc = jnp.zeros((BLOCK, BLOCK), F32)
            for b in range(NUM_BUCKETS):
                acc = jnp.where(bkt_p == b, tab_ref[b, h], acc)
            bias_p_ref[kh, :, g * BLOCK:(g + 1) * BLOCK] = acc
    bkt_s = bkt_s_ref[...]
    tabr = tabr_ref[...]
    acc = jnp.zeros(bkt_s.shape, F32)
    for b in range(NUM_BUCKETS):
        acc = jnp.where(bkt_s == b, tabr[:, b:b + 1], acc)
    bias_s_ref[...] = acc
    ii = lax.broadcasted_iota(jnp.int32, (BLOCK, BLOCK), 0)
    jj = lax.broadcasted_iota(jnp.int32, (BLOCK, BLOCK), 1)
    for h in range(GMLP_HEADS):
        wtril_ref[h] = jnp.where(jj <= ii, ws_ref[h], 0.0).astype(BF16)


def _prep(rel_table, w_s, t_len):
    bkt_p = _t5_bucket(_prompt_dist())
    bkt_s = _t5_bucket(_sample_dist(t_len))
    n_rows = bkt_s.shape[0]
    tab_rows = jnp.broadcast_to(rel_table.T.reshape(N_KV_HEADS, 1, GROUP, NUM_BUCKETS),
                                (N_KV_HEADS, t_len, GROUP, NUM_BUCKETS)).reshape(n_rows, NUM_BUCKETS)
    smem = pl.BlockSpec(memory_space=pltpu.SMEM)
    vmem = pl.BlockSpec(memory_space=pltpu.VMEM)
    return pl.pallas_call(
        _prep_body,
        out_shape=(jax.ShapeDtypeStruct((N_KV_HEADS, BLOCK, GROUP * BLOCK), F32),
                   jax.ShapeDtypeStruct((n_rows, SMP_KEYS), F32),
                   jax.ShapeDtypeStruct((GMLP_HEADS, BLOCK, BLOCK), BF16)),
        in_specs=[smem, vmem, vmem, vmem, vmem],
        out_specs=(vmem, vmem, vmem),
        compiler_params=pltpu.CompilerParams(vmem_limit_bytes=32 * 1024 * 1024),
        name="prep",
    )(rel_table, tab_rows, bkt_p, bkt_s, w_s)


ADA_TN = 1024


def _adaln_body(c_ref, w_ref, b_ref, os_ref, op_ref, *, n_s, n_p):
    c = c_ref[...]
    s = (c * jax.nn.sigmoid(c)).astype(BF16)
    r = _dot(s, w_ref[...].astype(BF16)) + b_ref[...]
    os_ref[...] = r[:n_s]
    op_ref[...] = r[n_s:n_s + n_p]


def _adaln(c_prompt, c_sample, w_ada, b_ada):
    n_p, n_s = c_prompt.shape[0], c_sample.shape[0]
    pad = (-(n_p + n_s)) % 16
    c_all = jnp.concatenate([c_sample, c_prompt, jnp.zeros((pad, D_MODEL), F32)], axis=0)
    n_all = c_all.shape[0]
    n_out = w_ada.shape[1]
    return pl.pallas_call(
        functools.partial(_adaln_body, n_s=n_s, n_p=n_p),
        out_shape=(jax.ShapeDtypeStruct((n_s, n_out), F32),
                   jax.ShapeDtypeStruct((n_p, n_out), F32)),
        grid=(n_out // ADA_TN,),
        in_specs=[_resident((n_all, D_MODEL), lambda j: (0, 0)),
                  pl.BlockSpec((D_MODEL, ADA_TN), lambda j: (0, j)),
                  pl.BlockSpec((1, ADA_TN), lambda j: (0, j))],
        out_specs=(pl.BlockSpec((n_s, ADA_TN), lambda j: (0, j)),
                   pl.BlockSpec((n_p, ADA_TN), lambda j: (0, j))),
        compiler_params=_params(1, 40 * 1024 * 1024),
        name="adaln",
    )(c_all, w_ada, b_ada.reshape(1, n_out))


W_STAGE_COLS = 256


def _stage_w_in(w_hbm, wb_ref, stage_ref, sem_ref):
    n = IN_COLS // W_STAGE_COLS

    def fetch(c):
        return pltpu.make_async_copy(w_hbm.at[:, pl.ds(c * W_STAGE_COLS, W_STAGE_COLS)],
                                     stage_ref.at[c % 2], sem_ref.at[c % 2])

    fetch(0).start()
    for c in range(n):
        if c + 1 < n:
            fetch(c + 1).start()
        fetch(c).wait()
        wb_ref[:, c * W_STAGE_COLS:(c + 1) * W_STAGE_COLS] = stage_ref[c % 2].astype(BF16)


def _mix_in_body(x_ref, sh_ref, sc_ref, g_ref, w_ref, vg_ref, *refs, transposed):
    if transposed:
        (wc_ref, woa_ref, wob_ref, q_ref, k_ref, v_ref, gu_ref, gv_ref, kwin_ref, vwin_ref,
         wcb_ref, wo_ref, wib_hbm, wb_ref, stage_ref, sem_ref, wsem_ref) = refs
        write_back = pltpu.make_async_copy(wb_ref, wib_hbm, wsem_ref)

        @pl.when(pl.program_id(0) == 0)
        def _():
            _stage_w_in(w_ref, wb_ref, stage_ref, sem_ref)
            write_back.start()

        @pl.when(pl.program_id(0) == pl.num_programs(0) - 1)
        def _():
            write_back.wait()

        w_ref = wb_ref
        wcb_ref[...] = wc_ref[...].astype(wcb_ref.dtype)
        wo_ref[:HEAD_DIM, :] = woa_ref[...].astype(wo_ref.dtype)
        wo_ref[HEAD_DIM:, :] = wob_ref[...].astype(wo_ref.dtype)
    else:
        q_ref, k_ref, v_ref, gu_ref, gv_ref = refs
    tm = x_ref.shape[0]
    sub = min(tm, MIX_SUB)
    c0, c1, c2, c3 = Q_COLS, Q_COLS + KV_COLS, Q_COLS + 2 * KV_COLS, Q_COLS + 2 * KV_COLS + GMLP_WIDTH
    vg = vg_ref[...]
    for r in range(0, tm, sub):
        rows = slice(r, r + sub)
        mod = lambda ref: ref[...] if ref.shape[0] == 1 else ref[rows, :]
        h = (_rms(x_ref[rows, :], g_ref[...] * (1.0 + mod(sc_ref))) + mod(sh_ref)).astype(BF16)
        q = _dot(h, w_ref[:, :c0]) * ATTN_SCALE
        kv = _dot(h, w_ref[:, c0:c2])
        k_ref[rows, :] = kv[:, :KV_COLS].astype(k_ref.dtype)
        if transposed:
            q_ref[:, rows] = q.T.astype(q_ref.dtype)
            vt = kv[:, KV_COLS:].T
            v_ref[:, rows] = vt
            if r + sub == tm:
                kwin_ref[...] = kv[-WINDOW:, :KV_COLS].T
                vwin_ref[...] = vt[:, -WINDOW:]
        else:
            q_ref[rows, :] = q.astype(q_ref.dtype)
            v_ref[rows, :] = kv[:, KV_COLS:]
        gu_ref[rows, :] = jax.nn.gelu(_dot(h, w_ref[:, c2:c3])).astype(gu_ref.dtype)
        gv = jax.nn.gelu(_dot(h, w_ref[:, c3:]))
        for hd in range(GMLP_HEADS):
            sl = slice(hd * GMLP_HEAD_DIM, (hd + 1) * GMLP_HEAD_DIM)
            gv_ref[rows, sl] = _rms(gv[:, sl], vg[:, sl]).astype(gv_ref.dtype)


def _mix_in(x, x_spec, sh_spec, sc_spec, mod, grid, g_pre, w_in, v_gain, outs, out_specs,
            side=((), ())):
    extra, extra_specs = side
    prompt = len(extra) > 0
    any_space = pl.BlockSpec(memory_space=pl.ANY)
    scratch = [pltpu.VMEM((D_MODEL, IN_COLS), BF16),
               pltpu.VMEM((2, D_MODEL, W_STAGE_COLS), F32),
               pltpu.SemaphoreType.DMA((2,)), pltpu.SemaphoreType.DMA(())] if prompt else []
    return pl.pallas_call(
        functools.partial(_mix_in_body, transposed=prompt),
        out_shape=outs,
        grid=grid,
        in_specs=[x_spec, sh_spec, sc_spec,
                  _resident((1, D_MODEL), lambda i: (0, 0)),
                  any_space if prompt else _resident((D_MODEL, IN_COLS), lambda i: (0, 0)),
                  _resident((1, GMLP_WIDTH), lambda i: (0, 0)), *extra_specs],
        out_specs=out_specs,
        scratch_shapes=scratch,
        compiler_params=_params(1),
        name="mix_in",
    )(x, mod, mod, g_pre, w_in, v_gain, *extra)


PROJ_TN = 256


def _out_proj_epilogue(o, rows, x_ref, gt_ref, sh_ref, sc_ref, gpm_ref, gpf_ref, x1_ref, hf_ref):
    mod = lambda ref: ref[...] if ref.shape[0] == 1 else ref[rows, :]
    x1 = x_ref[rows, :] + _rms(o, gpm_ref[...] * mod(gt_ref))
    x1_ref[rows, :] = x1
    hf = _rms(x1, gpf_ref[...] * (1.0 + mod(sc_ref))) + mod(sh_ref)
    hf_ref[rows, :] = hf.astype(hf_ref.dtype)


CORE_NB = 4


def _mix_core_body(qt_ref, kc_ref, kp_ref, vtc_ref, vtp_ref, gu_ref, gv_ref,
                   bias_ref, pm_ref, sink_ref, wtril_ref, bs_ref, gat_ref, gg_ref, wc_ref,
                   x_ref, gt_ref, sh_ref, sc_ref, wb_ref, gpm_ref, gpf_ref,
                   x1_ref, hf_ref, wcb_ref, *, steps_per_seq):
    wcb_ref[...] = wc_ref[...].astype(wcb_ref.dtype)
    first = pl.program_id(0) % steps_per_seq == 0
    tri = (lax.broadcasted_iota(jnp.int32, (BLOCK, BLOCK), 0)
           > lax.broadcasted_iota(jnp.int32, (BLOCK, BLOCK), 1))
    from_prev = jnp.concatenate([tri] * GROUP, axis=1)
    zeros = jnp.zeros((HEAD_DIM, GROUP * BLOCK), BF16)
    bs = bs_ref[...]
    def mix_stages(u, st):
        tok = slice(u * BLOCK, (u + 1) * BLOCK)

        def logits(kh):
            if kh == 0:
                k_prev = kp_ref[...] if u == 0 else kc_ref[(u - 1) * BLOCK:u * BLOCK, :]
                vt_prev = vtp_ref[...] if u == 0 else vtc_ref[:, (u - 1) * BLOCK:u * BLOCK]
                st['k'] = jnp.concatenate([k_prev, kc_ref[tok, :]], axis=0)
                st['vt'] = jnp.concatenate([vt_prev, vtc_ref[:, tok]], axis=1).astype(BF16)
                st['outs'] = []
            base = kh * GROUP * HEAD_DIM
            qs = jnp.concatenate(
                [qt_ref[base + g * HEAD_DIM: base + (g + 1) * HEAD_DIM, tok] for g in range(GROUP)],
                axis=1)
            qp = jnp.concatenate([qs, zeros] if kh == 0 else [zeros, qs], axis=0)
            st['qk'] = _dot(st['k'], qp)

        def softmax(kh):
            qk = st['qk']
            s = jnp.where(from_prev, qk[:BLOCK], qk[BLOCK:]) + bias_ref[kh]
            if u == 0:
                s = jnp.where(jnp.logical_and(first, from_prev), NEG_INF, s)
            sink = sink_ref[kh]
            m =jnp.maximum(jnp.max(s, axis=0, keepdims=True), sink)
            p = jnp.exp(s - m)
            st['den'] = jnp.sum(p, axis=0, keepdims=True) + jnp.exp(sink - m)
            p = p.astype(BF16)
            st['p2'] = jnp.concatenate([p * pm_ref[0], p * pm_ref[1]], axis=0)

        def values(kh):
            vt = st['vt'][kh * HEAD_DIM:(kh + 1) * HEAD_DIM, :]
            st['outs'].append(_dot(vt, st['p2']) / st['den'])

        def attn_norm():
            ot = jnp.concatenate(st['outs'], axis=0)
            sq = jnp.sum(ot * ot, axis=0, keepdims=True)
            tot = sq[:, :BLOCK]
            for g in range(1, GROUP):
                tot = tot + sq[:, g * BLOCK:(g + 1) * BLOCK]
            r = lax.rsqrt(tot / ATTN_WIDTH + EPS)
            a = ot * jnp.concatenate([r] * GROUP, axis=1) * gat_ref[...]
            st['merged'] = [a[:, g * BLOCK:(g + 1) * BLOCK].T.astype(BF16) for g in range(GROUP)]

        def gating():
            gated = []
            for hd in range(GMLP_HEADS):
                sl = slice(hd * GMLP_HEAD_DIM, (hd + 1) * GMLP_HEAD_DIM)
                mixed = _dot(wtril_ref[hd], gv_ref[tok, sl]) + bs[:, hd:hd + 1]
                gated.append(gu_ref[tok, sl] * mixed)
            gm = jnp.concatenate(gated, axis=-1)
            st['merged'] = jnp.concatenate(
                st['merged'] + [_rms(gm, gg_ref[...]).astype(BF16)], axis=1)

        P = functools.partial
        return [P(logits, 0), P(softmax, 0), P(values, 0), P(logits, 1), P(softmax, 1),
                P(values, 1), attn_norm, gating]

    def proj_stages(u, st):
        tok = slice(u * BLOCK, (u + 1) * BLOCK)
        chunks = []

        def chunk(c):
            chunks.append(_dot(st['merged'], wb_ref[:, c * PROJ_TN:(c + 1) * PROJ_TN]))

        def finish():
            _out_proj_epilogue(jnp.concatenate(chunks, axis=1), tok, x_ref, gt_ref, sh_ref,
                               sc_ref, gpm_ref, gpf_ref, x1_ref, hf_ref)

        return [functools.partial(chunk, c) for c in range(D_MODEL // PROJ_TN)] + [finish]

    states = [dict() for _ in range(CORE_NB)]
    for u in range(CORE_NB + 1):
        a = mix_stages(u, states[u]) if u < CORE_NB else []
        b = proj_stages(u - 1, states[u - 1]) if u > 0 else []
        for i in range(max(len(a), len(b))):
            for stage in (b[i:i + 1] + a[i:i + 1]):
                stage()


def _mix_core(qt, k, vt, gu, gvn, bias_t, sink_row, wtril, bs_t, gain_t, g_gmlp, seq, w_side,
              x, mod, mod_spec, w_out_b, g_post_mix, g_pre_ff):
    r_gt_i = np.tile(np.arange(BLOCK)[:, None] > np.arange(BLOCK)[None, :], (1, GROUP))
    pmask = jnp.asarray(np.stack([r_gt_i, ~r_gt_i]), BF16)
    n_tok = k.shape[0]
    tm = CORE_NB * BLOCK
    n_steps = n_tok // tm
    side_rows = w_side.shape[0] // n_steps
    GT_M, SH_F, SC_F = 2, 3, 4
    cur = lambda i: (i, 0)
    prev = lambda i: (jnp.maximum(i * CORE_NB - 1, 0), 0)
    cur_t = lambda i: (0, i)
    prev_t = lambda i: (0, jnp.maximum(i * CORE_NB - 1, 0))
    full2 = lambda i: (0, 0)
    full3 = lambda i: (0, 0, 0)
    return pl.pallas_call(
        functools.partial(_mix_core_body, steps_per_seq=seq // tm),
        out_shape=(jax.ShapeDtypeStruct((n_tok, D_MODEL), F32),
                   jax.ShapeDtypeStruct((n_tok, D_MODEL), BF16),
                   jax.ShapeDtypeStruct(w_side.shape, BF16)),
        grid=(n_steps,),
        in_specs=[pl.BlockSpec((Q_COLS, tm), cur_t),
                  pl.BlockSpec((tm, KV_COLS), cur), pl.BlockSpec((BLOCK, KV_COLS), prev),
                  pl.BlockSpec((KV_COLS, tm), cur_t), pl.BlockSpec((KV_COLS, BLOCK), prev_t),
                  pl.BlockSpec((tm, GMLP_WIDTH), cur), pl.BlockSpec((tm, GMLP_WIDTH), cur),
                  _resident(bias_t.shape, full3), _resident(pmask.shape, full3),
                  _resident(sink_row.shape, full3), _resident(wtril.shape, full3),
                  _resident(bs_t.shape, full2),
                  _resident(gain_t.shape, full2), _resident((1, GMLP_WIDTH), full2),
                  pl.BlockSpec((side_rows, w_side.shape[1]), cur),
                  pl.BlockSpec((tm, D_MODEL), cur),
                  mod_spec(GT_M, tm), mod_spec(SH_F, tm), mod_spec(SC_F, tm),
                  _resident((D_MODEL, D_MODEL), full2),
                  _resident((1, D_MODEL), full2), _resident((1, D_MODEL), full2)],
        out_specs=(pl.BlockSpec((tm, D_MODEL), cur), pl.BlockSpec((tm, D_MODEL), cur),
                   pl.BlockSpec((side_rows, w_side.shape[1]), cur)),
        compiler_params=_params(1, 58 * 1024 * 1024),
        name="mix_core",
    )(qt, k, k, vt, vt, gu, gvn, bias_t, pmask, sink_row, wtril, bs_t, gain_t, g_gmlp, w_side,
      x, mod, mod, mod, w_out_b, g_post_mix, g_pre_ff)


SMP_BB = 16


def _attn_smp_body(q_ref, ckt_ref, cvt_ref, nk_ref, nv_ref, bias_ref, valid_ref, sink_ref,
                   o_ref, *, t_len):
    zpad = jnp.zeros((WINDOW - t_len, KV_COLS), F32)
    tiles = []
    for b in range(SMP_BB):
        q = q_ref[b]
        s_cache = _dot(q, ckt_ref[b].astype(BF16))
        kn = jnp.concatenate([nk_ref[b], zpad], axis=0).astype(BF16)
        tiles.append(jnp.concatenate([s_cache, _dot_nt(q, kn)], axis=1)[None])
    s = jnp.concatenate(tiles, axis=0)
    s = jnp.where(valid_ref[...] > 0.5, s + bias_ref[...], NEG_INF)
    sink = sink_ref[...]
    m = jnp.maximum(jnp.max(s, axis=-1, keepdims=True), sink)
    p = jnp.exp(s - m)
    den = jnp.sum(p, axis=-1, keepdims=True) + jnp.exp(sink - m)
    p = p.astype(BF16)
    for b in range(SMP_BB):
        vn = jnp.concatenate([nv_ref[b], zpad], axis=0).astype(BF16)
        o = _dot_nt(p[b, :, :WINDOW], cvt_ref[b].astype(BF16)) + _dot(p[b, :, WINDOW:], vn)
        o_ref[b] = o / den[b]


def _attn_smp(q6, cache_kt, cache_vt, new_k, new_v, bias_s, valid_s, sink_s, t_len):
    n_b, rows, _ = q6.shape
    b3 = lambda i: (i, 0, 0)
    one = lambda i: (0, 0)
    return pl.pallas_call(
        functools.partial(_attn_smp_body, t_len=t_len),
        out_shape=jax.ShapeDtypeStruct((n_b, rows, KV_COLS), F32),
        grid=(n_b // SMP_BB,),
        in_specs=[pl.BlockSpec((SMP_BB, rows, KV_COLS), b3),
                  pl.BlockSpec((SMP_BB, KV_COLS, WINDOW), b3),
                  pl.BlockSpec((SMP_BB, KV_COLS, WINDOW), b3),
                  pl.BlockSpec((SMP_BB, t_len, KV_COLS), b3),
                  pl.BlockSpec((SMP_BB, t_len, KV_COLS), b3),
                  _resident(bias_s.shape, one), _resident(valid_s.shape, one),
                  _resident(sink_s.shape, one)],
        out_specs=pl.BlockSpec((SMP_BB, rows, KV_COLS), b3),
        compiler_params=_params(1, 32 * 1024 * 1024),
        name="attn_smp",
    )(q6, cache_kt, cache_vt, new_k, new_v, bias_s, valid_s, sink_s)


def _out_proj_smp_body(w4_ref, b4_ref, a_ref, gu_ref, gv_ref, ga_ref, gg_ref,
                       x_ref, gt_ref, sh_ref, sc_ref, wb_ref, gpm_ref, gpf_ref, x1_ref, hf_ref, *,
                       t_len):
    i = pl.program_id(0)
    gated = []
    for hd in range(GMLP_HEADS):
        acc = jnp.zeros((a_ref.shape[0], GMLP_HEAD_DIM), F32)
        for j in range(t_len):
            w = jnp.where(j <= i, w4_ref[(hd * t_len + i) * t_len + j], 0.0)
            lo = j * GMLP_WIDTH + hd * GMLP_HEAD_DIM
            acc = acc + w * gv_ref[:, lo:lo + GMLP_HEAD_DIM]
        mixed = acc + b4_ref[hd * t_len + i]
        gated.append(gu_ref[:, hd * GMLP_HEAD_DIM:(hd + 1) * GMLP_HEAD_DIM] * mixed)
    gm = jnp.concatenate(gated, axis=-1)
    merged = jnp.concatenate([_rms(a_ref[...], ga_ref[...]), _rms(gm, gg_ref[...])], axis=-1)
    _out_proj_epilogue(_dot(merged.astype(BF16), wb_ref[...]), slice(0, a_ref.shape[0]),
                       x_ref, gt_ref, sh_ref, sc_ref, gpm_ref, gpf_ref, x1_ref, hf_ref)


FFN_TF = 1024
FFN_SUB = 256


def _ffn_body(h_ref, w1_hbm, w2_hbm, x1_ref, gt_ref, g_ref, o_ref,
              acc_ref, done_ref, w1_buf, w2_buf, sem_ref, *, n_tiles, n_chunks):
    i = pl.program_id(0)
    tm = o_ref.shape[0]
    whole = slice(0, tm)

    def copies(c, slot):
        off = c * FFN_TF
        cols = pl.ds(off if isinstance(off, int) else pl.multiple_of(off, FFN_TF), FFN_TF)
        return (pltpu.make_async_copy(w1_hbm.at[:, cols], w1_buf.at[slot], sem_ref.at[0, slot]),
                pltpu.make_async_copy(w2_hbm.at[cols, :], w2_buf.at[slot], sem_ref.at[1, slot]))

    def start(c, slot):
        for cp in copies(c, slot):
            cp.start()

    def wait(c, slot):
        for cp in copies(c, slot):
            cp.wait()

    def contribution(rows, slot):
        a = jnp.maximum(_dot(h_ref[rows, :], w1_buf[slot]), 0.0)
        return _dot((a * a).astype(BF16), w2_buf[slot])

    def finalize(rows):
        gt = gt_ref[...] if gt_ref.shape[0] == 1 else _rows(gt_ref[...], tm)[rows, :]
        o_ref[rows, :] = x1_ref[rows, :] + _rms(done_ref[rows, :], g_ref[...] * gt)

    @pl.when(i < n_tiles)
    def _():
        @pl.when(i == 0)
        def _():
            start(0, 0)

        wait(0, 0)
        start(1, 1)

        @pl.when(i == 0)
        def _():
            acc_ref[...] = contribution(whole, 0)

        @pl.when(i > 0)
        def _():
            for r in range(0, tm, FFN_SUB):
                rows = slice(r, r + FFN_SUB)
                acc_ref[rows, :] = contribution(rows, 0)
                finalize(rows)

        def chunk_step(j, carry):
            slot = j % 2
            wait(j, slot)
            start(j + 1, 1 - slot)
            acc_ref[...] += contribution(whole, slot)
            return carry

        lax.fori_loop(1, n_chunks - 1, chunk_step, 0)
        last = n_chunks - 1
        wait(last, last % 2)

        @pl.when(i < n_tiles - 1)
        def _():
            start(0, 0)

        done_ref[...] = acc_ref[...] + contribution(whole, last % 2)

    @pl.when(i == n_tiles)
    def _():
        finalize(whole)


def _ffn(hf, w1, w2, x1, mod, gt_spec, g_post, tm):
    n_tok = hf.shape[0]
    n_tiles, n_chunks = n_tok // tm, D_FF // FFN_TF
    assert n_chunks % 2 == 0 and n_chunks >= 4
    mm_tile = lambda i: jnp.minimum(i, n_tiles - 1)
    out_tile = lambda i: jnp.maximum(i - 1, 0)
    any_space = pl.BlockSpec(memory_space=pl.ANY)
    return pl.pallas_call(
        functools.partial(_ffn_body, n_tiles=n_tiles, n_chunks=n_chunks),
        out_shape=jax.ShapeDtypeStruct((n_tok, D_MODEL), F32),
        grid=(n_tiles + 1,),
        in_specs=[pl.BlockSpec((tm, D_MODEL), lambda i: (mm_tile(i), 0)),
                  any_space, any_space,
                  pl.BlockSpec((tm, D_MODEL), lambda i: (out_tile(i), 0)),
                  gt_spec(out_tile),
                  _resident((1, D_MODEL), lambda i: (0, 0))],
        out_specs=pl.BlockSpec((tm, D_MODEL), lambda i: (out_tile(i), 0)),
        scratch_shapes=[pltpu.VMEM((tm, D_MODEL), F32), pltpu.VMEM((tm, D_MODEL), F32),
                        pltpu.VMEM((2, D_MODEL, FFN_TF), BF16), pltpu.VMEM((2, FFN_TF, D_MODEL), BF16),
                        pltpu.SemaphoreType.DMA((2, 2))],
        compiler_params=_params(1),
        name="ffn",
    )(hf, w1, w2, x1, mod, g_post)


MIX_TM = 512
MIX_SUB = 128
FFN_TM = 512


def _row(v):
    return v.reshape(1, -1)


def _layer(x_prompt, x_sample, cache_k, cache_v, c_prompt, c_sample, rel_table, w_ada, b_ada,
           g_pre_mix, w_in, sinks, v_gain, w_s, b_s, g_attn, g_gmlp, w_out, g_post_mix,
           g_pre_ff, w_ff1, w_ff2, g_post_ff):
    n_b, seq, _ = x_prompt.shape
    n_db, t_len, _ = x_sample.shape
    n_tok = n_b * seq
    D = D_MODEL

    g_pre_mix, g_attn, g_gmlp = _row(g_pre_mix), _row(g_attn), _row(g_gmlp)
    g_post_mix, g_pre_ff, g_post_ff = _row(g_post_mix), _row(g_pre_ff), _row(g_post_ff)
    v_gain = _row(v_gain)

    bias_p, bias_s, wtril = _prep(rel_table, w_s, t_len)
    mod_s, mod_p = _adaln(c_prompt, c_sample, w_ada, b_ada)
    mod_p = mod_p.reshape(n_b, 1, 6 * D)
    SH_M, SC_M, GT_M, SH_F, SC_F, GT_F = range(6)

    def pmod(chunk, tm):
        per = seq // tm
        return pl.BlockSpec((None, 1, D), lambda i: (i // per, 0, chunk))

    def smod(chunk):
        return pl.BlockSpec((n_db, D), lambda i: (0, chunk))

    sd = _sample_dist(t_len)
    valid_s = ((sd >= 0) & (sd < WINDOW) & (np.arange(SMP_KEYS)[None, :] < WINDOW + t_len))
    valid_s = valid_s.astype(np.float32)
    sink2 = sinks.reshape(N_KV_HEADS, GROUP)
    sink_p = jnp.repeat(sink2, BLOCK, axis=1)[:, None, :]
    sink_s = jnp.tile(sink2, (1, t_len)).reshape(-1, 1)
    ga3 = g_attn.reshape(N_KV_HEADS, GROUP, HEAD_DIM)
    gain_t = jnp.repeat(ga3.transpose(0, 2, 1).reshape(KV_COLS, GROUP), BLOCK, axis=1)
    g_attn = ga3.transpose(1, 0, 2).reshape(1, ATTN_WIDTH)

    xp = x_prompt.reshape(n_tok, D)
    tile = lambda w: pl.BlockSpec((MIX_TM, w), lambda i: (i, 0))
    ttile = lambda w: pl.BlockSpec((w, MIX_TM), lambda i: (0, i))
    n_mix = n_tok // MIX_TM
    slab1 = pl.BlockSpec((D // n_mix, D_FF), lambda i: (i, 0))
    assert seq >= WINDOW and MIX_TM >= WINDOW
    mix_per = seq // MIX_TM
    win = pl.BlockSpec((None, KV_COLS, WINDOW), lambda i: (i // mix_per, 0, 0))
    assert n_mix * BLOCK == D and GROUP * BLOCK == ATTN_WIDTH
    piece = lambda off: pl.BlockSpec(
        (HEAD_DIM, D), lambda i: (jnp.where(i < GROUP, i + off * GROUP, 2 * i + off), 0))
    slab_o = pl.BlockSpec((BLOCK, D), lambda i: (i, 0))
    qt_p, k_p, vt_p, gu_p, gvn_p, kwin, vwin, w1_b, w_out_b, w_in_b = _mix_in(
        xp, tile(D), pmod(SH_M, MIX_TM), pmod(SC_M, MIX_TM), mod_p, (n_tok // MIX_TM,),
        g_pre_mix, w_in, v_gain,
        (jax.ShapeDtypeStruct((Q_COLS, n_tok), BF16),
         jax.ShapeDtypeStruct((n_tok, KV_COLS), BF16),
         jax.ShapeDtypeStruct((KV_COLS, n_tok), F32),
         jax.ShapeDtypeStruct((n_tok, GMLP_WIDTH), F32),
         jax.ShapeDtypeStruct((n_tok, GMLP_WIDTH), BF16),
         jax.ShapeDtypeStruct((n_b, KV_COLS, WINDOW), F32),
         jax.ShapeDtypeStruct((n_b, KV_COLS, WINDOW), F32),
         jax.ShapeDtypeStruct((D, D_FF), BF16),
         jax.ShapeDtypeStruct((D, D), BF16),
         jax.ShapeDtypeStruct((D, IN_COLS), BF16)),
        (ttile(Q_COLS), tile(KV_COLS), ttile(KV_COLS), tile(GMLP_WIDTH), tile(GMLP_WIDTH),
         win, win, slab1, slab_o, pl.BlockSpec(memory_space=pl.ANY)),
        side=((w_ff1, w_out, w_out), (slab1, piece(0), piece(1))))

    x1_p, hf_p, w2_b = _mix_core(qt_p, k_p, vt_p, gu_p, gvn_p, bias_p, sink_p,
                                 wtril, b_s.T, gain_t, g_gmlp, seq, w_ff2,
                                 xp, mod_p, pmod, w_out_b, g_post_mix, g_pre_ff)
    one = lambda i: (0, 0)

    ffn_per = seq // FFN_TM
    gate_p = lambda tile: pl.BlockSpec((None, 1, D), lambda s: (tile(s) // ffn_per, 0, GT_F))
    y_p = _ffn(hf_p, w1_b, w2_b, x1_p, mod_p, gate_p, g_post_ff, FFN_TM)

    xs = x_sample.reshape(n_db, t_len * D)
    lane = lambda w: pl.BlockSpec((n_db, w), lambda t: (0, t))
    q_s, k_s, v_s, gu_s, gvn_s = _mix_in(
        xs, lane(D), smod(SH_M), smod(SC_M), mod_s, (t_len,),
        g_pre_mix, w_in_b, v_gain,
        (jax.ShapeDtypeStruct((n_db, t_len * Q_COLS), F32),
         jax.ShapeDtypeStruct((n_db, t_len * KV_COLS), F32),
         jax.ShapeDtypeStruct((n_db, t_len * KV_COLS), F32),
         jax.ShapeDtypeStruct((n_db, t_len * GMLP_WIDTH), F32),
         jax.ShapeDtypeStruct((n_db, t_len * GMLP_WIDTH), F32)),
        (lane(Q_COLS), lane(KV_COLS), lane(KV_COLS), lane(GMLP_WIDTH), lane(GMLP_WIDTH)))

    q5 = q_s.reshape(n_db, t_len, N_KV_HEADS, GROUP, HEAD_DIM).transpose(0, 2, 1, 3, 4)
    q5 = q5.reshape(n_db, N_KV_HEADS, t_len * GROUP, HEAD_DIM).astype(BF16)
    z5 = jnp.zeros_like(q5[:, 0])
    q6 = jnp.concatenate([jnp.concatenate([q5[:, 0], z5], axis=-1),
                          jnp.concatenate([z5, q5[:, 1]], axis=-1)], axis=1)
    new_k = k_s.reshape(n_db, t_len, KV_COLS)
    new_v = v_s.reshape(n_db, t_len, KV_COLS)
    cache_kt = cache_k.transpose(0, 2, 3, 1).reshape(n_db, KV_COLS, WINDOW)
    cache_vt = cache_v.transpose(0, 2, 3, 1).reshape(n_db, KV_COLS, WINDOW)
    o6 = _attn_smp(q6, cache_kt, cache_vt, new_k, new_v, bias_s, jnp.asarray(valid_s), sink_s, t_len)
    o6 = o6.reshape(n_db, N_KV_HEADS, t_len, GROUP, N_KV_HEADS, HEAD_DIM)
    attn_s = jnp.stack([o6[:, kh, :, :, kh, :] for kh in range(N_KV_HEADS)], axis=3)
    attn_s = attn_s.reshape(n_db, t_len * ATTN_WIDTH)
    smem = pl.BlockSpec(memory_space=pltpu.SMEM)
    w4 = w_s[:, :t_len, :t_len].reshape(-1)
    b4 = b_s[:, :t_len].reshape(-1)
    tmaj = lambda: pl.BlockSpec((None, n_db, D), lambda t: (t, 0, 0))
    x1_s, hf_s = pl.pallas_call(
        functools.partial(_out_proj_smp_body, t_len=t_len),
        out_shape=(jax.ShapeDtypeStruct((t_len, n_db, D), F32),
                   jax.ShapeDtypeStruct((t_len, n_db, D), BF16)),
        grid=(t_len,),
        in_specs=[smem, smem, lane(ATTN_WIDTH), lane(GMLP_WIDTH),
                  _resident((n_db, t_len * GMLP_WIDTH), one),
                  _resident((1, ATTN_WIDTH), one), _resident((1, GMLP_WIDTH), one),
                  lane(D), smod(GT_M), smod(SH_F), smod(SC_F),
                  _resident((D, D), one), _resident((1, D), one), _resident((1, D), one)],
        out_specs=(tmaj(), tmaj()),
        compiler_params=_params(1),
        name="out_proj_smp",
    )(w4, b4, attn_s, gu_s, gvn_s, g_attn, g_gmlp, xs, mod_s, mod_s, mod_s,
      w_out_b, g_post_mix, g_pre_ff)

    n_st = t_len * n_db
    y_s = _ffn(hf_s.reshape(n_st, D), w1_b, w2_b, x1_s.reshape(n_st, D), mod_s,
               lambda tile: smod(GT_F), g_post_ff, n_st)
    y_s = y_s.reshape(t_len, n_db, D).transpose(1, 0, 2)

    last = lambda t: t.reshape(n_b, N_KV_HEADS, HEAD_DIM, WINDOW).transpose(0, 3, 1, 2)
    k_p4, v_p4 = last(kwin), last(vwin)
    return (y_p.reshape(n_b, seq, D), y_s, k_p4, v_p4,
            k_s.reshape(n_db, t_len, N_KV_HEADS, HEAD_DIM),
            v_s.reshape(n_db, t_len, N_KV_HEADS, HEAD_DIM),
            gvn_s.reshape(n_db, t_len, GMLP_HEADS, GMLP_HEAD_DIM))


def kernel(x_prompt, x_sample, cache_k, cache_v, c_prompt, c_sample, rel_bias_table, w_ada, b_ada,
           g_pre_mix, w_in, attn_sinks, gmlp_v_gain, gmlp_w_s, gmlp_b_s, g_attn_out, g_gmlp_out,
           w_out, g_post_mix, g_pre_ff, w_ff1, w_ff2, g_post_ff):
    depth = w_in.shape[0]
    assert depth == 1, "single-layer step"
    outs = _layer(x_prompt, x_sample, cache_k[0], cache_v[0], c_prompt, c_sample, rel_bias_table,
                  w_ada[0], b_ada[0], g_pre_mix[0], w_in[0], attn_sinks[0], gmlp_v_gain[0],
                  gmlp_w_s[0], gmlp_b_s[0], g_attn_out[0], g_gmlp_out[0], w_out[0], g_post_mix[0],
                  g_pre_ff[0], w_ff1[0], w_ff2[0], g_post_ff[0])
    y_p, y_s, k_p, v_p, k_s, v_s, gv_s = outs
    return (y_p, y_s, k_p[None], v_p[None], k_s[None], v_s[None], gv_s[None])
```

```python
import functools
import math

import numpy as np
import jax
import jax.numpy as jnp
from jax import lax
from jax.experimental import pallas as pl
from jax.experimental.pallas import tpu as pltpu

D_MODEL = 2048
HEAD_DIM = 64
ATTN_WIDTH = 1024
N_HEADS = 16
N_KV_HEADS = 2
GROUP = 8
WINDOW = 128
BLOCK = 128
GMLP_WIDTH = 1024
GMLP_HEADS = 8
GMLP_HEAD_DIM = 128
D_FF = 4 * D_MODEL
NUM_BUCKETS = 32
MAX_DISTANCE = 128
EPS = 1e-6
KV_COLS = N_KV_HEADS * HEAD_DIM
Q_COLS = N_HEADS * HEAD_DIM
IN_COLS = Q_COLS + 2 * KV_COLS + 2 * GMLP_WIDTH
ATTN_SCALE = HEAD_DIM ** -0.5
NEG_INF = -1e30

V7X_VMEM_BYTES = 64 * 1024 * 1024
VMEM_LIMIT_BYTES = 56 * 1024 * 1024

BF16 = jnp.bfloat16
F32 = jnp.float32


def _params(n_axes, vmem=VMEM_LIMIT_BYTES):
    return pltpu.CompilerParams(
        dimension_semantics=("arbitrary",) * n_axes, vmem_limit_bytes=vmem)


def _resident(shape, index_map):
    return pl.BlockSpec(shape, index_map, pipeline_mode=pl.Buffered(1))


def _rms(x, gain):
    return x * lax.rsqrt(jnp.mean(x * x, axis=-1, keepdims=True) + EPS) * gain


def _rows(m, n_rows):
    r = m.shape[0]
    if r == 1 or r == n_rows:
        return m
    return jnp.concatenate([m] * (n_rows // r), axis=0)


def _dot(a, b):
    return jnp.dot(a, b, preferred_element_type=F32)


def _dot_nt(a, b):
    return lax.dot_general(a, b, (((1,), (1,)), ((), ())), preferred_element_type=F32)


def _t5_bucket(dist):
    n = np.maximum(dist, 0)
    max_exact = NUM_BUCKETS // 2
    nf = np.maximum(n, 1).astype(np.float64)
    large = max_exact + (np.log(nf / max_exact) / math.log(MAX_DISTANCE / max_exact)
                         * (NUM_BUCKETS - max_exact)).astype(np.int32)
    large = np.minimum(large, NUM_BUCKETS - 1)
    return jnp.asarray(np.where(n < max_exact, n, large), jnp.int32)


SMP_KEYS = 2 * WINDOW


def _prompt_dist():
    r = np.arange(BLOCK)[:, None]
    i = np.arange(BLOCK)[None, :]
    return (i - r) % BLOCK


def _sample_rows(t_len):
    r = np.arange(N_KV_HEADS * t_len * GROUP)
    return r // (t_len * GROUP), (r // GROUP) % t_len, r % GROUP


def _sample_dist(t_len):
    _, t, _ = _sample_rows(t_len)
    return WINDOW + t[:, None] - np.arange(SMP_KEYS)[None, :]


def _prep_body(tab_ref, tabr_ref, bkt_p_ref, bkt_s_ref, ws_ref,
               bias_p_ref, bias_s_ref, wtril_ref):
    bkt_p = bkt_p_ref[...]
    for kh in range(N_KV_HEADS):
        for g in range(GROUP):
            h = kh * GROUP + g
            acc = jnp.zeros((BLOCK, BLOCK), F32)
            for b in range(NUM_BUCKETS):
                acc = jnp.where(bkt_p == b, tab_ref[b, h], acc)
            bias_p_ref[kh, :, g * BLOCK:(g + 1) * BLOCK] = acc
    bkt_s = bkt_s_ref[...]
    tabr = tabr_ref[...]
    acc = jnp.zeros(bkt_s.shape, F32)
    for b in range(NUM_BUCKETS):
        acc = jnp.where(bkt_s == b, tabr[:, b:b + 1], acc)
    bias_s_ref[...] = acc
    ii = lax.broadcasted_iota(jnp.int32, (BLOCK, BLOCK), 0)
    jj = lax.broadcasted_iota(jnp.int32, (BLOCK, BLOCK), 1)
    for h in range(GMLP_HEADS):
        wtril_ref[h] = jnp.where(jj <= ii, ws_ref[h], 0.0).astype(BF16)


def _prep(rel_table, w_s, t_len):
    bkt_p = _t5_bucket(_prompt_dist())
    bkt_s = _t5_bucket(_sample_dist(t_len))
    n_rows = bkt_s.shape[0]
    tab_rows = jnp.broadcast_to(rel_table.T.reshape(N_KV_HEADS, 1, GROUP, NUM_BUCKETS),
                                (N_KV_HEADS, t_len, GROUP, NUM_BUCKETS)).reshape(n_rows, NUM_BUCKETS)
    smem = pl.BlockSpec(memory_space=pltpu.SMEM)
    vmem = pl.BlockSpec(memory_space=pltpu.VMEM)
    return pl.pallas_call(
        _prep_body,
        out_shape=(jax.ShapeDtypeStruct((N_KV_HEADS, BLOCK, GROUP * BLOCK), F32),
                   jax.ShapeDtypeStruct((n_rows, SMP_KEYS), F32),
                   jax.ShapeDtypeStruct((GMLP_HEADS, BLOCK, BLOCK), BF16)),
        in_specs=[smem, vmem, vmem, vmem, vmem],
        out_specs=(vmem, vmem, vmem),
        compiler_params=pltpu.CompilerParams(vmem_limit_bytes=32 * 1024 * 1024),
        name="prep",
    )(rel_table, tab_rows, bkt_p, bkt_s, w_s)


ADA_TN = 1024


def _adaln_body(c_ref, w_ref, b_ref, os_ref, op_ref):
    n_s, n_p = os_ref.shape[0], op_ref.shape[0]
    c = c_ref[...]
    s = (c * jax.nn.sigmoid(c)).astype(BF16)
    r = _dot(s, w_ref[...].astype(BF16)) + b_ref[...]
    os_ref[...] = r[:n_s]
    for p in range(n_p):
        op_ref[p] = r[n_s + p:n_s + p + 1]


def _adaln(c_all, n_s, n_p, w_ada, b_ada, n_out):
    n_all = c_all.shape[0]
    return pl.pallas_call(
        _adaln_body,
        out_shape=(jax.ShapeDtypeStruct((n_s, n_out), F32),
                   jax.ShapeDtypeStruct((n_p, 1, n_out), F32)),
        grid=(n_out // ADA_TN,),
        in_specs=[_resident((n_all, D_MODEL), lambda j: (0, 0)),
                  pl.BlockSpec((D_MODEL, ADA_TN), lambda j: (0, j)),
                  pl.BlockSpec((1, ADA_TN), lambda j: (0, j))],
        out_specs=(pl.BlockSpec((n_s, ADA_TN), lambda j: (0, j)),
                   pl.BlockSpec((n_p, 1, ADA_TN), lambda j: (0, 0, j))),
        compiler_params=_params(1, 40 * 1024 * 1024),
        name="adaln",
    )(c_all, w_ada, b_ada)


W_STAGE_COLS = 256


def _stage_w_in(w_hbm, wb_ref, stage_ref, sem_ref):
    n = IN_COLS // W_STAGE_COLS

    def fetch(c):
        return pltpu.make_async_copy(w_hbm.at[:, pl.ds(c * W_STAGE_COLS, W_STAGE_COLS)],
                                     stage_ref.at[c % 2], sem_ref.at[c % 2])

    fetch(0).start()
    for c in range(n):
        if c + 1 < n:
            fetch(c + 1).start()
        fetch(c).wait()
        wb_ref[:, c * W_STAGE_COLS:(c + 1) * W_STAGE_COLS] = stage_ref[c % 2].astype(BF16)


def _mix_in_body(x_ref, sh_ref, sc_ref, g_ref, w_ref, vg_ref, *refs, transposed):
    if transposed:
        (wc_ref, woa_ref, wob_ref, ca_ref, wa_ref, ba_ref,
         q_ref, k_ref, v_ref, gu_ref, gv_ref, kwin_ref, vwin_ref,
         wcb_ref, wo_ref, wib_hbm, ms_ref, mp_ref, wb_ref, stage_ref, sem_ref, wsem_ref) = refs
        write_back = pltpu.make_async_copy(wb_ref, wib_hbm, wsem_ref)

        @pl.when(pl.program_id(0) == 0)
        def _():
            _stage_w_in(w_ref, wb_ref, stage_ref, sem_ref)
            write_back.start()

        @pl.when(pl.program_id(0) == pl.num_programs(0) - 1)
        def _():
            write_back.wait()

        w_ref = wb_ref
        wcb_ref[...] = wc_ref[...].astype(wcb_ref.dtype)
        wo_ref[:HEAD_DIM, :] = woa_ref[...].astype(wo_ref.dtype)
        wo_ref[HEAD_DIM:, :] = wob_ref[...].astype(wo_ref.dtype)
        _adaln_body(ca_ref, wa_ref, ba_ref, ms_ref, mp_ref)
    else:
        q_ref, k_ref, v_ref, gu_ref, gv_ref = refs
    tm = x_ref.shape[0]
    sub = min(tm, MIX_SUB)
    c0, c1, c2, c3 = Q_COLS, Q_COLS + KV_COLS, Q_COLS + 2 * KV_COLS, Q_COLS + 2 * KV_COLS + GMLP_WIDTH
    vg = vg_ref[...]
    for r in range(0, tm, sub):
        rows = slice(r, r + sub)
        mod = lambda ref: ref[...] if ref.shape[0] == 1 else ref[rows, :]
        h = (_rms(x_ref[rows, :], g_ref[...] * (1.0 + mod(sc_ref))) + mod(sh_ref)).astype(BF16)
        q = _dot(h, w_ref[:, :c0]) * ATTN_SCALE
        kv = _dot(h, w_ref[:, c0:c2])
        k_ref[rows, :] = kv[:, :KV_COLS].astype(k_ref.dtype)
        if transposed:
            q_ref[:, rows] = q.T.astype(q_ref.dtype)
            vt = kv[:, KV_COLS:].T
            v_ref[:, rows] = vt
            if r + sub == tm:
                kwin_ref[...] = kv[-WINDOW:, :KV_COLS].T
                vwin_ref[...] = vt[:, -WINDOW:]
        else:
            q_ref[rows, :] = q.astype(q_ref.dtype)
            v_ref[rows, :] = kv[:, KV_COLS:]
        gu_ref[rows, :] = jax.nn.gelu(_dot(h, w_ref[:, c2:c3])).astype(gu_ref.dtype)
        gv = jax.nn.gelu(_dot(h, w_ref[:, c3:]))
        for hd in range(GMLP_HEADS):
            sl = slice(hd * GMLP_HEAD_DIM, (hd + 1) * GMLP_HEAD_DIM)
            gv_ref[rows, sl] = _rms(gv[:, sl], vg[:, sl]).astype(gv_ref.dtype)


def _mix_in(x, x_spec, sh_spec, sc_spec, mod, grid, g_pre, w_in, v_gain, outs, out_specs,
            side=((), ())):
    extra, extra_specs = side
    prompt = len(extra) > 0
    any_space = pl.BlockSpec(memory_space=pl.ANY)
    scratch = [pltpu.VMEM((D_MODEL, IN_COLS), BF16),
               pltpu.VMEM((2, D_MODEL, W_STAGE_COLS), F32),
               pltpu.SemaphoreType.DMA((2,)), pltpu.SemaphoreType.DMA(())] if prompt else []
    return pl.pallas_call(
        functools.partial(_mix_in_body, transposed=prompt),
        out_shape=outs,
        grid=grid,
        in_specs=[x_spec, sh_spec, sc_spec,
                  _resident((1, D_MODEL), lambda i: (0, 0)),
                  any_space if prompt else _resident((D_MODEL, IN_COLS), lambda i: (0, 0)),
                  _resident((1, GMLP_WIDTH), lambda i: (0, 0)), *extra_specs],
        out_specs=out_specs,
        scratch_shapes=scratch,
        compiler_params=_params(1, (60 if prompt else 56) * 1024 * 1024),
        name="mix_in",
    )(x, mod, mod, g_pre, w_in, v_gain, *extra)


PROJ_TN = 256


def _out_proj_epilogue(o, rows, x_ref, gt_ref, sh_ref, sc_ref, gpm_ref, gpf_ref, x1_ref, hf_ref):
    mod = lambda ref: ref[...] if ref.shape[0] == 1 else ref[rows, :]
    x1 = x_ref[rows, :] + _rms(o, gpm_ref[...] * mod(gt_ref))
    x1_ref[rows, :] = x1
    hf = _rms(x1, gpf_ref[...] * (1.0 + mod(sc_ref))) + mod(sh_ref)
    hf_ref[rows, :] = hf.astype(hf_ref.dtype)


CORE_NB = 4


def _mix_core_body(qt_ref, kc_ref, kp_ref, vtc_ref, vtp_ref, gu_ref, gv_ref,
                   bias_ref, pm_ref, sink_ref, wtril_ref, bs_ref, gat_ref, gg_ref, wc_ref,
                   x_ref, gt_ref, sh_ref, sc_ref, wb_ref, gpm_ref, gpf_ref,
                   x1_ref, hf_ref, wcb_ref, *, steps_per_seq):
    wcb_ref[...] = wc_ref[...].astype(wcb_ref.dtype)
    first = pl.program_id(0) % steps_per_seq == 0
    tri = (lax.broadcasted_iota(jnp.int32, (BLOCK, BLOCK), 0)
           > lax.broadcasted_iota(jnp.int32, (BLOCK, BLOCK), 1))
    from_prev = jnp.concatenate([tri] * GROUP, axis=1)
    zeros = jnp.zeros((HEAD_DIM, GROUP * BLOCK), BF16)
    bs = bs_ref[...]
    def mix_stages(u, st):
        tok = slice(u * BLOCK, (u + 1) * BLOCK)

        def logits(kh):
            if kh == 0:
                k_prev = kp_ref[...] if u == 0 else kc_ref[(u - 1) * BLOCK:u * BLOCK, :]
                vt_prev = vtp_ref[...] if u == 0 else vtc_ref[:, (u - 1) * BLOCK:u * BLOCK]
                st['k'] = jnp.concatenate([k_prev, kc_ref[tok, :]], axis=0)
                st['vt'] = jnp.concatenate([vt_prev, vtc_ref[:, tok]], axis=1).astype(BF16)
                st['outs'] = []
            base = kh * GROUP * HEAD_DIM
            qs = jnp.concatenate(
                [qt_ref[base + g * HEAD_DIM: base + (g + 1) * HEAD_DIM, tok] for g in range(GROUP)],
                axis=1)
            qp = jnp.concatenate([qs, zeros] if kh == 0 else [zeros, qs], axis=0)
            st['qk'] = _dot(st['k'], qp)

        def softmax(kh):
            qk = st['qk']
            s = jnp.where(from_prev, qk[:BLOCK], qk[BLOCK:]) + bias_ref[kh]
            if u == 0:
                s = jnp.where(jnp.logical_and(first, from_prev), NEG_INF, s)
            sink = sink_ref[kh]
            m =jnp.maximum(jnp.max(s, axis=0, keepdims=True), sink)
            p = jnp.exp(s - m)
            st['den'] = jnp.sum(p, axis=0, keepdims=True) + jnp.exp(sink - m)
            p = p.astype(BF16)
            st['p2'] = jnp.concatenate([p * pm_ref[0], p * pm_ref[1]], axis=0)

        def values(kh):
            vt = st['vt'][kh * HEAD_DIM:(kh + 1) * HEAD_DIM, :]
            st['outs'].append(_dot(vt, st['p2']) / st['den'])

        def attn_norm():
            ot = jnp.concatenate(st['outs'], axis=0)
            sq = jnp.sum(ot * ot, axis=0, keepdims=True)
            tot = sq[:, :BLOCK]
            for g in range(1, GROUP):
                tot = tot + sq[:, g * BLOCK:(g + 1) * BLOCK]
            r = lax.rsqrt(tot / ATTN_WIDTH + EPS)
            a = ot * jnp.concatenate([r] * GROUP, axis=1) * gat_ref[...]
            st['merged'] = [a[:, g * BLOCK:(g + 1) * BLOCK].T.astype(BF16) for g in range(GROUP)]

        def gating():
            gated = []
            for hd in range(GMLP_HEADS):
                sl = slice(hd * GMLP_HEAD_DIM, (hd + 1) * GMLP_HEAD_DIM)
                mixed = _dot(wtril_ref[hd], gv_ref[tok, sl]) + bs[:, hd:hd + 1]
                gated.append(gu_ref[tok, sl] * mixed)
            gm = jnp.concatenate(gated, axis=-1)
            st['merged'] = jnp.concatenate(
                st['merged'] + [_rms(gm, gg_ref[...]).astype(BF16)], axis=1)

        P = functools.partial
        return [P(logits, 0), P(softmax, 0), P(values, 0), P(logits, 1), P(softmax, 1),
                P(values, 1), attn_norm, gating]

    def proj_stages(u, st):
        tok = slice(u * BLOCK, (u + 1) * BLOCK)
        chunks = []

        def chunk(c):
            chunks.append(_dot(st['merged'], wb_ref[:, c * PROJ_TN:(c + 1) * PROJ_TN]))

        def finish():
            _out_proj_epilogue(jnp.concatenate(chunks, axis=1), tok, x_ref, gt_ref, sh_ref,
                               sc_ref, gpm_ref, gpf_ref, x1_ref, hf_ref)

        return [functools.partial(chunk, c) for c in range(D_MODEL // PROJ_TN)] + [finish]

    states = [dict() for _ in range(CORE_NB)]
    for u in range(CORE_NB + 1):
        a = mix_stages(u, states[u]) if u < CORE_NB else []
        b = proj_stages(u - 1, states[u - 1]) if u > 0 else []
        for i in range(max(len(a), len(b))):
            for stage in (b[i:i + 1] + a[i:i + 1]):
                stage()


def _mix_core(qt, k, vt, gu, gvn, bias_t, sink_row, wtril, bs_t, gain_t, g_gmlp, seq, w_side,
              x, mod, mod_spec, w_out_b, g_post_mix, g_pre_ff):
    r_gt_i = np.tile(np.arange(BLOCK)[:, None] > np.arange(BLOCK)[None, :], (1, GROUP))
    pmask = jnp.asarray(np.stack([r_gt_i, ~r_gt_i]), BF16)
    n_tok = k.shape[0]
    tm = CORE_NB * BLOCK
    n_steps = n_tok // tm
    side_rows = w_side.shape[0] // n_steps
    GT_M, SH_F, SC_F = 0, 1, 2
    cur = lambda i: (i, 0)
    prev = lambda i: (jnp.maximum(i * CORE_NB - 1, 0), 0)
    cur_t = lambda i: (0, i)
    prev_t = lambda i: (0, jnp.maximum(i * CORE_NB - 1, 0))
    full2 = lambda i: (0, 0)
    full3 = lambda i: (0, 0, 0)
    return pl.pallas_call(
        functools.partial(_mix_core_body, steps_per_seq=seq // tm),
        out_shape=(jax.ShapeDtypeStruct((n_tok, D_MODEL), F32),
                   jax.ShapeDtypeStruct((n_tok, D_MODEL), BF16),
                   jax.ShapeDtypeStruct(w_side.shape, BF16)),
        grid=(n_steps,),
        in_specs=[pl.BlockSpec((Q_COLS, tm), cur_t),
                  pl.BlockSpec((tm, KV_COLS), cur), pl.BlockSpec((BLOCK, KV_COLS), prev),
                  pl.BlockSpec((KV_COLS, tm), cur_t), pl.BlockSpec((KV_COLS, BLOCK), prev_t),
                  pl.BlockSpec((tm, GMLP_WIDTH), cur), pl.BlockSpec((tm, GMLP_WIDTH), cur),
                  _resident(bias_t.shape, full3), _resident(pmask.shape, full3),
                  _resident(sink_row.shape, full3), _resident(wtril.shape, full3),
                  _resident(bs_t.shape, full2),
                  _resident(gain_t.shape, full2), _resident((1, GMLP_WIDTH), full2),
                  pl.BlockSpec((side_rows, w_side.shape[1]), cur),
                  pl.BlockSpec((tm, D_MODEL), cur),
                  mod_spec(GT_M, tm), mod_spec(SH_F, tm), mod_spec(SC_F, tm),
                  _resident((D_MODEL, D_MODEL), full2),
                  _resident((1, D_MODEL), full2), _resident((1, D_MODEL), full2)],
        out_specs=(pl.BlockSpec((tm, D_MODEL), cur), pl.BlockSpec((tm, D_MODEL), cur),
                   pl.BlockSpec((side_rows, w_side.shape[1]), cur)),
        compiler_params=_params(1, 58 * 1024 * 1024),
        name="mix_core",
    )(qt, k, k, vt, vt, gu, gvn, bias_t, pmask, sink_row, wtril, bs_t, gain_t, g_gmlp, w_side,
      x, mod, mod, mod, w_out_b, g_post_mix, g_pre_ff)


SMP_BB = 16


def _attn_smp_body(q_ref, ckt_ref, cvt_ref, nk_ref, nv_ref, bias_ref, valid_ref, sink_ref,
                   o_ref, *, t_len):
    zpad = jnp.zeros((WINDOW - t_len, KV_COLS), F32)
    tiles = []
    for b in range(SMP_BB):
        q = q_ref[b]
        s_cache = _dot(q, ckt_ref[b].astype(BF16))
        kn = jnp.concatenate([nk_ref[b], zpad], axis=0).astype(BF16)
        tiles.append(jnp.concatenate([s_cache, _dot_nt(q, kn)], axis=1)[None])
    s = jnp.concatenate(tiles, axis=0)
    s = jnp.where(valid_ref[...] > 0.5, s + bias_ref[...], NEG_INF)
    sink = sink_ref[...]
    m = jnp.maximum(jnp.max(s, axis=-1, keepdims=True), sink)
    p = jnp.exp(s - m)
    den = jnp.sum(p, axis=-1, keepdims=True) + jnp.exp(sink - m)
    p = p.astype(BF16)
    half = t_len * GROUP
    own_head = lax.broadcasted_iota(jnp.int32, (half, KV_COLS), 1) < HEAD_DIM
    for b in range(SMP_BB):
        vn = jnp.concatenate([nv_ref[b], zpad], axis=0).astype(BF16)
        o = _dot_nt(p[b, :, :WINDOW], cvt_ref[b].astype(BF16)) + _dot(p[b, :, WINDOW:], vn)
        o = o / den[b]
        o_ref[b] = jnp.where(own_head, o[:half], o[half:])


def _attn_smp(q6, cache_kt, cache_vt, new_k, new_v, bias_s, valid_s, sink_s, t_len):
    n_b, rows, _ = q6.shape
    b3 = lambda i: (i, 0, 0)
    one = lambda i: (0, 0)
    return pl.pallas_call(
        functools.partial(_attn_smp_body, t_len=t_len),
        out_shape=jax.ShapeDtypeStruct((n_b, rows // N_KV_HEADS, KV_COLS), F32),
        grid=(n_b // SMP_BB,),
        in_specs=[pl.BlockSpec((SMP_BB, rows, KV_COLS), b3),
                  pl.BlockSpec((SMP_BB, KV_COLS, WINDOW), b3),
                  pl.BlockSpec((SMP_BB, KV_COLS, WINDOW), b3),
                  pl.BlockSpec((SMP_BB, t_len, KV_COLS), b3),
                  pl.BlockSpec((SMP_BB, t_len, KV_COLS), b3),
                  _resident(bias_s.shape, one), _resident(valid_s.shape, one),
                  _resident(sink_s.shape, one)],
        out_specs=pl.BlockSpec((SMP_BB, rows // N_KV_HEADS, KV_COLS), b3),
        compiler_params=_params(1, 32 * 1024 * 1024),
        name="attn_smp",
    )(q6, cache_kt, cache_vt, new_k, new_v, bias_s, valid_s, sink_s)


def _out_proj_smp_body(w4_ref, b4_ref, a_ref, gu_ref, gv_ref, ga_ref, gg_ref,
                       x_ref, gt_ref, sh_ref, sc_ref, wb_ref, gpm_ref, gpf_ref, x1_ref, hf_ref, *,
                       t_len):
    i = pl.program_id(0)
    gated = []
    for hd in range(GMLP_HEADS):
        acc = jnp.zeros((a_ref.shape[0], GMLP_HEAD_DIM), F32)
        for j in range(t_len):
            w = jnp.where(j <= i, w4_ref[(hd * t_len + i) * t_len + j], 0.0)
            lo = j * GMLP_WIDTH + hd * GMLP_HEAD_DIM
            acc = acc + w * gv_ref[:, lo:lo + GMLP_HEAD_DIM]
        mixed = acc + b4_ref[hd * t_len + i]
        gated.append(gu_ref[:, hd * GMLP_HEAD_DIM:(hd + 1) * GMLP_HEAD_DIM] * mixed)
    gm = jnp.concatenate(gated, axis=-1)
    merged = jnp.concatenate([_rms(a_ref[...], ga_ref[...]), _rms(gm, gg_ref[...])], axis=-1)
    _out_proj_epilogue(_dot(merged.astype(BF16), wb_ref[...]), slice(0, a_ref.shape[0]),
                       x_ref, gt_ref, sh_ref, sc_ref, gpm_ref, gpf_ref, x1_ref, hf_ref)


FFN_TF = 1024
FFN_SUB = 256


def _ffn_body(h_ref, w1_ref, w2_ref, x1_ref, gt_ref, g_ref, o_ref, acc_ref, done_ref, *,
              n_tiles, n_chunks):
    s = pl.program_id(0)
    i = s // n_chunks
    j = s % n_chunks
    tm = o_ref.shape[0]
    whole = slice(0, tm)

    def contribution(rows):
        a = jnp.maximum(_dot(h_ref[rows, :], w1_ref[...]), 0.0)
        return _dot((a * a).astype(BF16), w2_ref[...])

    def finalize(rows):
        gt = gt_ref[...] if gt_ref.shape[0] == 1 else _rows(gt_ref[...], tm)[rows, :]
        o_ref[rows, :] = x1_ref[rows, :] + _rms(done_ref[rows, :], g_ref[...] * gt)

    @pl.when(s == 0)
    def _():
        acc_ref[...] = contribution(whole)

    @pl.when(jnp.logical_and(j == 0, jnp.logical_and(i > 0, i < n_tiles)))
    def _():
        for r in range(0, tm, FFN_SUB):
            rows = slice(r, r + FFN_SUB)
            acc_ref[rows, :] = contribution(rows)
            finalize(rows)

    @pl.when(jnp.logical_and(j > 0, j < n_chunks - 1))
    def _():
        acc_ref[...] += contribution(whole)

    @pl.when(jnp.logical_and(j == n_chunks - 1, i < n_tiles))
    def _():
        done_ref[...] = acc_ref[...] + contribution(whole)

    @pl.when(i == n_tiles)
    def _():
        finalize(whole)


def _ffn(hf, w1, w2, x1, mod, gt_spec, g_post, tm):
    n_tok = hf.shape[0]
    n_tiles, n_chunks = n_tok // tm, D_FF // FFN_TF
    last = n_tiles * n_chunks
    mm_tile = lambda s: jnp.minimum(s // n_chunks, n_tiles - 1)
    chunk = lambda s: jnp.where(s == last, n_chunks - 1, s % n_chunks)
    out_tile = lambda s: jnp.maximum(s // n_chunks - (s % n_chunks == 0).astype(jnp.int32), 0)
    return pl.pallas_call(
        functools.partial(_ffn_body, n_tiles=n_tiles, n_chunks=n_chunks),
        out_shape=jax.ShapeDtypeStruct((n_tok, D_MODEL), F32),
        grid=(last + 1,),
        in_specs=[pl.BlockSpec((tm, D_MODEL), lambda s: (mm_tile(s), 0)),
                  pl.BlockSpec((D_MODEL, FFN_TF), lambda s: (0, chunk(s))),
                  pl.BlockSpec((FFN_TF, D_MODEL), lambda s: (chunk(s), 0)),
                  pl.BlockSpec((tm, D_MODEL), lambda s: (out_tile(s), 0)),
                  gt_spec(out_tile),
                  _resident((1, D_MODEL), lambda s: (0, 0))],
        out_specs=pl.BlockSpec((tm, D_MODEL), lambda s: (out_tile(s), 0)),
        scratch_shapes=[pltpu.VMEM((tm, D_MODEL), F32), pltpu.VMEM((tm, D_MODEL), F32)],
        compiler_params=_params(1),
        name="ffn",
    )(hf, w1, w2, x1, mod, g_post)


MIX_TM = 512
MIX_SUB = 128
FFN_TM = 512


def _row(v):
    return v.reshape(1, -1)


def _layer(x_prompt, x_sample, cache_k, cache_v, c_prompt, c_sample, rel_table, w_ada, b_ada,
           g_pre_mix, w_in, sinks, v_gain, w_s, b_s, g_attn, g_gmlp, w_out, g_post_mix,
           g_pre_ff, w_ff1, w_ff2, g_post_ff):
    n_b, seq, _ = x_prompt.shape
    n_db, t_len, _ = x_sample.shape
    n_tok = n_b * seq
    D = D_MODEL

    g_pre_mix, g_attn, g_gmlp = _row(g_pre_mix), _row(g_attn), _row(g_gmlp)
    g_post_mix, g_pre_ff, g_post_ff = _row(g_post_mix), _row(g_pre_ff), _row(g_post_ff)
    v_gain = _row(v_gain)

    bias_p, bias_s, wtril = _prep(rel_table, w_s, t_len)
    c_all = jnp.concatenate(
        [c_sample, c_prompt, jnp.zeros(((-(n_b + n_db)) % 16, D), F32)], axis=0)
    b_ada = b_ada.reshape(1, 6 * D)
    mod_s, mod_p = _adaln(c_all, n_db, n_b, w_ada, b_ada, 2 * D)
    SH_M, SC_M = 0, 1
    GT_M, SH_F, SC_F, GT_F = range(4)

    def pmod(chunk, tm):
        per = seq // tm
        return pl.BlockSpec((None, 1, D), lambda i: (i // per, 0, chunk))

    def smod(chunk):
        return pl.BlockSpec((n_db, D), lambda i: (0, chunk))

    sd = _sample_dist(t_len)
    valid_s = ((sd >= 0) & (sd < WINDOW) & (np.arange(SMP_KEYS)[None, :] < WINDOW + t_len))
    valid_s = valid_s.astype(np.float32)
    sink2 = sinks.reshape(N_KV_HEADS, GROUP)
    sink_p = jnp.repeat(sink2, BLOCK, axis=1)[:, None, :]
    sink_s = jnp.tile(sink2, (1, t_len)).reshape(-1, 1)
    ga3 = g_attn.reshape(N_KV_HEADS, GROUP, HEAD_DIM)
    gain_t = jnp.repeat(ga3.transpose(0, 2, 1).reshape(KV_COLS, GROUP), BLOCK, axis=1)
    g_attn = ga3.transpose(1, 0, 2).reshape(1, ATTN_WIDTH)

    xp = x_prompt.reshape(n_tok, D)
    tile = lambda w: pl.BlockSpec((MIX_TM, w), lambda i: (i, 0))
    ttile = lambda w: pl.BlockSpec((w, MIX_TM), lambda i: (0, i))
    n_mix = n_tok // MIX_TM
    slab1 = pl.BlockSpec((D // n_mix, D_FF), lambda i: (i, 0))
    assert seq >= WINDOW and MIX_TM >= WINDOW
    mix_per = seq // MIX_TM
    win = pl.BlockSpec((None, KV_COLS, WINDOW), lambda i: (i // mix_per, 0, 0))
    assert n_mix * BLOCK == D and GROUP * BLOCK == ATTN_WIDTH
    piece = lambda off: pl.BlockSpec(
        (HEAD_DIM, D), lambda i: (jnp.where(i < GROUP, i + off * GROUP, 2 * i + off), 0))
    slab_o = pl.BlockSpec((BLOCK, D), lambda i: (i, 0))
    ada_tn = 4 * D // n_mix
    ada_col = lambda i: (0, 2 * D // ada_tn + i)
    (qt_p, k_p, vt_p, gu_p, gvn_p, kwin, vwin, w1_b, w_out_b, w_in_b,
     mod2_s, mod2_p) = _mix_in(
        xp, tile(D), pmod(SH_M, MIX_TM), pmod(SC_M, MIX_TM), mod_p, (n_tok // MIX_TM,),
        g_pre_mix, w_in, v_gain,
        (jax.ShapeDtypeStruct((Q_COLS, n_tok), BF16),
         jax.ShapeDtypeStruct((n_tok, KV_COLS), BF16),
         jax.ShapeDtypeStruct((KV_COLS, n_tok), F32),
         jax.ShapeDtypeStruct((n_tok, GMLP_WIDTH), F32),
         jax.ShapeDtypeStruct((n_tok, GMLP_WIDTH), BF16),
         jax.ShapeDtypeStruct((n_b, KV_COLS, WINDOW), F32),
         jax.ShapeDtypeStruct((n_b, KV_COLS, WINDOW), F32),
         jax.ShapeDtypeStruct((D, D_FF), BF16),
         jax.ShapeDtypeStruct((D, D), BF16),
         jax.ShapeDtypeStruct((D, IN_COLS), BF16),
         jax.ShapeDtypeStruct((n_db, 4 * D), F32),
         jax.ShapeDtypeStruct((n_b, 1, 4 * D), F32)),
        (ttile(Q_COLS), tile(KV_COLS), ttile(KV_COLS), tile(GMLP_WIDTH), tile(GMLP_WIDTH),
         win, win, slab1, slab_o, pl.BlockSpec(memory_space=pl.ANY),
         pl.BlockSpec((n_db, ada_tn), lambda i: (0, i)), pl.BlockSpec((n_b, 1, ada_tn), lambda i: (0, 0, i))),
        side=((w_ff1, w_out, w_out, c_all, w_ada, b_ada),
              (slab1, piece(0), piece(1), _resident(c_all.shape, lambda i: (0, 0)),
               pl.BlockSpec((D, ada_tn), ada_col), pl.BlockSpec((1, ada_tn), ada_col))))

    x1_p, hf_p, w2_b = _mix_core(qt_p, k_p, vt_p, gu_p, gvn_p, bias_p, sink_p,
                                 wtril, b_s.T, gain_t, g_gmlp, seq, w_ff2,
                                 xp, mod2_p, pmod, w_out_b, g_post_mix, g_pre_ff)
    one = lambda i: (0, 0)

    ffn_per = seq // FFN_TM
    gate_p = lambda tile: pl.BlockSpec((None, 1, D), lambda s: (tile(s) // ffn_per, 0, GT_F))
    y_p = _ffn(hf_p, w1_b, w2_b, x1_p, mod2_p, gate_p, g_post_ff, FFN_TM)

    xs = x_sample.reshape(n_db, t_len * D)
    lane = lambda w: pl.BlockSpec((n_db, w), lambda t: (0, t))
    q_s, k_s, v_s, gu_s, gvn_s = _mix_in(
        xs, lane(D), smod(SH_M), smod(SC_M), mod_s, (t_len,),
        g_pre_mix, w_in_b, v_gain,
        (jax.ShapeDtypeStruct((n_db, t_len * Q_COLS), F32),
         jax.ShapeDtypeStruct((n_db, t_len * KV_COLS), F32),
         jax.ShapeDtypeStruct((n_db, t_len * KV_COLS), F32),
         jax.ShapeDtypeStruct((n_db, t_len * GMLP_WIDTH), F32),
         jax.ShapeDtypeStruct((n_db, t_len * GMLP_WIDTH), F32)),
        (lane(Q_COLS), lane(KV_COLS), lane(KV_COLS), lane(GMLP_WIDTH), lane(GMLP_WIDTH)))

    q5 = q_s.reshape(n_db, t_len, N_KV_HEADS, GROUP, HEAD_DIM).transpose(0, 2, 1, 3, 4)
    q5 = q5.reshape(n_db, N_KV_HEADS, t_len * GROUP, HEAD_DIM).astype(BF16)
    z5 = jnp.zeros_like(q5[:, 0])
    q6 = jnp.concatenate([jnp.concatenate([q5[:, 0], z5], axis=-1),
                          jnp.concatenate([z5, q5[:, 1]], axis=-1)], axis=1)
    new_k = k_s.reshape(n_db, t_len, KV_COLS)
    new_v = v_s.reshape(n_db, t_len, KV_COLS)
    cache_kt = cache_k.transpose(0, 2, 3, 1).reshape(n_db, KV_COLS, WINDOW)
    cache_vt = cache_v.transpose(0, 2, 3, 1).reshape(n_db, KV_COLS, WINDOW)
    attn_s = _attn_smp(q6, cache_kt, cache_vt, new_k, new_v, bias_s, jnp.asarray(valid_s), sink_s,
                       t_len)
    attn_s = attn_s.reshape(n_db, t_len * ATTN_WIDTH)
    smem = pl.BlockSpec(memory_space=pltpu.SMEM)
    w4 = w_s[:, :t_len, :t_len].reshape(-1)
    b4 = b_s[:, :t_len].reshape(-1)
    tmaj = lambda: pl.BlockSpec((None, n_db, D), lambda t: (t, 0, 0))
    x1_s, hf_s = pl.pallas_call(
        functools.partial(_out_proj_smp_body, t_len=t_len),
        out_shape=(jax.ShapeDtypeStruct((t_len, n_db, D), F32),
                   jax.ShapeDtypeStruct((t_len, n_db, D), BF16)),
        grid=(t_len,),
        in_specs=[smem, smem, lane(ATTN_WIDTH), lane(GMLP_WIDTH),
                  _resident((n_db, t_len * GMLP_WIDTH), one),
                  _resident((1, ATTN_WIDTH), one), _resident((1, GMLP_WIDTH), one),
                  lane(D), smod(GT_M), smod(SH_F), smod(SC_F),
                  _resident((D, D), one), _resident((1, D), one), _resident((1, D), one)],
        out_specs=(tmaj(), tmaj()),
        compiler_params=_params(1),
        name="out_proj_smp",
    )(w4, b4, attn_s, gu_s, gvn_s, g_attn, g_gmlp, xs, mod2_s, mod2_s, mod2_s,
      w_out_b, g_post_mix, g_pre_ff)

    n_st = t_len * n_db
    y_s = _ffn(hf_s.reshape(n_st, D), w1_b, w2_b, x1_s.reshape(n_st, D), mod2_s,
               lambda tile: smod(GT_F), g_post_ff, n_st)
    y_s = y_s.reshape(t_len, n_db, D).transpose(1, 0, 2)

    last = lambda t: t.reshape(n_b, N_KV_HEADS, HEAD_DIM, WINDOW).transpose(0, 3, 1, 2)
    k_p4, v_p4 = last(kwin), last(vwin)
    return (y_p.reshape(n_b, seq, D), y_s, k_p4, v_p4,
            k_s.reshape(n_db, t_len, N_KV_HEADS, HEAD_DIM),
            v_s.reshape(n_db, t_len, N_KV_HEADS, HEAD_DIM),
            gvn_s.reshape(n_db, t_len, GMLP_HEADS, GMLP_HEAD_DIM))


def kernel(x_prompt, x_sample, cache_k, cache_v, c_prompt, c_sample, rel_bias_table, w_ada, b_ada,
           g_pre_mix, w_in, attn_sinks, gmlp_v_gain, gmlp_w_s, gmlp_b_s, g_attn_out, g_gmlp_out,
           w_out, g_post_mix, g_pre_ff, w_ff1, w_ff2, g_post_ff):
    depth = w_in.shape[0]
    assert depth == 1, "single-layer step"
    only = lambda a: a.reshape(a.shape[1:])
    outs = _layer(x_prompt, x_sample, only(cache_k), only(cache_v), c_prompt, c_sample,
                  rel_bias_table, *map(only, (w_ada, b_ada, g_pre_mix, w_in, attn_sinks, gmlp_v_gain,
                                              gmlp_w_s, gmlp_b_s, g_attn_out, g_gmlp_out, w_out,
                                              g_post_mix, g_pre_ff, w_ff1, w_ff2, g_post_ff)))
    y_p, y_s, k_p, v_p, k_s, v_s, gv_s = outs
    return (y_p, y_s, k_p[None], v_p[None], k_s[None], v_s[None], gv_s[None])
```

```python
import functools
import math

import numpy as np
import jax
import jax.numpy as jnp
from jax import lax
from jax.experimental import pallas as pl
from jax.experimental.pallas import tpu as pltpu

D_MODEL = 2048
HEAD_DIM = 64
ATTN_WIDTH = 1024
N_HEADS = 16
N_KV_HEADS = 2
GROUP = 8
WINDOW = 128
BLOCK = 128
GMLP_WIDTH = 1024
GMLP_HEADS = 8
GMLP_HEAD_DIM = 128
D_FF = 4 * D_MODEL
NUM_BUCKETS = 32
MAX_DISTANCE = 128
EPS = 1e-6
KV_COLS = N_KV_HEADS * HEAD_DIM
Q_COLS = N_HEADS * HEAD_DIM
IN_COLS = Q_COLS + 2 * KV_COLS + 2 * GMLP_WIDTH
ATTN_SCALE = HEAD_DIM ** -0.5
NEG_INF = -1e30

MIB = 1024 * 1024
VMEM_LIMIT_BYTES = 56 * MIB
MIX_IN_VMEM_BYTES = 60 * MIB
MIX_CORE_VMEM_BYTES = 58 * MIB

BF16 = jnp.bfloat16
F32 = jnp.float32


def _params(n_axes, vmem=VMEM_LIMIT_BYTES):
    return pltpu.CompilerParams(
        dimension_semantics=("arbitrary",) * n_axes, vmem_limit_bytes=vmem)


def _resident(shape, index_map):
    return pl.BlockSpec(shape, index_map, pipeline_mode=pl.Buffered(1))


def _rms(x, gain):
    return x * lax.rsqrt(jnp.mean(x * x, axis=-1, keepdims=True) + EPS) * gain


def _rows(m, n_rows):
    r = m.shape[0]
    if r == 1 or r == n_rows:
        return m
    return jnp.concatenate([m] * (n_rows // r), axis=0)


def _dot(a, b):
    return jnp.dot(a, b, preferred_element_type=F32)


def _dot_nt(a, b):
    return lax.dot_general(a, b, (((1,), (1,)), ((), ())), preferred_element_type=F32)


def _t5_bucket(dist):
    n = np.maximum(dist, 0)
    max_exact = NUM_BUCKETS // 2
    nf = np.maximum(n, 1).astype(np.float64)
    large = max_exact + (np.log(nf / max_exact) / math.log(MAX_DISTANCE / max_exact)
                         * (NUM_BUCKETS - max_exact)).astype(np.int32)
    large = np.minimum(large, NUM_BUCKETS - 1)
    return jnp.asarray(np.where(n < max_exact, n, large), jnp.int32)


SMP_KEYS = 2 * WINDOW


def _prompt_dist():
    r = np.arange(BLOCK)[:, None]
    i = np.arange(BLOCK)[None, :]
    return (i - r) % BLOCK


def _sample_rows(t_len):
    r = np.arange(N_KV_HEADS * t_len * GROUP)
    return r // (t_len * GROUP), (r // GROUP) % t_len, r % GROUP


def _sample_dist(t_len):
    _, t, _ = _sample_rows(t_len)
    return WINDOW + t[:, None] - np.arange(SMP_KEYS)[None, :]


def _prep_body(tab_ref, tabr_ref, bkt_p_ref, bkt_s_ref, ws_ref,
               bias_p_ref, bias_s_ref, wtril_ref):
    bkt_p = bkt_p_ref[...]
    for kh in range(N_KV_HEADS):
        for g in range(GROUP):
            h = kh * GROUP + g
            acc = jnp.zeros((BLOCK, BLOCK), F32)
            for b in range(NUM_BUCKETS):
                acc = jnp.where(bkt_p == b, tab_ref[b, h], acc)
            bias_p_ref[kh, :, g * BLOCK:(g + 1) * BLOCK] = acc
    bkt_s = bkt_s_ref[...]
    tabr = tabr_ref[...]
    acc = jnp.zeros(bkt_s.shape, F32)
    for b in range(NUM_BUCKETS):
        acc = jnp.where(bkt_s == b, tabr[:, b:b + 1], acc)
    bias_s_ref[...] = acc
    ii = lax.broadcasted_iota(jnp.int32, (BLOCK, BLOCK), 0)
    jj = lax.broadcasted_iota(jnp.int32, (BLOCK, BLOCK), 1)
    for h in range(GMLP_HEADS):
        wtril_ref[h] = jnp.where(jj <= ii, ws_ref[h], 0.0).astype(BF16)


ADA_TN = 1024


def _adaln_body(c_ref, w_ref, b_ref, os_ref, op_ref):
    n_s, n_p = os_ref.shape[0], op_ref.shape[0]
    c = c_ref[...]
    s = (c * jax.nn.sigmoid(c)).astype(BF16)
    r = _dot(s, w_ref[...].astype(BF16)) + b_ref[...]
    os_ref[...] = r[:n_s]
    for p in range(n_p):
        op_ref[p] = r[n_s + p:n_s + p + 1]


def _cond_body(tab_ref, tabr_ref, bkt_p_ref, bkt_s_ref, ws_ref, c_ref, w_ref, b_ref,
               bias_p_ref, bias_s_ref, wtril_ref, os_ref, op_ref):
    @pl.when(pl.program_id(0) == 0)
    def _():
        _prep_body(tab_ref, tabr_ref, bkt_p_ref, bkt_s_ref, ws_ref, bias_p_ref, bias_s_ref, wtril_ref)

    _adaln_body(c_ref, w_ref, b_ref, os_ref, op_ref)


def _cond(rel_table, w_s, t_len, c_all, n_s, n_p, w_ada, b_ada, n_out):
    bkt_p = _t5_bucket(_prompt_dist())
    bkt_s = _t5_bucket(_sample_dist(t_len))
    n_rows = bkt_s.shape[0]
    tab_rows = jnp.broadcast_to(rel_table.T.reshape(N_KV_HEADS, 1, GROUP, NUM_BUCKETS),
                                (N_KV_HEADS, t_len, GROUP, NUM_BUCKETS)).reshape(n_rows, NUM_BUCKETS)
    n_all = c_all.shape[0]
    whole = lambda a: _resident(a.shape, lambda j: (0,) * a.ndim)
    outs = (jax.ShapeDtypeStruct((N_KV_HEADS, BLOCK, GROUP * BLOCK), F32),
            jax.ShapeDtypeStruct((n_rows, SMP_KEYS), F32),
            jax.ShapeDtypeStruct((GMLP_HEADS, BLOCK, BLOCK), BF16))
    return pl.pallas_call(
        _cond_body,
        out_shape=outs + (jax.ShapeDtypeStruct((n_s, n_out), F32),
                          jax.ShapeDtypeStruct((n_p, 1, n_out), F32)),
        grid=(n_out // ADA_TN,),
        in_specs=[pl.BlockSpec(memory_space=pltpu.SMEM),
                  whole(tab_rows), whole(bkt_p), whole(bkt_s), whole(w_s),
                  _resident((n_all, D_MODEL), lambda j: (0, 0)),
                  pl.BlockSpec((D_MODEL, ADA_TN), lambda j: (0, j)),
                  pl.BlockSpec((1, ADA_TN), lambda j: (0, j))],
        out_specs=tuple(pl.BlockSpec(o.shape, lambda j, nd=len(o.shape): (0,) * nd) for o in outs)
        + (pl.BlockSpec((n_s, ADA_TN), lambda j: (0, j)),
           pl.BlockSpec((n_p, 1, ADA_TN), lambda j: (0, 0, j))),
        compiler_params=_params(1, 40 * MIB),
        name="cond",
    )(rel_table, tab_rows, bkt_p, bkt_s, w_s, c_all, w_ada, b_ada)


W_STAGE_COLS = 256


def _stage_w_in(w_hbm, wb_ref, stage_ref, sem_ref):
    n = IN_COLS // W_STAGE_COLS

    def fetch(c):
        return pltpu.make_async_copy(w_hbm.at[:, pl.ds(c * W_STAGE_COLS, W_STAGE_COLS)],
                                     stage_ref.at[c % 2], sem_ref.at[c % 2])

    fetch(0).start()
    for c in range(n):
        if c + 1 < n:
            fetch(c + 1).start()
        fetch(c).wait()
        wb_ref[:, c * W_STAGE_COLS:(c + 1) * W_STAGE_COLS] = stage_ref[c % 2].astype(BF16)


def _mix_in_body(x_ref, sh_ref, sc_ref, g_ref, w_ref, vg_ref, *refs, transposed):
    if transposed:
        (wc_ref, woa_ref, wob_ref, ca_ref, wa_ref, ba_ref,
         q_ref, k_ref, v_ref, gu_ref, gv_ref, kwin_ref, vwin_ref,
         wcb_ref, wo_ref, wib_hbm, ms_ref, mp_ref, wb_ref, stage_ref, sem_ref, wsem_ref) = refs
        write_back = pltpu.make_async_copy(wb_ref, wib_hbm, wsem_ref)

        @pl.when(pl.program_id(0) == 0)
        def _():
            _stage_w_in(w_ref, wb_ref, stage_ref, sem_ref)
            write_back.start()

        @pl.when(pl.program_id(0) == pl.num_programs(0) - 1)
        def _():
            write_back.wait()

        w_ref = wb_ref
        wcb_ref[...] = wc_ref[...].astype(wcb_ref.dtype)
        wo_ref[:HEAD_DIM, :] = woa_ref[...].astype(wo_ref.dtype)
        wo_ref[HEAD_DIM:, :] = wob_ref[...].astype(wo_ref.dtype)
        _adaln_body(ca_ref, wa_ref, ba_ref, ms_ref, mp_ref)
    else:
        q_ref, k_ref, v_ref, gu_ref, gv_ref = refs
    tm = x_ref.shape[0]
    sub = min(tm, MIX_SUB)
    c0, c1, c2, c3 = Q_COLS, Q_COLS + KV_COLS, Q_COLS + 2 * KV_COLS, Q_COLS + 2 * KV_COLS + GMLP_WIDTH
    vg = vg_ref[...]
    for r in range(0, tm, sub):
        rows = slice(r, r + sub)
        mod = lambda ref: ref[...] if ref.shape[0] == 1 else ref[rows, :]
        h = (_rms(x_ref[rows, :], g_ref[...] * (1.0 + mod(sc_ref))) + mod(sh_ref)).astype(BF16)
        q = _dot(h, w_ref[:, :c0]) * ATTN_SCALE
        kv = _dot(h, w_ref[:, c0:c2])
        k_ref[rows, :] = kv[:, :KV_COLS].astype(k_ref.dtype)
        if transposed:
            q_ref[:, rows] = q.T.astype(q_ref.dtype)
            vt = kv[:, KV_COLS:].T
            v_ref[:, rows] = vt
            if r + sub == tm:
                kwin_ref[...] = kv[-WINDOW:, :KV_COLS].T
                vwin_ref[...] = vt[:, -WINDOW:]
        else:
            q_ref[rows, :] = q.astype(q_ref.dtype)
            v_ref[rows, :] = kv[:, KV_COLS:]
        gu_ref[rows, :] = jax.nn.gelu(_dot(h, w_ref[:, c2:c3])).astype(gu_ref.dtype)
        gv = jax.nn.gelu(_dot(h, w_ref[:, c3:]))
        for hd in range(GMLP_HEADS):
            sl = slice(hd * GMLP_HEAD_DIM, (hd + 1) * GMLP_HEAD_DIM)
            gv_ref[rows, sl] = _rms(gv[:, sl], vg[:, sl]).astype(gv_ref.dtype)


def _mix_in(x, x_spec, sh_spec, sc_spec, mod, grid, g_pre, w_in, v_gain, outs, out_specs,
            side=((), ())):
    extra, extra_specs = side
    prompt = len(extra) > 0
    any_space = pl.BlockSpec(memory_space=pl.ANY)
    scratch = [pltpu.VMEM((D_MODEL, IN_COLS), BF16),
               pltpu.VMEM((2, D_MODEL, W_STAGE_COLS), F32),
               pltpu.SemaphoreType.DMA((2,)), pltpu.SemaphoreType.DMA(())] if prompt else []
    return pl.pallas_call(
        functools.partial(_mix_in_body, transposed=prompt),
        out_shape=outs,
        grid=grid,
        in_specs=[x_spec, sh_spec, sc_spec,
                  _resident((1, D_MODEL), lambda i: (0, 0)),
                  any_space if prompt else _resident((D_MODEL, IN_COLS), lambda i: (0, 0)),
                  _resident((1, GMLP_WIDTH), lambda i: (0, 0)), *extra_specs],
        out_specs=out_specs,
        scratch_shapes=scratch,
        compiler_params=_params(1, MIX_IN_VMEM_BYTES if prompt else VMEM_LIMIT_BYTES),
        name="mix_in",
    )(x, mod, mod, g_pre, w_in, v_gain, *extra)


PROJ_TN = 256


def _out_proj_epilogue(o, rows, x_ref, gt_ref, sh_ref, sc_ref, gpm_ref, gpf_ref, x1_ref, hf_ref):
    mod = lambda ref: ref[...] if ref.shape[0] == 1 else ref[rows, :]
    x1 = x_ref[rows, :] + _rms(o, gpm_ref[...] * mod(gt_ref))
    x1_ref[rows, :] = x1
    hf = _rms(x1, gpf_ref[...] * (1.0 + mod(sc_ref))) + mod(sh_ref)
    hf_ref[rows, :] = hf.astype(hf_ref.dtype)


CORE_NB = 4


def _mix_core_body(qt_ref, kc_ref, kp_ref, vtc_ref, vtp_ref, gu_ref, gv_ref,
                   bias_ref, pm_ref, sink_ref, wtril_ref, bs_ref, gat_ref, gg_ref, wc_ref,
                   x_ref, gt_ref, sh_ref, sc_ref, wb_ref, gpm_ref, gpf_ref,
                   x1_ref, hf_ref, wcb_ref, *, steps_per_seq):
    wcb_ref[...] = wc_ref[...].astype(wcb_ref.dtype)
    first = pl.program_id(0) % steps_per_seq == 0
    tri = (lax.broadcasted_iota(jnp.int32, (BLOCK, BLOCK), 0)
           > lax.broadcasted_iota(jnp.int32, (BLOCK, BLOCK), 1))
    from_prev = jnp.concatenate([tri] * GROUP, axis=1)
    zeros = jnp.zeros((HEAD_DIM, GROUP * BLOCK), BF16)
    bs = bs_ref[...]
    def mix_stages(u, st):
        tok = slice(u * BLOCK, (u + 1) * BLOCK)

        def logits(kh):
            if kh == 0:
                k_prev = kp_ref[...] if u == 0 else kc_ref[(u - 1) * BLOCK:u * BLOCK, :]
                vt_prev = vtp_ref[...] if u == 0 else vtc_ref[:, (u - 1) * BLOCK:u * BLOCK]
                st['k'] = jnp.concatenate([k_prev, kc_ref[tok, :]], axis=0)
                st['vt'] = jnp.concatenate([vt_prev, vtc_ref[:, tok]], axis=1).astype(BF16)
                st['outs'] = []
            base = kh * GROUP * HEAD_DIM
            qs = jnp.concatenate(
                [qt_ref[base + g * HEAD_DIM: base + (g + 1) * HEAD_DIM, tok] for g in range(GROUP)],
                axis=1)
            qp = jnp.concatenate([qs, zeros] if kh == 0 else [zeros, qs], axis=0)
            st['qk'] = _dot(st['k'], qp)

        def softmax(kh):
            qk = st['qk']
            s = jnp.where(from_prev, qk[:BLOCK], qk[BLOCK:]) + bias_ref[kh]
            if u == 0:
                s = jnp.where(jnp.logical_and(first, from_prev), NEG_INF, s)
            sink = sink_ref[kh]
            m =jnp.maximum(jnp.max(s, axis=0, keepdims=True), sink)
            p = jnp.exp(s - m)
            st['den'] = jnp.sum(p, axis=0, keepdims=True) + jnp.exp(sink - m)
            p = p.astype(BF16)
            st['p2'] = jnp.concatenate([p * pm_ref[0], p * pm_ref[1]], axis=0)

        def values(kh):
            vt = st['vt'][kh * HEAD_DIM:(kh + 1) * HEAD_DIM, :]
            st['outs'].append(_dot(vt, st['p2']) / st['den'])

        def attn_norm():
            ot = jnp.concatenate(st['outs'], axis=0)
            sq = jnp.sum(ot * ot, axis=0, keepdims=True)
            tot = sq[:, :BLOCK]
            for g in range(1, GROUP):
                tot = tot + sq[:, g * BLOCK:(g + 1) * BLOCK]
            r = lax.rsqrt(tot / ATTN_WIDTH + EPS)
            a = ot * jnp.concatenate([r] * GROUP, axis=1) * gat_ref[...]
            st['merged'] = [a[:, g * BLOCK:(g + 1) * BLOCK].T.astype(BF16) for g in range(GROUP)]

        def gating():
            gated = []
            for hd in range(GMLP_HEADS):
                sl = slice(hd * GMLP_HEAD_DIM, (hd + 1) * GMLP_HEAD_DIM)
                mixed = _dot(wtril_ref[hd], gv_ref[tok, sl]) + bs[:, hd:hd + 1]
                gated.append(gu_ref[tok, sl] * mixed)
            gm = jnp.concatenate(gated, axis=-1)
            st['merged'] = jnp.concatenate(
                st['merged'] + [_rms(gm, gg_ref[...]).astype(BF16)], axis=1)

        P = functools.partial
        return [P(logits, 0), P(softmax, 0), P(values, 0), P(logits, 1), P(softmax, 1),
                P(values, 1), attn_norm, gating]

    def proj_stages(u, st):
        tok = slice(u * BLOCK, (u + 1) * BLOCK)
        chunks = []

        def chunk(c):
            chunks.append(_dot(st['merged'], wb_ref[:, c * PROJ_TN:(c + 1) * PROJ_TN]))

        def finish():
            _out_proj_epilogue(jnp.concatenate(chunks, axis=1), tok, x_ref, gt_ref, sh_ref,
                               sc_ref, gpm_ref, gpf_ref, x1_ref, hf_ref)

        return [functools.partial(chunk, c) for c in range(D_MODEL // PROJ_TN)] + [finish]

    states = [dict() for _ in range(CORE_NB)]
    for u in range(CORE_NB + 1):
        a = mix_stages(u, states[u]) if u < CORE_NB else []
        b = proj_stages(u - 1, states[u - 1]) if u > 0 else []
        for i in range(max(len(a), len(b))):
            for stage in (b[i:i + 1] + a[i:i + 1]):
                stage()


def _mix_core(qt, k, vt, gu, gvn, bias_t, sink_row, wtril, bs_t, gain_t, g_gmlp, seq, w_side,
              x, mod, mod_spec, w_out_b, g_post_mix, g_pre_ff, extra_rows):
    r_gt_i = np.tile(np.arange(BLOCK)[:, None] > np.arange(BLOCK)[None, :], (1, GROUP))
    pmask = jnp.asarray(np.stack([r_gt_i, ~r_gt_i]), BF16)
    n_tok = k.shape[0]
    tm = CORE_NB * BLOCK
    n_steps = n_tok // tm
    side_rows = w_side.shape[0] // n_steps
    GT_M, SH_F, SC_F = 0, 1, 2
    cur = lambda i: (i, 0)
    prev = lambda i: (jnp.maximum(i * CORE_NB - 1, 0), 0)
    cur_t = lambda i: (0, i)
    prev_t = lambda i: (0, jnp.maximum(i * CORE_NB - 1, 0))
    full2 = lambda i: (0, 0)
    full3 = lambda i: (0, 0, 0)
    return pl.pallas_call(
        functools.partial(_mix_core_body, steps_per_seq=seq // tm),
        out_shape=(jax.ShapeDtypeStruct((n_tok + extra_rows, D_MODEL), F32),
                   jax.ShapeDtypeStruct((n_tok + extra_rows, D_MODEL), BF16),
                   jax.ShapeDtypeStruct(w_side.shape, BF16)),
        grid=(n_steps,),
        in_specs=[pl.BlockSpec((Q_COLS, tm), cur_t),
                  pl.BlockSpec((tm, KV_COLS), cur), pl.BlockSpec((BLOCK, KV_COLS), prev),
                  pl.BlockSpec((KV_COLS, tm), cur_t), pl.BlockSpec((KV_COLS, BLOCK), prev_t),
                  pl.BlockSpec((tm, GMLP_WIDTH), cur), pl.BlockSpec((tm, GMLP_WIDTH), cur),
                  _resident(bias_t.shape, full3), _resident(pmask.shape, full3),
                  _resident(sink_row.shape, full3), _resident(wtril.shape, full3),
                  _resident(bs_t.shape, full2),
                  _resident(gain_t.shape, full2), _resident((1, GMLP_WIDTH), full2),
                  pl.BlockSpec((side_rows, w_side.shape[1]), cur),
                  pl.BlockSpec((tm, D_MODEL), cur),
                  mod_spec(GT_M, tm), mod_spec(SH_F, tm), mod_spec(SC_F, tm),
                  _resident((D_MODEL, D_MODEL), full2),
                  _resident((1, D_MODEL), full2), _resident((1, D_MODEL), full2)],
        out_specs=(pl.BlockSpec((tm, D_MODEL), cur), pl.BlockSpec((tm, D_MODEL), cur),
                   pl.BlockSpec((side_rows, w_side.shape[1]), cur)),
        compiler_params=_params(1, MIX_CORE_VMEM_BYTES),
        name="mix_core",
    )(qt, k, k, vt, vt, gu, gvn, bias_t, pmask, sink_row, wtril, bs_t, gain_t, g_gmlp, w_side,
      x, mod, mod, mod, w_out_b, g_post_mix, g_pre_ff)


SMP_BB = 32


def _attn_smp_body(q_ref, ckt_ref, cvt_ref, nk_ref, nv_ref, bias_ref, valid_ref, sink_ref,
                   o_ref, *, t_len):
    zpad = jnp.zeros((WINDOW - t_len, KV_COLS), F32)
    tiles = []
    for b in range(SMP_BB):
        q = q_ref[b]
        s_cache = _dot(q, ckt_ref[b].astype(BF16))
        kn = jnp.concatenate([nk_ref[b], zpad], axis=0).astype(BF16)
        tiles.append(jnp.concatenate([s_cache, _dot_nt(q, kn)], axis=1)[None])
    s = jnp.concatenate(tiles, axis=0)
    s = jnp.where(valid_ref[...] > 0.5, s + bias_ref[...], NEG_INF)
    sink = sink_ref[...]
    m = jnp.maximum(jnp.max(s, axis=-1, keepdims=True), sink)
    p = jnp.exp(s - m)
    den = jnp.sum(p, axis=-1, keepdims=True) + jnp.exp(sink - m)
    p = p.astype(BF16)
    half = t_len * GROUP
    own_head = lax.broadcasted_iota(jnp.int32, (half, KV_COLS), 1) < HEAD_DIM
    for b in range(SMP_BB):
        vn = jnp.concatenate([nv_ref[b], zpad], axis=0).astype(BF16)
        o = _dot_nt(p[b, :, :WINDOW], cvt_ref[b].astype(BF16)) + _dot(p[b, :, WINDOW:], vn)
        o = o / den[b]
        o_ref[b] = jnp.where(own_head, o[:half], o[half:])


def _attn_smp(q6, cache_kt, cache_vt, new_k, new_v, bias_s, valid_s, sink_s, t_len):
    n_b, rows, _ = q6.shape
    b3 = lambda i: (i, 0, 0)
    one = lambda i: (0, 0)
    return pl.pallas_call(
        functools.partial(_attn_smp_body, t_len=t_len),
        out_shape=jax.ShapeDtypeStruct((n_b, rows // N_KV_HEADS, KV_COLS), F32),
        grid=(n_b // SMP_BB,),
        in_specs=[pl.BlockSpec((SMP_BB, rows, KV_COLS), b3),
                  pl.BlockSpec((SMP_BB, KV_COLS, WINDOW), b3),
                  pl.BlockSpec((SMP_BB, KV_COLS, WINDOW), b3),
                  pl.BlockSpec((SMP_BB, t_len, KV_COLS), b3),
                  pl.BlockSpec((SMP_BB, t_len, KV_COLS), b3),
                  _resident(bias_s.shape, one), _resident(valid_s.shape, one),
                  _resident(sink_s.shape, one)],
        out_specs=pl.BlockSpec((SMP_BB, rows // N_KV_HEADS, KV_COLS), b3),
        compiler_params=_params(1, 32 * MIB),
        name="attn_smp",
    )(q6, cache_kt, cache_vt, new_k, new_v, bias_s, valid_s, sink_s)


def _out_proj_smp_body(w4_ref, b4_ref, a_ref, gu_ref, gv_ref, ga_ref, gg_ref,
                       x_ref, gt_ref, sh_ref, sc_ref, wb_ref, gpm_ref, gpf_ref,
                       x1_all_ref, hf_all_ref, x1_ref, hf_ref, *, t_len):
    del x1_all_ref, hf_all_ref
    i = pl.program_id(0)
    gated = []
    for hd in range(GMLP_HEADS):
        acc = jnp.zeros((a_ref.shape[0], GMLP_HEAD_DIM), F32)
        for j in range(t_len):
            w = jnp.where(j <= i, w4_ref[(hd * t_len + i) * t_len + j], 0.0)
            lo = j * GMLP_WIDTH + hd * GMLP_HEAD_DIM
            acc = acc + w * gv_ref[:, lo:lo + GMLP_HEAD_DIM]
        mixed = acc + b4_ref[hd * t_len + i]
        gated.append(gu_ref[:, hd * GMLP_HEAD_DIM:(hd + 1) * GMLP_HEAD_DIM] * mixed)
    gm = jnp.concatenate(gated, axis=-1)
    merged = jnp.concatenate([_rms(a_ref[...], ga_ref[...]), _rms(gm, gg_ref[...])], axis=-1)
    _out_proj_epilogue(_dot(merged.astype(BF16), wb_ref[...]), slice(0, a_ref.shape[0]),
                       x_ref, gt_ref, sh_ref, sc_ref, gpm_ref, gpf_ref, x1_ref, hf_ref)


FFN_TF = 1024
FFN_SUB = 256


def _ffn_body(h_ref, w1_ref, w2_ref, x1_ref, gp_ref, gs_ref, g_ref, op_ref, os_ref,
              acc_ref, done_ref, *, n_tiles, n_chunks):
    s = pl.program_id(0)
    i = s // n_chunks
    j = s % n_chunks
    tm = op_ref.shape[0]
    whole = slice(0, tm)

    def contribution(rows):
        a = jnp.maximum(_dot(h_ref[rows, :], w1_ref[...]), 0.0)
        return _dot((a * a).astype(BF16), w2_ref[...])

    def finalize(rows, gate, o_ref):
        o_ref[rows, :] = x1_ref[rows, :] + _rms(done_ref[rows, :], g_ref[...] * gate)

    @pl.when(s == 0)
    def _():
        acc_ref[...] = contribution(whole)

    @pl.when(jnp.logical_and(j == 0, jnp.logical_and(i > 0, i < n_tiles)))
    def _():
        for r in range(0, tm, FFN_SUB):
            rows = slice(r, r + FFN_SUB)
            acc_ref[rows, :] = contribution(rows)
            finalize(rows, gp_ref[...], op_ref)

    @pl.when(jnp.logical_and(j > 0, j < n_chunks - 1))
    def _():
        acc_ref[...] += contribution(whole)

    @pl.when(jnp.logical_and(j == n_chunks - 1, i < n_tiles))
    def _():
        done_ref[...] = acc_ref[...] + contribution(whole)

    @pl.when(i == n_tiles)
    def _():
        finalize(whole, _rows(gs_ref[...], tm), os_ref)


def _ffn(hf, w1, w2, x1, mod_p, gate_p, mod_s, gate_s, g_post, tm):
    n_rows = hf.shape[0]
    n_tiles, n_chunks = n_rows // tm, D_FF // FFN_TF
    n_prompt = n_tiles - 1
    last = n_tiles * n_chunks
    mm_tile = lambda s: jnp.minimum(s // n_chunks, n_tiles - 1)
    chunk = lambda s: jnp.where(s == last, n_chunks - 1, s % n_chunks)
    out_tile = lambda s: jnp.maximum(s // n_chunks - (s % n_chunks == 0).astype(jnp.int32), 0)
    prompt_tile = lambda s: jnp.minimum(out_tile(s), n_prompt - 1)
    return pl.pallas_call(
        functools.partial(_ffn_body, n_tiles=n_tiles, n_chunks=n_chunks),
        out_shape=(jax.ShapeDtypeStruct((n_prompt * tm, D_MODEL), F32),
                   jax.ShapeDtypeStruct((tm, D_MODEL), F32)),
        grid=(last + 1,),
        in_specs=[pl.BlockSpec((tm, D_MODEL), lambda s: (mm_tile(s), 0)),
                  pl.BlockSpec((D_MODEL, FFN_TF), lambda s: (0, chunk(s))),
                  pl.BlockSpec((FFN_TF, D_MODEL), lambda s: (chunk(s), 0)),
                  pl.BlockSpec((tm, D_MODEL), lambda s: (out_tile(s), 0)),
                  gate_p(prompt_tile), gate_s,
                  _resident((1, D_MODEL), lambda s: (0, 0))],
        out_specs=(pl.BlockSpec((tm, D_MODEL), lambda s: (prompt_tile(s), 0)),
                   pl.BlockSpec((tm, D_MODEL), lambda s: (0, 0))),
        scratch_shapes=[pltpu.VMEM((tm, D_MODEL), F32), pltpu.VMEM((tm, D_MODEL), F32)],
        compiler_params=_params(1),
        name="ffn",
    )(hf, w1, w2, x1, mod_p, mod_s, g_post)


MIX_TM = 512
MIX_SUB = 128
FFN_TM = 512


def _row(v):
    return v.reshape(1, -1)


def _layer(x_prompt, x_sample, cache_k, cache_v, c_prompt, c_sample, rel_table, w_ada, b_ada,
           g_pre_mix, w_in, sinks, v_gain, w_s, b_s, g_attn, g_gmlp, w_out, g_post_mix,
           g_pre_ff, w_ff1, w_ff2, g_post_ff):
    n_b, seq, _ = x_prompt.shape
    n_db, t_len, _ = x_sample.shape
    n_tok = n_b * seq
    D = D_MODEL

    g_pre_mix, g_attn, g_gmlp = _row(g_pre_mix), _row(g_attn), _row(g_gmlp)
    g_post_mix, g_pre_ff, g_post_ff = _row(g_post_mix), _row(g_pre_ff), _row(g_post_ff)
    v_gain = _row(v_gain)

    c_all = jnp.concatenate(
        [c_sample, c_prompt, jnp.zeros(((-(n_b + n_db)) % 16, D), F32)], axis=0)
    b_ada = b_ada.reshape(1, 6 * D)
    bias_p, bias_s, wtril, mod_s, mod_p = _cond(rel_table, w_s, t_len, c_all, n_db, n_b,
                                                w_ada, b_ada, 2 * D)
    SH_M, SC_M = 0, 1
    GT_M, SH_F, SC_F, GT_F = range(4)

    def pmod(chunk, tm):
        per = seq // tm
        return pl.BlockSpec((None, 1, D), lambda i: (i // per, 0, chunk))

    def smod(chunk):
        return pl.BlockSpec((n_db, D), lambda i: (0, chunk))

    sd = _sample_dist(t_len)
    valid_s = ((sd >= 0) & (sd < WINDOW) & (np.arange(SMP_KEYS)[None, :] < WINDOW + t_len))
    valid_s = valid_s.astype(np.float32)
    sink2 = sinks.reshape(N_KV_HEADS, GROUP)
    sink_p = jnp.repeat(sink2, BLOCK, axis=1)[:, None, :]
    sink_s = jnp.tile(sink2, (1, t_len)).reshape(-1, 1)
    ga3 = g_attn.reshape(N_KV_HEADS, GROUP, HEAD_DIM)
    gain_t = jnp.repeat(ga3.transpose(0, 2, 1).reshape(KV_COLS, GROUP), BLOCK, axis=1)
    g_attn = ga3.transpose(1, 0, 2).reshape(1, ATTN_WIDTH)

    xp = x_prompt.reshape(n_tok, D)
    tile = lambda w: pl.BlockSpec((MIX_TM, w), lambda i: (i, 0))
    ttile = lambda w: pl.BlockSpec((w, MIX_TM), lambda i: (0, i))
    n_mix = n_tok // MIX_TM
    slab1 = pl.BlockSpec((D // n_mix, D_FF), lambda i: (i, 0))
    assert seq >= WINDOW and MIX_TM >= WINDOW
    mix_per = seq // MIX_TM
    win = pl.BlockSpec((None, KV_COLS, WINDOW), lambda i: (i // mix_per, 0, 0))
    assert n_mix * BLOCK == D and GROUP * BLOCK == ATTN_WIDTH
    piece = lambda off: pl.BlockSpec(
        (HEAD_DIM, D), lambda i: (jnp.where(i < GROUP, i + off * GROUP, 2 * i + off), 0))
    slab_o = pl.BlockSpec((BLOCK, D), lambda i: (i, 0))
    ada_tn = 4 * D // n_mix
    ada_col = lambda i: (0, 2 * D // ada_tn + i)
    (qt_p, k_p, vt_p, gu_p, gvn_p, kwin, vwin, w1_b, w_out_b, w_in_b,
     mod2_s, mod2_p) = _mix_in(
        xp, tile(D), pmod(SH_M, MIX_TM), pmod(SC_M, MIX_TM), mod_p, (n_tok // MIX_TM,),
        g_pre_mix, w_in, v_gain,
        (jax.ShapeDtypeStruct((Q_COLS, n_tok), BF16),
         jax.ShapeDtypeStruct((n_tok, KV_COLS), BF16),
         jax.ShapeDtypeStruct((KV_COLS, n_tok), F32),
         jax.ShapeDtypeStruct((n_tok, GMLP_WIDTH), F32),
         jax.ShapeDtypeStruct((n_tok, GMLP_WIDTH), BF16),
         jax.ShapeDtypeStruct((n_b, KV_COLS, WINDOW), F32),
         jax.ShapeDtypeStruct((n_b, KV_COLS, WINDOW), F32),
         jax.ShapeDtypeStruct((D, D_FF), BF16),
         jax.ShapeDtypeStruct((D, D), BF16),
         jax.ShapeDtypeStruct((D, IN_COLS), BF16),
         jax.ShapeDtypeStruct((n_db, 4 * D), F32),
         jax.ShapeDtypeStruct((n_b, 1, 4 * D), F32)),
        (ttile(Q_COLS), tile(KV_COLS), ttile(KV_COLS), tile(GMLP_WIDTH), tile(GMLP_WIDTH),
         win, win, slab1, slab_o, pl.BlockSpec(memory_space=pl.ANY),
         pl.BlockSpec((n_db, ada_tn), lambda i: (0, i)), pl.BlockSpec((n_b, 1, ada_tn), lambda i: (0, 0, i))),
        side=((w_ff1, w_out, w_out, c_all, w_ada, b_ada),
              (slab1, piece(0), piece(1), _resident(c_all.shape, lambda i: (0, 0)),
               pl.BlockSpec((D, ada_tn), ada_col), pl.BlockSpec((1, ada_tn), ada_col))))

    x1_p, hf_p, w2_b = _mix_core(qt_p, k_p, vt_p, gu_p, gvn_p, bias_p, sink_p,
                                 wtril, b_s.T, gain_t, g_gmlp, seq, w_ff2,
                                 xp, mod2_p, pmod, w_out_b, g_post_mix, g_pre_ff,
                                 extra_rows=t_len * n_db)
    one = lambda i: (0, 0)

    xs = x_sample.reshape(n_db, t_len * D)
    lane = lambda w: pl.BlockSpec((n_db, w), lambda t: (0, t))
    q_s, k_s, v_s, gu_s, gvn_s = _mix_in(
        xs, lane(D), smod(SH_M), smod(SC_M), mod_s, (t_len,),
        g_pre_mix, w_in_b, v_gain,
        (jax.ShapeDtypeStruct((n_db, t_len * Q_COLS), F32),
         jax.ShapeDtypeStruct((n_db, t_len * KV_COLS), F32),
         jax.ShapeDtypeStruct((n_db, t_len * KV_COLS), F32),
         jax.ShapeDtypeStruct((n_db, t_len * GMLP_WIDTH), F32),
         jax.ShapeDtypeStruct((n_db, t_len * GMLP_WIDTH), F32)),
        (lane(Q_COLS), lane(KV_COLS), lane(KV_COLS), lane(GMLP_WIDTH), lane(GMLP_WIDTH)))

    q5 = q_s.reshape(n_db, t_len, N_KV_HEADS, GROUP, HEAD_DIM).transpose(0, 2, 1, 3, 4)
    q5 = q5.reshape(n_db, N_KV_HEADS, t_len * GROUP, HEAD_DIM).astype(BF16)
    z5 = jnp.zeros_like(q5[:, 0])
    q6 = jnp.concatenate([jnp.concatenate([q5[:, 0], z5], axis=-1),
                          jnp.concatenate([z5, q5[:, 1]], axis=-1)], axis=1)
    new_k = k_s.reshape(n_db, t_len, KV_COLS)
    new_v = v_s.reshape(n_db, t_len, KV_COLS)
    cache_kt = cache_k.transpose(0, 2, 3, 1).reshape(n_db, KV_COLS, WINDOW)
    cache_vt = cache_v.transpose(0, 2, 3, 1).reshape(n_db, KV_COLS, WINDOW)
    attn_s = _attn_smp(q6, cache_kt, cache_vt, new_k, new_v, bias_s, jnp.asarray(valid_s), sink_s,
                       t_len)
    attn_s = attn_s.reshape(n_db, t_len * ATTN_WIDTH)
    smem = pl.BlockSpec(memory_space=pltpu.SMEM)
    w4 = w_s[:, :t_len, :t_len].reshape(-1)
    b4 = b_s[:, :t_len].reshape(-1)
    assert n_tok % n_db == 0 and t_len * n_db == FFN_TM
    tail = lambda: pl.BlockSpec((n_db, D), lambda t: (n_tok // n_db + t, 0))
    any_space = pl.BlockSpec(memory_space=pl.ANY)
    x1_all, hf_all = pl.pallas_call(
        functools.partial(_out_proj_smp_body, t_len=t_len),
        out_shape=(jax.ShapeDtypeStruct(x1_p.shape, F32), jax.ShapeDtypeStruct(hf_p.shape, BF16)),
        grid=(t_len,),
        in_specs=[smem, smem, lane(ATTN_WIDTH), lane(GMLP_WIDTH),
                  _resident((n_db, t_len * GMLP_WIDTH), one),
                  _resident((1, ATTN_WIDTH), one), _resident((1, GMLP_WIDTH), one),
                  lane(D), smod(GT_M), smod(SH_F), smod(SC_F),
                  _resident((D, D), one), _resident((1, D), one), _resident((1, D), one),
                  any_space, any_space],
        out_specs=(tail(), tail()),
        input_output_aliases={14: 0, 15: 1},
        compiler_params=_params(1),
        name="out_proj_smp",
    )(w4, b4, attn_s, gu_s, gvn_s, g_attn, g_gmlp, xs, mod2_s, mod2_s, mod2_s,
      w_out_b, g_post_mix, g_pre_ff, x1_p, hf_p)

    ffn_per = seq // FFN_TM
    gate_p = lambda tile: pl.BlockSpec((None, 1, D), lambda s: (tile(s) // ffn_per, 0, GT_F))
    y_p, y_s = _ffn(hf_all, w1_b, w2_b, x1_all, mod2_p, gate_p, mod2_s,
                    _resident((n_db, D), lambda s: (0, GT_F)), g_post_ff, FFN_TM)
    y_s = y_s.reshape(t_len, n_db, D).transpose(1, 0, 2)

    last = lambda t: t.reshape(n_b, N_KV_HEADS, HEAD_DIM, WINDOW).transpose(0, 3, 1, 2)
    k_p4, v_p4 = last(kwin), last(vwin)
    return (y_p.reshape(n_b, seq, D), y_s, k_p4, v_p4,
            k_s.reshape(n_db, t_len, N_KV_HEADS, HEAD_DIM),
            v_s.reshape(n_db, t_len, N_KV_HEADS, HEAD_DIM),
            gvn_s.reshape(n_db, t_len, GMLP_HEADS, GMLP_HEAD_DIM))


def kernel(x_prompt, x_sample, cache_k, cache_v, c_prompt, c_sample, rel_bias_table, w_ada, b_ada,
           g_pre_mix, w_in, attn_sinks, gmlp_v_gain, gmlp_w_s, gmlp_b_s, g_attn_out, g_gmlp_out,
           w_out, g_post_mix, g_pre_ff, w_ff1, w_ff2, g_post_ff):
    depth = w_in.shape[0]
    assert depth == 1, "single-layer step"
    only = lambda a: a.reshape(a.shape[1:])
    outs = _layer(x_prompt, x_sample, only(cache_k), only(cache_v), c_prompt, c_sample,
                  rel_bias_table, *map(only, (w_ada, b_ada, g_pre_mix, w_in, attn_sinks, gmlp_v_gain,
                                              gmlp_w_s, gmlp_b_s, g_attn_out, g_gmlp_out, w_out,
                                              g_post_mix, g_pre_ff, w_ff1, w_ff2, g_post_ff)))
    y_p, y_s, k_p, v_p, k_s, v_s, gv_s = outs
    return (y_p, y_s, k_p[None], v_p[None], k_s[None], v_s[None], gv_s[None])
```

```python
import functools
import math

import numpy as np
import jax
import jax.numpy as jnp
from jax import lax
from jax.experimental import pallas as pl
from jax.experimental.pallas import tpu as pltpu

D_MODEL = 2048
HEAD_DIM = 64
ATTN_WIDTH = 1024
N_HEADS = 16
N_KV_HEADS = 2
GROUP = 8
WINDOW = 128
BLOCK = 128
GMLP_WIDTH = 1024
GMLP_HEADS = 8
GMLP_HEAD_DIM = 128
D_FF = 4 * D_MODEL
NUM_BUCKETS = 32
MAX_DISTANCE = 128
EPS = 1e-6
KV_COLS = N_KV_HEADS * HEAD_DIM
Q_COLS = N_HEADS * HEAD_DIM
IN_COLS = Q_COLS + 2 * KV_COLS + 2 * GMLP_WIDTH
ATTN_SCALE = HEAD_DIM ** -0.5
NEG_INF = -1e30

MIB = 1024 * 1024
VMEM_LIMIT_BYTES = 56 * MIB
MIX_IN_VMEM_BYTES = 60 * MIB
MIX_CORE_VMEM_BYTES = 58 * MIB

BF16 = jnp.bfloat16
F32 = jnp.float32


def _params(n_axes, vmem=VMEM_LIMIT_BYTES):
    return pltpu.CompilerParams(
        dimension_semantics=("arbitrary",) * n_axes, vmem_limit_bytes=vmem)


def _resident(shape, index_map):
    return pl.BlockSpec(shape, index_map, pipeline_mode=pl.Buffered(1))


def _rms(x, gain):
    return x * lax.rsqrt(jnp.mean(x * x, axis=-1, keepdims=True) + EPS) * gain


def _rows(m, n_rows):
    r = m.shape[0]
    if r == 1 or r == n_rows:
        return m
    return jnp.concatenate([m] * (n_rows // r), axis=0)


def _dot(a, b):
    return jnp.dot(a, b, preferred_element_type=F32)


def _dot_nt(a, b):
    return lax.dot_general(a, b, (((1,), (1,)), ((), ())), preferred_element_type=F32)


def _t5_bucket(dist):
    n = np.maximum(dist, 0)
    max_exact = NUM_BUCKETS // 2
    nf = np.maximum(n, 1).astype(np.float64)
    large = max_exact + (np.log(nf / max_exact) / math.log(MAX_DISTANCE / max_exact)
                         * (NUM_BUCKETS - max_exact)).astype(np.int32)
    large = np.minimum(large, NUM_BUCKETS - 1)
    return jnp.asarray(np.where(n < max_exact, n, large), jnp.int32)


SMP_KEYS = 2 * WINDOW


def _prompt_dist():
    r = np.arange(BLOCK)[:, None]
    i = np.arange(BLOCK)[None, :]
    return (i - r) % BLOCK


def _sample_rows(t_len):
    r = np.arange(N_KV_HEADS * t_len * GROUP)
    return r // (t_len * GROUP), (r // GROUP) % t_len, r % GROUP


def _sample_dist(t_len):
    _, t, _ = _sample_rows(t_len)
    return WINDOW + t[:, None] - np.arange(SMP_KEYS)[None, :]


def _prep_body(tab_ref, tabr_ref, bkt_p_ref, bkt_s_ref, ws_ref,
               bias_p_ref, bias_s_ref, wtril_ref):
    bkt_p = bkt_p_ref[...]
    for kh in range(N_KV_HEADS):
        for g in range(GROUP):
            h = kh * GROUP + g
            acc = jnp.zeros((BLOCK, BLOCK), F32)
            for b in range(NUM_BUCKETS):
                acc = jnp.where(bkt_p == b, tab_ref[b, h], acc)
            bias_p_ref[kh, :, g * BLOCK:(g + 1) * BLOCK] = acc
    bkt_s = bkt_s_ref[...]
    tabr = tabr_ref[...]
    acc = jnp.zeros(bkt_s.shape, F32)
    for b in range(NUM_BUCKETS):
        acc = jnp.where(bkt_s == b, tabr[:, b:b + 1], acc)
    bias_s_ref[...] = acc
    ii = lax.broadcasted_iota(jnp.int32, (BLOCK, BLOCK), 0)
    jj = lax.broadcasted_iota(jnp.int32, (BLOCK, BLOCK), 1)
    for h in range(GMLP_HEADS):
        wtril_ref[h] = jnp.where(jj <= ii, ws_ref[h], 0.0).astype(BF16)


ADA_TN = 1024


def _adaln_body(c_ref, w_ref, b_ref, os_ref, op_ref):
    n_s, n_p = os_ref.shape[0], op_ref.shape[0]
    c = c_ref[...]
    s = (c * jax.nn.sigmoid(c)).astype(BF16)
    r = _dot(s, w_ref[...].astype(BF16)) + b_ref[...]
    os_ref[...] = r[:n_s]
    for p in range(n_p):
        op_ref[p] = r[n_s + p:n_s + p + 1]


def _cond_body(tab_ref, tabr_ref, bkt_p_ref, bkt_s_ref, ws_ref, c_ref, w_ref, b_ref,
               bias_p_ref, bias_s_ref, wtril_ref, os_ref, op_ref):
    @pl.when(pl.program_id(0) == 0)
    def _():
        _prep_body(tab_ref, tabr_ref, bkt_p_ref, bkt_s_ref, ws_ref, bias_p_ref, bias_s_ref, wtril_ref)

    _adaln_body(c_ref, w_ref, b_ref, os_ref, op_ref)


def _cond(rel_table, w_s, t_len, c_all, n_s, n_p, w_ada, b_ada, n_out):
    bkt_p = _t5_bucket(_prompt_dist())
    bkt_s = _t5_bucket(_sample_dist(t_len))
    n_rows = bkt_s.shape[0]
    tab_rows = jnp.broadcast_to(rel_table.T.reshape(N_KV_HEADS, 1, GROUP, NUM_BUCKETS),
                                (N_KV_HEADS, t_len, GROUP, NUM_BUCKETS)).reshape(n_rows, NUM_BUCKETS)
    n_all = c_all.shape[0]
    whole = lambda a: _resident(a.shape, lambda j: (0,) * a.ndim)
    outs = (jax.ShapeDtypeStruct((N_KV_HEADS, BLOCK, GROUP * BLOCK), F32),
            jax.ShapeDtypeStruct((n_rows, SMP_KEYS), F32),
            jax.ShapeDtypeStruct((GMLP_HEADS, BLOCK, BLOCK), BF16))
    return pl.pallas_call(
        _cond_body,
        out_shape=outs + (jax.ShapeDtypeStruct((n_s, n_out), F32),
                          jax.ShapeDtypeStruct((n_p, 1, n_out), F32)),
        grid=(n_out // ADA_TN,),
        in_specs=[pl.BlockSpec(memory_space=pltpu.SMEM),
                  whole(tab_rows), whole(bkt_p), whole(bkt_s), whole(w_s),
                  _resident((n_all, D_MODEL), lambda j: (0, 0)),
                  pl.BlockSpec((D_MODEL, ADA_TN), lambda j: (0, j)),
                  pl.BlockSpec((1, ADA_TN), lambda j: (0, j))],
        out_specs=tuple(pl.BlockSpec(o.shape, lambda j, nd=len(o.shape): (0,) * nd) for o in outs)
        + (pl.BlockSpec((n_s, ADA_TN), lambda j: (0, j)),
           pl.BlockSpec((n_p, 1, ADA_TN), lambda j: (0, 0, j))),
        compiler_params=_params(1, 40 * MIB),
        name="cond",
    )(rel_table, tab_rows, bkt_p, bkt_s, w_s, c_all, w_ada, b_ada)


W_STAGE_COLS = 256


def _stage_w_in(w_hbm, wb_ref, stage_ref, sem_ref):
    n = IN_COLS // W_STAGE_COLS

    def fetch(c):
        return pltpu.make_async_copy(w_hbm.at[:, pl.ds(c * W_STAGE_COLS, W_STAGE_COLS)],
                                     stage_ref.at[c % 2], sem_ref.at[c % 2])

    fetch(0).start()
    for c in range(n):
        if c + 1 < n:
            fetch(c + 1).start()
        fetch(c).wait()
        wb_ref[:, c * W_STAGE_COLS:(c + 1) * W_STAGE_COLS] = stage_ref[c % 2].astype(BF16)


def _mix_in_body(x_ref, sh_ref, sc_ref, g_ref, w_ref, vg_ref, *refs, transposed):
    if transposed:
        (wc_ref, woa_ref, wob_ref, ca_ref, wa_ref, ba_ref,
         q_ref, k_ref, v_ref, gu_ref, gv_ref, kwin_ref, vwin_ref,
         wcb_ref, wo_ref, wib_hbm, ms_ref, mp_ref, wb_ref, stage_ref, sem_ref, wsem_ref) = refs
        write_back = pltpu.make_async_copy(wb_ref, wib_hbm, wsem_ref)

        @pl.when(pl.program_id(0) == 0)
        def _():
            _stage_w_in(w_ref, wb_ref, stage_ref, sem_ref)
            write_back.start()

        @pl.when(pl.program_id(0) == pl.num_programs(0) - 1)
        def _():
            write_back.wait()

        w_ref = wb_ref
        wcb_ref[...] = wc_ref[...].astype(wcb_ref.dtype)
        wo_ref[:HEAD_DIM, :] = woa_ref[...].astype(wo_ref.dtype)
        wo_ref[HEAD_DIM:, :] = wob_ref[...].astype(wo_ref.dtype)
        _adaln_body(ca_ref, wa_ref, ba_ref, ms_ref, mp_ref)
    else:
        q_ref, k_ref, v_ref, gu_ref, gv_ref = refs
    tm = x_ref.shape[0]
    sub = min(tm, MIX_SUB)
    c0, c1, c2, c3 = Q_COLS, Q_COLS + KV_COLS, Q_COLS + 2 * KV_COLS, Q_COLS + 2 * KV_COLS + GMLP_WIDTH
    vg = vg_ref[...]
    for r in range(0, tm, sub):
        rows = slice(r, r + sub)
        mod = lambda ref: ref[...] if ref.shape[0] == 1 else ref[rows, :]
        h = (_rms(x_ref[rows, :], g_ref[...] * (1.0 + mod(sc_ref))) + mod(sh_ref)).astype(BF16)
        q = _dot(h, w_ref[:, :c0]) * ATTN_SCALE
        kv = _dot(h, w_ref[:, c0:c2])
        k_ref[rows, :] = kv[:, :KV_COLS].astype(k_ref.dtype)
        if transposed:
            q_ref[:, rows] = q.T.astype(q_ref.dtype)
            vt = kv[:, KV_COLS:].T
            v_ref[:, rows] = vt
            if r + sub == tm:
                kwin_ref[...] = kv[-WINDOW:, :KV_COLS].T
                vwin_ref[...] = vt[:, -WINDOW:]
        else:
            q_ref[rows, :] = q.astype(q_ref.dtype)
            v_ref[rows, :] = kv[:, KV_COLS:]
        gu_ref[rows, :] = jax.nn.gelu(_dot(h, w_ref[:, c2:c3])).astype(gu_ref.dtype)
        gv = jax.nn.gelu(_dot(h, w_ref[:, c3:]))
        for hd in range(GMLP_HEADS):
            sl = slice(hd * GMLP_HEAD_DIM, (hd + 1) * GMLP_HEAD_DIM)
            gv_ref[rows, sl] = _rms(gv[:, sl], vg[:, sl]).astype(gv_ref.dtype)


def _mix_in(x, x_spec, sh_spec, sc_spec, mod, grid, g_pre, w_in, v_gain, outs, out_specs,
            side=((), ())):
    extra, extra_specs = side
    prompt = len(extra) > 0
    any_space = pl.BlockSpec(memory_space=pl.ANY)
    scratch = [pltpu.VMEM((D_MODEL, IN_COLS), BF16),
               pltpu.VMEM((2, D_MODEL, W_STAGE_COLS), F32),
               pltpu.SemaphoreType.DMA((2,)), pltpu.SemaphoreType.DMA(())] if prompt else []
    return pl.pallas_call(
        functools.partial(_mix_in_body, transposed=prompt),
        out_shape=outs,
        grid=grid,
        in_specs=[x_spec, sh_spec, sc_spec,
                  _resident((1, D_MODEL), lambda i: (0, 0)),
                  any_space if prompt else _resident((D_MODEL, IN_COLS), lambda i: (0, 0)),
                  _resident((1, GMLP_WIDTH), lambda i: (0, 0)), *extra_specs],
        out_specs=out_specs,
        scratch_shapes=scratch,
        compiler_params=_params(1, MIX_IN_VMEM_BYTES if prompt else VMEM_LIMIT_BYTES),
        name="mix_in",
    )(x, mod, mod, g_pre, w_in, v_gain, *extra)


PROJ_TN = 256


def _out_proj_epilogue(o, rows, x_ref, gt_ref, sh_ref, sc_ref, gpm_ref, gpf_ref, x1_ref, hf_ref):
    mod = lambda ref: ref[...] if ref.shape[0] == 1 else ref[rows, :]
    x1 = x_ref[rows, :] + _rms(o, gpm_ref[...] * mod(gt_ref))
    x1_ref[rows, :] = x1
    hf = _rms(x1, gpf_ref[...] * (1.0 + mod(sc_ref))) + mod(sh_ref)
    hf_ref[rows, :] = hf.astype(hf_ref.dtype)


CORE_NB = 4


def _mix_core_body(qt_ref, kc_ref, kp_ref, vtc_ref, vtp_ref, gu_ref, gv_ref,
                   bias_ref, pm_ref, sink_ref, wtril_ref, bs_ref, gat_ref, gg_ref, wc_ref,
                   x_ref, gt_ref, sh_ref, sc_ref, wb_ref, gpm_ref, gpf_ref,
                   x1_ref, hf_ref, wcb_ref, *, steps_per_seq):
    wcb_ref[...] = wc_ref[...].astype(wcb_ref.dtype)
    first = pl.program_id(0) % steps_per_seq == 0
    tri = (lax.broadcasted_iota(jnp.int32, (BLOCK, BLOCK), 0)
           > lax.broadcasted_iota(jnp.int32, (BLOCK, BLOCK), 1))
    from_prev = jnp.concatenate([tri] * GROUP, axis=1)
    zeros = jnp.zeros((HEAD_DIM, GROUP * BLOCK), BF16)
    bs = bs_ref[...]
    def mix_stages(u, st):
        tok = slice(u * BLOCK, (u + 1) * BLOCK)

        def logits(kh):
            if kh == 0:
                k_prev = kp_ref[...] if u == 0 else kc_ref[(u - 1) * BLOCK:u * BLOCK, :]
                vt_prev = vtp_ref[...] if u == 0 else vtc_ref[:, (u - 1) * BLOCK:u * BLOCK]
                st['k'] = jnp.concatenate([k_prev, kc_ref[tok, :]], axis=0)
                st['vt'] = jnp.concatenate([vt_prev, vtc_ref[:, tok]], axis=1).astype(BF16)
                st['outs'] = []
            base = kh * GROUP * HEAD_DIM
            qs = jnp.concatenate(
                [qt_ref[base + g * HEAD_DIM: base + (g + 1) * HEAD_DIM, tok] for g in range(GROUP)],
                axis=1)
            qp = jnp.concatenate([qs, zeros] if kh == 0 else [zeros, qs], axis=0)
            st['qk'] = _dot(st['k'], qp)

        def softmax(kh):
            qk = st['qk']
            s = jnp.where(from_prev, qk[:BLOCK], qk[BLOCK:]) + bias_ref[kh]
            if u == 0:
                s = jnp.where(jnp.logical_and(first, from_prev), NEG_INF, s)
            sink = sink_ref[kh]
            m =jnp.maximum(jnp.max(s, axis=0, keepdims=True), sink)
            p = jnp.exp(s - m)
            st['den'] = jnp.sum(p, axis=0, keepdims=True) + jnp.exp(sink - m)
            p = p.astype(BF16)
            st['p2'] = jnp.concatenate([p * pm_ref[0], p * pm_ref[1]], axis=0)

        def values(kh):
            vt = st['vt'][kh * HEAD_DIM:(kh + 1) * HEAD_DIM, :]
            st['outs'].append(_dot(vt, st['p2']) / st['den'])

        def attn_norm():
            ot = jnp.concatenate(st['outs'], axis=0)
            sq = jnp.sum(ot * ot, axis=0, keepdims=True)
            tot = sq[:, :BLOCK]
            for g in range(1, GROUP):
                tot = tot + sq[:, g * BLOCK:(g + 1) * BLOCK]
            r = lax.rsqrt(tot / ATTN_WIDTH + EPS)
            a = ot * jnp.concatenate([r] * GROUP, axis=1) * gat_ref[...]
            st['merged'] = [a[:, g * BLOCK:(g + 1) * BLOCK].T.astype(BF16) for g in range(GROUP)]

        def gating():
            gated = []
            for hd in range(GMLP_HEADS):
                sl = slice(hd * GMLP_HEAD_DIM, (hd + 1) * GMLP_HEAD_DIM)
                mixed = _dot(wtril_ref[hd], gv_ref[tok, sl]) + bs[:, hd:hd + 1]
                gated.append(gu_ref[tok, sl] * mixed)
            gm = jnp.concatenate(gated, axis=-1)
            st['merged'] = jnp.concatenate(
                st['merged'] + [_rms(gm, gg_ref[...]).astype(BF16)], axis=1)

        P = functools.partial
        return [P(logits, 0), P(softmax, 0), P(values, 0), P(logits, 1), P(softmax, 1),
                P(values, 1), attn_norm, gating]

    def proj_stages(u, st):
        tok = slice(u * BLOCK, (u + 1) * BLOCK)
        chunks = []

        def chunk(c):
            chunks.append(_dot(st['merged'], wb_ref[:, c * PROJ_TN:(c + 1) * PROJ_TN]))

        def finish():
            _out_proj_epilogue(jnp.concatenate(chunks, axis=1), tok, x_ref, gt_ref, sh_ref,
                               sc_ref, gpm_ref, gpf_ref, x1_ref, hf_ref)

        return [functools.partial(chunk, c) for c in range(D_MODEL // PROJ_TN)] + [finish]

    states = [dict() for _ in range(CORE_NB)]
    for u in range(CORE_NB + 1):
        a = mix_stages(u, states[u]) if u < CORE_NB else []
        b = proj_stages(u - 1, states[u - 1]) if u > 0 else []
        for i in range(max(len(a), len(b))):
            for stage in (b[i:i + 1] + a[i:i + 1]):
                stage()


def _mix_core(qt, k, vt, gu, gvn, bias_t, sink_row, wtril, bs_t, gain_t, g_gmlp, seq, w_side,
              x, mod, mod_spec, w_out_b, g_post_mix, g_pre_ff):
    r_gt_i = np.tile(np.arange(BLOCK)[:, None] > np.arange(BLOCK)[None, :], (1, GROUP))
    pmask = jnp.asarray(np.stack([r_gt_i, ~r_gt_i]), BF16)
    n_tok = k.shape[0]
    tm = CORE_NB * BLOCK
    n_steps = n_tok // tm
    side_rows = w_side.shape[0] // n_steps
    GT_M, SH_F, SC_F = 0, 1, 2
    cur = lambda i: (i, 0)
    prev = lambda i: (jnp.maximum(i * CORE_NB - 1, 0), 0)
    cur_t = lambda i: (0, i)
    prev_t = lambda i: (0, jnp.maximum(i * CORE_NB - 1, 0))
    full2 = lambda i: (0, 0)
    full3 = lambda i: (0, 0, 0)
    return pl.pallas_call(
        functools.partial(_mix_core_body, steps_per_seq=seq // tm),
        out_shape=(jax.ShapeDtypeStruct((n_tok, D_MODEL), F32),
                   jax.ShapeDtypeStruct((n_tok, D_MODEL), BF16),
                   jax.ShapeDtypeStruct(w_side.shape, BF16)),
        grid=(n_steps,),
        in_specs=[pl.BlockSpec((Q_COLS, tm), cur_t),
                  pl.BlockSpec((tm, KV_COLS), cur), pl.BlockSpec((BLOCK, KV_COLS), prev),
                  pl.BlockSpec((KV_COLS, tm), cur_t), pl.BlockSpec((KV_COLS, BLOCK), prev_t),
                  pl.BlockSpec((tm, GMLP_WIDTH), cur), pl.BlockSpec((tm, GMLP_WIDTH), cur),
                  _resident(bias_t.shape, full3), _resident(pmask.shape, full3),
                  _resident(sink_row.shape, full3), _resident(wtril.shape, full3),
                  _resident(bs_t.shape, full2),
                  _resident(gain_t.shape, full2), _resident((1, GMLP_WIDTH), full2),
                  pl.BlockSpec((side_rows, w_side.shape[1]), cur),
                  pl.BlockSpec((tm, D_MODEL), cur),
                  mod_spec(GT_M, tm), mod_spec(SH_F, tm), mod_spec(SC_F, tm),
                  _resident((D_MODEL, D_MODEL), full2),
                  _resident((1, D_MODEL), full2), _resident((1, D_MODEL), full2)],
        out_specs=(pl.BlockSpec((tm, D_MODEL), cur), pl.BlockSpec((tm, D_MODEL), cur),
                   pl.BlockSpec((side_rows, w_side.shape[1]), cur)),
        compiler_params=_params(1, MIX_CORE_VMEM_BYTES),
        name="mix_core",
    )(qt, k, k, vt, vt, gu, gvn, bias_t, pmask, sink_row, wtril, bs_t, gain_t, g_gmlp, w_side,
      x, mod, mod, mod, w_out_b, g_post_mix, g_pre_ff)


SMP_BB = 32


def _attn_smp_body(q_ref, ckt_ref, cvt_ref, nk_ref, nv_ref, bias_ref, valid_ref, sink_ref,
                   o_ref, *, t_len):
    zpad = jnp.zeros((WINDOW - t_len, KV_COLS), F32)
    tiles = []
    for b in range(SMP_BB):
        q = q_ref[b]
        s_cache = _dot(q, ckt_ref[b].astype(BF16))
        kn = jnp.concatenate([nk_ref[b], zpad], axis=0).astype(BF16)
        tiles.append(jnp.concatenate([s_cache, _dot_nt(q, kn)], axis=1)[None])
    s = jnp.concatenate(tiles, axis=0)
    s = jnp.where(valid_ref[...] > 0.5, s + bias_ref[...], NEG_INF)
    sink = sink_ref[...]
    m = jnp.maximum(jnp.max(s, axis=-1, keepdims=True), sink)
    p = jnp.exp(s - m)
    den = jnp.sum(p, axis=-1, keepdims=True) + jnp.exp(sink - m)
    p = p.astype(BF16)
    half = t_len * GROUP
    own_head = lax.broadcasted_iota(jnp.int32, (half, KV_COLS), 1) < HEAD_DIM
    for b in range(SMP_BB):
        vn = jnp.concatenate([nv_ref[b], zpad], axis=0).astype(BF16)
        o = _dot_nt(p[b, :, :WINDOW], cvt_ref[b].astype(BF16)) + _dot(p[b, :, WINDOW:], vn)
        o = o / den[b]
        o_ref[b] = jnp.where(own_head, o[:half], o[half:])


def _attn_smp(q6, cache_kt, cache_vt, new_k, new_v, bias_s, valid_s, sink_s, t_len):
    n_b, rows, _ = q6.shape
    b3 = lambda i: (i, 0, 0)
    one = lambda i: (0, 0)
    return pl.pallas_call(
        functools.partial(_attn_smp_body, t_len=t_len),
        out_shape=jax.ShapeDtypeStruct((n_b, rows // N_KV_HEADS, KV_COLS), F32),
        grid=(n_b // SMP_BB,),
        in_specs=[pl.BlockSpec((SMP_BB, rows, KV_COLS), b3),
                  pl.BlockSpec((SMP_BB, KV_COLS, WINDOW), b3),
                  pl.BlockSpec((SMP_BB, KV_COLS, WINDOW), b3),
                  pl.BlockSpec((SMP_BB, t_len, KV_COLS), b3),
                  pl.BlockSpec((SMP_BB, t_len, KV_COLS), b3),
                  _resident(bias_s.shape, one), _resident(valid_s.shape, one),
                  _resident(sink_s.shape, one)],
        out_specs=pl.BlockSpec((SMP_BB, rows // N_KV_HEADS, KV_COLS), b3),
        compiler_params=_params(1, MIX_IN_VMEM_BYTES),
        name="attn_smp",
    )(q6, cache_kt, cache_vt, new_k, new_v, bias_s, valid_s, sink_s)


def _out_proj_smp_body(w4_ref, b4_ref, a_ref, gu_ref, gv_ref, ga_ref, gg_ref,
                       x_ref, gt_ref, sh_ref, sc_ref, wb_ref, gpm_ref, gpf_ref, x1_ref, hf_ref, *,
                       t_len):
    i = pl.program_id(0)
    gated = []
    for hd in range(GMLP_HEADS):
        acc = jnp.zeros((a_ref.shape[0], GMLP_HEAD_DIM), F32)
        for j in range(t_len):
            w = jnp.where(j <= i, w4_ref[(hd * t_len + i) * t_len + j], 0.0)
            lo = j * GMLP_WIDTH + hd * GMLP_HEAD_DIM
            acc = acc + w * gv_ref[:, lo:lo + GMLP_HEAD_DIM]
        mixed = acc + b4_ref[hd * t_len + i]
        gated.append(gu_ref[:, hd * GMLP_HEAD_DIM:(hd + 1) * GMLP_HEAD_DIM] * mixed)
    gm = jnp.concatenate(gated, axis=-1)
    merged = jnp.concatenate([_rms(a_ref[...], ga_ref[...]), _rms(gm, gg_ref[...])], axis=-1)
    _out_proj_epilogue(_dot(merged.astype(BF16), wb_ref[...]), slice(0, a_ref.shape[0]),
                       x_ref, gt_ref, sh_ref, sc_ref, gpm_ref, gpf_ref, x1_ref, hf_ref)


FFN_TF = 1024
FFN_SUB = 256


def _ffn_body(h_ref, w1_ref, w2_ref, x1_ref, gt_ref, g_ref, o_ref, acc_ref, done_ref, *,
              n_tiles, n_chunks):
    s = pl.program_id(0)
    i = s // n_chunks
    j = s % n_chunks
    tm = o_ref.shape[0]
    whole = slice(0, tm)

    def contribution(rows):
        a = jnp.maximum(_dot(h_ref[rows, :], w1_ref[...]), 0.0)
        return _dot((a * a).astype(BF16), w2_ref[...])

    def finalize(rows):
        gt = gt_ref[...] if gt_ref.shape[0] == 1 else _rows(gt_ref[...], tm)[rows, :]
        o_ref[rows, :] = x1_ref[rows, :] + _rms(done_ref[rows, :], g_ref[...] * gt)

    @pl.when(s == 0)
    def _():
        acc_ref[...] = contribution(whole)

    @pl.when(jnp.logical_and(j == 0, jnp.logical_and(i > 0, i < n_tiles)))
    def _():
        for r in range(0, tm, FFN_SUB):
            rows = slice(r, r + FFN_SUB)
            acc_ref[rows, :] = contribution(rows)
            finalize(rows)

    @pl.when(jnp.logical_and(j > 0, j < n_chunks - 1))
    def _():
        acc_ref[...] += contribution(whole)

    @pl.when(jnp.logical_and(j == n_chunks - 1, i < n_tiles))
    def _():
        done_ref[...] = acc_ref[...] + contribution(whole)

    @pl.when(i == n_tiles)
    def _():
        finalize(whole)


def _ffn(hf, w1, w2, x1, mod, gt_spec, g_post, tm):
    n_tok = hf.shape[0]
    n_tiles, n_chunks = n_tok // tm, D_FF // FFN_TF
    last = n_tiles * n_chunks
    mm_tile = lambda s: jnp.minimum(s // n_chunks, n_tiles - 1)
    chunk = lambda s: jnp.where(s == last, n_chunks - 1, s % n_chunks)
    out_tile = lambda s: jnp.maximum(s // n_chunks - (s % n_chunks == 0).astype(jnp.int32), 0)
    return pl.pallas_call(
        functools.partial(_ffn_body, n_tiles=n_tiles, n_chunks=n_chunks),
        out_shape=jax.ShapeDtypeStruct((n_tok, D_MODEL), F32),
        grid=(last + 1,),
        in_specs=[pl.BlockSpec((tm, D_MODEL), lambda s: (mm_tile(s), 0)),
                  pl.BlockSpec((D_MODEL, FFN_TF), lambda s: (0, chunk(s))),
                  pl.BlockSpec((FFN_TF, D_MODEL), lambda s: (chunk(s), 0)),
                  pl.BlockSpec((tm, D_MODEL), lambda s: (out_tile(s), 0)),
                  gt_spec(out_tile),
                  _resident((1, D_MODEL), lambda s: (0, 0))],
        out_specs=pl.BlockSpec((tm, D_MODEL), lambda s: (out_tile(s), 0)),
        scratch_shapes=[pltpu.VMEM((tm, D_MODEL), F32), pltpu.VMEM((tm, D_MODEL), F32)],
        compiler_params=_params(1),
        name="ffn",
    )(hf, w1, w2, x1, mod, g_post)


MIX_TM = 512
MIX_SUB = 128
FFN_TM = 512


def _row(v):
    return v.reshape(1, -1)


def _layer(x_prompt, x_sample, cache_k, cache_v, c_prompt, c_sample, rel_table, w_ada, b_ada,
           g_pre_mix, w_in, sinks, v_gain, w_s, b_s, g_attn, g_gmlp, w_out, g_post_mix,
           g_pre_ff, w_ff1, w_ff2, g_post_ff):
    n_b, seq, _ = x_prompt.shape
    n_db, t_len, _ = x_sample.shape
    n_tok = n_b * seq
    D = D_MODEL

    g_pre_mix, g_attn, g_gmlp = _row(g_pre_mix), _row(g_attn), _row(g_gmlp)
    g_post_mix, g_pre_ff, g_post_ff = _row(g_post_mix), _row(g_pre_ff), _row(g_post_ff)
    v_gain = _row(v_gain)

    c_all = jnp.concatenate(
        [c_sample, c_prompt, jnp.zeros(((-(n_b + n_db)) % 16, D), F32)], axis=0)
    b_ada = b_ada.reshape(1, 6 * D)
    bias_p, bias_s, wtril, mod_s, mod_p = _cond(rel_table, w_s, t_len, c_all, n_db, n_b,
                                                w_ada, b_ada, 2 * D)
    SH_M, SC_M = 0, 1
    GT_M, SH_F, SC_F, GT_F = range(4)

    def pmod(chunk, tm):
        per = seq // tm
        return pl.BlockSpec((None, 1, D), lambda i: (i // per, 0, chunk))

    def smod(chunk):
        return pl.BlockSpec((n_db, D), lambda i: (0, chunk))

    sd = _sample_dist(t_len)
    valid_s = ((sd >= 0) & (sd < WINDOW) & (np.arange(SMP_KEYS)[None, :] < WINDOW + t_len))
    valid_s = valid_s.astype(np.float32)
    sink2 = sinks.reshape(N_KV_HEADS, GROUP)
    sink_p = jnp.repeat(sink2, BLOCK, axis=1)[:, None, :]
    sink_s = jnp.tile(sink2, (1, t_len)).reshape(-1, 1)
    ga3 = g_attn.reshape(N_KV_HEADS, GROUP, HEAD_DIM)
    gain_t = jnp.repeat(ga3.transpose(0, 2, 1).reshape(KV_COLS, GROUP), BLOCK, axis=1)
    g_attn = ga3.transpose(1, 0, 2).reshape(1, ATTN_WIDTH)

    xp = x_prompt.reshape(n_tok, D)
    tile = lambda w: pl.BlockSpec((MIX_TM, w), lambda i: (i, 0))
    ttile = lambda w: pl.BlockSpec((w, MIX_TM), lambda i: (0, i))
    n_mix = n_tok // MIX_TM
    slab1 = pl.BlockSpec((D // n_mix, D_FF), lambda i: (i, 0))
    assert seq >= WINDOW and MIX_TM >= WINDOW
    mix_per = seq // MIX_TM
    win = pl.BlockSpec((None, KV_COLS, WINDOW), lambda i: (i // mix_per, 0, 0))
    assert n_mix * BLOCK == D and GROUP * BLOCK == ATTN_WIDTH
    piece = lambda off: pl.BlockSpec(
        (HEAD_DIM, D), lambda i: (jnp.where(i < GROUP, i + off * GROUP, 2 * i + off), 0))
    slab_o = pl.BlockSpec((BLOCK, D), lambda i: (i, 0))
    ada_tn = 4 * D // n_mix
    ada_col = lambda i: (0, 2 * D // ada_tn + i)
    (qt_p, k_p, vt_p, gu_p, gvn_p, kwin, vwin, w1_b, w_out_b, w_in_b,
     mod2_s, mod2_p) = _mix_in(
        xp, tile(D), pmod(SH_M, MIX_TM), pmod(SC_M, MIX_TM), mod_p, (n_tok // MIX_TM,),
        g_pre_mix, w_in, v_gain,
        (jax.ShapeDtypeStruct((Q_COLS, n_tok), BF16),
         jax.ShapeDtypeStruct((n_tok, KV_COLS), BF16),
         jax.ShapeDtypeStruct((KV_COLS, n_tok), F32),
         jax.ShapeDtypeStruct((n_tok, GMLP_WIDTH), F32),
         jax.ShapeDtypeStruct((n_tok, GMLP_WIDTH), BF16),
         jax.ShapeDtypeStruct((n_b, KV_COLS, WINDOW), F32),
         jax.ShapeDtypeStruct((n_b, KV_COLS, WINDOW), F32),
         jax.ShapeDtypeStruct((D, D_FF), BF16),
         jax.ShapeDtypeStruct((D, D), BF16),
         jax.ShapeDtypeStruct((D, IN_COLS), BF16),
         jax.ShapeDtypeStruct((n_db, 4 * D), F32),
         jax.ShapeDtypeStruct((n_b, 1, 4 * D), F32)),
        (ttile(Q_COLS), tile(KV_COLS), ttile(KV_COLS), tile(GMLP_WIDTH), tile(GMLP_WIDTH),
         win, win, slab1, slab_o, pl.BlockSpec(memory_space=pl.ANY),
         pl.BlockSpec((n_db, ada_tn), lambda i: (0, i)), pl.BlockSpec((n_b, 1, ada_tn), lambda i: (0, 0, i))),
        side=((w_ff1, w_out, w_out, c_all, w_ada, b_ada),
              (slab1, piece(0), piece(1), _resident(c_all.shape, lambda i: (0, 0)),
               pl.BlockSpec((D, ada_tn), ada_col), pl.BlockSpec((1, ada_tn), ada_col))))

    x1_p, hf_p, w2_b = _mix_core(qt_p, k_p, vt_p, gu_p, gvn_p, bias_p, sink_p,
                                 wtril, b_s.T, gain_t, g_gmlp, seq, w_ff2,
                                 xp, mod2_p, pmod, w_out_b, g_post_mix, g_pre_ff)
    one = lambda i: (0, 0)

    ffn_per = seq // FFN_TM
    gate_p = lambda tile: pl.BlockSpec((None, 1, D), lambda s: (tile(s) // ffn_per, 0, GT_F))
    y_p = _ffn(hf_p, w1_b, w2_b, x1_p, mod2_p, gate_p, g_post_ff, FFN_TM)

    xs = x_sample.reshape(n_db, t_len * D)
    lane = lambda w: pl.BlockSpec((n_db, w), lambda t: (0, t))
    q_s, k_s, v_s, gu_s, gvn_s = _mix_in(
        xs, lane(D), smod(SH_M), smod(SC_M), mod_s, (t_len,),
        g_pre_mix, w_in_b, v_gain,
        (jax.ShapeDtypeStruct((n_db, t_len * Q_COLS), F32),
         jax.ShapeDtypeStruct((n_db, t_len * KV_COLS), F32),
         jax.ShapeDtypeStruct((n_db, t_len * KV_COLS), F32),
         jax.ShapeDtypeStruct((n_db, t_len * GMLP_WIDTH), F32),
         jax.ShapeDtypeStruct((n_db, t_len * GMLP_WIDTH), F32)),
        (lane(Q_COLS), lane(KV_COLS), lane(KV_COLS), lane(GMLP_WIDTH), lane(GMLP_WIDTH)))

    q5 = q_s.reshape(n_db, t_len, N_KV_HEADS, GROUP, HEAD_DIM).transpose(0, 2, 1, 3, 4)
    q5 = q5.reshape(n_db, N_KV_HEADS, t_len * GROUP, HEAD_DIM).astype(BF16)
    z5 = jnp.zeros_like(q5[:, 0])
    q6 = jnp.concatenate([jnp.concatenate([q5[:, 0], z5], axis=-1),
                          jnp.concatenate([z5, q5[:, 1]], axis=-1)], axis=1)
    new_k = k_s.reshape(n_db, t_len, KV_COLS)
    new_v = v_s.reshape(n_db, t_len, KV_COLS)
    cache_kt = cache_k.transpose(0, 2, 3, 1).reshape(n_db, KV_COLS, WINDOW)
    cache_vt = cache_v.transpose(0, 2, 3, 1).reshape(n_db, KV_COLS, WINDOW)
    attn_s = _attn_smp(q6, cache_kt, cache_vt, new_k, new_v, bias_s, jnp.asarray(valid_s), sink_s,
                       t_len)
    attn_s = attn_s.reshape(n_db, t_len * ATTN_WIDTH)
    smem = pl.BlockSpec(memory_space=pltpu.SMEM)
    w4 = w_s[:, :t_len, :t_len].reshape(-1)
    b4 = b_s[:, :t_len].reshape(-1)
    tmaj = lambda: pl.BlockSpec((None, n_db, D), lambda t: (t, 0, 0))
    x1_s, hf_s = pl.pallas_call(
        functools.partial(_out_proj_smp_body, t_len=t_len),
        out_shape=(jax.ShapeDtypeStruct((t_len, n_db, D), F32),
                   jax.ShapeDtypeStruct((t_len, n_db, D), BF16)),
        grid=(t_len,),
        in_specs=[smem, smem, lane(ATTN_WIDTH), lane(GMLP_WIDTH),
                  _resident((n_db, t_len * GMLP_WIDTH), one),
                  _resident((1, ATTN_WIDTH), one), _resident((1, GMLP_WIDTH), one),
                  lane(D), smod(GT_M), smod(SH_F), smod(SC_F),
                  _resident((D, D), one), _resident((1, D), one), _resident((1, D), one)],
        out_specs=(tmaj(), tmaj()),
        compiler_params=_params(1),
        name="out_proj_smp",
    )(w4, b4, attn_s, gu_s, gvn_s, g_attn, g_gmlp, xs, mod2_s, mod2_s, mod2_s,
      w_out_b, g_post_mix, g_pre_ff)

    n_st = t_len * n_db
    y_s = _ffn(hf_s.reshape(n_st, D), w1_b, w2_b, x1_s.reshape(n_st, D), mod2_s,
               lambda tile: smod(GT_F), g_post_ff, n_st)
    y_s = y_s.reshape(t_len, n_db, D).transpose(1, 0, 2)

    last = lambda t: t.reshape(n_b, N_KV_HEADS, HEAD_DIM, WINDOW).transpose(0, 3, 1, 2)
    k_p4, v_p4 = last(kwin), last(vwin)
    return (y_p.reshape(n_b, seq, D), y_s, k_p4, v_p4,
            k_s.reshape(n_db, t_len, N_KV_HEADS, HEAD_DIM),
            v_s.reshape(n_db, t_len, N_KV_HEADS, HEAD_DIM),
            gvn_s.reshape(n_db, t_len, GMLP_HEADS, GMLP_HEAD_DIM))


def kernel(x_prompt, x_sample, cache_k, cache_v, c_prompt, c_sample, rel_bias_table, w_ada, b_ada,
           g_pre_mix, w_in, attn_sinks, gmlp_v_gain, gmlp_w_s, gmlp_b_s, g_attn_out, g_gmlp_out,
           w_out, g_post_mix, g_pre_ff, w_ff1, w_ff2, g_post_ff):
    depth = w_in.shape[0]
    assert depth == 1, "single-layer step"
    only = lambda a: a.reshape(a.shape[1:])
    outs = _layer(x_prompt, x_sample, only(cache_k), only(cache_v), c_prompt, c_sample,
                  rel_bias_table, *map(only, (w_ada, b_ada, g_pre_mix, w_in, attn_sinks, gmlp_v_gain,
                                              gmlp_w_s, gmlp_b_s, g_attn_out, g_gmlp_out, w_out,
                                              g_post_mix, g_pre_ff, w_ff1, w_ff2, g_post_ff)))
    y_p, y_s, k_p, v_p, k_s, v_s, gv_s = outs
    return (y_p, y_s, k_p[None], v_p[None], k_s[None], v_s[None], gv_s[None])
```

```python
import functools
import math

import numpy as np
import jax
import jax.numpy as jnp
from jax import lax
from jax.experimental import pallas as pl
from jax.experimental.pallas import tpu as pltpu

D_MODEL = 2048
HEAD_DIM = 64
ATTN_WIDTH = 1024
N_HEADS = 16
N_KV_HEADS = 2
GROUP = 8
WINDOW = 128
BLOCK = 128
GMLP_WIDTH = 1024
GMLP_HEADS = 8
GMLP_HEAD_DIM = 128
D_FF = 4 * D_MODEL
NUM_BUCKETS = 32
MAX_DISTANCE = 128
EPS = 1e-6
KV_COLS = N_KV_HEADS * HEAD_DIM
Q_COLS = N_HEADS * HEAD_DIM
IN_COLS = Q_COLS + 2 * KV_COLS + 2 * GMLP_WIDTH
ATTN_SCALE = HEAD_DIM ** -0.5
NEG_INF = -1e30

MIB = 1024 * 1024
VMEM_LIMIT_BYTES = 56 * MIB
MIX_IN_VMEM_BYTES = 60 * MIB
MIX_CORE_VMEM_BYTES = 58 * MIB

BF16 = jnp.bfloat16
F32 = jnp.float32


def _params(n_axes, vmem=VMEM_LIMIT_BYTES):
    return pltpu.CompilerParams(
        dimension_semantics=("arbitrary",) * n_axes, vmem_limit_bytes=vmem)


def _resident(shape, index_map):
    return pl.BlockSpec(shape, index_map, pipeline_mode=pl.Buffered(1))


def _rms(x, gain):
    return x * lax.rsqrt(jnp.mean(x * x, axis=-1, keepdims=True) + EPS) * gain


def _rows(m, n_rows):
    r = m.shape[0]
    if r == 1 or r == n_rows:
        return m
    return jnp.concatenate([m] * (n_rows // r), axis=0)


def _dot(a, b):
    return jnp.dot(a, b, preferred_element_type=F32)


def _dot_nt(a, b):
    return lax.dot_general(a, b, (((1,), (1,)), ((), ())), preferred_element_type=F32)


def _t5_bucket(dist):
    n = np.maximum(dist, 0)
    max_exact = NUM_BUCKETS // 2
    nf = np.maximum(n, 1).astype(np.float64)
    large = max_exact + (np.log(nf / max_exact) / math.log(MAX_DISTANCE / max_exact)
                         * (NUM_BUCKETS - max_exact)).astype(np.int32)
    large = np.minimum(large, NUM_BUCKETS - 1)
    return jnp.asarray(np.where(n < max_exact, n, large), jnp.int32)


SMP_KEYS = 2 * WINDOW


def _prompt_dist():
    r = np.arange(BLOCK)[:, None]
    i = np.arange(BLOCK)[None, :]
    return (i - r) % BLOCK


def _sample_rows(t_len):
    r = np.arange(N_KV_HEADS * t_len * GROUP)
    return r // (t_len * GROUP), (r // GROUP) % t_len, r % GROUP


def _sample_dist(t_len):
    _, t, _ = _sample_rows(t_len)
    return WINDOW + t[:, None] - np.arange(SMP_KEYS)[None, :]


def _prep_body(tab_ref, tabr_ref, bkt_p_ref, bkt_s_ref, ws_ref,
               bias_p_ref, bias_s_ref, wtril_ref):
    bkt_p = bkt_p_ref[...]
    for kh in range(N_KV_HEADS):
        for g in range(GROUP):
            h = kh * GROUP + g
            acc = jnp.zeros((BLOCK, BLOCK), F32)
            for b in range(NUM_BUCKETS):
                acc = jnp.where(bkt_p == b, tab_ref[b, h], acc)
            bias_p_ref[kh, :, g * BLOCK:(g + 1) * BLOCK] = acc
    bkt_s = bkt_s_ref[...]
    tabr = tabr_ref[...]
    acc = jnp.zeros(bkt_s.shape, F32)
    for b in range(NUM_BUCKETS):
        acc = jnp.where(bkt_s == b, tabr[:, b:b + 1], acc)
    bias_s_ref[...] = acc
    ii = lax.broadcasted_iota(jnp.int32, (BLOCK, BLOCK), 0)
    jj = lax.broadcasted_iota(jnp.int32, (BLOCK, BLOCK), 1)
    for h in range(GMLP_HEADS):
        wtril_ref[h] = jnp.where(jj <= ii, ws_ref[h], 0.0).astype(BF16)


ADA_TN = 1024


def _adaln_body(c_ref, w_ref, b_ref, os_ref, op_ref):
    n_s, n_p = os_ref.shape[0], op_ref.shape[0]
    c = c_ref[...]
    s = (c * jax.nn.sigmoid(c)).astype(BF16)
    r = _dot(s, w_ref[...].astype(BF16)) + b_ref[...]
    os_ref[...] = r[:n_s]
    for p in range(n_p):
        op_ref[p] = r[n_s + p:n_s + p + 1]


def _cond_body(tab_ref, tabr_ref, bkt_p_ref, bkt_s_ref, ws_ref, c_ref, w_ref, b_ref,
               bias_p_ref, bias_s_ref, wtril_ref, os_ref, op_ref):
    @pl.when(pl.program_id(0) == 0)
    def _():
        _prep_body(tab_ref, tabr_ref, bkt_p_ref, bkt_s_ref, ws_ref, bias_p_ref, bias_s_ref, wtril_ref)

    _adaln_body(c_ref, w_ref, b_ref, os_ref, op_ref)


def _cond(rel_table, w_s, t_len, c_all, n_s, n_p, w_ada, b_ada, n_out):
    bkt_p = _t5_bucket(_prompt_dist())
    bkt_s = _t5_bucket(_sample_dist(t_len))
    n_rows = bkt_s.shape[0]
    tab_rows = jnp.broadcast_to(rel_table.T.reshape(N_KV_HEADS, 1, GROUP, NUM_BUCKETS),
                                (N_KV_HEADS, t_len, GROUP, NUM_BUCKETS)).reshape(n_rows, NUM_BUCKETS)
    n_all = c_all.shape[0]
    whole = lambda a: _resident(a.shape, lambda j: (0,) * a.ndim)
    outs = (jax.ShapeDtypeStruct((N_KV_HEADS, BLOCK, GROUP * BLOCK), F32),
            jax.ShapeDtypeStruct((n_rows, SMP_KEYS), F32),
            jax.ShapeDtypeStruct((GMLP_HEADS, BLOCK, BLOCK), BF16))
    return pl.pallas_call(
        _cond_body,
        out_shape=outs + (jax.ShapeDtypeStruct((n_s, n_out), F32),
                          jax.ShapeDtypeStruct((n_p, 1, n_out), F32)),
        grid=(n_out // ADA_TN,),
        in_specs=[pl.BlockSpec(memory_space=pltpu.SMEM),
                  whole(tab_rows), whole(bkt_p), whole(bkt_s), whole(w_s),
                  _resident((n_all, D_MODEL), lambda j: (0, 0)),
                  pl.BlockSpec((D_MODEL, ADA_TN), lambda j: (0, j)),
                  pl.BlockSpec((1, ADA_TN), lambda j: (0, j))],
        out_specs=tuple(pl.BlockSpec(o.shape, lambda j, nd=len(o.shape): (0,) * nd) for o in outs)
        + (pl.BlockSpec((n_s, ADA_TN), lambda j: (0, j)),
           pl.BlockSpec((n_p, 1, ADA_TN), lambda j: (0, 0, j))),
        compiler_params=_params(1, 40 * MIB),
        name="cond",
    )(rel_table, tab_rows, bkt_p, bkt_s, w_s, c_all, w_ada, b_ada)


W_STAGE_COLS = 256


def _stage_w_in(w_hbm, wb_ref, stage_ref, sem_ref):
    n = IN_COLS // W_STAGE_COLS

    def fetch(c):
        return pltpu.make_async_copy(w_hbm.at[:, pl.ds(c * W_STAGE_COLS, W_STAGE_COLS)],
                                     stage_ref.at[c % 2], sem_ref.at[c % 2])

    fetch(0).start()
    for c in range(n):
        if c + 1 < n:
            fetch(c + 1).start()
        fetch(c).wait()
        wb_ref[:, c * W_STAGE_COLS:(c + 1) * W_STAGE_COLS] = stage_ref[c % 2].astype(BF16)


def _mix_in_body(x_ref, sh_ref, sc_ref, g_ref, w_ref, vg_ref, *refs, transposed):
    if transposed:
        (woa_ref, wob_ref, ca_ref, wa_ref, ba_ref,
         q_ref, k_ref, v_ref, gu_ref, gv_ref, kwin_ref, vwin_ref,
         wo_ref, wib_hbm, ms_ref, mp_ref, wb_ref, stage_ref, sem_ref, wsem_ref) = refs
        write_back = pltpu.make_async_copy(wb_ref, wib_hbm, wsem_ref)

        @pl.when(pl.program_id(0) == 0)
        def _():
            _stage_w_in(w_ref, wb_ref, stage_ref, sem_ref)
            write_back.start()

        @pl.when(pl.program_id(0) == pl.num_programs(0) - 1)
        def _():
            write_back.wait()

        w_ref = wb_ref
        wo_ref[:HEAD_DIM, :] = woa_ref[...].astype(wo_ref.dtype)
        wo_ref[HEAD_DIM:, :] = wob_ref[...].astype(wo_ref.dtype)
        _adaln_body(ca_ref, wa_ref, ba_ref, ms_ref, mp_ref)
    else:
        q_ref, k_ref, v_ref, gu_ref, gv_ref = refs
    tm = x_ref.shape[0]
    sub = min(tm, MIX_SUB)
    c0, c1, c2, c3 = Q_COLS, Q_COLS + KV_COLS, Q_COLS + 2 * KV_COLS, Q_COLS + 2 * KV_COLS + GMLP_WIDTH
    vg = vg_ref[...]
    for r in range(0, tm, sub):
        rows = slice(r, r + sub)
        mod = lambda ref: ref[...] if ref.shape[0] == 1 else ref[rows, :]
        h = (_rms(x_ref[rows, :], g_ref[...] * (1.0 + mod(sc_ref))) + mod(sh_ref)).astype(BF16)
        q = _dot(h, w_ref[:, :c0]) * ATTN_SCALE
        kv = _dot(h, w_ref[:, c0:c2])
        k_ref[rows, :] = kv[:, :KV_COLS].astype(k_ref.dtype)
        if transposed:
            q_ref[:, rows] = q.T.astype(q_ref.dtype)
            vt = kv[:, KV_COLS:].T
            v_ref[:, rows] = vt
            if r + sub == tm:
                kwin_ref[...] = kv[-WINDOW:, :KV_COLS].T
                vwin_ref[...] = vt[:, -WINDOW:]
        else:
            q_ref[rows, :] = q.astype(q_ref.dtype)
            v_ref[rows, :] = kv[:, KV_COLS:]
        gu_ref[rows, :] = jax.nn.gelu(_dot(h, w_ref[:, c2:c3])).astype(gu_ref.dtype)
        gv = jax.nn.gelu(_dot(h, w_ref[:, c3:]))
        for hd in range(GMLP_HEADS):
            sl = slice(hd * GMLP_HEAD_DIM, (hd + 1) * GMLP_HEAD_DIM)
            gv_ref[rows, sl] = _rms(gv[:, sl], vg[:, sl]).astype(gv_ref.dtype)


def _mix_in(x, x_spec, sh_spec, sc_spec, mod, grid, g_pre, w_in, v_gain, outs, out_specs,
            side=((), ())):
    extra, extra_specs = side
    prompt = len(extra) > 0
    any_space = pl.BlockSpec(memory_space=pl.ANY)
    scratch = [pltpu.VMEM((D_MODEL, IN_COLS), BF16),
               pltpu.VMEM((2, D_MODEL, W_STAGE_COLS), F32),
               pltpu.SemaphoreType.DMA((2,)), pltpu.SemaphoreType.DMA(())] if prompt else []
    return pl.pallas_call(
        functools.partial(_mix_in_body, transposed=prompt),
        out_shape=outs,
        grid=grid,
        in_specs=[x_spec, sh_spec, sc_spec,
                  _resident((1, D_MODEL), lambda i: (0, 0)),
                  any_space if prompt else _resident((D_MODEL, IN_COLS), lambda i: (0, 0)),
                  _resident((1, GMLP_WIDTH), lambda i: (0, 0)), *extra_specs],
        out_specs=out_specs,
        scratch_shapes=scratch,
        compiler_params=_params(1, MIX_IN_VMEM_BYTES if prompt else VMEM_LIMIT_BYTES),
        name="mix_in",
    )(x, mod, mod, g_pre, w_in, v_gain, *extra)


PROJ_TN = 256


def _out_proj_epilogue(o, rows, x_ref, gt_ref, sh_ref, sc_ref, gpm_ref, gpf_ref, x1_ref, hf_ref):
    mod = lambda ref: ref[...] if ref.shape[0] == 1 else ref[rows, :]
    x1 = x_ref[rows, :] + _rms(o, gpm_ref[...] * mod(gt_ref))
    x1_ref[rows, :] = x1
    hf = _rms(x1, gpf_ref[...] * (1.0 + mod(sc_ref))) + mod(sh_ref)
    hf_ref[rows, :] = hf.astype(hf_ref.dtype)


CORE_NB = 4


def _mix_core_body(qt_ref, kc_ref, kp_ref, vtc_ref, vtp_ref, gu_ref, gv_ref,
                   bias_ref, pm_ref, sink_ref, wtril_ref, bs_ref, gat_ref, gg_ref,
                   x_ref, gt_ref, sh_ref, sc_ref, wb_ref, gpm_ref, gpf_ref,
                   x1_ref, hf_ref, *, steps_per_seq):
    first = pl.program_id(0) % steps_per_seq == 0
    tri = (lax.broadcasted_iota(jnp.int32, (BLOCK, BLOCK), 0)
           > lax.broadcasted_iota(jnp.int32, (BLOCK, BLOCK), 1))
    from_prev = jnp.concatenate([tri] * GROUP, axis=1)
    zeros = jnp.zeros((HEAD_DIM, GROUP * BLOCK), BF16)
    bs = bs_ref[...]
    def mix_stages(u, st):
        tok = slice(u * BLOCK, (u + 1) * BLOCK)

        def logits(kh):
            if kh == 0:
                k_prev = kp_ref[...] if u == 0 else kc_ref[(u - 1) * BLOCK:u * BLOCK, :]
                vt_prev = vtp_ref[...] if u == 0 else vtc_ref[:, (u - 1) * BLOCK:u * BLOCK]
                st['k'] = jnp.concatenate([k_prev, kc_ref[tok, :]], axis=0)
                st['vt'] = jnp.concatenate([vt_prev, vtc_ref[:, tok]], axis=1).astype(BF16)
                st['outs'] = []
            base = kh * GROUP * HEAD_DIM
            qs = jnp.concatenate(
                [qt_ref[base + g * HEAD_DIM: base + (g + 1) * HEAD_DIM, tok] for g in range(GROUP)],
                axis=1)
            qp = jnp.concatenate([qs, zeros] if kh == 0 else [zeros, qs], axis=0)
            st['qk'] = _dot(st['k'], qp)

        def softmax(kh):
            qk = st['qk']
            s = jnp.where(from_prev, qk[:BLOCK], qk[BLOCK:]) + bias_ref[kh]
            if u == 0:
                s = jnp.where(jnp.logical_and(first, from_prev), NEG_INF, s)
            sink = sink_ref[kh]
            m =jnp.maximum(jnp.max(s, axis=0, keepdims=True), sink)
            p = jnp.exp(s - m)
            st['den'] = jnp.sum(p, axis=0, keepdims=True) + jnp.exp(sink - m)
            p = p.astype(BF16)
            st['p2'] = jnp.concatenate([p * pm_ref[0], p * pm_ref[1]], axis=0)

        def values(kh):
            vt = st['vt'][kh * HEAD_DIM:(kh + 1) * HEAD_DIM, :]
            st['outs'].append(_dot(vt, st['p2']) / st['den'])

        def attn_norm():
            ot = jnp.concatenate(st['outs'], axis=0)
            sq = jnp.sum(ot * ot, axis=0, keepdims=True)
            tot = sq[:, :BLOCK]
            for g in range(1, GROUP):
                tot = tot + sq[:, g * BLOCK:(g + 1) * BLOCK]
            r = lax.rsqrt(tot / ATTN_WIDTH + EPS)
            a = ot * jnp.concatenate([r] * GROUP, axis=1) * gat_ref[...]
            st['merged'] = [a[:, g * BLOCK:(g + 1) * BLOCK].T.astype(BF16) for g in range(GROUP)]

        def gating():
            gated = []
            for hd in range(GMLP_HEADS):
                sl = slice(hd * GMLP_HEAD_DIM, (hd + 1) * GMLP_HEAD_DIM)
                mixed = _dot(wtril_ref[hd], gv_ref[tok, sl]) + bs[:, hd:hd + 1]
                gated.append(gu_ref[tok, sl] * mixed)
            gm = jnp.concatenate(gated, axis=-1)
            st['merged'] = jnp.concatenate(
                st['merged'] + [_rms(gm, gg_ref[...]).astype(BF16)], axis=1)

        P = functools.partial
        return [P(logits, 0), P(softmax, 0), P(values, 0), P(logits, 1), P(softmax, 1),
                P(values, 1), attn_norm, gating]

    def proj_stages(u, st):
        tok = slice(u * BLOCK, (u + 1) * BLOCK)
        chunks = []

        def chunk(c):
            chunks.append(_dot(st['merged'], wb_ref[:, c * PROJ_TN:(c + 1) * PROJ_TN]))

        def finish():
            _out_proj_epilogue(jnp.concatenate(chunks, axis=1), tok, x_ref, gt_ref, sh_ref,
                               sc_ref, gpm_ref, gpf_ref, x1_ref, hf_ref)

        return [functools.partial(chunk, c) for c in range(D_MODEL // PROJ_TN)] + [finish]

    states = [dict() for _ in range(CORE_NB)]
    for u in range(CORE_NB + 1):
        a = mix_stages(u, states[u]) if u < CORE_NB else []
        b = proj_stages(u - 1, states[u - 1]) if u > 0 else []
        for i in range(max(len(a), len(b))):
            for stage in (b[i:i + 1] + a[i:i + 1]):
                stage()


def _mix_core(qt, k, vt, gu, gvn, bias_t, sink_row, wtril, bs_t, gain_t, g_gmlp, seq,
              x, mod, mod_spec, w_out_b, g_post_mix, g_pre_ff):
    r_gt_i = np.tile(np.arange(BLOCK)[:, None] > np.arange(BLOCK)[None, :], (1, GROUP))
    pmask = jnp.asarray(np.stack([r_gt_i, ~r_gt_i]), BF16)
    n_tok = k.shape[0]
    tm = CORE_NB * BLOCK
    n_steps = n_tok // tm
    GT_M, SH_F, SC_F = 0, 1, 2
    cur = lambda i: (i, 0)
    prev = lambda i: (jnp.maximum(i * CORE_NB - 1, 0), 0)
    cur_t = lambda i: (0, i)
    prev_t = lambda i: (0, jnp.maximum(i * CORE_NB - 1, 0))
    full2 = lambda i: (0, 0)
    full3 = lambda i: (0, 0, 0)
    return pl.pallas_call(
        functools.partial(_mix_core_body, steps_per_seq=seq // tm),
        out_shape=(jax.ShapeDtypeStruct((n_tok, D_MODEL), F32),
                   jax.ShapeDtypeStruct((n_tok, D_MODEL), BF16)),
        grid=(n_steps,),
        in_specs=[pl.BlockSpec((Q_COLS, tm), cur_t),
                  pl.BlockSpec((tm, KV_COLS), cur), pl.BlockSpec((BLOCK, KV_COLS), prev),
                  pl.BlockSpec((KV_COLS, tm), cur_t), pl.BlockSpec((KV_COLS, BLOCK), prev_t),
                  pl.BlockSpec((tm, GMLP_WIDTH), cur), pl.BlockSpec((tm, GMLP_WIDTH), cur),
                  _resident(bias_t.shape, full3), _resident(pmask.shape, full3),
                  _resident(sink_row.shape, full3), _resident(wtril.shape, full3),
                  _resident(bs_t.shape, full2),
                  _resident(gain_t.shape, full2), _resident((1, GMLP_WIDTH), full2),
                  pl.BlockSpec((tm, D_MODEL), cur),
                  mod_spec(GT_M, tm), mod_spec(SH_F, tm), mod_spec(SC_F, tm),
                  _resident((D_MODEL, D_MODEL), full2),
                  _resident((1, D_MODEL), full2), _resident((1, D_MODEL), full2)],
        out_specs=(pl.BlockSpec((tm, D_MODEL), cur), pl.BlockSpec((tm, D_MODEL), cur)),
        compiler_params=_params(1, MIX_CORE_VMEM_BYTES),
        name="mix_core",
    )(qt, k, k, vt, vt, gu, gvn, bias_t, pmask, sink_row, wtril, bs_t, gain_t, g_gmlp,
      x, mod, mod, mod, w_out_b, g_post_mix, g_pre_ff)


SMP_BB = 32


def _attn_smp_body(q_ref, ckt_ref, cvt_ref, nk_ref, nv_ref, bias_ref, valid_ref, sink_ref,
                   o_ref, *, t_len):
    zpad = jnp.zeros((WINDOW - t_len, KV_COLS), F32)
    tiles = []
    for b in range(SMP_BB):
        q = q_ref[b]
        s_cache = _dot(q, ckt_ref[b].astype(BF16))
        kn = jnp.concatenate([nk_ref[b], zpad], axis=0).astype(BF16)
        tiles.append(jnp.concatenate([s_cache, _dot_nt(q, kn)], axis=1)[None])
    s = jnp.concatenate(tiles, axis=0)
    s = jnp.where(valid_ref[...] > 0.5, s + bias_ref[...], NEG_INF)
    sink = sink_ref[...]
    m = jnp.maximum(jnp.max(s, axis=-1, keepdims=True), sink)
    p = jnp.exp(s - m)
    den = jnp.sum(p, axis=-1, keepdims=True) + jnp.exp(sink - m)
    p = p.astype(BF16)
    half = t_len * GROUP
    own_head = lax.broadcasted_iota(jnp.int32, (half, KV_COLS), 1) < HEAD_DIM
    for b in range(SMP_BB):
        vn = jnp.concatenate([nv_ref[b], zpad], axis=0).astype(BF16)
        o = _dot_nt(p[b, :, :WINDOW], cvt_ref[b].astype(BF16)) + _dot(p[b, :, WINDOW:], vn)
        o = o / den[b]
        o_ref[b] = jnp.where(own_head, o[:half], o[half:])


def _attn_smp(q6, cache_kt, cache_vt, new_k, new_v, bias_s, valid_s, sink_s, t_len):
    n_b, rows, _ = q6.shape
    b3 = lambda i: (i, 0, 0)
    one = lambda i: (0, 0)
    return pl.pallas_call(
        functools.partial(_attn_smp_body, t_len=t_len),
        out_shape=jax.ShapeDtypeStruct((n_b, rows // N_KV_HEADS, KV_COLS), F32),
        grid=(n_b // SMP_BB,),
        in_specs=[pl.BlockSpec((SMP_BB, rows, KV_COLS), b3),
                  pl.BlockSpec((SMP_BB, KV_COLS, WINDOW), b3),
                  pl.BlockSpec((SMP_BB, KV_COLS, WINDOW), b3),
                  pl.BlockSpec((SMP_BB, t_len, KV_COLS), b3),
                  pl.BlockSpec((SMP_BB, t_len, KV_COLS), b3),
                  _resident(bias_s.shape, one), _resident(valid_s.shape, one),
                  _resident(sink_s.shape, one)],
        out_specs=pl.BlockSpec((SMP_BB, rows // N_KV_HEADS, KV_COLS), b3),
        compiler_params=_params(1, 32 * MIB),
        name="attn_smp",
    )(q6, cache_kt, cache_vt, new_k, new_v, bias_s, valid_s, sink_s)


def _out_proj_smp_body(w4_ref, b4_ref, a_ref, gu_ref, gv_ref, ga_ref, gg_ref,
                       x_ref, gt_ref, sh_ref, sc_ref, wb_ref, gpm_ref, gpf_ref, x1_ref, hf_ref, *,
                       t_len):
    i = pl.program_id(0)
    gated = []
    for hd in range(GMLP_HEADS):
        acc = jnp.zeros((a_ref.shape[0], GMLP_HEAD_DIM), F32)
        for j in range(t_len):
            w = jnp.where(j <= i, w4_ref[(hd * t_len + i) * t_len + j], 0.0)
            lo = j * GMLP_WIDTH + hd * GMLP_HEAD_DIM
            acc = acc + w * gv_ref[:, lo:lo + GMLP_HEAD_DIM]
        mixed = acc + b4_ref[hd * t_len + i]
        gated.append(gu_ref[:, hd * GMLP_HEAD_DIM:(hd + 1) * GMLP_HEAD_DIM] * mixed)
    gm = jnp.concatenate(gated, axis=-1)
    merged = jnp.concatenate([_rms(a_ref[...], ga_ref[...]), _rms(gm, gg_ref[...])], axis=-1)
    _out_proj_epilogue(_dot(merged.astype(BF16), wb_ref[...]), slice(0, a_ref.shape[0]),
                       x_ref, gt_ref, sh_ref, sc_ref, gpm_ref, gpf_ref, x1_ref, hf_ref)


FFN_TF = 1024
FFN_TF_ROUNDING = 512
FFN_ROUNDING_VMEM_BYTES = 60 * MIB
FFN_SUB = 256


def _ffn_body(h_ref, w1_ref, w2_ref, x1_ref, gt_ref, g_ref, o_ref, *refs, n_tiles, n_chunks):
    if len(refs) == 4:
        w1b_ref, w2b_ref, acc_ref, done_ref = refs
        w1b_ref[...] = w1_ref[...].astype(BF16)
        w2b_ref[...] = w2_ref[...].astype(BF16)
        w1_ref, w2_ref = w1b_ref, w2b_ref
    else:
        acc_ref, done_ref = refs
    s = pl.program_id(0)
    i = s // n_chunks
    j = s % n_chunks
    tm = o_ref.shape[0]
    whole = slice(0, tm)

    def contribution(rows):
        a = jnp.maximum(_dot(h_ref[rows, :], w1_ref[...]), 0.0)
        return _dot((a * a).astype(BF16), w2_ref[...])

    def finalize(rows):
        gt = gt_ref[...] if gt_ref.shape[0] == 1 else _rows(gt_ref[...], tm)[rows, :]
        o_ref[rows, :] = x1_ref[rows, :] + _rms(done_ref[rows, :], g_ref[...] * gt)

    @pl.when(s == 0)
    def _():
        acc_ref[...] = contribution(whole)

    @pl.when(jnp.logical_and(j == 0, jnp.logical_and(i > 0, i < n_tiles)))
    def _():
        for r in range(0, tm, FFN_SUB):
            rows = slice(r, r + FFN_SUB)
            acc_ref[rows, :] = contribution(rows)
            finalize(rows)

    @pl.when(jnp.logical_and(j > 0, j < n_chunks - 1))
    def _():
        acc_ref[...] += contribution(whole)

    @pl.when(jnp.logical_and(j == n_chunks - 1, i < n_tiles))
    def _():
        done_ref[...] = acc_ref[...] + contribution(whole)

    @pl.when(i == n_tiles)
    def _():
        finalize(whole)


def _ffn(hf, w1, w2, x1, mod, gt_spec, g_post, tm):
    n_tok = hf.shape[0]
    rounds = w1.dtype == F32
    tf = FFN_TF_ROUNDING if rounds else FFN_TF
    n_tiles, n_chunks = n_tok // tm, D_FF // tf
    last = n_tiles * n_chunks
    mm_tile = lambda s: jnp.minimum(s // n_chunks, n_tiles - 1)
    chunk = lambda s: jnp.where(s == last, n_chunks - 1, s % n_chunks)
    out_tile = lambda s: jnp.maximum(s // n_chunks - (s % n_chunks == 0).astype(jnp.int32), 0)
    w1_spec = pl.BlockSpec((D_MODEL, tf), lambda s: (0, chunk(s)))
    w2_spec = pl.BlockSpec((tf, D_MODEL), lambda s: (chunk(s), 0))
    y_shape = jax.ShapeDtypeStruct((n_tok, D_MODEL), F32)
    y_spec = pl.BlockSpec((tm, D_MODEL), lambda s: (out_tile(s), 0))
    return pl.pallas_call(
        functools.partial(_ffn_body, n_tiles=n_tiles, n_chunks=n_chunks),
        out_shape=((y_shape, jax.ShapeDtypeStruct(w1.shape, BF16),
                    jax.ShapeDtypeStruct(w2.shape, BF16)) if rounds else y_shape),
        grid=(last + 1,),
        in_specs=[pl.BlockSpec((tm, D_MODEL), lambda s: (mm_tile(s), 0)),
                  w1_spec, w2_spec,
                  pl.BlockSpec((tm, D_MODEL), lambda s: (out_tile(s), 0)),
                  gt_spec(out_tile),
                  _resident((1, D_MODEL), lambda s: (0, 0))],
        out_specs=(y_spec, w1_spec, w2_spec) if rounds else y_spec,
        scratch_shapes=[pltpu.VMEM((tm, D_MODEL), F32), pltpu.VMEM((tm, D_MODEL), F32)],
        compiler_params=_params(1, FFN_ROUNDING_VMEM_BYTES if rounds else VMEM_LIMIT_BYTES),
        name="ffn",
    )(hf, w1, w2, x1, mod, g_post)


MIX_TM = 512
MIX_SUB = 128
FFN_TM = 512


def _row(v):
    return v.reshape(1, -1)


def _layer(x_prompt, x_sample, cache_k, cache_v, c_prompt, c_sample, rel_table, w_ada, b_ada,
           g_pre_mix, w_in, sinks, v_gain, w_s, b_s, g_attn, g_gmlp, w_out, g_post_mix,
           g_pre_ff, w_ff1, w_ff2, g_post_ff):
    n_b, seq, _ = x_prompt.shape
    n_db, t_len, _ = x_sample.shape
    n_tok = n_b * seq
    D = D_MODEL

    g_pre_mix, g_attn, g_gmlp = _row(g_pre_mix), _row(g_attn), _row(g_gmlp)
    g_post_mix, g_pre_ff, g_post_ff = _row(g_post_mix), _row(g_pre_ff), _row(g_post_ff)
    v_gain = _row(v_gain)

    c_all = jnp.concatenate(
        [c_sample, c_prompt, jnp.zeros(((-(n_b + n_db)) % 16, D), F32)], axis=0)
    b_ada = b_ada.reshape(1, 6 * D)
    bias_p, bias_s, wtril, mod_s, mod_p = _cond(rel_table, w_s, t_len, c_all, n_db, n_b,
                                                w_ada, b_ada, 2 * D)
    SH_M, SC_M = 0, 1
    GT_M, SH_F, SC_F, GT_F = range(4)

    def pmod(chunk, tm):
        per = seq // tm
        return pl.BlockSpec((None, 1, D), lambda i: (i // per, 0, chunk))

    def smod(chunk):
        return pl.BlockSpec((n_db, D), lambda i: (0, chunk))

    sd = _sample_dist(t_len)
    valid_s = ((sd >= 0) & (sd < WINDOW) & (np.arange(SMP_KEYS)[None, :] < WINDOW + t_len))
    valid_s = valid_s.astype(np.float32)
    sink2 = sinks.reshape(N_KV_HEADS, GROUP)
    sink_p = jnp.repeat(sink2, BLOCK, axis=1)[:, None, :]
    sink_s = jnp.tile(sink2, (1, t_len)).reshape(-1, 1)
    ga3 = g_attn.reshape(N_KV_HEADS, GROUP, HEAD_DIM)
    gain_t = jnp.repeat(ga3.transpose(0, 2, 1).reshape(KV_COLS, GROUP), BLOCK, axis=1)
    g_attn = ga3.transpose(1, 0, 2).reshape(1, ATTN_WIDTH)

    xp = x_prompt.reshape(n_tok, D)
    tile = lambda w: pl.BlockSpec((MIX_TM, w), lambda i: (i, 0))
    ttile = lambda w: pl.BlockSpec((w, MIX_TM), lambda i: (0, i))
    n_mix = n_tok // MIX_TM
    assert seq >= WINDOW and MIX_TM >= WINDOW
    mix_per = seq // MIX_TM
    win = pl.BlockSpec((None, KV_COLS, WINDOW), lambda i: (i // mix_per, 0, 0))
    assert n_mix * BLOCK == D and GROUP * BLOCK == ATTN_WIDTH
    piece = lambda off: pl.BlockSpec(
        (HEAD_DIM, D), lambda i: (jnp.where(i < GROUP, i + off * GROUP, 2 * i + off), 0))
    slab_o = pl.BlockSpec((BLOCK, D), lambda i: (i, 0))
    ada_tn = 4 * D // n_mix
    ada_col = lambda i: (0, 2 * D // ada_tn + i)
    (qt_p, k_p, vt_p, gu_p, gvn_p, kwin, vwin, w_out_b, w_in_b,
     mod2_s, mod2_p) = _mix_in(
        xp, tile(D), pmod(SH_M, MIX_TM), pmod(SC_M, MIX_TM), mod_p, (n_tok // MIX_TM,),
        g_pre_mix, w_in, v_gain,
        (jax.ShapeDtypeStruct((Q_COLS, n_tok), BF16),
         jax.ShapeDtypeStruct((n_tok, KV_COLS), BF16),
         jax.ShapeDtypeStruct((KV_COLS, n_tok), F32),
         jax.ShapeDtypeStruct((n_tok, GMLP_WIDTH), F32),
         jax.ShapeDtypeStruct((n_tok, GMLP_WIDTH), BF16),
         jax.ShapeDtypeStruct((n_b, KV_COLS, WINDOW), F32),
         jax.ShapeDtypeStruct((n_b, KV_COLS, WINDOW), F32),
         jax.ShapeDtypeStruct((D, D), BF16),
         jax.ShapeDtypeStruct((D, IN_COLS), BF16),
         jax.ShapeDtypeStruct((n_db, 4 * D), F32),
         jax.ShapeDtypeStruct((n_b, 1, 4 * D), F32)),
        (ttile(Q_COLS), tile(KV_COLS), ttile(KV_COLS), tile(GMLP_WIDTH), tile(GMLP_WIDTH),
         win, win, slab_o, pl.BlockSpec(memory_space=pl.ANY),
         pl.BlockSpec((n_db, ada_tn), lambda i: (0, i)), pl.BlockSpec((n_b, 1, ada_tn), lambda i: (0, 0, i))),
        side=((w_out, w_out, c_all, w_ada, b_ada),
              (piece(0), piece(1), _resident(c_all.shape, lambda i: (0, 0)),
               pl.BlockSpec((D, ada_tn), ada_col), pl.BlockSpec((1, ada_tn), ada_col))))

    x1_p, hf_p = _mix_core(qt_p, k_p, vt_p, gu_p, gvn_p, bias_p, sink_p,
                           wtril, b_s.T, gain_t, g_gmlp, seq,
                           xp, mod2_p, pmod, w_out_b, g_post_mix, g_pre_ff)
    one = lambda i: (0, 0)

    xs = x_sample.reshape(n_db, t_len * D)
    lane = lambda w: pl.BlockSpec((n_db, w), lambda t: (0, t))
    q_s, k_s, v_s, gu_s, gvn_s = _mix_in(
        xs, lane(D), smod(SH_M), smod(SC_M), mod_s, (t_len,),
        g_pre_mix, w_in_b, v_gain,
        (jax.ShapeDtypeStruct((n_db, t_len * Q_COLS), F32),
         jax.ShapeDtypeStruct((n_db, t_len * KV_COLS), F32),
         jax.ShapeDtypeStruct((n_db, t_len * KV_COLS), F32),
         jax.ShapeDtypeStruct((n_db, t_len * GMLP_WIDTH), F32),
         jax.ShapeDtypeStruct((n_db, t_len * GMLP_WIDTH), F32)),
        (lane(Q_COLS), lane(KV_COLS), lane(KV_COLS), lane(GMLP_WIDTH), lane(GMLP_WIDTH)))

    q5 = q_s.reshape(n_db, t_len, N_KV_HEADS, GROUP, HEAD_DIM).transpose(0, 2, 1, 3, 4)
    q5 = q5.reshape(n_db, N_KV_HEADS, t_len * GROUP, HEAD_DIM).astype(BF16)
    z5 = jnp.zeros_like(q5[:, 0])
    q6 = jnp.concatenate([jnp.concatenate([q5[:, 0], z5], axis=-1),
                          jnp.concatenate([z5, q5[:, 1]], axis=-1)], axis=1)
    new_k = k_s.reshape(n_db, t_len, KV_COLS)
    new_v = v_s.reshape(n_db, t_len, KV_COLS)
    cache_kt = cache_k.transpose(0, 2, 3, 1).reshape(n_db, KV_COLS, WINDOW)
    cache_vt = cache_v.transpose(0, 2, 3, 1).reshape(n_db, KV_COLS, WINDOW)
    attn_s = _attn_smp(q6, cache_kt, cache_vt, new_k, new_v, bias_s, jnp.asarray(valid_s), sink_s,
                       t_len)
    attn_s = attn_s.reshape(n_db, t_len * ATTN_WIDTH)
    smem = pl.BlockSpec(memory_space=pltpu.SMEM)
    w4 = w_s[:, :t_len, :t_len].reshape(-1)
    b4 = b_s[:, :t_len].reshape(-1)
    tmaj = lambda: pl.BlockSpec((None, n_db, D), lambda t: (t, 0, 0))
    x1_s, hf_s = pl.pallas_call(
        functools.partial(_out_proj_smp_body, t_len=t_len),
        out_shape=(jax.ShapeDtypeStruct((t_len, n_db, D), F32),
                   jax.ShapeDtypeStruct((t_len, n_db, D), BF16)),
        grid=(t_len,),
        in_specs=[smem, smem, lane(ATTN_WIDTH), lane(GMLP_WIDTH),
                  _resident((n_db, t_len * GMLP_WIDTH), one),
                  _resident((1, ATTN_WIDTH), one), _resident((1, GMLP_WIDTH), one),
                  lane(D), smod(GT_M), smod(SH_F), smod(SC_F),
                  _resident((D, D), one), _resident((1, D), one), _resident((1, D), one)],
        out_specs=(tmaj(), tmaj()),
        compiler_params=_params(1),
        name="out_proj_smp",
    )(w4, b4, attn_s, gu_s, gvn_s, g_attn, g_gmlp, xs, mod2_s, mod2_s, mod2_s,
      w_out_b, g_post_mix, g_pre_ff)

    n_st = t_len * n_db
    y_s, w1_b, w2_b = _ffn(hf_s.reshape(n_st, D), w_ff1, w_ff2, x1_s.reshape(n_st, D), mod2_s,
                           lambda tile: smod(GT_F), g_post_ff, n_st)
    y_s = y_s.reshape(t_len, n_db, D).transpose(1, 0, 2)

    ffn_per = seq // FFN_TM
    gate_p = lambda tile: pl.BlockSpec((None, 1, D), lambda s: (tile(s) // ffn_per, 0, GT_F))
    y_p = _ffn(hf_p, w1_b, w2_b, x1_p, mod2_p, gate_p, g_post_ff, FFN_TM)

    last = lambda t: t.reshape(n_b, N_KV_HEADS, HEAD_DIM, WINDOW).transpose(0, 3, 1, 2)
    k_p4, v_p4 = last(kwin), last(vwin)
    return (y_p.reshape(n_b, seq, D), y_s, k_p4, v_p4,
            k_s.reshape(n_db, t_len, N_KV_HEADS, HEAD_DIM),
            v_s.reshape(n_db, t_len, N_KV_HEADS, HEAD_DIM),
            gvn_s.reshape(n_db, t_len, GMLP_HEADS, GMLP_HEAD_DIM))


def kernel(x_prompt, x_sample, cache_k, cache_v, c_prompt, c_sample, rel_bias_table, w_ada, b_ada,
           g_pre_mix, w_in, attn_sinks, gmlp_v_gain, gmlp_w_s, gmlp_b_s, g_attn_out, g_gmlp_out,
           w_out, g_post_mix, g_pre_ff, w_ff1, w_ff2, g_post_ff):
    depth = w_in.shape[0]
    assert depth == 1, "single-layer step"
    only = lambda a: a.reshape(a.shape[1:])
    outs = _layer(x_prompt, x_sample, only(cache_k), only(cache_v), c_prompt, c_sample,
                  rel_bias_table, *map(only, (w_ada, b_ada, g_pre_mix, w_in, attn_sinks, gmlp_v_gain,
                                              gmlp_w_s, gmlp_b_s, g_attn_out, g_gmlp_out, w_out,
                                              g_post_mix, g_pre_ff, w_ff1, w_ff2, g_post_ff)))
    y_p, y_s, k_p, v_p, k_s, v_s, gv_s = outs
    return (y_p, y_s, k_p[None], v_p[None], k_s[None], v_s[None], gv_s[None])
```

```python
import functools
import math

import numpy as np
import jax
import jax.numpy as jnp
from jax import lax
from jax.experimental import pallas as pl
from jax.experimental.pallas import tpu as pltpu

D_MODEL = 2048
HEAD_DIM = 64
ATTN_WIDTH = 1024
N_HEADS = 16
N_KV_HEADS = 2
GROUP = 8
WINDOW = 128
BLOCK = 128
GMLP_WIDTH = 1024
GMLP_HEADS = 8
GMLP_HEAD_DIM = 128
D_FF = 4 * D_MODEL
NUM_BUCKETS = 32
MAX_DISTANCE = 128
EPS = 1e-6
KV_COLS = N_KV_HEADS * HEAD_DIM
Q_COLS = N_HEADS * HEAD_DIM
IN_COLS = Q_COLS + 2 * KV_COLS + 2 * GMLP_WIDTH
ATTN_SCALE = HEAD_DIM ** -0.5
NEG_INF = -1e30

MIB = 1024 * 1024
VMEM_LIMIT_BYTES = 56 * MIB
MIX_IN_VMEM_BYTES = 62 * MIB
MIX_CORE_VMEM_BYTES = 58 * MIB

BF16 = jnp.bfloat16
F32 = jnp.float32


def _params(n_axes, vmem=VMEM_LIMIT_BYTES):
    return pltpu.CompilerParams(
        dimension_semantics=("arbitrary",) * n_axes, vmem_limit_bytes=vmem)


def _resident(shape, index_map):
    return pl.BlockSpec(shape, index_map, pipeline_mode=pl.Buffered(1))


def _rms(x, gain):
    return x * lax.rsqrt(jnp.mean(x * x, axis=-1, keepdims=True) + EPS) * gain


def _rows(m, n_rows):
    r = m.shape[0]
    if r == 1 or r == n_rows:
        return m
    return jnp.concatenate([m] * (n_rows // r), axis=0)


def _dot(a, b):
    return jnp.dot(a, b, preferred_element_type=F32)


def _dot_nt(a, b):
    return lax.dot_general(a, b, (((1,), (1,)), ((), ())), preferred_element_type=F32)


def _t5_bucket(dist):
    n = np.maximum(dist, 0)
    max_exact = NUM_BUCKETS // 2
    nf = np.maximum(n, 1).astype(np.float64)
    large = max_exact + (np.log(nf / max_exact) / math.log(MAX_DISTANCE / max_exact)
                         * (NUM_BUCKETS - max_exact)).astype(np.int32)
    large = np.minimum(large, NUM_BUCKETS - 1)
    return jnp.asarray(np.where(n < max_exact, n, large), jnp.int32)


SMP_KEYS = 2 * WINDOW


def _prompt_dist():
    r = np.arange(BLOCK)[:, None]
    i = np.arange(BLOCK)[None, :]
    return (i - r) % BLOCK


def _sample_rows(t_len):
    r = np.arange(N_KV_HEADS * t_len * GROUP)
    return r // (t_len * GROUP), (r // GROUP) % t_len, r % GROUP


def _sample_dist(t_len):
    _, t, _ = _sample_rows(t_len)
    return WINDOW + t[:, None] - np.arange(SMP_KEYS)[None, :]


def _prep_body(tab_ref, tabr_ref, bkt_p_ref, bkt_s_ref, ws_ref,
               bias_p_ref, bias_s_ref, wtril_ref):
    bkt_p = bkt_p_ref[...]
    for kh in range(N_KV_HEADS):
        for g in range(GROUP):
            h = kh * GROUP + g
            acc = jnp.zeros((BLOCK, BLOCK), F32)
            for b in range(NUM_BUCKETS):
                acc = jnp.where(bkt_p == b, tab_ref[b, h], acc)
            bias_p_ref[kh, :, g * BLOCK:(g + 1) * BLOCK] = acc
    bkt_s = bkt_s_ref[...]
    tabr = tabr_ref[...]
    acc = jnp.zeros(bkt_s.shape, F32)
    for b in range(NUM_BUCKETS):
        acc = jnp.where(bkt_s == b, tabr[:, b:b + 1], acc)
    bias_s_ref[...] = acc
    ii = lax.broadcasted_iota(jnp.int32, (BLOCK, BLOCK), 0)
    jj = lax.broadcasted_iota(jnp.int32, (BLOCK, BLOCK), 1)
    for h in range(GMLP_HEADS):
        wtril_ref[h] = jnp.where(jj <= ii, ws_ref[h], 0.0).astype(BF16)


ADA_TN = 1024


def _adaln_body(c_ref, w_ref, b_ref, os_ref, op_ref):
    n_s, n_p = os_ref.shape[0], op_ref.shape[0]
    c = c_ref[...]
    s = (c * jax.nn.sigmoid(c)).astype(BF16)
    r = _dot(s, w_ref[...].astype(BF16)) + b_ref[...]
    os_ref[...] = r[:n_s]
    for p in range(n_p):
        op_ref[p] = r[n_s + p:n_s + p + 1]


def _cond_body(tab_ref, tabr_ref, bkt_p_ref, bkt_s_ref, ws_ref, c_ref, w_ref, b_ref,
               bias_p_ref, bias_s_ref, wtril_ref, os_ref, op_ref):
    @pl.when(pl.program_id(0) == 0)
    def _():
        _prep_body(tab_ref, tabr_ref, bkt_p_ref, bkt_s_ref, ws_ref, bias_p_ref, bias_s_ref, wtril_ref)

    _adaln_body(c_ref, w_ref, b_ref, os_ref, op_ref)


def _cond(rel_table, w_s, t_len, c_all, n_s, n_p, w_ada, b_ada, n_out):
    bkt_p = _t5_bucket(_prompt_dist())
    bkt_s = _t5_bucket(_sample_dist(t_len))
    n_rows = bkt_s.shape[0]
    tab_rows = jnp.broadcast_to(rel_table.T.reshape(N_KV_HEADS, 1, GROUP, NUM_BUCKETS),
                                (N_KV_HEADS, t_len, GROUP, NUM_BUCKETS)).reshape(n_rows, NUM_BUCKETS)
    n_all = c_all.shape[0]
    whole = lambda a: _resident(a.shape, lambda j: (0,) * a.ndim)
    outs = (jax.ShapeDtypeStruct((N_KV_HEADS, BLOCK, GROUP * BLOCK), F32),
            jax.ShapeDtypeStruct((n_rows, SMP_KEYS), F32),
            jax.ShapeDtypeStruct((GMLP_HEADS, BLOCK, BLOCK), BF16))
    return pl.pallas_call(
        _cond_body,
        out_shape=outs + (jax.ShapeDtypeStruct((n_s, n_out), F32),
                          jax.ShapeDtypeStruct((n_p, 1, n_out), F32)),
        grid=(n_out // ADA_TN,),
        in_specs=[pl.BlockSpec(memory_space=pltpu.SMEM),
                  whole(tab_rows), whole(bkt_p), whole(bkt_s), whole(w_s),
                  _resident((n_all, D_MODEL), lambda j: (0, 0)),
                  pl.BlockSpec((D_MODEL, ADA_TN), lambda j: (0, j)),
                  pl.BlockSpec((1, ADA_TN), lambda j: (0, j))],
        out_specs=tuple(pl.BlockSpec(o.shape, lambda j, nd=len(o.shape): (0,) * nd) for o in outs)
        + (pl.BlockSpec((n_s, ADA_TN), lambda j: (0, j)),
           pl.BlockSpec((n_p, 1, ADA_TN), lambda j: (0, 0, j))),
        compiler_params=_params(1, 40 * MIB),
        name="cond",
    )(rel_table, tab_rows, bkt_p, bkt_s, w_s, c_all, w_ada, b_ada)


W_STAGE_COLS = 256


def _mix_in_first_tile(x_ref, sh_ref, sc_ref, g_ref, w_hbm, vg_ref, q_ref, k_ref, v_ref, gu_ref,
                       gv_ref, kwin_ref, vwin_ref, wb_ref, stage_ref, sem_ref):
    n = IN_COLS // W_STAGE_COLS
    assert W_STAGE_COLS == 2 * KV_COLS and Q_COLS % W_STAGE_COLS == 0
    c2, c3 = Q_COLS + 2 * KV_COLS, Q_COLS + 2 * KV_COLS + GMLP_WIDTH

    def fetch(c):
        return pltpu.make_async_copy(w_hbm.at[:, pl.ds(c * W_STAGE_COLS, W_STAGE_COLS)],
                                     stage_ref.at[c % 2], sem_ref.at[c % 2])

    tm = x_ref.shape[0]
    subs = [slice(r, r + MIX_SUB) for r in range(0, tm, MIX_SUB)]

    def project(lo):
        cols = slice(lo, lo + W_STAGE_COLS)
        for rows, h in zip(subs, hs):
            o = _dot(h, wb_ref[:, cols])
            if lo < Q_COLS:
                q_ref[cols, rows] = (o * ATTN_SCALE).T.astype(q_ref.dtype)
            elif lo < c2:
                k_ref[rows, :] = o[:, :KV_COLS].astype(k_ref.dtype)
                vt = o[:, KV_COLS:].T
                v_ref[:, rows] = vt
                if rows.stop == tm:
                    kwin_ref[...] = o[-WINDOW:, :KV_COLS].T
                    vwin_ref[...] = vt[:, -WINDOW:]
            elif lo < c3:
                gu_ref[rows, lo - c2:lo - c2 + W_STAGE_COLS] = jax.nn.gelu(o).astype(gu_ref.dtype)
            else:
                gv = jax.nn.gelu(o)
                for hd in range(W_STAGE_COLS // GMLP_HEAD_DIM):
                    sl = slice(lo - c3 + hd * GMLP_HEAD_DIM, lo - c3 + (hd + 1) * GMLP_HEAD_DIM)
                    part = gv[:, hd * GMLP_HEAD_DIM:(hd + 1) * GMLP_HEAD_DIM]
                    gv_ref[rows, sl] = _rms(part, vg[:, sl]).astype(gv_ref.dtype)

    fetch(0).start()
    hs = [(_rms(x_ref[rows, :], g_ref[...] * (1.0 + sc_ref[...])) + sh_ref[...]).astype(BF16)
          for rows in subs]
    vg = vg_ref[...]
    for c in range(n):
        if c + 1 < n:
            fetch(c + 1).start()
        fetch(c).wait()
        lo = c * W_STAGE_COLS
        wb_ref[:, lo:lo + W_STAGE_COLS] = stage_ref[c % 2].astype(BF16)
        if c > 0:
            project(lo - W_STAGE_COLS)
    project((n - 1) * W_STAGE_COLS)


def _mix_in_body(x_ref, sh_ref, sc_ref, g_ref, w_ref, vg_ref, *refs, transposed):
    if transposed:
        (wc_ref, woa_ref, wob_ref, ca_ref, wa_ref, ba_ref,
         q_ref, k_ref, v_ref, gu_ref, gv_ref, kwin_ref, vwin_ref,
         wcb_ref, wo_ref, wib_hbm, ms_ref, mp_ref, wb_ref, stage_ref, sem_ref, wsem_ref) = refs
        wcb_ref[...] = wc_ref[...].astype(wcb_ref.dtype)
        wo_ref[:HEAD_DIM, :] = woa_ref[...].astype(wo_ref.dtype)
        wo_ref[HEAD_DIM:, :] = wob_ref[...].astype(wo_ref.dtype)
        _adaln_body(ca_ref, wa_ref, ba_ref, ms_ref, mp_ref)
    else:
        q_ref, k_ref, v_ref, gu_ref, gv_ref = refs
    tm = x_ref.shape[0]
    sub = min(tm, MIX_SUB)
    c0, c1, c2, c3 = Q_COLS, Q_COLS + KV_COLS, Q_COLS + 2 * KV_COLS, Q_COLS + 2 * KV_COLS + GMLP_WIDTH
    vg = vg_ref[...]

    def tile(w_ref):
        for r in range(0, tm, sub):
            rows = slice(r, r + sub)
            mod = lambda ref: ref[...] if ref.shape[0] == 1 else ref[rows, :]
            h = (_rms(x_ref[rows, :], g_ref[...] * (1.0 + mod(sc_ref))) + mod(sh_ref)).astype(BF16)
            q = _dot(h, w_ref[:, :c0]) * ATTN_SCALE
            kv = _dot(h, w_ref[:, c0:c2])
            k_ref[rows, :] = kv[:, :KV_COLS].astype(k_ref.dtype)
            if transposed:
                q_ref[:, rows] = q.T.astype(q_ref.dtype)
                vt = kv[:, KV_COLS:].T
                v_ref[:, rows] = vt
                if r + sub == tm:
                    kwin_ref[...] = kv[-WINDOW:, :KV_COLS].T
                    vwin_ref[...] = vt[:, -WINDOW:]
            else:
                q_ref[rows, :] = q.astype(q_ref.dtype)
                v_ref[rows, :] = kv[:, KV_COLS:]
            gu_ref[rows, :] = jax.nn.gelu(_dot(h, w_ref[:, c2:c3])).astype(gu_ref.dtype)
            gv = jax.nn.gelu(_dot(h, w_ref[:, c3:]))
            for hd in range(GMLP_HEADS):
                sl = slice(hd * GMLP_HEAD_DIM, (hd + 1) * GMLP_HEAD_DIM)
                gv_ref[rows, sl] = _rms(gv[:, sl], vg[:, sl]).astype(gv_ref.dtype)

    if not transposed:
        tile(w_ref)
        return
    write_back = pltpu.make_async_copy(wb_ref, wib_hbm, wsem_ref)

    @pl.when(pl.program_id(0) == 0)
    def _():
        _mix_in_first_tile(x_ref, sh_ref, sc_ref, g_ref, w_ref, vg_ref, q_ref, k_ref, v_ref,
                           gu_ref, gv_ref, kwin_ref, vwin_ref, wb_ref, stage_ref, sem_ref)
        write_back.start()

    @pl.when(pl.program_id(0) > 0)
    def _():
        tile(wb_ref)

    @pl.when(pl.program_id(0) == pl.num_programs(0) - 1)
    def _():
        write_back.wait()


def _mix_in(x, x_spec, sh_spec, sc_spec, mod, grid, g_pre, w_in, v_gain, outs, out_specs,
            side=((), ())):
    extra, extra_specs = side
    prompt = len(extra) > 0
    any_space = pl.BlockSpec(memory_space=pl.ANY)
    scratch = [pltpu.VMEM((D_MODEL, IN_COLS), BF16),
               pltpu.VMEM((2, D_MODEL, W_STAGE_COLS), F32),
               pltpu.SemaphoreType.DMA((2,)), pltpu.SemaphoreType.DMA(())] if prompt else []
    return pl.pallas_call(
        functools.partial(_mix_in_body, transposed=prompt),
        out_shape=outs,
        grid=grid,
        in_specs=[x_spec, sh_spec, sc_spec,
                  _resident((1, D_MODEL), lambda i: (0, 0)),
                  any_space if prompt else _resident((D_MODEL, IN_COLS), lambda i: (0, 0)),
                  _resident((1, GMLP_WIDTH), lambda i: (0, 0)), *extra_specs],
        out_specs=out_specs,
        scratch_shapes=scratch,
        compiler_params=_params(1, MIX_IN_VMEM_BYTES if prompt else VMEM_LIMIT_BYTES),
        name="mix_in",
    )(x, mod, mod, g_pre, w_in, v_gain, *extra)


PROJ_TN = 256


def _out_proj_epilogue(o, rows, x_ref, gt_ref, sh_ref, sc_ref, gpm_ref, gpf_ref, x1_ref, hf_ref):
    mod = lambda ref: ref[...] if ref.shape[0] == 1 else ref[rows, :]
    x1 = x_ref[rows, :] + _rms(o, gpm_ref[...] * mod(gt_ref))
    x1_ref[rows, :] = x1
    hf = _rms(x1, gpf_ref[...] * (1.0 + mod(sc_ref))) + mod(sh_ref)
    hf_ref[rows, :] = hf.astype(hf_ref.dtype)


CORE_NB = 4


def _mix_core_body(qt_ref, kc_ref, kp_ref, vtc_ref, vtp_ref, gu_ref, gv_ref,
                   bias_ref, pm_ref, sink_ref, wtril_ref, bs_ref, gat_ref, gg_ref, wc_ref,
                   x_ref, gt_ref, sh_ref, sc_ref, wb_ref, gpm_ref, gpf_ref,
                   x1_ref, hf_ref, wcb_ref, *, steps_per_seq):
    wcb_ref[...] = wc_ref[...].astype(wcb_ref.dtype)
    first = pl.program_id(0) % steps_per_seq == 0
    tri = (lax.broadcasted_iota(jnp.int32, (BLOCK, BLOCK), 0)
           > lax.broadcasted_iota(jnp.int32, (BLOCK, BLOCK), 1))
    from_prev = jnp.concatenate([tri] * GROUP, axis=1)
    zeros = jnp.zeros((HEAD_DIM, GROUP * BLOCK), BF16)
    bs = bs_ref[...]
    def mix_stages(u, st):
        tok = slice(u * BLOCK, (u + 1) * BLOCK)

        def logits(kh):
            if kh == 0:
                k_prev = kp_ref[...] if u == 0 else kc_ref[(u - 1) * BLOCK:u * BLOCK, :]
                vt_prev = vtp_ref[...] if u == 0 else vtc_ref[:, (u - 1) * BLOCK:u * BLOCK]
                st['k'] = jnp.concatenate([k_prev, kc_ref[tok, :]], axis=0)
                st['vt'] = jnp.concatenate([vt_prev, vtc_ref[:, tok]], axis=1).astype(BF16)
                st['outs'] = []
            base = kh * GROUP * HEAD_DIM
            qs = jnp.concatenate(
                [qt_ref[base + g * HEAD_DIM: base + (g + 1) * HEAD_DIM, tok] for g in range(GROUP)],
                axis=1)
            qp = jnp.concatenate([qs, zeros] if kh == 0 else [zeros, qs], axis=0)
            st['qk'] = _dot(st['k'], qp)

        def softmax(kh):
            qk = st['qk']
            s = jnp.where(from_prev, qk[:BLOCK], qk[BLOCK:]) + bias_ref[kh]
            if u == 0:
                s = jnp.where(jnp.logical_and(first, from_prev), NEG_INF, s)
            sink = sink_ref[kh]
            m =jnp.maximum(jnp.max(s, axis=0, keepdims=True), sink)
            p = jnp.exp(s - m)
            st['den'] = jnp.sum(p, axis=0, keepdims=True) + jnp.exp(sink - m)
            p = p.astype(BF16)
            st['p2'] = jnp.concatenate([p * pm_ref[0], p * pm_ref[1]], axis=0)

        def values(kh):
            vt = st['vt'][kh * HEAD_DIM:(kh + 1) * HEAD_DIM, :]
            st['outs'].append(_dot(vt, st['p2']) / st['den'])

        def attn_norm():
            ot = jnp.concatenate(st['outs'], axis=0)
            sq = jnp.sum(ot * ot, axis=0, keepdims=True)
            tot = sq[:, :BLOCK]
            for g in range(1, GROUP):
                tot = tot + sq[:, g * BLOCK:(g + 1) * BLOCK]
            r = lax.rsqrt(tot / ATTN_WIDTH + EPS)
            a = ot * jnp.concatenate([r] * GROUP, axis=1) * gat_ref[...]
            st['merged'] = [a[:, g * BLOCK:(g + 1) * BLOCK].T.astype(BF16) for g in range(GROUP)]

        def gating():
            gated = []
            for hd in range(GMLP_HEADS):
                sl = slice(hd * GMLP_HEAD_DIM, (hd + 1) * GMLP_HEAD_DIM)
                mixed = _dot(wtril_ref[hd], gv_ref[tok, sl]) + bs[:, hd:hd + 1]
                gated.append(gu_ref[tok, sl] * mixed)
            gm = jnp.concatenate(gated, axis=-1)
            st['merged'] = jnp.concatenate(
                st['merged'] + [_rms(gm, gg_ref[...]).astype(BF16)], axis=1)

        P = functools.partial
        return [P(logits, 0), P(softmax, 0), P(values, 0), P(logits, 1), P(softmax, 1),
                P(values, 1), attn_norm, gating]

    def proj_stages(u, st):
        tok = slice(u * BLOCK, (u + 1) * BLOCK)
        chunks = []

        def chunk(c):
            chunks.append(_dot(st['merged'], wb_ref[:, c * PROJ_TN:(c + 1) * PROJ_TN]))

        def finish():
            _out_proj_epilogue(jnp.concatenate(chunks, axis=1), tok, x_ref, gt_ref, sh_ref,
                               sc_ref, gpm_ref, gpf_ref, x1_ref, hf_ref)

        return [functools.partial(chunk, c) for c in range(D_MODEL // PROJ_TN)] + [finish]

    states = [dict() for _ in range(CORE_NB)]
    for u in range(CORE_NB + 1):
        a = mix_stages(u, states[u]) if u < CORE_NB else []
        b = proj_stages(u - 1, states[u - 1]) if u > 0 else []
        for i in range(max(len(a), len(b))):
            for stage in (b[i:i + 1] + a[i:i + 1]):
                stage()


def _mix_core(qt, k, vt, gu, gvn, bias_t, sink_row, wtril, bs_t, gain_t, g_gmlp, seq, w_side,
              x, mod, mod_spec, w_out_b, g_post_mix, g_pre_ff):
    r_gt_i = np.tile(np.arange(BLOCK)[:, None] > np.arange(BLOCK)[None, :], (1, GROUP))
    pmask = jnp.asarray(np.stack([r_gt_i, ~r_gt_i]), BF16)
    n_tok = k.shape[0]
    tm = CORE_NB * BLOCK
    n_steps = n_tok // tm
    side_rows = w_side.shape[0] // n_steps
    GT_M, SH_F, SC_F = 0, 1, 2
    cur = lambda i: (i, 0)
    prev = lambda i: (jnp.maximum(i * CORE_NB - 1, 0), 0)
    cur_t = lambda i: (0, i)
    prev_t = lambda i: (0, jnp.maximum(i * CORE_NB - 1, 0))
    full2 = lambda i: (0, 0)
    full3 = lambda i: (0, 0, 0)
    return pl.pallas_call(
        functools.partial(_mix_core_body, steps_per_seq=seq // tm),
        out_shape=(jax.ShapeDtypeStruct((n_tok, D_MODEL), F32),
                   jax.ShapeDtypeStruct((n_tok, D_MODEL), BF16),
                   jax.ShapeDtypeStruct(w_side.shape, BF16)),
        grid=(n_steps,),
        in_specs=[pl.BlockSpec((Q_COLS, tm), cur_t),
                  pl.BlockSpec((tm, KV_COLS), cur), pl.BlockSpec((BLOCK, KV_COLS), prev),
                  pl.BlockSpec((KV_COLS, tm), cur_t), pl.BlockSpec((KV_COLS, BLOCK), prev_t),
                  pl.BlockSpec((tm, GMLP_WIDTH), cur), pl.BlockSpec((tm, GMLP_WIDTH), cur),
                  _resident(bias_t.shape, full3), _resident(pmask.shape, full3),
                  _resident(sink_row.shape, full3), _resident(wtril.shape, full3),
                  _resident(bs_t.shape, full2),
                  _resident(gain_t.shape, full2), _resident((1, GMLP_WIDTH), full2),
                  pl.BlockSpec((side_rows, w_side.shape[1]), cur),
                  pl.BlockSpec((tm, D_MODEL), cur),
                  mod_spec(GT_M, tm), mod_spec(SH_F, tm), mod_spec(SC_F, tm),
                  _resident((D_MODEL, D_MODEL), full2),
                  _resident((1, D_MODEL), full2), _resident((1, D_MODEL), full2)],
        out_specs=(pl.BlockSpec((tm, D_MODEL), cur), pl.BlockSpec((tm, D_MODEL), cur),
                   pl.BlockSpec((side_rows, w_side.shape[1]), cur)),
        compiler_params=_params(1, MIX_CORE_VMEM_BYTES),
        name="mix_core",
    )(qt, k, k, vt, vt, gu, gvn, bias_t, pmask, sink_row, wtril, bs_t, gain_t, g_gmlp, w_side,
      x, mod, mod, mod, w_out_b, g_post_mix, g_pre_ff)


SMP_BB = 32


def _attn_smp_body(q_ref, ckt_ref, cvt_ref, nk_ref, nv_ref, bias_ref, valid_ref, sink_ref,
                   o_ref, *, t_len):
    zpad = jnp.zeros((WINDOW - t_len, KV_COLS), F32)
    tiles = []
    for b in range(SMP_BB):
        q = q_ref[b]
        s_cache = _dot(q, ckt_ref[b].astype(BF16))
        kn = jnp.concatenate([nk_ref[b], zpad], axis=0).astype(BF16)
        tiles.append(jnp.concatenate([s_cache, _dot_nt(q, kn)], axis=1)[None])
    s = jnp.concatenate(tiles, axis=0)
    s = jnp.where(valid_ref[...] > 0.5, s + bias_ref[...], NEG_INF)
    sink = sink_ref[...]
    m = jnp.maximum(jnp.max(s, axis=-1, keepdims=True), sink)
    p = jnp.exp(s - m)
    den = jnp.sum(p, axis=-1, keepdims=True) + jnp.exp(sink - m)
    p = p.astype(BF16)
    half = t_len * GROUP
    own_head = lax.broadcasted_iota(jnp.int32, (half, KV_COLS), 1) < HEAD_DIM
    for b in range(SMP_BB):
        vn = jnp.concatenate([nv_ref[b], zpad], axis=0).astype(BF16)
        o = _dot_nt(p[b, :, :WINDOW], cvt_ref[b].astype(BF16)) + _dot(p[b, :, WINDOW:], vn)
        o = o / den[b]
        o_ref[b] = jnp.where(own_head, o[:half], o[half:])


def _attn_smp(q6, cache_kt, cache_vt, new_k, new_v, bias_s, valid_s, sink_s, t_len):
    n_b, rows, _ = q6.shape
    b3 = lambda i: (i, 0, 0)
    one = lambda i: (0, 0)
    return pl.pallas_call(
        functools.partial(_attn_smp_body, t_len=t_len),
        out_shape=jax.ShapeDtypeStruct((n_b, rows // N_KV_HEADS, KV_COLS), F32),
        grid=(n_b // SMP_BB,),
        in_specs=[pl.BlockSpec((SMP_BB, rows, KV_COLS), b3),
                  pl.BlockSpec((SMP_BB, KV_COLS, WINDOW), b3),
                  pl.BlockSpec((SMP_BB, KV_COLS, WINDOW), b3),
                  pl.BlockSpec((SMP_BB, t_len, KV_COLS), b3),
                  pl.BlockSpec((SMP_BB, t_len, KV_COLS), b3),
                  _resident(bias_s.shape, one), _resident(valid_s.shape, one),
                  _resident(sink_s.shape, one)],
        out_specs=pl.BlockSpec((SMP_BB, rows // N_KV_HEADS, KV_COLS), b3),
        compiler_params=_params(1, 32 * MIB),
        name="attn_smp",
    )(q6, cache_kt, cache_vt, new_k, new_v, bias_s, valid_s, sink_s)


def _out_proj_smp_body(w4_ref, b4_ref, a_ref, gu_ref, gv_ref, ga_ref, gg_ref,
                       x_ref, gt_ref, sh_ref, sc_ref, wb_ref, gpm_ref, gpf_ref, x1_ref, hf_ref, *,
                       t_len):
    i = pl.program_id(0)
    gated = []
    for hd in range(GMLP_HEADS):
        acc = jnp.zeros((a_ref.shape[0], GMLP_HEAD_DIM), F32)
        for j in range(t_len):
            w = jnp.where(j <= i, w4_ref[(hd * t_len + i) * t_len + j], 0.0)
            lo = j * GMLP_WIDTH + hd * GMLP_HEAD_DIM
            acc = acc + w * gv_ref[:, lo:lo + GMLP_HEAD_DIM]
        mixed = acc + b4_ref[hd * t_len + i]
        gated.append(gu_ref[:, hd * GMLP_HEAD_DIM:(hd + 1) * GMLP_HEAD_DIM] * mixed)
    gm = jnp.concatenate(gated, axis=-1)
    merged = jnp.concatenate([_rms(a_ref[...], ga_ref[...]), _rms(gm, gg_ref[...])], axis=-1)
    _out_proj_epilogue(_dot(merged.astype(BF16), wb_ref[...]), slice(0, a_ref.shape[0]),
                       x_ref, gt_ref, sh_ref, sc_ref, gpm_ref, gpf_ref, x1_ref, hf_ref)


FFN_TF = 1024
FFN_SUB = 256


def _ffn_body(h_ref, w1_ref, w2_ref, x1_ref, gt_ref, g_ref, o_ref, acc_ref, done_ref, *,
              n_tiles, n_chunks):
    s = pl.program_id(0)
    i = s // n_chunks
    j = s % n_chunks
    tm = o_ref.shape[0]
    whole = slice(0, tm)

    def contribution(rows):
        a = jnp.maximum(_dot(h_ref[rows, :], w1_ref[...]), 0.0)
        return _dot((a * a).astype(BF16), w2_ref[...])

    def finalize(rows):
        gt = gt_ref[...] if gt_ref.shape[0] == 1 else _rows(gt_ref[...], tm)[rows, :]
        o_ref[rows, :] = x1_ref[rows, :] + _rms(done_ref[rows, :], g_ref[...] * gt)

    @pl.when(s == 0)
    def _():
        acc_ref[...] = contribution(whole)

    @pl.when(jnp.logical_and(j == 0, jnp.logical_and(i > 0, i < n_tiles)))
    def _():
        for r in range(0, tm, FFN_SUB):
            rows = slice(r, r + FFN_SUB)
            acc_ref[rows, :] = contribution(rows)
            finalize(rows)

    @pl.when(jnp.logical_and(j > 0, j < n_chunks - 1))
    def _():
        acc_ref[...] += contribution(whole)

    @pl.when(jnp.logical_and(j == n_chunks - 1, i < n_tiles))
    def _():
        done_ref[...] = acc_ref[...] + contribution(whole)

    @pl.when(i == n_tiles)
    def _():
        finalize(whole)


def _ffn(hf, w1, w2, x1, mod, gt_spec, g_post, tm):
    n_tok = hf.shape[0]
    n_tiles, n_chunks = n_tok // tm, D_FF // FFN_TF
    last = n_tiles * n_chunks
    mm_tile = lambda s: jnp.minimum(s // n_chunks, n_tiles - 1)
    chunk = lambda s: jnp.where(s == last, n_chunks - 1, s % n_chunks)
    out_tile = lambda s: jnp.maximum(s // n_chunks - (s % n_chunks == 0).astype(jnp.int32), 0)
    return pl.pallas_call(
        functools.partial(_ffn_body, n_tiles=n_tiles, n_chunks=n_chunks),
        out_shape=jax.ShapeDtypeStruct((n_tok, D_MODEL), F32),
        grid=(last + 1,),
        in_specs=[pl.BlockSpec((tm, D_MODEL), lambda s: (mm_tile(s), 0)),
                  pl.BlockSpec((D_MODEL, FFN_TF), lambda s: (0, chunk(s))),
                  pl.BlockSpec((FFN_TF, D_MODEL), lambda s: (chunk(s), 0)),
                  pl.BlockSpec((tm, D_MODEL), lambda s: (out_tile(s), 0)),
                  gt_spec(out_tile),
                  _resident((1, D_MODEL), lambda s: (0, 0))],
        out_specs=pl.BlockSpec((tm, D_MODEL), lambda s: (out_tile(s), 0)),
        scratch_shapes=[pltpu.VMEM((tm, D_MODEL), F32), pltpu.VMEM((tm, D_MODEL), F32)],
        compiler_params=_params(1),
        name="ffn",
    )(hf, w1, w2, x1, mod, g_post)


MIX_TM = 512
MIX_SUB = 128
FFN_TM = 512


def _row(v):
    return v.reshape(1, -1)


def _layer(x_prompt, x_sample, cache_k, cache_v, c_prompt, c_sample, rel_table, w_ada, b_ada,
           g_pre_mix, w_in, sinks, v_gain, w_s, b_s, g_attn, g_gmlp, w_out, g_post_mix,
           g_pre_ff, w_ff1, w_ff2, g_post_ff):
    n_b, seq, _ = x_prompt.shape
    n_db, t_len, _ = x_sample.shape
    n_tok = n_b * seq
    D = D_MODEL

    g_pre_mix, g_attn, g_gmlp = _row(g_pre_mix), _row(g_attn), _row(g_gmlp)
    g_post_mix, g_pre_ff, g_post_ff = _row(g_post_mix), _row(g_pre_ff), _row(g_post_ff)
    v_gain = _row(v_gain)

    c_all = jnp.concatenate(
        [c_sample, c_prompt, jnp.zeros(((-(n_b + n_db)) % 16, D), F32)], axis=0)
    b_ada = b_ada.reshape(1, 6 * D)
    bias_p, bias_s, wtril, mod_s, mod_p = _cond(rel_table, w_s, t_len, c_all, n_db, n_b,
                                                w_ada, b_ada, 2 * D)
    SH_M, SC_M = 0, 1
    GT_M, SH_F, SC_F, GT_F = range(4)

    def pmod(chunk, tm):
        per = seq // tm
        return pl.BlockSpec((None, 1, D), lambda i: (i // per, 0, chunk))

    def smod(chunk):
        return pl.BlockSpec((n_db, D), lambda i: (0, chunk))

    sd = _sample_dist(t_len)
    valid_s = ((sd >= 0) & (sd < WINDOW) & (np.arange(SMP_KEYS)[None, :] < WINDOW + t_len))
    valid_s = valid_s.astype(np.float32)
    sink2 = sinks.reshape(N_KV_HEADS, GROUP)
    sink_p = jnp.repeat(sink2, BLOCK, axis=1)[:, None, :]
    sink_s = jnp.tile(sink2, (1, t_len)).reshape(-1, 1)
    ga3 = g_attn.reshape(N_KV_HEADS, GROUP, HEAD_DIM)
    gain_t = jnp.repeat(ga3.transpose(0, 2, 1).reshape(KV_COLS, GROUP), BLOCK, axis=1)
    g_attn = ga3.transpose(1, 0, 2).reshape(1, ATTN_WIDTH)

    xp = x_prompt.reshape(n_tok, D)
    tile = lambda w: pl.BlockSpec((MIX_TM, w), lambda i: (i, 0))
    ttile = lambda w: pl.BlockSpec((w, MIX_TM), lambda i: (0, i))
    n_mix = n_tok // MIX_TM
    slab1 = pl.BlockSpec((D // n_mix, D_FF), lambda i: (i, 0))
    assert seq >= WINDOW and MIX_TM >= WINDOW
    mix_per = seq // MIX_TM
    win = pl.BlockSpec((None, KV_COLS, WINDOW), lambda i: (i // mix_per, 0, 0))
    assert n_mix * BLOCK == D and GROUP * BLOCK == ATTN_WIDTH
    piece = lambda off: pl.BlockSpec(
        (HEAD_DIM, D), lambda i: (jnp.where(i < GROUP, i + off * GROUP, 2 * i + off), 0))
    slab_o = pl.BlockSpec((BLOCK, D), lambda i: (i, 0))
    ada_tn = 4 * D // n_mix
    ada_col = lambda i: (0, 2 * D // ada_tn + i)
    (qt_p, k_p, vt_p, gu_p, gvn_p, kwin, vwin, w1_b, w_out_b, w_in_b,
     mod2_s, mod2_p) = _mix_in(
        xp, tile(D), pmod(SH_M, MIX_TM), pmod(SC_M, MIX_TM), mod_p, (n_tok // MIX_TM,),
        g_pre_mix, w_in, v_gain,
        (jax.ShapeDtypeStruct((Q_COLS, n_tok), BF16),
         jax.ShapeDtypeStruct((n_tok, KV_COLS), BF16),
         jax.ShapeDtypeStruct((KV_COLS, n_tok), F32),
         jax.ShapeDtypeStruct((n_tok, GMLP_WIDTH), F32),
         jax.ShapeDtypeStruct((n_tok, GMLP_WIDTH), BF16),
         jax.ShapeDtypeStruct((n_b, KV_COLS, WINDOW), F32),
         jax.ShapeDtypeStruct((n_b, KV_COLS, WINDOW), F32),
         jax.ShapeDtypeStruct((D, D_FF), BF16),
         jax.ShapeDtypeStruct((D, D), BF16),
         jax.ShapeDtypeStruct((D, IN_COLS), BF16),
         jax.ShapeDtypeStruct((n_db, 4 * D), F32),
         jax.ShapeDtypeStruct((n_b, 1, 4 * D), F32)),
        (ttile(Q_COLS), tile(KV_COLS), ttile(KV_COLS), tile(GMLP_WIDTH), tile(GMLP_WIDTH),
         win, win, slab1, slab_o, pl.BlockSpec(memory_space=pl.ANY),
         pl.BlockSpec((n_db, ada_tn), lambda i: (0, i)), pl.BlockSpec((n_b, 1, ada_tn), lambda i: (0, 0, i))),
        side=((w_ff1, w_out, w_out, c_all, w_ada, b_ada),
              (slab1, piece(0), piece(1), _resident(c_all.shape, lambda i: (0, 0)),
               pl.BlockSpec((D, ada_tn), ada_col), pl.BlockSpec((1, ada_tn), ada_col))))

    x1_p, hf_p, w2_b = _mix_core(qt_p, k_p, vt_p, gu_p, gvn_p, bias_p, sink_p,
                                 wtril, b_s.T, gain_t, g_gmlp, seq, w_ff2,
                                 xp, mod2_p, pmod, w_out_b, g_post_mix, g_pre_ff)
    one = lambda i: (0, 0)

    ffn_per = seq // FFN_TM
    gate_p = lambda tile: pl.BlockSpec((None, 1, D), lambda s: (tile(s) // ffn_per, 0, GT_F))
    y_p = _ffn(hf_p, w1_b, w2_b, x1_p, mod2_p, gate_p, g_post_ff, FFN_TM)

    xs = x_sample.reshape(n_db, t_len * D)
    lane = lambda w: pl.BlockSpec((n_db, w), lambda t: (0, t))
    q_s, k_s, v_s, gu_s, gvn_s = _mix_in(
        xs, lane(D), smod(SH_M), smod(SC_M), mod_s, (t_len,),
        g_pre_mix, w_in_b, v_gain,
        (jax.ShapeDtypeStruct((n_db, t_len * Q_COLS), F32),
         jax.ShapeDtypeStruct((n_db, t_len * KV_COLS), F32),
         jax.ShapeDtypeStruct((n_db, t_len * KV_COLS), F32),
         jax.ShapeDtypeStruct((n_db, t_len * GMLP_WIDTH), F32),
         jax.ShapeDtypeStruct((n_db, t_len * GMLP_WIDTH), F32)),
        (lane(Q_COLS), lane(KV_COLS), lane(KV_COLS), lane(GMLP_WIDTH), lane(GMLP_WIDTH)))

    q5 = q_s.reshape(n_db, t_len, N_KV_HEADS, GROUP, HEAD_DIM).transpose(0, 2, 1, 3, 4)
    q5 = q5.reshape(n_db, N_KV_HEADS, t_len * GROUP, HEAD_DIM).astype(BF16)
    z5 = jnp.zeros_like(q5[:, 0])
    q6 = jnp.concatenate([jnp.concatenate([q5[:, 0], z5], axis=-1),
                          jnp.concatenate([z5, q5[:, 1]], axis=-1)], axis=1)
    new_k = k_s.reshape(n_db, t_len, KV_COLS)
    new_v = v_s.reshape(n_db, t_len, KV_COLS)
    cache_kt = cache_k.transpose(0, 2, 3, 1).reshape(n_db, KV_COLS, WINDOW)
    cache_vt = cache_v.transpose(0, 2, 3, 1).reshape(n_db, KV_COLS, WINDOW)
    attn_s = _attn_smp(q6, cache_kt, cache_vt, new_k, new_v, bias_s, jnp.asarray(valid_s), sink_s,
                       t_len)
    attn_s = attn_s.reshape(n_db, t_len * ATTN_WIDTH)
    smem = pl.BlockSpec(memory_space=pltpu.SMEM)
    w4 = w_s[:, :t_len, :t_len].reshape(-1)
    b4 = b_s[:, :t_len].reshape(-1)
    tmaj = lambda: pl.BlockSpec((None, n_db, D), lambda t: (t, 0, 0))
    x1_s, hf_s = pl.pallas_call(
        functools.partial(_out_proj_smp_body, t_len=t_len),
        out_shape=(jax.ShapeDtypeStruct((t_len, n_db, D), F32),
                   jax.ShapeDtypeStruct((t_len, n_db, D), BF16)),
        grid=(t_len,),
        in_specs=[smem, smem, lane(ATTN_WIDTH), lane(GMLP_WIDTH),
                  _resident((n_db, t_len * GMLP_WIDTH), one),
                  _resident((1, ATTN_WIDTH), one), _resident((1, GMLP_WIDTH), one),
                  lane(D), smod(GT_M), smod(SH_F), smod(SC_F),
                  _resident((D, D), one), _resident((1, D), one), _resident((1, D), one)],
        out_specs=(tmaj(), tmaj()),
        compiler_params=_params(1),
        name="out_proj_smp",
    )(w4, b4, attn_s, gu_s, gvn_s, g_attn, g_gmlp, xs, mod2_s, mod2_s, mod2_s,
      w_out_b, g_post_mix, g_pre_ff)

    n_st = t_len * n_db
    y_s = _ffn(hf_s.reshape(n_st, D), w1_b, w2_b, x1_s.reshape(n_st, D), mod2_s,
               lambda tile: smod(GT_F), g_post_ff, n_st)
    y_s = y_s.reshape(t_len, n_db, D).transpose(1, 0, 2)

    last = lambda t: t.reshape(n_b, N_KV_HEADS, HEAD_DIM, WINDOW).transpose(0, 3, 1, 2)
    k_p4, v_p4 = last(kwin), last(vwin)
    return (y_p.reshape(n_b, seq, D), y_s, k_p4, v_p4,
            k_s.reshape(n_db, t_len, N_KV_HEADS, HEAD_DIM),
            v_s.reshape(n_db, t_len, N_KV_HEADS, HEAD_DIM),
            gvn_s.reshape(n_db, t_len, GMLP_HEADS, GMLP_HEAD_DIM))


def kernel(x_prompt, x_sample, cache_k, cache_v, c_prompt, c_sample, rel_bias_table, w_ada, b_ada,
           g_pre_mix, w_in, attn_sinks, gmlp_v_gain, gmlp_w_s, gmlp_b_s, g_attn_out, g_gmlp_out,
           w_out, g_post_mix, g_pre_ff, w_ff1, w_ff2, g_post_ff):
    depth = w_in.shape[0]
    assert depth == 1, "single-layer step"
    only = lambda a: a.reshape(a.shape[1:])
    outs = _layer(x_prompt, x_sample, only(cache_k), only(cache_v), c_prompt, c_sample,
                  rel_bias_table, *map(only, (w_ada, b_ada, g_pre_mix, w_in, attn_sinks, gmlp_v_gain,
                                              gmlp_w_s, gmlp_b_s, g_attn_out, g_gmlp_out, w_out,
                                              g_post_mix, g_pre_ff, w_ff1, w_ff2, g_post_ff)))
    y_p, y_s, k_p, v_p, k_s, v_s, gv_s = outs
    return (y_p, y_s, k_p[None], v_p[None], k_s[None], v_s[None], gv_s[None])
```

```python
import functools
import math

import numpy as np
import jax
import jax.numpy as jnp
from jax import lax
from jax.experimental import pallas as pl
from jax.experimental.pallas import tpu as pltpu

D_MODEL = 2048
HEAD_DIM = 64
ATTN_WIDTH = 1024
N_HEADS = 16
N_KV_HEADS = 2
GROUP = 8
WINDOW = 128
BLOCK = 128
GMLP_WIDTH = 1024
GMLP_HEADS = 8
GMLP_HEAD_DIM = 128
D_FF = 4 * D_MODEL
NUM_BUCKETS = 32
MAX_DISTANCE = 128
EPS = 1e-6
KV_COLS = N_KV_HEADS * HEAD_DIM
Q_COLS = N_HEADS * HEAD_DIM
IN_COLS = Q_COLS + 2 * KV_COLS + 2 * GMLP_WIDTH
ATTN_SCALE = HEAD_DIM ** -0.5
NEG_INF = -1e30

MIB = 1024 * 1024
VMEM_LIMIT_BYTES = 56 * MIB
MIX_IN_VMEM_BYTES = 62 * MIB
MIX_CORE_VMEM_BYTES = 58 * MIB

BF16 = jnp.bfloat16
F32 = jnp.float32


def _params(n_axes, vmem=VMEM_LIMIT_BYTES):
    return pltpu.CompilerParams(
        dimension_semantics=("arbitrary",) * n_axes, vmem_limit_bytes=vmem)


def _resident(shape, index_map):
    return pl.BlockSpec(shape, index_map, pipeline_mode=pl.Buffered(1))


def _rms(x, gain):
    return x * lax.rsqrt(jnp.mean(x * x, axis=-1, keepdims=True) + EPS) * gain


def _rows(m, n_rows):
    r = m.shape[0]
    if r == 1 or r == n_rows:
        return m
    return jnp.concatenate([m] * (n_rows // r), axis=0)


def _dot(a, b):
    return jnp.dot(a, b, preferred_element_type=F32)


def _dot_nt(a, b):
    return lax.dot_general(a, b, (((1,), (1,)), ((), ())), preferred_element_type=F32)


def _t5_bucket(dist):
    n = np.maximum(dist, 0)
    max_exact = NUM_BUCKETS // 2
    nf = np.maximum(n, 1).astype(np.float64)
    large = max_exact + (np.log(nf / max_exact) / math.log(MAX_DISTANCE / max_exact)
                         * (NUM_BUCKETS - max_exact)).astype(np.int32)
    large = np.minimum(large, NUM_BUCKETS - 1)
    return jnp.asarray(np.where(n < max_exact, n, large), jnp.int32)


SMP_KEYS = 2 * WINDOW


def _prompt_dist():
    r = np.arange(BLOCK)[:, None]
    i = np.arange(BLOCK)[None, :]
    return (i - r) % BLOCK


def _sample_rows(t_len):
    r = np.arange(N_KV_HEADS * t_len * GROUP)
    return r // (t_len * GROUP), (r // GROUP) % t_len, r % GROUP


def _sample_dist(t_len):
    _, t, _ = _sample_rows(t_len)
    return WINDOW + t[:, None] - np.arange(SMP_KEYS)[None, :]


def _prep_body(tab_ref, tabr_ref, bkt_p_ref, bkt_s_ref, ws_ref,
               bias_p_ref, bias_s_ref, wtril_ref):
    bkt_p = bkt_p_ref[...]
    for kh in range(N_KV_HEADS):
        for g in range(GROUP):
            h = kh * GROUP + g
            acc = jnp.zeros((BLOCK, BLOCK), F32)
            for b in range(NUM_BUCKETS):
                acc = jnp.where(bkt_p == b, tab_ref[b, h], acc)
            bias_p_ref[kh, :, g * BLOCK:(g + 1) * BLOCK] = acc
    bkt_s = bkt_s_ref[...]
    tabr = tabr_ref[...]
    acc = jnp.zeros(bkt_s.shape, F32)
    for b in range(NUM_BUCKETS):
        acc = jnp.where(bkt_s == b, tabr[:, b:b + 1], acc)
    bias_s_ref[...] = acc
    ii = lax.broadcasted_iota(jnp.int32, (BLOCK, BLOCK), 0)
    jj = lax.broadcasted_iota(jnp.int32, (BLOCK, BLOCK), 1)
    for h in range(GMLP_HEADS):
        wtril_ref[h] = jnp.where(jj <= ii, ws_ref[h], 0.0).astype(BF16)


ADA_TN = 1024


def _adaln_body(c_ref, w_ref, b_ref, os_ref, op_ref):
    n_s, n_p = os_ref.shape[0], op_ref.shape[0]
    c = c_ref[...]
    s = (c * jax.nn.sigmoid(c)).astype(BF16)
    r = _dot(s, w_ref[...].astype(BF16)) + b_ref[...]
    os_ref[...] = r[:n_s]
    for p in range(n_p):
        op_ref[p] = r[n_s + p:n_s + p + 1]


def _cond_body(tab_ref, tabr_ref, bkt_p_ref, bkt_s_ref, ws_ref, c_ref, w_ref, b_ref,
               bias_p_ref, bias_s_ref, wtril_ref, os_ref, op_ref):
    @pl.when(pl.program_id(0) == 0)
    def _():
        _prep_body(tab_ref, tabr_ref, bkt_p_ref, bkt_s_ref, ws_ref, bias_p_ref, bias_s_ref, wtril_ref)

    _adaln_body(c_ref, w_ref, b_ref, os_ref, op_ref)


def _cond(rel_table, w_s, t_len, c_all, n_s, n_p, w_ada, b_ada, n_out):
    bkt_p = _t5_bucket(_prompt_dist())
    bkt_s = _t5_bucket(_sample_dist(t_len))
    n_rows = bkt_s.shape[0]
    tab_rows = jnp.broadcast_to(rel_table.T.reshape(N_KV_HEADS, 1, GROUP, NUM_BUCKETS),
                                (N_KV_HEADS, t_len, GROUP, NUM_BUCKETS)).reshape(n_rows, NUM_BUCKETS)
    n_all = c_all.shape[0]
    whole = lambda a: _resident(a.shape, lambda j: (0,) * a.ndim)
    outs = (jax.ShapeDtypeStruct((N_KV_HEADS, BLOCK, GROUP * BLOCK), F32),
            jax.ShapeDtypeStruct((n_rows, SMP_KEYS), F32),
            jax.ShapeDtypeStruct((GMLP_HEADS, BLOCK, BLOCK), BF16))
    return pl.pallas_call(
        _cond_body,
        out_shape=outs + (jax.ShapeDtypeStruct((n_s, n_out), F32),
                          jax.ShapeDtypeStruct((n_p, 1, n_out), F32)),
        grid=(n_out // ADA_TN,),
        in_specs=[pl.BlockSpec(memory_space=pltpu.SMEM),
                  whole(tab_rows), whole(bkt_p), whole(bkt_s), whole(w_s),
                  _resident((n_all, D_MODEL), lambda j: (0, 0)),
                  pl.BlockSpec((D_MODEL, ADA_TN), lambda j: (0, j)),
                  pl.BlockSpec((1, ADA_TN), lambda j: (0, j))],
        out_specs=tuple(pl.BlockSpec(o.shape, lambda j, nd=len(o.shape): (0,) * nd) for o in outs)
        + (pl.BlockSpec((n_s, ADA_TN), lambda j: (0, j)),
           pl.BlockSpec((n_p, 1, ADA_TN), lambda j: (0, 0, j))),
        compiler_params=_params(1, 40 * MIB),
        name="cond",
    )(rel_table, tab_rows, bkt_p, bkt_s, w_s, c_all, w_ada, b_ada)


W_STAGE_COLS = 256


def _mix_in_first_tile(x_ref, sh_ref, sc_ref, g_ref, w_hbm, vg_ref, q_ref, k_ref, v_ref, gu_ref,
                       gv_ref, kwin_ref, vwin_ref, wb_ref, stage_ref, sem_ref):
    n = IN_COLS // W_STAGE_COLS
    assert W_STAGE_COLS == 2 * KV_COLS and Q_COLS % W_STAGE_COLS == 0
    c2, c3 = Q_COLS + 2 * KV_COLS, Q_COLS + 2 * KV_COLS + GMLP_WIDTH

    def fetch(c):
        return pltpu.make_async_copy(w_hbm.at[:, pl.ds(c * W_STAGE_COLS, W_STAGE_COLS)],
                                     stage_ref.at[c % 2], sem_ref.at[c % 2])

    tm = x_ref.shape[0]
    subs = [slice(r, r + MIX_SUB) for r in range(0, tm, MIX_SUB)]

    def project(lo):
        cols = slice(lo, lo + W_STAGE_COLS)
        for rows, h in zip(subs, hs):
            o = _dot(h, wb_ref[:, cols])
            if lo < Q_COLS:
                q_ref[cols, rows] = (o * ATTN_SCALE).T.astype(q_ref.dtype)
            elif lo < c2:
                k_ref[rows, :] = o[:, :KV_COLS].astype(k_ref.dtype)
                vt = o[:, KV_COLS:].T
                v_ref[:, rows] = vt
                if rows.stop == tm:
                    kwin_ref[...] = o[-WINDOW:, :KV_COLS].T
                    vwin_ref[...] = vt[:, -WINDOW:]
            elif lo < c3:
                gu_ref[rows, lo - c2:lo - c2 + W_STAGE_COLS] = jax.nn.gelu(o).astype(gu_ref.dtype)
            else:
                gv = jax.nn.gelu(o)
                for hd in range(W_STAGE_COLS // GMLP_HEAD_DIM):
                    sl = slice(lo - c3 + hd * GMLP_HEAD_DIM, lo - c3 + (hd + 1) * GMLP_HEAD_DIM)
                    part = gv[:, hd * GMLP_HEAD_DIM:(hd + 1) * GMLP_HEAD_DIM]
                    gv_ref[rows, sl] = _rms(part, vg[:, sl]).astype(gv_ref.dtype)

    fetch(0).start()
    hs = [(_rms(x_ref[rows, :], g_ref[...] * (1.0 + sc_ref[...])) + sh_ref[...]).astype(BF16)
          for rows in subs]
    vg = vg_ref[...]
    fetch(1).start()
    for c in range(n):
        fetch(c).wait()
        lo = c * W_STAGE_COLS
        wb_ref[:, lo:lo + W_STAGE_COLS] = stage_ref[c % 2].astype(BF16)
        if c + 2 < n:
            fetch(c + 2).start()
        if c > 0:
            project(lo - W_STAGE_COLS)
    project((n - 1) * W_STAGE_COLS)


def _mix_in_body(x_ref, sh_ref, sc_ref, g_ref, w_ref, vg_ref, *refs, transposed):
    if transposed:
        (wc_ref, woa_ref, wob_ref, ca_ref, wa_ref, ba_ref,
         q_ref, k_ref, v_ref, gu_ref, gv_ref, kwin_ref, vwin_ref,
         wcb_ref, wo_ref, wib_hbm, ms_ref, mp_ref, wb_ref, stage_ref, sem_ref, wsem_ref) = refs
        wcb_ref[...] = wc_ref[...].astype(wcb_ref.dtype)
        wo_ref[:HEAD_DIM, :] = woa_ref[...].astype(wo_ref.dtype)
        wo_ref[HEAD_DIM:, :] = wob_ref[...].astype(wo_ref.dtype)
        _adaln_body(ca_ref, wa_ref, ba_ref, ms_ref, mp_ref)
    else:
        q_ref, k_ref, v_ref, gu_ref, gv_ref = refs
    tm = x_ref.shape[0]
    sub = min(tm, MIX_SUB)
    c0, c1, c2, c3 = Q_COLS, Q_COLS + KV_COLS, Q_COLS + 2 * KV_COLS, Q_COLS + 2 * KV_COLS + GMLP_WIDTH
    vg = vg_ref[...]

    def tile(w_ref):
        for r in range(0, tm, sub):
            rows = slice(r, r + sub)
            mod = lambda ref: ref[...] if ref.shape[0] == 1 else ref[rows, :]
            h = (_rms(x_ref[rows, :], g_ref[...] * (1.0 + mod(sc_ref))) + mod(sh_ref)).astype(BF16)
            q = _dot(h, w_ref[:, :c0]) * ATTN_SCALE
            kv = _dot(h, w_ref[:, c0:c2])
            k_ref[rows, :] = kv[:, :KV_COLS].astype(k_ref.dtype)
            if transposed:
                q_ref[:, rows] = q.T.astype(q_ref.dtype)
                vt = kv[:, KV_COLS:].T
                v_ref[:, rows] = vt
                if r + sub == tm:
                    kwin_ref[...] = kv[-WINDOW:, :KV_COLS].T
                    vwin_ref[...] = vt[:, -WINDOW:]
            else:
                q_ref[rows, :] = q.astype(q_ref.dtype)
                v_ref[rows, :] = kv[:, KV_COLS:]
            gu_ref[rows, :] = jax.nn.gelu(_dot(h, w_ref[:, c2:c3])).astype(gu_ref.dtype)
            gv = jax.nn.gelu(_dot(h, w_ref[:, c3:]))
            for hd in range(GMLP_HEADS):
                sl = slice(hd * GMLP_HEAD_DIM, (hd + 1) * GMLP_HEAD_DIM)
                gv_ref[rows, sl] = _rms(gv[:, sl], vg[:, sl]).astype(gv_ref.dtype)

    if not transposed:
        tile(w_ref)
        return
    write_back = pltpu.make_async_copy(wb_ref, wib_hbm, wsem_ref)

    @pl.when(pl.program_id(0) == 0)
    def _():
        _mix_in_first_tile(x_ref, sh_ref, sc_ref, g_ref, w_ref, vg_ref, q_ref, k_ref, v_ref,
                           gu_ref, gv_ref, kwin_ref, vwin_ref, wb_ref, stage_ref, sem_ref)
        write_back.start()

    @pl.when(pl.program_id(0) > 0)
    def _():
        tile(wb_ref)

    @pl.when(pl.program_id(0) == pl.num_programs(0) - 1)
    def _():
        write_back.wait()


def _mix_in(x, x_spec, sh_spec, sc_spec, mod, grid, g_pre, w_in, v_gain, outs, out_specs,
            side=((), ())):
    extra, extra_specs = side
    prompt = len(extra) > 0
    any_space = pl.BlockSpec(memory_space=pl.ANY)
    scratch = [pltpu.VMEM((D_MODEL, IN_COLS), BF16),
               pltpu.VMEM((2, D_MODEL, W_STAGE_COLS), F32),
               pltpu.SemaphoreType.DMA((2,)), pltpu.SemaphoreType.DMA(())] if prompt else []
    return pl.pallas_call(
        functools.partial(_mix_in_body, transposed=prompt),
        out_shape=outs,
        grid=grid,
        in_specs=[x_spec, sh_spec, sc_spec,
                  _resident((1, D_MODEL), lambda i: (0, 0)),
                  any_space if prompt else _resident((D_MODEL, IN_COLS), lambda i: (0, 0)),
                  _resident((1, GMLP_WIDTH), lambda i: (0, 0)), *extra_specs],
        out_specs=out_specs,
        scratch_shapes=scratch,
        compiler_params=_params(1, MIX_IN_VMEM_BYTES if prompt else VMEM_LIMIT_BYTES),
        name="mix_in",
    )(x, mod, mod, g_pre, w_in, v_gain, *extra)


PROJ_TN = 256


def _out_proj_epilogue(o, rows, x_ref, gt_ref, sh_ref, sc_ref, gpm_ref, gpf_ref, x1_ref, hf_ref):
    mod = lambda ref: ref[...] if ref.shape[0] == 1 else ref[rows, :]
    x1 = x_ref[rows, :] + _rms(o, gpm_ref[...] * mod(gt_ref))
    x1_ref[rows, :] = x1
    hf = _rms(x1, gpf_ref[...] * (1.0 + mod(sc_ref))) + mod(sh_ref)
    hf_ref[rows, :] = hf.astype(hf_ref.dtype)


CORE_NB = 4


def _mix_core_body(qt_ref, kc_ref, kp_ref, vtc_ref, vtp_ref, gu_ref, gv_ref,
                   bias_ref, pm_ref, sink_ref, wtril_ref, bs_ref, gat_ref, gg_ref, wc_ref,
                   x_ref, gt_ref, sh_ref, sc_ref, wb_ref, gpm_ref, gpf_ref,
                   x1_ref, hf_ref, wcb_ref, *, steps_per_seq):
    wcb_ref[...] = wc_ref[...].astype(wcb_ref.dtype)
    first = pl.program_id(0) % steps_per_seq == 0
    tri = (lax.broadcasted_iota(jnp.int32, (BLOCK, BLOCK), 0)
           > lax.broadcasted_iota(jnp.int32, (BLOCK, BLOCK), 1))
    from_prev = jnp.concatenate([tri] * GROUP, axis=1)
    zeros = jnp.zeros((HEAD_DIM, GROUP * BLOCK), BF16)
    bs = bs_ref[...]
    def mix_stages(u, st):
        tok = slice(u * BLOCK, (u + 1) * BLOCK)

        def logits(kh):
            if kh == 0:
                k_prev = kp_ref[...] if u == 0 else kc_ref[(u - 1) * BLOCK:u * BLOCK, :]
                vt_prev = vtp_ref[...] if u == 0 else vtc_ref[:, (u - 1) * BLOCK:u * BLOCK]
                st['k'] = jnp.concatenate([k_prev, kc_ref[tok, :]], axis=0)
                st['vt'] = jnp.concatenate([vt_prev, vtc_ref[:, tok]], axis=1).astype(BF16)
                st['outs'] = []
            base = kh * GROUP * HEAD_DIM
            qs = jnp.concatenate(
                [qt_ref[base + g * HEAD_DIM: base + (g + 1) * HEAD_DIM, tok] for g in range(GROUP)],
                axis=1)
            qp = jnp.concatenate([qs, zeros] if kh == 0 else [zeros, qs], axis=0)
            st['qk'] = _dot(st['k'], qp)

        def softmax(kh):
            qk = st['qk']
            s = jnp.where(from_prev, qk[:BLOCK], qk[BLOCK:]) + bias_ref[kh]
            if u == 0:
                s = jnp.where(jnp.logical_and(first, from_prev), NEG_INF, s)
            sink = sink_ref[kh]
            m =jnp.maximum(jnp.max(s, axis=0, keepdims=True), sink)
            p = jnp.exp(s - m)
            st['den'] = jnp.sum(p, axis=0, keepdims=True) + jnp.exp(sink - m)
            p = p.astype(BF16)
            st['p2'] = jnp.concatenate([p * pm_ref[0], p * pm_ref[1]], axis=0)

        def values(kh):
            vt = st['vt'][kh * HEAD_DIM:(kh + 1) * HEAD_DIM, :]
            st['outs'].append(_dot(vt, st['p2']) / st['den'])

        def attn_norm():
            ot = jnp.concatenate(st['outs'], axis=0)
            sq = jnp.sum(ot * ot, axis=0, keepdims=True)
            tot = sq[:, :BLOCK]
            for g in range(1, GROUP):
                tot = tot + sq[:, g * BLOCK:(g + 1) * BLOCK]
            r = lax.rsqrt(tot / ATTN_WIDTH + EPS)
            a = ot * jnp.concatenate([r] * GROUP, axis=1) * gat_ref[...]
            st['merged'] = [a[:, g * BLOCK:(g + 1) * BLOCK].T.astype(BF16) for g in range(GROUP)]

        def gating():
            gated = []
            for hd in range(GMLP_HEADS):
                sl = slice(hd * GMLP_HEAD_DIM, (hd + 1) * GMLP_HEAD_DIM)
                mixed = _dot(wtril_ref[hd], gv_ref[tok, sl]) + bs[:, hd:hd + 1]
                gated.append(gu_ref[tok, sl] * mixed)
            gm = jnp.concatenate(gated, axis=-1)
            st['merged'] = jnp.concatenate(
                st['merged'] + [_rms(gm, gg_ref[...]).astype(BF16)], axis=1)

        P = functools.partial
        return [P(logits, 0), P(softmax, 0), P(values, 0), P(logits, 1), P(softmax, 1),
                P(values, 1), attn_norm, gating]

    def proj_stages(u, st):
        tok = slice(u * BLOCK, (u + 1) * BLOCK)
        chunks = []

        def chunk(c):
            chunks.append(_dot(st['merged'], wb_ref[:, c * PROJ_TN:(c + 1) * PROJ_TN]))

        def finish():
            _out_proj_epilogue(jnp.concatenate(chunks, axis=1), tok, x_ref, gt_ref, sh_ref,
                               sc_ref, gpm_ref, gpf_ref, x1_ref, hf_ref)

        return [functools.partial(chunk, c) for c in range(D_MODEL // PROJ_TN)] + [finish]

    states = [dict() for _ in range(CORE_NB)]
    for u in range(CORE_NB + 1):
        a = mix_stages(u, states[u]) if u < CORE_NB else []
        b = proj_stages(u - 1, states[u - 1]) if u > 0 else []
        for i in range(max(len(a), len(b))):
            for stage in (b[i:i + 1] + a[i:i + 1]):
                stage()


def _mix_core(qt, k, vt, gu, gvn, bias_t, sink_row, wtril, bs_t, gain_t, g_gmlp, seq, w_side,
              x, mod, mod_spec, w_out_b, g_post_mix, g_pre_ff):
    r_gt_i = np.tile(np.arange(BLOCK)[:, None] > np.arange(BLOCK)[None, :], (1, GROUP))
    pmask = jnp.asarray(np.stack([r_gt_i, ~r_gt_i]), BF16)
    n_tok = k.shape[0]
    tm = CORE_NB * BLOCK
    n_steps = n_tok // tm
    side_rows = w_side.shape[0] // n_steps
    GT_M, SH_F, SC_F = 0, 1, 2
    cur = lambda i: (i, 0)
    prev = lambda i: (jnp.maximum(i * CORE_NB - 1, 0), 0)
    cur_t = lambda i: (0, i)
    prev_t = lambda i: (0, jnp.maximum(i * CORE_NB - 1, 0))
    full2 = lambda i: (0, 0)
    full3 = lambda i: (0, 0, 0)
    return pl.pallas_call(
        functools.partial(_mix_core_body, steps_per_seq=seq // tm),
        out_shape=(jax.ShapeDtypeStruct((n_tok, D_MODEL), F32),
                   jax.ShapeDtypeStruct((n_tok, D_MODEL), BF16),
                   jax.ShapeDtypeStruct(w_side.shape, BF16)),
        grid=(n_steps,),
        in_specs=[pl.BlockSpec((Q_COLS, tm), cur_t),
                  pl.BlockSpec((tm, KV_COLS), cur), pl.BlockSpec((BLOCK, KV_COLS), prev),
                  pl.BlockSpec((KV_COLS, tm), cur_t), pl.BlockSpec((KV_COLS, BLOCK), prev_t),
                  pl.BlockSpec((tm, GMLP_WIDTH), cur), pl.BlockSpec((tm, GMLP_WIDTH), cur),
                  _resident(bias_t.shape, full3), _resident(pmask.shape, full3),
                  _resident(sink_row.shape, full3), _resident(wtril.shape, full3),
                  _resident(bs_t.shape, full2),
                  _resident(gain_t.shape, full2), _resident((1, GMLP_WIDTH), full2),
                  pl.BlockSpec((side_rows, w_side.shape[1]), cur),
                  pl.BlockSpec((tm, D_MODEL), cur),
                  mod_spec(GT_M, tm), mod_spec(SH_F, tm), mod_spec(SC_F, tm),
                  _resident((D_MODEL, D_MODEL), full2),
                  _resident((1, D_MODEL), full2), _resident((1, D_MODEL), full2)],
        out_specs=(pl.BlockSpec((tm, D_MODEL), cur), pl.BlockSpec((tm, D_MODEL), cur),
                   pl.BlockSpec((side_rows, w_side.shape[1]), cur)),
        compiler_params=_params(1, MIX_CORE_VMEM_BYTES),
        name="mix_core",
    )(qt, k, k, vt, vt, gu, gvn, bias_t, pmask, sink_row, wtril, bs_t, gain_t, g_gmlp, w_side,
      x, mod, mod, mod, w_out_b, g_post_mix, g_pre_ff)


SMP_BB = 32


def _attn_smp_body(q_ref, ckt_ref, cvt_ref, nk_ref, nv_ref, bias_ref, valid_ref, sink_ref,
                   o_ref, *, t_len):
    zpad = jnp.zeros((WINDOW - t_len, KV_COLS), F32)
    tiles = []
    for b in range(SMP_BB):
        q = q_ref[b]
        s_cache = _dot(q, ckt_ref[b].astype(BF16))
        kn = jnp.concatenate([nk_ref[b], zpad], axis=0).astype(BF16)
        tiles.append(jnp.concatenate([s_cache, _dot_nt(q, kn)], axis=1)[None])
    s = jnp.concatenate(tiles, axis=0)
    s = jnp.where(valid_ref[...] > 0.5, s + bias_ref[...], NEG_INF)
    sink = sink_ref[...]
    m = jnp.maximum(jnp.max(s, axis=-1, keepdims=True), sink)
    p = jnp.exp(s - m)
    den = jnp.sum(p, axis=-1, keepdims=True) + jnp.exp(sink - m)
    p = p.astype(BF16)
    half = t_len * GROUP
    own_head = lax.broadcasted_iota(jnp.int32, (half, KV_COLS), 1) < HEAD_DIM
    for b in range(SMP_BB):
        vn = jnp.concatenate([nv_ref[b], zpad], axis=0).astype(BF16)
        o = _dot_nt(p[b, :, :WINDOW], cvt_ref[b].astype(BF16)) + _dot(p[b, :, WINDOW:], vn)
        o = o / den[b]
        o_ref[b] = jnp.where(own_head, o[:half], o[half:])


def _attn_smp(q6, cache_kt, cache_vt, new_k, new_v, bias_s, valid_s, sink_s, t_len):
    n_b, rows, _ = q6.shape
    b3 = lambda i: (i, 0, 0)
    one = lambda i: (0, 0)
    return pl.pallas_call(
        functools.partial(_attn_smp_body, t_len=t_len),
        out_shape=jax.ShapeDtypeStruct((n_b, rows // N_KV_HEADS, KV_COLS), F32),
        grid=(n_b // SMP_BB,),
        in_specs=[pl.BlockSpec((SMP_BB, rows, KV_COLS), b3),
                  pl.BlockSpec((SMP_BB, KV_COLS, WINDOW), b3),
                  pl.BlockSpec((SMP_BB, KV_COLS, WINDOW), b3),
                  pl.BlockSpec((SMP_BB, t_len, KV_COLS), b3),
                  pl.BlockSpec((SMP_BB, t_len, KV_COLS), b3),
                  _resident(bias_s.shape, one), _resident(valid_s.shape, one),
                  _resident(sink_s.shape, one)],
        out_specs=pl.BlockSpec((SMP_BB, rows // N_KV_HEADS, KV_COLS), b3),
        compiler_params=_params(1, 32 * MIB),
        name="attn_smp",
    )(q6, cache_kt, cache_vt, new_k, new_v, bias_s, valid_s, sink_s)


def _out_proj_smp_body(w4_ref, b4_ref, a_ref, gu_ref, gv_ref, ga_ref, gg_ref,
                       x_ref, gt_ref, sh_ref, sc_ref, wb_ref, gpm_ref, gpf_ref, x1_ref, hf_ref, *,
                       t_len):
    i = pl.program_id(0)
    gated = []
    for hd in range(GMLP_HEADS):
        acc = jnp.zeros((a_ref.shape[0], GMLP_HEAD_DIM), F32)
        for j in range(t_len):
            w = jnp.where(j <= i, w4_ref[(hd * t_len + i) * t_len + j], 0.0)
            lo = j * GMLP_WIDTH + hd * GMLP_HEAD_DIM
            acc = acc + w * gv_ref[:, lo:lo + GMLP_HEAD_DIM]
        mixed = acc + b4_ref[hd * t_len + i]
        gated.append(gu_ref[:, hd * GMLP_HEAD_DIM:(hd + 1) * GMLP_HEAD_DIM] * mixed)
    gm = jnp.concatenate(gated, axis=-1)
    merged = jnp.concatenate([_rms(a_ref[...], ga_ref[...]), _rms(gm, gg_ref[...])], axis=-1)
    _out_proj_epilogue(_dot(merged.astype(BF16), wb_ref[...]), slice(0, a_ref.shape[0]),
                       x_ref, gt_ref, sh_ref, sc_ref, gpm_ref, gpf_ref, x1_ref, hf_ref)


FFN_TF = 1024
FFN_SUB = 256


def _ffn_body(h_ref, w1_ref, w2_ref, x1_ref, gt_ref, g_ref, o_ref, acc_ref, done_ref, *,
              n_tiles, n_chunks):
    s = pl.program_id(0)
    i = s // n_chunks
    j = s % n_chunks
    tm = o_ref.shape[0]
    whole = slice(0, tm)

    def contribution(rows):
        a = jnp.maximum(_dot(h_ref[rows, :], w1_ref[...]), 0.0)
        return _dot((a * a).astype(BF16), w2_ref[...])

    def finalize(rows):
        gt = gt_ref[...] if gt_ref.shape[0] == 1 else _rows(gt_ref[...], tm)[rows, :]
        o_ref[rows, :] = x1_ref[rows, :] + _rms(done_ref[rows, :], g_ref[...] * gt)

    @pl.when(s == 0)
    def _():
        acc_ref[...] = contribution(whole)

    @pl.when(jnp.logical_and(j == 0, jnp.logical_and(i > 0, i < n_tiles)))
    def _():
        for r in range(0, tm, FFN_SUB):
            rows = slice(r, r + FFN_SUB)
            acc_ref[rows, :] = contribution(rows)
            finalize(rows)

    @pl.when(jnp.logical_and(j > 0, j < n_chunks - 1))
    def _():
        acc_ref[...] += contribution(whole)

    @pl.when(jnp.logical_and(j == n_chunks - 1, i < n_tiles))
    def _():
        done_ref[...] = acc_ref[...] + contribution(whole)

    @pl.when(i == n_tiles)
    def _():
        finalize(whole)


def _ffn(hf, w1, w2, x1, mod, gt_spec, g_post, tm):
    n_tok = hf.shape[0]
    n_tiles, n_chunks = n_tok // tm, D_FF // FFN_TF
    last = n_tiles * n_chunks
    mm_tile = lambda s: jnp.minimum(s // n_chunks, n_tiles - 1)
    chunk = lambda s: jnp.where(s == last, n_chunks - 1, s % n_chunks)
    out_tile = lambda s: jnp.maximum(s // n_chunks - (s % n_chunks == 0).astype(jnp.int32), 0)
    return pl.pallas_call(
        functools.partial(_ffn_body, n_tiles=n_tiles, n_chunks=n_chunks),
        out_shape=jax.ShapeDtypeStruct((n_tok, D_MODEL), F32),
        grid=(last + 1,),
        in_specs=[pl.BlockSpec((tm, D_MODEL), lambda s: (mm_tile(s), 0)),
                  pl.BlockSpec((D_MODEL, FFN_TF), lambda s: (0, chunk(s))),
                  pl.BlockSpec((FFN_TF, D_MODEL), lambda s: (chunk(s), 0)),
                  pl.BlockSpec((tm, D_MODEL), lambda s: (out_tile(s), 0)),
                  gt_spec(out_tile),
                  _resident((1, D_MODEL), lambda s: (0, 0))],
        out_specs=pl.BlockSpec((tm, D_MODEL), lambda s: (out_tile(s), 0)),
        scratch_shapes=[pltpu.VMEM((tm, D_MODEL), F32), pltpu.VMEM((tm, D_MODEL), F32)],
        compiler_params=_params(1),
        name="ffn",
    )(hf, w1, w2, x1, mod, g_post)


MIX_TM = 512
MIX_SUB = 128
FFN_TM = 512


def _row(v):
    return v.reshape(1, -1)


def _layer(x_prompt, x_sample, cache_k, cache_v, c_prompt, c_sample, rel_table, w_ada, b_ada,
           g_pre_mix, w_in, sinks, v_gain, w_s, b_s, g_attn, g_gmlp, w_out, g_post_mix,
           g_pre_ff, w_ff1, w_ff2, g_post_ff):
    n_b, seq, _ = x_prompt.shape
    n_db, t_len, _ = x_sample.shape
    n_tok = n_b * seq
    D = D_MODEL

    g_pre_mix, g_attn, g_gmlp = _row(g_pre_mix), _row(g_attn), _row(g_gmlp)
    g_post_mix, g_pre_ff, g_post_ff = _row(g_post_mix), _row(g_pre_ff), _row(g_post_ff)
    v_gain = _row(v_gain)

    c_all = jnp.concatenate(
        [c_sample, c_prompt, jnp.zeros(((-(n_b + n_db)) % 16, D), F32)], axis=0)
    b_ada = b_ada.reshape(1, 6 * D)
    bias_p, bias_s, wtril, mod_s, mod_p = _cond(rel_table, w_s, t_len, c_all, n_db, n_b,
                                                w_ada, b_ada, 2 * D)
    SH_M, SC_M = 0, 1
    GT_M, SH_F, SC_F, GT_F = range(4)

    def pmod(chunk, tm):
        per = seq // tm
        return pl.BlockSpec((None, 1, D), lambda i: (i // per, 0, chunk))

    def smod(chunk):
        return pl.BlockSpec((n_db, D), lambda i: (0, chunk))

    sd = _sample_dist(t_len)
    valid_s = ((sd >= 0) & (sd < WINDOW) & (np.arange(SMP_KEYS)[None, :] < WINDOW + t_len))
    valid_s = valid_s.astype(np.float32)
    sink2 = sinks.reshape(N_KV_HEADS, GROUP)
    sink_p = jnp.repeat(sink2, BLOCK, axis=1)[:, None, :]
    sink_s = jnp.tile(sink2, (1, t_len)).reshape(-1, 1)
    ga3 = g_attn.reshape(N_KV_HEADS, GROUP, HEAD_DIM)
    gain_t = jnp.repeat(ga3.transpose(0, 2, 1).reshape(KV_COLS, GROUP), BLOCK, axis=1)
    g_attn = ga3.transpose(1, 0, 2).reshape(1, ATTN_WIDTH)

    xp = x_prompt.reshape(n_tok, D)
    tile = lambda w: pl.BlockSpec((MIX_TM, w), lambda i: (i, 0))
    ttile = lambda w: pl.BlockSpec((w, MIX_TM), lambda i: (0, i))
    n_mix = n_tok // MIX_TM
    slab1 = pl.BlockSpec((D // n_mix, D_FF), lambda i: (i, 0))
    assert seq >= WINDOW and MIX_TM >= WINDOW
    mix_per = seq // MIX_TM
    win = pl.BlockSpec((None, KV_COLS, WINDOW), lambda i: (i // mix_per, 0, 0))
    assert n_mix * BLOCK == D and GROUP * BLOCK == ATTN_WIDTH
    piece = lambda off: pl.BlockSpec(
        (HEAD_DIM, D), lambda i: (jnp.where(i < GROUP, i + off * GROUP, 2 * i + off), 0))
    slab_o = pl.BlockSpec((BLOCK, D), lambda i: (i, 0))
    ada_tn = 4 * D // n_mix
    ada_col = lambda i: (0, 2 * D // ada_tn + i)
    (qt_p, k_p, vt_p, gu_p, gvn_p, kwin, vwin, w1_b, w_out_b, w_in_b,
     mod2_s, mod2_p) = _mix_in(
        xp, tile(D), pmod(SH_M, MIX_TM), pmod(SC_M, MIX_TM), mod_p, (n_tok // MIX_TM,),
        g_pre_mix, w_in, v_gain,
        (jax.ShapeDtypeStruct((Q_COLS, n_tok), BF16),
         jax.ShapeDtypeStruct((n_tok, KV_COLS), BF16),
         jax.ShapeDtypeStruct((KV_COLS, n_tok), F32),
         jax.ShapeDtypeStruct((n_tok, GMLP_WIDTH), F32),
         jax.ShapeDtypeStruct((n_tok, GMLP_WIDTH), BF16),
         jax.ShapeDtypeStruct((n_b, KV_COLS, WINDOW), F32),
         jax.ShapeDtypeStruct((n_b, KV_COLS, WINDOW), F32),
         jax.ShapeDtypeStruct((D, D_FF), BF16),
         jax.ShapeDtypeStruct((D, D), BF16),
         jax.ShapeDtypeStruct((D, IN_COLS), BF16),
         jax.ShapeDtypeStruct((n_db, 4 * D), F32),
         jax.ShapeDtypeStruct((n_b, 1, 4 * D), F32)),
        (ttile(Q_COLS), tile(KV_COLS), ttile(KV_COLS), tile(GMLP_WIDTH), tile(GMLP_WIDTH),
         win, win, slab1, slab_o, pl.BlockSpec(memory_space=pl.ANY),
         pl.BlockSpec((n_db, ada_tn), lambda i: (0, i)), pl.BlockSpec((n_b, 1, ada_tn), lambda i: (0, 0, i))),
        side=((w_ff1, w_out, w_out, c_all, w_ada, b_ada),
              (slab1, piece(0), piece(1), _resident(c_all.shape, lambda i: (0, 0)),
               pl.BlockSpec((D, ada_tn), ada_col), pl.BlockSpec((1, ada_tn), ada_col))))

    x1_p, hf_p, w2_b = _mix_core(qt_p, k_p, vt_p, gu_p, gvn_p, bias_p, sink_p,
                                 wtril, b_s.T, gain_t, g_gmlp, seq, w_ff2,
                                 xp, mod2_p, pmod, w_out_b, g_post_mix, g_pre_ff)
    one = lambda i: (0, 0)

    ffn_per = seq // FFN_TM
    gate_p = lambda tile: pl.BlockSpec((None, 1, D), lambda s: (tile(s) // ffn_per, 0, GT_F))
    y_p = _ffn(hf_p, w1_b, w2_b, x1_p, mod2_p, gate_p, g_post_ff, FFN_TM)

    xs = x_sample.reshape(n_db, t_len * D)
    lane = lambda w: pl.BlockSpec((n_db, w), lambda t: (0, t))
    q_s, k_s, v_s, gu_s, gvn_s = _mix_in(
        xs, lane(D), smod(SH_M), smod(SC_M), mod_s, (t_len,),
        g_pre_mix, w_in_b, v_gain,
        (jax.ShapeDtypeStruct((n_db, t_len * Q_COLS), F32),
         jax.ShapeDtypeStruct((n_db, t_len * KV_COLS), F32),
         jax.ShapeDtypeStruct((n_db, t_len * KV_COLS), F32),
         jax.ShapeDtypeStruct((n_db, t_len * GMLP_WIDTH), F32),
         jax.ShapeDtypeStruct((n_db, t_len * GMLP_WIDTH), F32)),
        (lane(Q_COLS), lane(KV_COLS), lane(KV_COLS), lane(GMLP_WIDTH), lane(GMLP_WIDTH)))

    q5 = q_s.reshape(n_db, t_len, N_KV_HEADS, GROUP, HEAD_DIM).transpose(0, 2, 1, 3, 4)
    q5 = q5.reshape(n_db, N_KV_HEADS, t_len * GROUP, HEAD_DIM).astype(BF16)
    z5 = jnp.zeros_like(q5[:, 0])
    q6 = jnp.concatenate([jnp.concatenate([q5[:, 0], z5], axis=-1),
                          jnp.concatenate([z5, q5[:, 1]], axis=-1)], axis=1)
    new_k = k_s.reshape(n_db, t_len, KV_COLS)
    new_v = v_s.reshape(n_db, t_len, KV_COLS)
    cache_kt = cache_k.transpose(0, 2, 3, 1).reshape(n_db, KV_COLS, WINDOW)
    cache_vt = cache_v.transpose(0, 2, 3, 1).reshape(n_db, KV_COLS, WINDOW)
    attn_s = _attn_smp(q6, cache_kt, cache_vt, new_k, new_v, bias_s, jnp.asarray(valid_s), sink_s,
                       t_len)
    attn_s = attn_s.reshape(n_db, t_len * ATTN_WIDTH)
    smem = pl.BlockSpec(memory_space=pltpu.SMEM)
    w4 = w_s[:, :t_len, :t_len].reshape(-1)
    b4 = b_s[:, :t_len].reshape(-1)
    tmaj = lambda: pl.BlockSpec((None, n_db, D), lambda t: (t, 0, 0))
    x1_s, hf_s = pl.pallas_call(
        functools.partial(_out_proj_smp_body, t_len=t_len),
        out_shape=(jax.ShapeDtypeStruct((t_len, n_db, D), F32),
                   jax.ShapeDtypeStruct((t_len, n_db, D), BF16)),
        grid=(t_len,),
        in_specs=[smem, smem, lane(ATTN_WIDTH), lane(GMLP_WIDTH),
                  _resident((n_db, t_len * GMLP_WIDTH), one),
                  _resident((1, ATTN_WIDTH), one), _resident((1, GMLP_WIDTH), one),
                  lane(D), smod(GT_M), smod(SH_F), smod(SC_F),
                  _resident((D, D), one), _resident((1, D), one), _resident((1, D), one)],
        out_specs=(tmaj(), tmaj()),
        compiler_params=_params(1),
        name="out_proj_smp",
    )(w4, b4, attn_s, gu_s, gvn_s, g_attn, g_gmlp, xs, mod2_s, mod2_s, mod2_s,
      w_out_b, g_post_mix, g_pre_ff)

    n_st = t_len * n_db
    y_s = _ffn(hf_s.reshape(n_st, D), w1_b, w2_b, x1_s.reshape(n_st, D), mod2_s,
               lambda tile: smod(GT_F), g_post_ff, n_st)
    y_s = y_s.reshape(t_len, n_db, D).transpose(1, 0, 2)

    last = lambda t: t.reshape(n_b, N_KV_HEADS, HEAD_DIM, WINDOW).transpose(0, 3, 1, 2)
    k_p4, v_p4 = last(kwin), last(vwin)
    return (y_p.reshape(n_b, seq, D), y_s, k_p4, v_p4,
            k_s.reshape(n_db, t_len, N_KV_HEADS, HEAD_DIM),
            v_s.reshape(n_db, t_len, N_KV_HEADS, HEAD_DIM),
            gvn_s.reshape(n_db, t_len, GMLP_HEADS, GMLP_HEAD_DIM))


def kernel(x_prompt, x_sample, cache_k, cache_v, c_prompt, c_sample, rel_bias_table, w_ada, b_ada,
           g_pre_mix, w_in, attn_sinks, gmlp_v_gain, gmlp_w_s, gmlp_b_s, g_attn_out, g_gmlp_out,
           w_out, g_post_mix, g_pre_ff, w_ff1, w_ff2, g_post_ff):
    depth = w_in.shape[0]
    assert depth == 1, "single-layer step"
    only = lambda a: a.reshape(a.shape[1:])
    outs = _layer(x_prompt, x_sample, only(cache_k), only(cache_v), c_prompt, c_sample,
                  rel_bias_table, *map(only, (w_ada, b_ada, g_pre_mix, w_in, attn_sinks, gmlp_v_gain,
                                              gmlp_w_s, gmlp_b_s, g_attn_out, g_gmlp_out, w_out,
                                              g_post_mix, g_pre_ff, w_ff1, w_ff2, g_post_ff)))
    y_p, y_s, k_p, v_p, k_s, v_s, gv_s = outs
    return (y_p, y_s, k_p[None], v_p[None], k_s[None], v_s[None], gv_s[None])
```

```python
import functools
import math

import numpy as np
import jax
import jax.numpy as jnp
from jax import lax
from jax.experimental import pallas as pl
from jax.experimental.pallas import tpu as pltpu

D_MODEL = 2048
HEAD_DIM = 64
ATTN_WIDTH = 1024
N_HEADS = 16
N_KV_HEADS = 2
GROUP = 8
WINDOW = 128
BLOCK = 128
GMLP_WIDTH = 1024
GMLP_HEADS = 8
GMLP_HEAD_DIM = 128
D_FF = 4 * D_MODEL
NUM_BUCKETS = 32
MAX_DISTANCE = 128
EPS = 1e-6
KV_COLS = N_KV_HEADS * HEAD_DIM
Q_COLS = N_HEADS * HEAD_DIM
IN_COLS = Q_COLS + 2 * KV_COLS + 2 * GMLP_WIDTH
ATTN_SCALE = HEAD_DIM ** -0.5
NEG_INF = -1e30

MIB = 1024 * 1024
VMEM_LIMIT_BYTES = 56 * MIB
MIX_IN_VMEM_BYTES = 60 * MIB
MIX_IN_SMP_VMEM_BYTES = 36 * MIB
MIX_CORE_VMEM_BYTES = 58 * MIB

BF16 = jnp.bfloat16
F32 = jnp.float32


def _params(n_axes, vmem=VMEM_LIMIT_BYTES):
    return pltpu.CompilerParams(
        dimension_semantics=("arbitrary",) * n_axes, vmem_limit_bytes=vmem)


def _resident(shape, index_map):
    return pl.BlockSpec(shape, index_map, pipeline_mode=pl.Buffered(1))


def _rms(x, gain):
    return x * lax.rsqrt(jnp.mean(x * x, axis=-1, keepdims=True) + EPS) * gain


def _rows(m, n_rows):
    r = m.shape[0]
    if r == 1 or r == n_rows:
        return m
    return jnp.concatenate([m] * (n_rows // r), axis=0)


def _dot(a, b):
    return jnp.dot(a, b, preferred_element_type=F32)


def _dot_nt(a, b):
    return lax.dot_general(a, b, (((1,), (1,)), ((), ())), preferred_element_type=F32)


def _t5_bucket(dist):
    n = np.maximum(dist, 0)
    max_exact = NUM_BUCKETS // 2
    nf = np.maximum(n, 1).astype(np.float64)
    large = max_exact + (np.log(nf / max_exact) / math.log(MAX_DISTANCE / max_exact)
                         * (NUM_BUCKETS - max_exact)).astype(np.int32)
    large = np.minimum(large, NUM_BUCKETS - 1)
    return jnp.asarray(np.where(n < max_exact, n, large), jnp.int32)


SMP_KEYS = 2 * WINDOW


def _prompt_dist():
    r = np.arange(BLOCK)[:, None]
    i = np.arange(BLOCK)[None, :]
    return (i - r) % BLOCK


def _sample_rows(t_len):
    r = np.arange(N_KV_HEADS * t_len * GROUP)
    return r // (t_len * GROUP), (r // GROUP) % t_len, r % GROUP


def _sample_dist(t_len):
    _, t, _ = _sample_rows(t_len)
    return WINDOW + t[:, None] - np.arange(SMP_KEYS)[None, :]


def _prep_body(tab_ref, tabr_ref, bkt_p_ref, bkt_s_ref, ws_ref,
               bias_p_ref, bias_s_ref, wtril_ref):
    bkt_p = bkt_p_ref[...]
    for kh in range(N_KV_HEADS):
        for g in range(GROUP):
            h = kh * GROUP + g
            acc = jnp.zeros((BLOCK, BLOCK), F32)
            for b in range(NUM_BUCKETS):
                acc = jnp.where(bkt_p == b, tab_ref[b, h], acc)
            bias_p_ref[kh, :, g * BLOCK:(g + 1) * BLOCK] = acc
    bkt_s = bkt_s_ref[...]
    tabr = tabr_ref[...]
    acc = jnp.zeros(bkt_s.shape, F32)
    for b in range(NUM_BUCKETS):
        acc = jnp.where(bkt_s == b, tabr[:, b:b + 1], acc)
    bias_s_ref[...] = acc
    ii = lax.broadcasted_iota(jnp.int32, (BLOCK, BLOCK), 0)
    jj = lax.broadcasted_iota(jnp.int32, (BLOCK, BLOCK), 1)
    for h in range(GMLP_HEADS):
        wtril_ref[h] = jnp.where(jj <= ii, ws_ref[h], 0.0).astype(BF16)


ADA_TN = 1024


def _adaln_body(c_ref, w_ref, b_ref, os_ref, op_ref):
    n_s, n_p = os_ref.shape[0], op_ref.shape[0]
    c = c_ref[...]
    s = (c * jax.nn.sigmoid(c)).astype(BF16)
    r = _dot(s, w_ref[...].astype(BF16)) + b_ref[...]
    os_ref[...] = r[:n_s]
    for p in range(n_p):
        op_ref[p] = r[n_s + p:n_s + p + 1]


def _cond_body(tab_ref, tabr_ref, bkt_p_ref, bkt_s_ref, ws_ref, c_ref, w_ref, b_ref,
               bias_p_ref, bias_s_ref, wtril_ref, os_ref, op_ref):
    @pl.when(pl.program_id(0) == 0)
    def _():
        _prep_body(tab_ref, tabr_ref, bkt_p_ref, bkt_s_ref, ws_ref, bias_p_ref, bias_s_ref, wtril_ref)

    _adaln_body(c_ref, w_ref, b_ref, os_ref, op_ref)


def _cond(rel_table, w_s, t_len, c_all, n_s, n_p, w_ada, b_ada, n_out):
    bkt_p = _t5_bucket(_prompt_dist())
    bkt_s = _t5_bucket(_sample_dist(t_len))
    n_rows = bkt_s.shape[0]
    tab_rows = jnp.broadcast_to(rel_table.T.reshape(N_KV_HEADS, 1, GROUP, NUM_BUCKETS),
                                (N_KV_HEADS, t_len, GROUP, NUM_BUCKETS)).reshape(n_rows, NUM_BUCKETS)
    n_all = c_all.shape[0]
    whole = lambda a: _resident(a.shape, lambda j: (0,) * a.ndim)
    outs = (jax.ShapeDtypeStruct((N_KV_HEADS, BLOCK, GROUP * BLOCK), F32),
            jax.ShapeDtypeStruct((n_rows, SMP_KEYS), F32),
            jax.ShapeDtypeStruct((GMLP_HEADS, BLOCK, BLOCK), BF16))
    return pl.pallas_call(
        _cond_body,
        out_shape=outs + (jax.ShapeDtypeStruct((n_s, n_out), F32),
                          jax.ShapeDtypeStruct((n_p, 1, n_out), F32)),
        grid=(n_out // ADA_TN,),
        in_specs=[pl.BlockSpec(memory_space=pltpu.SMEM),
                  whole(tab_rows), whole(bkt_p), whole(bkt_s), whole(w_s),
                  _resident((n_all, D_MODEL), lambda j: (0, 0)),
                  pl.BlockSpec((D_MODEL, ADA_TN), lambda j: (0, j)),
                  pl.BlockSpec((1, ADA_TN), lambda j: (0, j))],
        out_specs=tuple(pl.BlockSpec(o.shape, lambda j, nd=len(o.shape): (0,) * nd) for o in outs)
        + (pl.BlockSpec((n_s, ADA_TN), lambda j: (0, j)),
           pl.BlockSpec((n_p, 1, ADA_TN), lambda j: (0, 0, j))),
        compiler_params=_params(1, 40 * MIB),
        name="cond",
    )(rel_table, tab_rows, bkt_p, bkt_s, w_s, c_all, w_ada, b_ada)


W_STAGE_COLS = 256


def _stage_w_in(w_hbm, wb_ref, stage_ref, sem_ref):
    n = IN_COLS // W_STAGE_COLS

    def fetch(c):
        return pltpu.make_async_copy(w_hbm.at[:, pl.ds(c * W_STAGE_COLS, W_STAGE_COLS)],
                                     stage_ref.at[c % 2], sem_ref.at[c % 2])

    fetch(0).start()
    for c in range(n):
        if c + 1 < n:
            fetch(c + 1).start()
        fetch(c).wait()
        wb_ref[:, c * W_STAGE_COLS:(c + 1) * W_STAGE_COLS] = stage_ref[c % 2].astype(BF16)


def _mix_in_body(x_ref, sh_ref, sc_ref, g_ref, w_ref, vg_ref, *refs, transposed):
    if transposed:
        (wc_ref, woa_ref, wob_ref, ca_ref, wa_ref, ba_ref,
         q_ref, k_ref, v_ref, gu_ref, gv_ref, kwin_ref, vwin_ref,
         wcb_ref, wo_ref, wib_hbm, ms_ref, mp_ref, wb_ref, stage_ref, sem_ref, wsem_ref) = refs
        write_back = pltpu.make_async_copy(wb_ref, wib_hbm, wsem_ref)

        @pl.when(pl.program_id(0) == 0)
        def _():
            _stage_w_in(w_ref, wb_ref, stage_ref, sem_ref)
            write_back.start()

        @pl.when(pl.program_id(0) == pl.num_programs(0) - 1)
        def _():
            write_back.wait()

        w_ref = wb_ref
        wcb_ref[...] = wc_ref[...].astype(wcb_ref.dtype)
        wo_ref[:HEAD_DIM, :] = woa_ref[...].astype(wo_ref.dtype)
        wo_ref[HEAD_DIM:, :] = wob_ref[...].astype(wo_ref.dtype)
        _adaln_body(ca_ref, wa_ref, ba_ref, ms_ref, mp_ref)
    else:
        q_ref, k_ref, v_ref, gu_ref, gv_ref = refs
    tm = x_ref.shape[0]
    sub = min(tm, MIX_SUB)
    c0, c1, c2, c3 = Q_COLS, Q_COLS + KV_COLS, Q_COLS + 2 * KV_COLS, Q_COLS + 2 * KV_COLS + GMLP_WIDTH
    vg = vg_ref[...]
    for r in range(0, tm, sub):
        rows = slice(r, r + sub)
        mod = lambda ref: ref[...] if ref.shape[0] == 1 else ref[rows, :]
        h = (_rms(x_ref[rows, :], g_ref[...] * (1.0 + mod(sc_ref))) + mod(sh_ref)).astype(BF16)
        q = _dot(h, w_ref[:, :c0]) * ATTN_SCALE
        kv = _dot(h, w_ref[:, c0:c2])
        k_ref[rows, :] = kv[:, :KV_COLS].astype(k_ref.dtype)
        if transposed:
            q_ref[:, rows] = q.T.astype(q_ref.dtype)
            vt = kv[:, KV_COLS:].T
            v_ref[:, rows] = vt
            if r + sub == tm:
                kwin_ref[...] = kv[-WINDOW:, :KV_COLS].T
                vwin_ref[...] = vt[:, -WINDOW:]
        else:
            q_ref[rows, :] = q.astype(q_ref.dtype)
            v_ref[rows, :] = kv[:, KV_COLS:]
        gu_ref[rows, :] = jax.nn.gelu(_dot(h, w_ref[:, c2:c3])).astype(gu_ref.dtype)
        gv = jax.nn.gelu(_dot(h, w_ref[:, c3:]))
        for hd in range(GMLP_HEADS):
            sl = slice(hd * GMLP_HEAD_DIM, (hd + 1) * GMLP_HEAD_DIM)
            gv_ref[rows, sl] = _rms(gv[:, sl], vg[:, sl]).astype(gv_ref.dtype)


def _mix_in(x, x_spec, sh_spec, sc_spec, mod, grid, g_pre, w_in, v_gain, outs, out_specs,
            side=((), ())):
    extra, extra_specs = side
    prompt = len(extra) > 0
    any_space = pl.BlockSpec(memory_space=pl.ANY)
    scratch = [pltpu.VMEM((D_MODEL, IN_COLS), BF16),
               pltpu.VMEM((2, D_MODEL, W_STAGE_COLS), F32),
               pltpu.SemaphoreType.DMA((2,)), pltpu.SemaphoreType.DMA(())] if prompt else []
    return pl.pallas_call(
        functools.partial(_mix_in_body, transposed=prompt),
        out_shape=outs,
        grid=grid,
        in_specs=[x_spec, sh_spec, sc_spec,
                  _resident((1, D_MODEL), lambda i: (0, 0)),
                  any_space if prompt else _resident((D_MODEL, IN_COLS), lambda i: (0, 0)),
                  _resident((1, GMLP_WIDTH), lambda i: (0, 0)), *extra_specs],
        out_specs=out_specs,
        scratch_shapes=scratch,
        compiler_params=_params(1, MIX_IN_VMEM_BYTES if prompt else MIX_IN_SMP_VMEM_BYTES),
        name="mix_in",
    )(x, mod, mod, g_pre, w_in, v_gain, *extra)


PROJ_TN = 256


def _out_proj_epilogue(o, rows, x_ref, gt_ref, sh_ref, sc_ref, gpm_ref, gpf_ref, x1_ref, hf_ref):
    mod = lambda ref: ref[...] if ref.shape[0] == 1 else ref[rows, :]
    x1 = x_ref[rows, :] + _rms(o, gpm_ref[...] * mod(gt_ref))
    x1_ref[rows, :] = x1
    hf = _rms(x1, gpf_ref[...] * (1.0 + mod(sc_ref))) + mod(sh_ref)
    hf_ref[rows, :] = hf.astype(hf_ref.dtype)


CORE_NB = 4


def _mix_core_body(qt_ref, kc_ref, kp_ref, vtc_ref, vtp_ref, gu_ref, gv_ref,
                   bias_ref, pm_ref, sink_ref, wtril_ref, bs_ref, gat_ref, gg_ref, wc_ref,
                   x_ref, gt_ref, sh_ref, sc_ref, wb_ref, gpm_ref, gpf_ref,
                   x1_ref, hf_ref, wcb_ref, *, steps_per_seq):
    wcb_ref[...] = wc_ref[...].astype(wcb_ref.dtype)
    first = pl.program_id(0) % steps_per_seq == 0
    tri = (lax.broadcasted_iota(jnp.int32, (BLOCK, BLOCK), 0)
           > lax.broadcasted_iota(jnp.int32, (BLOCK, BLOCK), 1))
    from_prev = jnp.concatenate([tri] * GROUP, axis=1)
    zeros = jnp.zeros((HEAD_DIM, GROUP * BLOCK), BF16)
    bs = bs_ref[...]
    def mix_stages(u, st):
        tok = slice(u * BLOCK, (u + 1) * BLOCK)

        def logits(kh):
            if kh == 0:
                k_prev = kp_ref[...] if u == 0 else kc_ref[(u - 1) * BLOCK:u * BLOCK, :]
                vt_prev = vtp_ref[...] if u == 0 else vtc_ref[:, (u - 1) * BLOCK:u * BLOCK]
                st['k'] = jnp.concatenate([k_prev, kc_ref[tok, :]], axis=0)
                st['vt'] = jnp.concatenate([vt_prev, vtc_ref[:, tok]], axis=1).astype(BF16)
                st['outs'] = []
            base = kh * GROUP * HEAD_DIM
            qs = jnp.concatenate(
                [qt_ref[base + g * HEAD_DIM: base + (g + 1) * HEAD_DIM, tok] for g in range(GROUP)],
                axis=1)
            qp = jnp.concatenate([qs, zeros] if kh == 0 else [zeros, qs], axis=0)
            st['qk'] = _dot(st['k'], qp)

        def softmax(kh):
            qk = st['qk']
            s = jnp.where(from_prev, qk[:BLOCK], qk[BLOCK:]) + bias_ref[kh]
            if u == 0:
                s = jnp.where(jnp.logical_and(first, from_prev), NEG_INF, s)
            sink = sink_ref[kh]
            m =jnp.maximum(jnp.max(s, axis=0, keepdims=True), sink)
            p = jnp.exp(s - m)
            st['den'] = jnp.sum(p, axis=0, keepdims=True) + jnp.exp(sink - m)
            p = p.astype(BF16)
            st['p2'] = jnp.concatenate([p * pm_ref[0], p * pm_ref[1]], axis=0)

        def values(kh):
            vt = st['vt'][kh * HEAD_DIM:(kh + 1) * HEAD_DIM, :]
            st['outs'].append(_dot(vt, st['p2']) / st['den'])

        def attn_norm():
            ot = jnp.concatenate(st['outs'], axis=0)
            sq = jnp.sum(ot * ot, axis=0, keepdims=True)
            tot = sq[:, :BLOCK]
            for g in range(1, GROUP):
                tot = tot + sq[:, g * BLOCK:(g + 1) * BLOCK]
            r = lax.rsqrt(tot / ATTN_WIDTH + EPS)
            a = ot * jnp.concatenate([r] * GROUP, axis=1) * gat_ref[...]
            st['merged'] = [a[:, g * BLOCK:(g + 1) * BLOCK].T.astype(BF16) for g in range(GROUP)]

        def gating():
            gated = []
            for hd in range(GMLP_HEADS):
                sl = slice(hd * GMLP_HEAD_DIM, (hd + 1) * GMLP_HEAD_DIM)
                mixed = _dot(wtril_ref[hd], gv_ref[tok, sl]) + bs[:, hd:hd + 1]
                gated.append(gu_ref[tok, sl] * mixed)
            gm = jnp.concatenate(gated, axis=-1)
            st['merged'] = jnp.concatenate(
                st['merged'] + [_rms(gm, gg_ref[...]).astype(BF16)], axis=1)

        P = functools.partial
        return [P(logits, 0), P(softmax, 0), P(values, 0), P(logits, 1), P(softmax, 1),
                P(values, 1), attn_norm, gating]

    def proj_stages(u, st):
        tok = slice(u * BLOCK, (u + 1) * BLOCK)
        chunks = []

        def chunk(c):
            chunks.append(_dot(st['merged'], wb_ref[:, c * PROJ_TN:(c + 1) * PROJ_TN]))

        def finish():
            _out_proj_epilogue(jnp.concatenate(chunks, axis=1), tok, x_ref, gt_ref, sh_ref,
                               sc_ref, gpm_ref, gpf_ref, x1_ref, hf_ref)

        return [functools.partial(chunk, c) for c in range(D_MODEL // PROJ_TN)] + [finish]

    states = [dict() for _ in range(CORE_NB)]
    for u in range(CORE_NB + 1):
        a = mix_stages(u, states[u]) if u < CORE_NB else []
        b = proj_stages(u - 1, states[u - 1]) if u > 0 else []
        for i in range(max(len(a), len(b))):
            for stage in (b[i:i + 1] + a[i:i + 1]):
                stage()


def _mix_core(qt, k, vt, gu, gvn, bias_t, sink_row, wtril, bs_t, gain_t, g_gmlp, seq, w_side,
              x, mod, mod_spec, w_out_b, g_post_mix, g_pre_ff):
    r_gt_i = np.tile(np.arange(BLOCK)[:, None] > np.arange(BLOCK)[None, :], (1, GROUP))
    pmask = jnp.asarray(np.stack([r_gt_i, ~r_gt_i]), BF16)
    n_tok = k.shape[0]
    tm = CORE_NB * BLOCK
    n_steps = n_tok // tm
    side_rows = w_side.shape[0] // n_steps
    GT_M, SH_F, SC_F = 0, 1, 2
    cur = lambda i: (i, 0)
    prev = lambda i: (jnp.maximum(i * CORE_NB - 1, 0), 0)
    cur_t = lambda i: (0, i)
    prev_t = lambda i: (0, jnp.maximum(i * CORE_NB - 1, 0))
    full2 = lambda i: (0, 0)
    full3 = lambda i: (0, 0, 0)
    return pl.pallas_call(
        functools.partial(_mix_core_body, steps_per_seq=seq // tm),
        out_shape=(jax.ShapeDtypeStruct((n_tok, D_MODEL), F32),
                   jax.ShapeDtypeStruct((n_tok, D_MODEL), BF16),
                   jax.ShapeDtypeStruct(w_side.shape, BF16)),
        grid=(n_steps,),
        in_specs=[pl.BlockSpec((Q_COLS, tm), cur_t),
                  pl.BlockSpec((tm, KV_COLS), cur), pl.BlockSpec((BLOCK, KV_COLS), prev),
                  pl.BlockSpec((KV_COLS, tm), cur_t), pl.BlockSpec((KV_COLS, BLOCK), prev_t),
                  pl.BlockSpec((tm, GMLP_WIDTH), cur), pl.BlockSpec((tm, GMLP_WIDTH), cur),
                  _resident(bias_t.shape, full3), _resident(pmask.shape, full3),
                  _resident(sink_row.shape, full3), _resident(wtril.shape, full3),
                  _resident(bs_t.shape, full2),
                  _resident(gain_t.shape, full2), _resident((1, GMLP_WIDTH), full2),
                  pl.BlockSpec((side_rows, w_side.shape[1]), cur),
                  pl.BlockSpec((tm, D_MODEL), cur),
                  mod_spec(GT_M, tm), mod_spec(SH_F, tm), mod_spec(SC_F, tm),
                  _resident((D_MODEL, D_MODEL), full2),
                  _resident((1, D_MODEL), full2), _resident((1, D_MODEL), full2)],
        out_specs=(pl.BlockSpec((tm, D_MODEL), cur), pl.BlockSpec((tm, D_MODEL), cur),
                   pl.BlockSpec((side_rows, w_side.shape[1]), cur)),
        compiler_params=_params(1, MIX_CORE_VMEM_BYTES),
        name="mix_core",
    )(qt, k, k, vt, vt, gu, gvn, bias_t, pmask, sink_row, wtril, bs_t, gain_t, g_gmlp, w_side,
      x, mod, mod, mod, w_out_b, g_post_mix, g_pre_ff)


SMP_BB = 32


def _attn_smp_body(q_ref, ckt_ref, cvt_ref, nk_ref, nv_ref, bias_ref, valid_ref, sink_ref,
                   o_ref, *, t_len):
    zpad = jnp.zeros((WINDOW - t_len, KV_COLS), F32)
    tiles = []
    for b in range(SMP_BB):
        q = q_ref[b]
        s_cache = _dot(q, ckt_ref[b].astype(BF16))
        kn = jnp.concatenate([nk_ref[b], zpad], axis=0).astype(BF16)
        tiles.append(jnp.concatenate([s_cache, _dot_nt(q, kn)], axis=1)[None])
    s = jnp.concatenate(tiles, axis=0)
    s = jnp.where(valid_ref[...] > 0.5, s + bias_ref[...], NEG_INF)
    sink = sink_ref[...]
    m = jnp.maximum(jnp.max(s, axis=-1, keepdims=True), sink)
    p = jnp.exp(s - m)
    den = jnp.sum(p, axis=-1, keepdims=True) + jnp.exp(sink - m)
    p = p.astype(BF16)
    half = t_len * GROUP
    own_head = lax.broadcasted_iota(jnp.int32, (half, KV_COLS), 1) < HEAD_DIM
    for b in range(SMP_BB):
        vn = jnp.concatenate([nv_ref[b], zpad], axis=0).astype(BF16)
        o = _dot_nt(p[b, :, :WINDOW], cvt_ref[b].astype(BF16)) + _dot(p[b, :, WINDOW:], vn)
        o = o / den[b]
        o_ref[b] = jnp.where(own_head, o[:half], o[half:])


def _attn_smp(q6, cache_kt, cache_vt, new_k, new_v, bias_s, valid_s, sink_s, t_len):
    n_b, rows, _ = q6.shape
    b3 = lambda i: (i, 0, 0)
    one = lambda i: (0, 0)
    return pl.pallas_call(
        functools.partial(_attn_smp_body, t_len=t_len),
        out_shape=jax.ShapeDtypeStruct((n_b, rows // N_KV_HEADS, KV_COLS), F32),
        grid=(n_b // SMP_BB,),
        in_specs=[pl.BlockSpec((SMP_BB, rows, KV_COLS), b3),
                  pl.BlockSpec((SMP_BB, KV_COLS, WINDOW), b3),
                  pl.BlockSpec((SMP_BB, KV_COLS, WINDOW), b3),
                  pl.BlockSpec((SMP_BB, t_len, KV_COLS), b3),
                  pl.BlockSpec((SMP_BB, t_len, KV_COLS), b3),
                  _resident(bias_s.shape, one), _resident(valid_s.shape, one),
                  _resident(sink_s.shape, one)],
        out_specs=pl.BlockSpec((SMP_BB, rows // N_KV_HEADS, KV_COLS), b3),
        compiler_params=_params(1, 32 * MIB),
        name="attn_smp",
    )(q6, cache_kt, cache_vt, new_k, new_v, bias_s, valid_s, sink_s)


def _out_proj_smp_body(w4_ref, b4_ref, a_ref, gu_ref, gv_ref, ga_ref, gg_ref,
                       x_ref, gt_ref, sh_ref, sc_ref, wb_ref, gpm_ref, gpf_ref, x1_ref, hf_ref, *,
                       t_len):
    i = pl.program_id(0)
    gated = []
    for hd in range(GMLP_HEADS):
        acc = jnp.zeros((a_ref.shape[0], GMLP_HEAD_DIM), F32)
        for j in range(t_len):
            w = jnp.where(j <= i, w4_ref[(hd * t_len + i) * t_len + j], 0.0)
            lo = j * GMLP_WIDTH + hd * GMLP_HEAD_DIM
            acc = acc + w * gv_ref[:, lo:lo + GMLP_HEAD_DIM]
        mixed = acc + b4_ref[hd * t_len + i]
        gated.append(gu_ref[:, hd * GMLP_HEAD_DIM:(hd + 1) * GMLP_HEAD_DIM] * mixed)
    gm = jnp.concatenate(gated, axis=-1)
    merged = jnp.concatenate([_rms(a_ref[...], ga_ref[...]), _rms(gm, gg_ref[...])], axis=-1)
    _out_proj_epilogue(_dot(merged.astype(BF16), wb_ref[...]), slice(0, a_ref.shape[0]),
                       x_ref, gt_ref, sh_ref, sc_ref, gpm_ref, gpf_ref, x1_ref, hf_ref)


FFN_TF = 1024
FFN_SUB = 256


def _ffn_body(h_ref, w1_ref, w2_ref, x1_ref, gt_ref, g_ref, o_ref, acc_ref, done_ref, *,
              n_tiles, n_chunks):
    s = pl.program_id(0)
    i = s // n_chunks
    j = s % n_chunks
    tm = o_ref.shape[0]
    whole = slice(0, tm)

    def contribution(rows):
        a = jnp.maximum(_dot(h_ref[rows, :], w1_ref[...]), 0.0)
        return _dot((a * a).astype(BF16), w2_ref[...])

    def finalize(rows):
        gt = gt_ref[...] if gt_ref.shape[0] == 1 else _rows(gt_ref[...], tm)[rows, :]
        o_ref[rows, :] = x1_ref[rows, :] + _rms(done_ref[rows, :], g_ref[...] * gt)

    @pl.when(s == 0)
    def _():
        acc_ref[...] = contribution(whole)

    @pl.when(jnp.logical_and(j == 0, jnp.logical_and(i > 0, i < n_tiles)))
    def _():
        for r in range(0, tm, FFN_SUB):
            rows = slice(r, r + FFN_SUB)
            acc_ref[rows, :] = contribution(rows)
            finalize(rows)

    @pl.when(jnp.logical_and(j > 0, j < n_chunks - 1))
    def _():
        acc_ref[...] += contribution(whole)

    @pl.when(jnp.logical_and(j == n_chunks - 1, i < n_tiles))
    def _():
        done_ref[...] = acc_ref[...] + contribution(whole)

    @pl.when(i == n_tiles)
    def _():
        finalize(whole)


def _ffn(hf, w1, w2, x1, mod, gt_spec, g_post, tm):
    n_tok = hf.shape[0]
    n_tiles, n_chunks = n_tok // tm, D_FF // FFN_TF
    last = n_tiles * n_chunks
    mm_tile = lambda s: jnp.minimum(s // n_chunks, n_tiles - 1)
    chunk = lambda s: jnp.where(s == last, n_chunks - 1, s % n_chunks)
    out_tile = lambda s: jnp.maximum(s // n_chunks - (s % n_chunks == 0).astype(jnp.int32), 0)
    return pl.pallas_call(
        functools.partial(_ffn_body, n_tiles=n_tiles, n_chunks=n_chunks),
        out_shape=jax.ShapeDtypeStruct((n_tok, D_MODEL), F32),
        grid=(last + 1,),
        in_specs=[pl.BlockSpec((tm, D_MODEL), lambda s: (mm_tile(s), 0)),
                  pl.BlockSpec((D_MODEL, FFN_TF), lambda s: (0, chunk(s))),
                  pl.BlockSpec((FFN_TF, D_MODEL), lambda s: (chunk(s), 0)),
                  pl.BlockSpec((tm, D_MODEL), lambda s: (out_tile(s), 0)),
                  gt_spec(out_tile),
                  _resident((1, D_MODEL), lambda s: (0, 0))],
        out_specs=pl.BlockSpec((tm, D_MODEL), lambda s: (out_tile(s), 0)),
        scratch_shapes=[pltpu.VMEM((tm, D_MODEL), F32), pltpu.VMEM((tm, D_MODEL), F32)],
        compiler_params=_params(1),
        name="ffn",
    )(hf, w1, w2, x1, mod, g_post)


MIX_TM = 512
MIX_SUB = 128
FFN_TM = 512


def _row(v):
    return v.reshape(1, -1)


def _layer(x_prompt, x_sample, cache_k, cache_v, c_prompt, c_sample, rel_table, w_ada, b_ada,
           g_pre_mix, w_in, sinks, v_gain, w_s, b_s, g_attn, g_gmlp, w_out, g_post_mix,
           g_pre_ff, w_ff1, w_ff2, g_post_ff):
    n_b, seq, _ = x_prompt.shape
    n_db, t_len, _ = x_sample.shape
    n_tok = n_b * seq
    D = D_MODEL

    g_pre_mix, g_attn, g_gmlp = _row(g_pre_mix), _row(g_attn), _row(g_gmlp)
    g_post_mix, g_pre_ff, g_post_ff = _row(g_post_mix), _row(g_pre_ff), _row(g_post_ff)
    v_gain = _row(v_gain)

    c_all = jnp.concatenate(
        [c_sample, c_prompt, jnp.zeros(((-(n_b + n_db)) % 16, D), F32)], axis=0)
    b_ada = b_ada.reshape(1, 6 * D)
    bias_p, bias_s, wtril, mod_s, mod_p = _cond(rel_table, w_s, t_len, c_all, n_db, n_b,
                                                w_ada, b_ada, 2 * D)
    SH_M, SC_M = 0, 1
    GT_M, SH_F, SC_F, GT_F = range(4)

    def pmod(chunk, tm):
        per = seq // tm
        return pl.BlockSpec((None, 1, D), lambda i: (i // per, 0, chunk))

    def smod(chunk):
        return pl.BlockSpec((n_db, D), lambda i: (0, chunk))

    sd = _sample_dist(t_len)
    valid_s = ((sd >= 0) & (sd < WINDOW) & (np.arange(SMP_KEYS)[None, :] < WINDOW + t_len))
    valid_s = valid_s.astype(np.float32)
    sink2 = sinks.reshape(N_KV_HEADS, GROUP)
    sink_p = jnp.repeat(sink2, BLOCK, axis=1)[:, None, :]
    sink_s = jnp.tile(sink2, (1, t_len)).reshape(-1, 1)
    ga3 = g_attn.reshape(N_KV_HEADS, GROUP, HEAD_DIM)
    gain_t = jnp.repeat(ga3.transpose(0, 2, 1).reshape(KV_COLS, GROUP), BLOCK, axis=1)
    g_attn = ga3.transpose(1, 0, 2).reshape(1, ATTN_WIDTH)

    xp = x_prompt.reshape(n_tok, D)
    tile = lambda w: pl.BlockSpec((MIX_TM, w), lambda i: (i, 0))
    ttile = lambda w: pl.BlockSpec((w, MIX_TM), lambda i: (0, i))
    n_mix = n_tok // MIX_TM
    slab1 = pl.BlockSpec((D // n_mix, D_FF), lambda i: (i, 0))
    assert seq >= WINDOW and MIX_TM >= WINDOW
    mix_per = seq // MIX_TM
    win = pl.BlockSpec((None, KV_COLS, WINDOW), lambda i: (i // mix_per, 0, 0))
    assert n_mix * BLOCK == D and GROUP * BLOCK == ATTN_WIDTH
    piece = lambda off: pl.BlockSpec(
        (HEAD_DIM, D), lambda i: (jnp.where(i < GROUP, i + off * GROUP, 2 * i + off), 0))
    slab_o = pl.BlockSpec((BLOCK, D), lambda i: (i, 0))
    ada_tn = 4 * D // n_mix
    ada_col = lambda i: (0, 2 * D // ada_tn + i)
    (qt_p, k_p, vt_p, gu_p, gvn_p, kwin, vwin, w1_b, w_out_b, w_in_b,
     mod2_s, mod2_p) = _mix_in(
        xp, tile(D), pmod(SH_M, MIX_TM), pmod(SC_M, MIX_TM), mod_p, (n_tok // MIX_TM,),
        g_pre_mix, w_in, v_gain,
        (jax.ShapeDtypeStruct((Q_COLS, n_tok), BF16),
         jax.ShapeDtypeStruct((n_tok, KV_COLS), BF16),
         jax.ShapeDtypeStruct((KV_COLS, n_tok), F32),
         jax.ShapeDtypeStruct((n_tok, GMLP_WIDTH), F32),
         jax.ShapeDtypeStruct((n_tok, GMLP_WIDTH), BF16),
         jax.ShapeDtypeStruct((n_b, KV_COLS, WINDOW), F32),
         jax.ShapeDtypeStruct((n_b, KV_COLS, WINDOW), F32),
         jax.ShapeDtypeStruct((D, D_FF), BF16),
         jax.ShapeDtypeStruct((D, D), BF16),
         jax.ShapeDtypeStruct((D, IN_COLS), BF16),
         jax.ShapeDtypeStruct((n_db, 4 * D), F32),
         jax.ShapeDtypeStruct((n_b, 1, 4 * D), F32)),
        (ttile(Q_COLS), tile(KV_COLS), ttile(KV_COLS), tile(GMLP_WIDTH), tile(GMLP_WIDTH),
         win, win, slab1, slab_o, pl.BlockSpec(memory_space=pl.ANY),
         pl.BlockSpec((n_db, ada_tn), lambda i: (0, i)), pl.BlockSpec((n_b, 1, ada_tn), lambda i: (0, 0, i))),
        side=((w_ff1, w_out, w_out, c_all, w_ada, b_ada),
              (slab1, piece(0), piece(1), _resident(c_all.shape, lambda i: (0, 0)),
               pl.BlockSpec((D, ada_tn), ada_col), pl.BlockSpec((1, ada_tn), ada_col))))

    x1_p, hf_p, w2_b = _mix_core(qt_p, k_p, vt_p, gu_p, gvn_p, bias_p, sink_p,
                                 wtril, b_s.T, gain_t, g_gmlp, seq, w_ff2,
                                 xp, mod2_p, pmod, w_out_b, g_post_mix, g_pre_ff)
    one = lambda i: (0, 0)

    ffn_per = seq // FFN_TM
    gate_p = lambda tile: pl.BlockSpec((None, 1, D), lambda s: (tile(s) // ffn_per, 0, GT_F))
    y_p = _ffn(hf_p, w1_b, w2_b, x1_p, mod2_p, gate_p, g_post_ff, FFN_TM)

    xs = x_sample.reshape(n_db, t_len * D)
    lane = lambda w: pl.BlockSpec((n_db, w), lambda t: (0, t))
    q_s, k_s, v_s, gu_s, gvn_s = _mix_in(
        xs, lane(D), smod(SH_M), smod(SC_M), mod_s, (t_len,),
        g_pre_mix, w_in_b, v_gain,
        (jax.ShapeDtypeStruct((n_db, t_len * Q_COLS), F32),
         jax.ShapeDtypeStruct((n_db, t_len * KV_COLS), F32),
         jax.ShapeDtypeStruct((n_db, t_len * KV_COLS), F32),
         jax.ShapeDtypeStruct((n_db, t_len * GMLP_WIDTH), F32),
         jax.ShapeDtypeStruct((n_db, t_len * GMLP_WIDTH), F32)),
        (lane(Q_COLS), lane(KV_COLS), lane(KV_COLS), lane(GMLP_WIDTH), lane(GMLP_WIDTH)))

    q5 = q_s.reshape(n_db, t_len, N_KV_HEADS, GROUP, HEAD_DIM).transpose(0, 2, 1, 3, 4)
    q5 = q5.reshape(n_db, N_KV_HEADS, t_len * GROUP, HEAD_DIM).astype(BF16)
    z5 = jnp.zeros_like(q5[:, 0])
    q6 = jnp.concatenate([jnp.concatenate([q5[:, 0], z5], axis=-1),
                          jnp.concatenate([z5, q5[:, 1]], axis=-1)], axis=1)
    new_k = k_s.reshape(n_db, t_len, KV_COLS)
    new_v = v_s.reshape(n_db, t_len, KV_COLS)
    cache_kt = cache_k.transpose(0, 2, 3, 1).reshape(n_db, KV_COLS, WINDOW)
    cache_vt = cache_v.transpose(0, 2, 3, 1).reshape(n_db, KV_COLS, WINDOW)
    attn_s = _attn_smp(q6, cache_kt, cache_vt, new_k, new_v, bias_s, jnp.asarray(valid_s), sink_s,
                       t_len)
    attn_s = attn_s.reshape(n_db, t_len * ATTN_WIDTH)
    smem = pl.BlockSpec(memory_space=pltpu.SMEM)
    w4 = w_s[:, :t_len, :t_len].reshape(-1)
    b4 = b_s[:, :t_len].reshape(-1)
    tmaj = lambda: pl.BlockSpec((None, n_db, D), lambda t: (t, 0, 0))
    x1_s, hf_s = pl.pallas_call(
        functools.partial(_out_proj_smp_body, t_len=t_len),
        out_shape=(jax.ShapeDtypeStruct((t_len, n_db, D), F32),
                   jax.ShapeDtypeStruct((t_len, n_db, D), BF16)),
        grid=(t_len,),
        in_specs=[smem, smem, lane(ATTN_WIDTH), lane(GMLP_WIDTH),
                  _resident((n_db, t_len * GMLP_WIDTH), one),
                  _resident((1, ATTN_WIDTH), one), _resident((1, GMLP_WIDTH), one),
                  lane(D), smod(GT_M), smod(SH_F), smod(SC_F),
                  _resident((D, D), one), _resident((1, D), one), _resident((1, D), one)],
        out_specs=(tmaj(), tmaj()),
        compiler_params=_params(1),
        name="out_proj_smp",
    )(w4, b4, attn_s, gu_s, gvn_s, g_attn, g_gmlp, xs, mod2_s, mod2_s, mod2_s,
      w_out_b, g_post_mix, g_pre_ff)

    n_st = t_len * n_db
    y_s = _ffn(hf_s.reshape(n_st, D), w1_b, w2_b, x1_s.reshape(n_st, D), mod2_s,
               lambda tile: smod(GT_F), g_post_ff, n_st)
    y_s = y_s.reshape(t_len, n_db, D).transpose(1, 0, 2)

    last = lambda t: t.reshape(n_b, N_KV_HEADS, HEAD_DIM, WINDOW).transpose(0, 3, 1, 2)
    k_p4, v_p4 = last(kwin), last(vwin)
    return (y_p.reshape(n_b, seq, D), y_s, k_p4, v_p4,
            k_s.reshape(n_db, t_len, N_KV_HEADS, HEAD_DIM),
            v_s.reshape(n_db, t_len, N_KV_HEADS, HEAD_DIM),
            gvn_s.reshape(n_db, t_len, GMLP_HEADS, GMLP_HEAD_DIM))


def kernel(x_prompt, x_sample, cache_k, cache_v, c_prompt, c_sample, rel_bias_table, w_ada, b_ada,
           g_pre_mix, w_in, attn_sinks, gmlp_v_gain, gmlp_w_s, gmlp_b_s, g_attn_out, g_gmlp_out,
           w_out, g_post_mix, g_pre_ff, w_ff1, w_ff2, g_post_ff):
    depth = w_in.shape[0]
    assert depth == 1, "single-layer step"
    only = lambda a: a.reshape(a.shape[1:])
    outs = _layer(x_prompt, x_sample, only(cache_k), only(cache_v), c_prompt, c_sample,
                  rel_bias_table, *map(only, (w_ada, b_ada, g_pre_mix, w_in, attn_sinks, gmlp_v_gain,
                                              gmlp_w_s, gmlp_b_s, g_attn_out, g_gmlp_out, w_out,
                                              g_post_mix, g_pre_ff, w_ff1, w_ff2, g_post_ff)))
    y_p, y_s, k_p, v_p, k_s, v_s, gv_s = outs
    return (y_p, y_s, k_p[None], v_p[None], k_s[None], v_s[None], gv_s[None])
```
